```python
import jax
import jax.numpy as jnp
from jax import lax
import numpy as np

D_MODEL = 1024
BATCH = 8
SEQ = 16384
DEPTH = 2

MEM_LEN = 256
CONV_WIDTH = 512
CONV_K = 3
SG_WIDTH = 512
SG_GROUPS = 4
SG_GROUP_DIM = SG_WIDTH // SG_GROUPS
CHUNK = 128
FOX_HEADS = 8
FOX_HEAD_DIM = 64
FOX_WIDTH = FOX_HEADS * FOX_HEAD_DIM
Q_BLOCK = 128
FORGET_BIAS_CENTER = 3.0
N_BRANCH = 3
BRANCH_WIDTH = 512
XA_HEADS = 4
XA_HEAD_DIM = D_MODEL // XA_HEADS
D_FF = 2816
RMS_EPS = 1e-6
SPLIT_SIZES = (CONV_WIDTH, CONV_WIDTH, CONV_WIDTH, SG_WIDTH, SG_WIDTH,
               FOX_WIDTH, FOX_WIDTH, FOX_WIDTH, FOX_HEADS, N_BRANCH * D_MODEL)
D_IN_PROJ = sum(SPLIT_SIZES)

kernel_name = "hybrid_conv_gmlp_fox_macaron"


def rmsnorm(x, g):
    xf = x.astype(jnp.float32)
    y = xf * lax.rsqrt(jnp.mean(xf * xf, axis=-1, keepdims=True) + RMS_EPS)
    return (y * g.astype(jnp.float32)).astype(x.dtype)


def swiglu(h, w_gate, w_up, w_down):
    return (jax.nn.silu(h @ w_gate) * (h @ w_up)) @ w_down


def short_conv_mixer(b_gate, c_gate, h, conv_w):
    z = c_gate * h
    s = z.shape[1]
    zp = jnp.pad(z, ((0, 0), (CONV_K - 1, 0), (0, 0)))
    y = sum(conv_w[k] * zp[:, k:k + s] for k in range(CONV_K))
    return b_gate * y


def spatial_gating_mixer(u, v, v_norm_g, w_s, b_s):
    bsz, s, _ = v.shape
    u = jax.nn.gelu(u)
    v = rmsnorm(jax.nn.gelu(v), v_norm_g)
    vc = v.reshape(bsz, s // CHUNK, CHUNK, SG_GROUPS, SG_GROUP_DIM)
    causal = jnp.tril(jnp.ones((CHUNK, CHUNK), dtype=bool))
    w = jnp.where(causal[None], w_s, 0.0).astype(v.dtype)
    sv = jnp.einsum('gts,bnsgc->bntgc', w, vc) + b_s.T[:, :, None].astype(v.dtype)
    return u * sv.reshape(bsz, s, SG_WIDTH)


def forgetting_attention(q, k, v, f_logit, b_f):
    bsz, s, _ = q.shape
    q = q.reshape(bsz, s, FOX_HEADS, FOX_HEAD_DIM)
    k = k.reshape(bsz, s, FOX_HEADS, FOX_HEAD_DIM)
    v = v.reshape(bsz, s, FOX_HEADS, FOX_HEAD_DIM)
    log_f = jax.nn.log_sigmoid(f_logit.astype(jnp.float32) + b_f.astype(jnp.float32))
    c = jnp.cumsum(log_f, axis=1).transpose(0, 2, 1)
    scale = FOX_HEAD_DIM ** -0.5
    kpos = jnp.arange(s)

    def block(i):
        start = i * Q_BLOCK
        qb = lax.dynamic_slice_in_dim(q, start, Q_BLOCK, axis=1)
        cq = lax.dynamic_slice_in_dim(c, start, Q_BLOCK, axis=2)
        logits = jnp.einsum('bqhd,bkhd->bhqk', qb, k,
                            preferred_element_type=jnp.float32) * scale
        logits = logits + cq[..., :, None] - c[..., None, :]
        qpos = start + jnp.arange(Q_BLOCK)
        logits = jnp.where(kpos[None, :] <= qpos[:, None], logits, -jnp.inf)
        p = jax.nn.softmax(logits, axis=-1)
        return jnp.einsum('bhqk,bkhd->bqhd', p.astype(v.dtype), v)

    out = lax.map(block, jnp.arange(s // Q_BLOCK))
    return out.transpose(1, 0, 2, 3, 4).reshape(bsz, s, FOX_WIDTH)


def memory_cross_attention(h, m, wq, wk, wv, wo):
    bsz, s, _ = h.shape
    q = (h @ wq).reshape(bsz, s, XA_HEADS, XA_HEAD_DIM)
    k = (m @ wk).reshape(bsz, m.shape[1], XA_HEADS, XA_HEAD_DIM)
    v = (m @ wv).reshape(bsz, m.shape[1], XA_HEADS, XA_HEAD_DIM)
    logits = jnp.einsum('bqhd,bkhd->bhqk', q, k,
                        preferred_element_type=jnp.float32) * (XA_HEAD_DIM ** -0.5)
    p = jax.nn.softmax(logits, axis=-1)
    o = jnp.einsum('bhqk,bkhd->bqhd', p.astype(v.dtype), v).reshape(bsz, s, D_MODEL)
    return o @ wo


def _fwd_setup_inputs(seed: int = 0) -> dict:
    key = jax.random.key(seed)
    ks = jax.random.split(key, 32)
    f32 = jnp.float32
    L, D = DEPTH, D_MODEL

    def nrm(k, shape, scale):
        return jax.random.normal(k, shape, f32) * scale

    def gain(k, shape):
        return 1.0 + 0.05 * jax.random.normal(k, shape, f32)

    return {
        "x": nrm(ks[0], (BATCH, SEQ, D), 1.0),
        "mem": nrm(ks[1], (BATCH, MEM_LEN, D), 1.0),
        "ffn1_norm": gain(ks[2], (L, D)),
        "ffn1_w_gate": nrm(ks[3], (L, D, D_FF), D ** -0.5),
        "ffn1_w_up": nrm(ks[4], (L, D, D_FF), D ** -0.5),
        "ffn1_w_down": nrm(ks[5], (L, D_FF, D), D_FF ** -0.5),
        "mix_norm": gain(ks[6], (L, D)),
        "w_in": nrm(ks[7], (L, D, D_IN_PROJ), D ** -0.5),
        "conv_w": nrm(ks[8], (L, CONV_K, CONV_WIDTH), CONV_K ** -0.5),
        "sg_norm": gain(ks[9], (L, SG_WIDTH)),
        "sg_w": nrm(ks[10], (L, SG_GROUPS, CHUNK, CHUNK), CHUNK ** -0.5),
        "sg_b": 1.0 + nrm(ks[11], (L, SG_GROUPS, CHUNK), 0.1),
        "fox_b_f": FORGET_BIAS_CENTER + nrm(ks[12], (L, FOX_HEADS), 1.0),
        "w_branch": nrm(ks[13], (L, N_BRANCH, BRANCH_WIDTH, D), BRANCH_WIDTH ** -0.5),
        "w_out": nrm(ks[14], (L, D, D), D ** -0.5),
        "xa_norm": gain(ks[15], (L, D)),
        "mem_norm": gain(ks[16], (L, D)),
        "xa_wq": nrm(ks[17], (L, D, D), D ** -0.5),
        "xa_wk": nrm(ks[18], (L, D, D), D ** -0.5),
        "xa_wv": nrm(ks[19], (L, D, D), D ** -0.5),
        "xa_wo": nrm(ks[20], (L, D, D), D ** -0.5),
        "ffn2_norm": gain(ks[21], (L, D)),
        "ffn2_w_gate": nrm(ks[22], (L, D, D_FF), D ** -0.5),
        "ffn2_w_up": nrm(ks[23], (L, D, D_FF), D ** -0.5),
        "ffn2_w_down": nrm(ks[24], (L, D_FF, D), D_FF ** -0.5),
        "final_norm": gain(ks[25], (D,)),
    }


def _fwd_reference(x, mem, ffn1_norm, ffn1_w_gate, ffn1_w_up, ffn1_w_down, mix_norm, w_in,
              conv_w, sg_norm, sg_w, sg_b, fox_b_f, w_branch, w_out, xa_norm, mem_norm,
              xa_wq, xa_wk, xa_wv, xa_wo, ffn2_norm, ffn2_w_gate, ffn2_w_up, ffn2_w_down,
              final_norm):
    bsz, s, _ = x.shape
    split_idx = np.cumsum(SPLIT_SIZES)[:-1].tolist()
    for l in range(DEPTH):
        h = rmsnorm(x, ffn1_norm[l])
        x = x + 0.5 * swiglu(h, ffn1_w_gate[l], ffn1_w_up[l], ffn1_w_down[l])

        h = rmsnorm(x, mix_norm[l])
        proj = h @ w_in[l]
        (a_b, a_c, a_h, sg_u, sg_v, fq, fk, fv, f_logit, gate_logits) = jnp.split(
            proj, split_idx, axis=-1)
        y_a = short_conv_mixer(a_b, a_c, a_h, conv_w[l])
        y_b = spatial_gating_mixer(sg_u, sg_v, sg_norm[l], sg_w[l], sg_b[l])
        y_c = forgetting_attention(fq, fk, fv, f_logit, fox_b_f[l])
        branches = jnp.stack([y_a, y_b, y_c], axis=2)
        branches = jnp.einsum('bsnc,ncd->bsnd', branches, w_branch[l])
        gates = jax.nn.sigmoid(gate_logits.reshape(bsz, s, N_BRANCH, D_MODEL))
        merged = jnp.einsum('bsnd,bsnd->bsd', gates, branches)
        x = x + merged @ w_out[l]

        h = rmsnorm(x, xa_norm[l])
        m = rmsnorm(mem, mem_norm[l])
        x = x + memory_cross_attention(h, m, xa_wq[l], xa_wk[l], xa_wv[l], xa_wo[l])

        h = rmsnorm(x, ffn2_norm[l])
        x = x + 0.5 * swiglu(h, ffn2_w_gate[l], ffn2_w_up[l], ffn2_w_down[l])
    return rmsnorm(x, final_norm)


import jax as _jax
import jax.numpy as _jnp

TWIN_FORMAT = 'train_step'
FWD_PARAMS = ['x', 'mem', 'ffn1_norm', 'ffn1_w_gate', 'ffn1_w_up', 'ffn1_w_down', 'mix_norm', 'w_in', 'conv_w', 'sg_norm', 'sg_w', 'sg_b', 'fox_b_f', 'w_branch', 'w_out', 'xa_norm', 'mem_norm', 'xa_wq', 'xa_wk', 'xa_wv', 'xa_wo', 'ffn2_norm', 'ffn2_w_gate', 'ffn2_w_up', 'ffn2_w_down', 'final_norm']
TWIN_WEIGHTS = ['ffn1_norm', 'ffn1_w_gate', 'ffn1_w_up', 'ffn1_w_down', 'mix_norm', 'w_in', 'conv_w', 'sg_norm', 'sg_w', 'sg_b', 'fox_b_f', 'w_branch', 'w_out', 'xa_norm', 'mem_norm', 'xa_wq', 'xa_wk', 'xa_wv', 'xa_wo', 'ffn2_norm', 'ffn2_w_gate', 'ffn2_w_up', 'ffn2_w_down', 'final_norm']
TWIN_DIFF_INPUT = 'x'
TWIN_INPUTS = ['x', 'mem', 'ffn1_norm', 'ffn1_w_gate', 'ffn1_w_up', 'ffn1_w_down', 'mix_norm', 'w_in', 'conv_w', 'sg_norm', 'sg_w', 'sg_b', 'fox_b_f', 'w_branch', 'w_out', 'xa_norm', 'mem_norm', 'xa_wq', 'xa_wk', 'xa_wv', 'xa_wo', 'ffn2_norm', 'ffn2_w_gate', 'ffn2_w_up', 'ffn2_w_down', 'final_norm', 'loss_target', 'm_ffn1_norm', 'm_ffn1_w_gate', 'm_ffn1_w_up', 'm_ffn1_w_down', 'm_mix_norm', 'm_w_in', 'm_conv_w', 'm_sg_norm', 'm_sg_w', 'm_sg_b', 'm_fox_b_f', 'm_w_branch', 'm_w_out', 'm_xa_norm', 'm_mem_norm', 'm_xa_wq', 'm_xa_wk', 'm_xa_wv', 'm_xa_wo', 'm_ffn2_norm', 'm_ffn2_w_gate', 'm_ffn2_w_up', 'm_ffn2_w_down', 'm_final_norm', 'v_ffn1_norm', 'v_ffn1_w_gate', 'v_ffn1_w_up', 'v_ffn1_w_down', 'v_mix_norm', 'v_w_in', 'v_conv_w', 'v_sg_norm', 'v_sg_w', 'v_sg_b', 'v_fox_b_f', 'v_w_branch', 'v_w_out', 'v_xa_norm', 'v_mem_norm', 'v_xa_wq', 'v_xa_wk', 'v_xa_wv', 'v_xa_wo', 'v_ffn2_norm', 'v_ffn2_w_gate', 'v_ffn2_w_up', 'v_ffn2_w_down', 'v_final_norm']
TWIN_OUTPUTS = ['loss', 'grad_x', 'grad_ffn1_norm', 'grad_ffn1_w_gate', 'grad_ffn1_w_up', 'grad_ffn1_w_down', 'grad_mix_norm', 'grad_w_in', 'grad_conv_w', 'grad_sg_norm', 'grad_sg_w', 'grad_sg_b', 'grad_fox_b_f', 'grad_w_branch', 'grad_w_out', 'grad_xa_norm', 'grad_mem_norm', 'grad_xa_wq', 'grad_xa_wk', 'grad_xa_wv', 'grad_xa_wo', 'grad_ffn2_norm', 'grad_ffn2_w_gate', 'grad_ffn2_w_up', 'grad_ffn2_w_down', 'grad_final_norm', 'delta_ffn1_norm', 'delta_ffn1_w_gate', 'delta_ffn1_w_up', 'delta_ffn1_w_down', 'delta_mix_norm', 'delta_w_in', 'delta_conv_w', 'delta_sg_norm', 'delta_sg_w', 'delta_sg_b', 'delta_fox_b_f', 'delta_w_branch', 'delta_w_out', 'delta_xa_norm', 'delta_mem_norm', 'delta_xa_wq', 'delta_xa_wk', 'delta_xa_wv', 'delta_xa_wo', 'delta_ffn2_norm', 'delta_ffn2_w_gate', 'delta_ffn2_w_up', 'delta_ffn2_w_down', 'delta_final_norm', 'new_m_ffn1_norm', 'new_m_ffn1_w_gate', 'new_m_ffn1_w_up', 'new_m_ffn1_w_down', 'new_m_mix_norm', 'new_m_w_in', 'new_m_conv_w', 'new_m_sg_norm', 'new_m_sg_w', 'new_m_sg_b', 'new_m_fox_b_f', 'new_m_w_branch', 'new_m_w_out', 'new_m_xa_norm', 'new_m_mem_norm', 'new_m_xa_wq', 'new_m_xa_wk', 'new_m_xa_wv', 'new_m_xa_wo', 'new_m_ffn2_norm', 'new_m_ffn2_w_gate', 'new_m_ffn2_w_up', 'new_m_ffn2_w_down', 'new_m_final_norm', 'new_v_ffn1_norm', 'new_v_ffn1_w_gate', 'new_v_ffn1_w_up', 'new_v_ffn1_w_down', 'new_v_mix_norm', 'new_v_w_in', 'new_v_conv_w', 'new_v_sg_norm', 'new_v_sg_w', 'new_v_sg_b', 'new_v_fox_b_f', 'new_v_w_branch', 'new_v_w_out', 'new_v_xa_norm', 'new_v_mem_norm', 'new_v_xa_wq', 'new_v_xa_wk', 'new_v_xa_wv', 'new_v_xa_wo', 'new_v_ffn2_norm', 'new_v_ffn2_w_gate', 'new_v_ffn2_w_up', 'new_v_ffn2_w_down', 'new_v_final_norm']
TWIN_LEAF_KINDS = {'loss': 'loss', 'grad_x': 'grad_x', 'grad_ffn1_norm': 'grad_w', 'grad_ffn1_w_gate': 'grad_w', 'grad_ffn1_w_up': 'grad_w', 'grad_ffn1_w_down': 'grad_w', 'grad_mix_norm': 'grad_w', 'grad_w_in': 'grad_w', 'grad_conv_w': 'grad_w', 'grad_sg_norm': 'grad_w', 'grad_sg_w': 'grad_w', 'grad_sg_b': 'grad_w', 'grad_fox_b_f': 'grad_w', 'grad_w_branch': 'grad_w', 'grad_w_out': 'grad_w', 'grad_xa_norm': 'grad_w', 'grad_mem_norm': 'grad_w', 'grad_xa_wq': 'grad_w', 'grad_xa_wk': 'grad_w', 'grad_xa_wv': 'grad_w', 'grad_xa_wo': 'grad_w', 'grad_ffn2_norm': 'grad_w', 'grad_ffn2_w_gate': 'grad_w', 'grad_ffn2_w_up': 'grad_w', 'grad_ffn2_w_down': 'grad_w', 'grad_final_norm': 'grad_w', 'delta_ffn1_norm': 'delta_w', 'delta_ffn1_w_gate': 'delta_w', 'delta_ffn1_w_up': 'delta_w', 'delta_ffn1_w_down': 'delta_w', 'delta_mix_norm': 'delta_w', 'delta_w_in': 'delta_w', 'delta_conv_w': 'delta_w', 'delta_sg_norm': 'delta_w', 'delta_sg_w': 'delta_w', 'delta_sg_b': 'delta_w', 'delta_fox_b_f': 'delta_w', 'delta_w_branch': 'delta_w', 'delta_w_out': 'delta_w', 'delta_xa_norm': 'delta_w', 'delta_mem_norm': 'delta_w', 'delta_xa_wq': 'delta_w', 'delta_xa_wk': 'delta_w', 'delta_xa_wv': 'delta_w', 'delta_xa_wo': 'delta_w', 'delta_ffn2_norm': 'delta_w', 'delta_ffn2_w_gate': 'delta_w', 'delta_ffn2_w_up': 'delta_w', 'delta_ffn2_w_down': 'delta_w', 'delta_final_norm': 'delta_w', 'new_m_ffn1_norm': 'new_m', 'new_m_ffn1_w_gate': 'new_m', 'new_m_ffn1_w_up': 'new_m', 'new_m_ffn1_w_down': 'new_m', 'new_m_mix_norm': 'new_m', 'new_m_w_in': 'new_m', 'new_m_conv_w': 'new_m', 'new_m_sg_norm': 'new_m', 'new_m_sg_w': 'new_m', 'new_m_sg_b': 'new_m', 'new_m_fox_b_f': 'new_m', 'new_m_w_branch': 'new_m', 'new_m_w_out': 'new_m', 'new_m_xa_norm': 'new_m', 'new_m_mem_norm': 'new_m', 'new_m_xa_wq': 'new_m', 'new_m_xa_wk': 'new_m', 'new_m_xa_wv': 'new_m', 'new_m_xa_wo': 'new_m', 'new_m_ffn2_norm': 'new_m', 'new_m_ffn2_w_gate': 'new_m', 'new_m_ffn2_w_up': 'new_m', 'new_m_ffn2_w_down': 'new_m', 'new_m_final_norm': 'new_m', 'new_v_ffn1_norm': 'new_v', 'new_v_ffn1_w_gate': 'new_v', 'new_v_ffn1_w_up': 'new_v', 'new_v_ffn1_w_down': 'new_v', 'new_v_mix_norm': 'new_v', 'new_v_w_in': 'new_v', 'new_v_conv_w': 'new_v', 'new_v_sg_norm': 'new_v', 'new_v_sg_w': 'new_v', 'new_v_sg_b': 'new_v', 'new_v_fox_b_f': 'new_v', 'new_v_w_branch': 'new_v', 'new_v_w_out': 'new_v', 'new_v_xa_norm': 'new_v', 'new_v_mem_norm': 'new_v', 'new_v_xa_wq': 'new_v', 'new_v_xa_wk': 'new_v', 'new_v_xa_wv': 'new_v', 'new_v_xa_wo': 'new_v', 'new_v_ffn2_norm': 'new_v', 'new_v_ffn2_w_gate': 'new_v', 'new_v_ffn2_w_up': 'new_v', 'new_v_ffn2_w_down': 'new_v', 'new_v_final_norm': 'new_v'}


def _forward(args):
    return _fwd_reference(*[args[k] for k in FWD_PARAMS])


def _output_shape():
    def fwd():
        inp = _fwd_setup_inputs(0)
        return _fwd_reference(*[inp[k] for k in FWD_PARAMS])
    out = _jax.eval_shape(fwd)
    return out.shape, out.dtype

N_MICROBATCH = 1
ADAM_LR = 0.001
ADAM_B1 = 0.9
ADAM_B2 = 0.999
ADAM_EPS = 1e-08
ADAM_WD = 0.01
ADAM_STEP = 10
PER_EXAMPLE_BATCH_AXIS = {'x': 0, 'mem': 0, 'loss_target': 0}
SHARED_INPUTS = []
_WEIGHT_DTYPES = {'ffn1_norm': _jnp.float32, 'ffn1_w_gate': _jnp.float32, 'ffn1_w_up': _jnp.float32, 'ffn1_w_down': _jnp.float32, 'mix_norm': _jnp.float32, 'w_in': _jnp.float32, 'conv_w': _jnp.float32, 'sg_norm': _jnp.float32, 'sg_w': _jnp.float32, 'sg_b': _jnp.float32, 'fox_b_f': _jnp.float32, 'w_branch': _jnp.float32, 'w_out': _jnp.float32, 'xa_norm': _jnp.float32, 'mem_norm': _jnp.float32, 'xa_wq': _jnp.float32, 'xa_wk': _jnp.float32, 'xa_wv': _jnp.float32, 'xa_wo': _jnp.float32, 'ffn2_norm': _jnp.float32, 'ffn2_w_gate': _jnp.float32, 'ffn2_w_up': _jnp.float32, 'ffn2_w_down': _jnp.float32, 'final_norm': _jnp.float32}
MOMENT_SCALE = {'ffn1_norm': 1.848213e-01, 'ffn1_w_gate': 8.039915e-02, 'ffn1_w_up': 7.803374e-02, 'ffn1_w_down': 1.292636e-01, 'mix_norm': 3.399980e-01, 'w_in': 1.291661e-01, 'conv_w': 2.386548e-01, 'sg_norm': 1.243540e-01, 'sg_w': 1.096346e-01, 'sg_b': 1.775120e-01, 'fox_b_f': 5.365619e-01, 'w_branch': 1.327800e-01, 'w_out': 2.313515e-01, 'xa_norm': 3.219688e-02, 'mem_norm': 5.647435e-02, 'xa_wq': 3.236648e-02, 'xa_wk': 3.222715e-02, 'xa_wv': 3.459577e-02, 'xa_wo': 3.457680e-02, 'ffn2_norm': 1.279105e-01, 'ffn2_w_gate': 5.389226e-02, 'ffn2_w_up': 5.278082e-02, 'ffn2_w_down': 8.776519e-02, 'final_norm': 1.280938e+02}


def _to_microbatches(a, axis):
    t = _jnp.moveaxis(a, axis, 0)
    t = t.reshape((N_MICROBATCH, t.shape[0] // N_MICROBATCH) + t.shape[1:])
    return _jnp.moveaxis(t, 1, axis + 1)


def setup_inputs(seed: int = 0) -> dict:
    inp = _fwd_setup_inputs(seed)
    key = _jax.random.fold_in(_jax.random.key(seed), 7919)
    shape, _ = _output_shape()
    out = dict(inp)
    out["loss_target"] = _jax.random.normal(_jax.random.fold_in(key, 0), shape, _jnp.float32)
    for i, name in enumerate(TWIN_WEIGHTS):
        w = inp[name].astype(_jnp.float32)
        if MOMENT_SCALE is None:
            s = _jnp.sqrt(_jnp.mean(_jnp.square(w)) + 1e-30)
        else:
            s = MOMENT_SCALE[name]
        km, kv = _jax.random.split(_jax.random.fold_in(key, i + 1))
        out[name] = w
        out["m_" + name] = s * _jax.random.normal(km, w.shape, _jnp.float32)
        out["v_" + name] = (s * s) * _jax.random.uniform(kv, w.shape, _jnp.float32, 0.5, 1.5)
    if N_MICROBATCH > 1:
        for name, axis in PER_EXAMPLE_BATCH_AXIS.items():
            out[name] = _to_microbatches(out[name], axis)
    return {'x': out['x'], 'mem': out['mem'], 'ffn1_norm': out['ffn1_norm'], 'ffn1_w_gate': out['ffn1_w_gate'], 'ffn1_w_up': out['ffn1_w_up'], 'ffn1_w_down': out['ffn1_w_down'], 'mix_norm': out['mix_norm'], 'w_in': out['w_in'], 'conv_w': out['conv_w'], 'sg_norm': out['sg_norm'], 'sg_w': out['sg_w'], 'sg_b': out['sg_b'], 'fox_b_f': out['fox_b_f'], 'w_branch': out['w_branch'], 'w_out': out['w_out'], 'xa_norm': out['xa_norm'], 'mem_norm': out['mem_norm'], 'xa_wq': out['xa_wq'], 'xa_wk': out['xa_wk'], 'xa_wv': out['xa_wv'], 'xa_wo': out['xa_wo'], 'ffn2_norm': out['ffn2_norm'], 'ffn2_w_gate': out['ffn2_w_gate'], 'ffn2_w_up': out['ffn2_w_up'], 'ffn2_w_down': out['ffn2_w_down'], 'final_norm': out['final_norm'], 'loss_target': out['loss_target'], 'm_ffn1_norm': out['m_ffn1_norm'], 'm_ffn1_w_gate': out['m_ffn1_w_gate'], 'm_ffn1_w_up': out['m_ffn1_w_up'], 'm_ffn1_w_down': out['m_ffn1_w_down'], 'm_mix_norm': out['m_mix_norm'], 'm_w_in': out['m_w_in'], 'm_conv_w': out['m_conv_w'], 'm_sg_norm': out['m_sg_norm'], 'm_sg_w': out['m_sg_w'], 'm_sg_b': out['m_sg_b'], 'm_fox_b_f': out['m_fox_b_f'], 'm_w_branch': out['m_w_branch'], 'm_w_out': out['m_w_out'], 'm_xa_norm': out['m_xa_norm'], 'm_mem_norm': out['m_mem_norm'], 'm_xa_wq': out['m_xa_wq'], 'm_xa_wk': out['m_xa_wk'], 'm_xa_wv': out['m_xa_wv'], 'm_xa_wo': out['m_xa_wo'], 'm_ffn2_norm': out['m_ffn2_norm'], 'm_ffn2_w_gate': out['m_ffn2_w_gate'], 'm_ffn2_w_up': out['m_ffn2_w_up'], 'm_ffn2_w_down': out['m_ffn2_w_down'], 'm_final_norm': out['m_final_norm'], 'v_ffn1_norm': out['v_ffn1_norm'], 'v_ffn1_w_gate': out['v_ffn1_w_gate'], 'v_ffn1_w_up': out['v_ffn1_w_up'], 'v_ffn1_w_down': out['v_ffn1_w_down'], 'v_mix_norm': out['v_mix_norm'], 'v_w_in': out['v_w_in'], 'v_conv_w': out['v_conv_w'], 'v_sg_norm': out['v_sg_norm'], 'v_sg_w': out['v_sg_w'], 'v_sg_b': out['v_sg_b'], 'v_fox_b_f': out['v_fox_b_f'], 'v_w_branch': out['v_w_branch'], 'v_w_out': out['v_w_out'], 'v_xa_norm': out['v_xa_norm'], 'v_mem_norm': out['v_mem_norm'], 'v_xa_wq': out['v_xa_wq'], 'v_xa_wk': out['v_xa_wk'], 'v_xa_wv': out['v_xa_wv'], 'v_xa_wo': out['v_xa_wo'], 'v_ffn2_norm': out['v_ffn2_norm'], 'v_ffn2_w_gate': out['v_ffn2_w_gate'], 'v_ffn2_w_up': out['v_ffn2_w_up'], 'v_ffn2_w_down': out['v_ffn2_w_down'], 'v_final_norm': out['v_final_norm']}


def _loss(weights, diff, rest, loss_target):
    with _jax.named_scope("forward"):
        args = {**rest, TWIN_DIFF_INPUT: diff, **{k: w.astype(_WEIGHT_DTYPES[k]) for k, w in weights.items()}}
        y = _forward(args)
    with _jax.named_scope("loss_head"):
        err = _jnp.square(y.astype(_jnp.float32) - loss_target)
        return 0.5 * _jnp.sum(_jnp.mean(err, axis=-1)) if err.ndim else 0.5 * err


def _adamw(w, g, m, v):
    m = ADAM_B1 * m + (1.0 - ADAM_B1) * g
    v = ADAM_B2 * v + (1.0 - ADAM_B2) * _jnp.square(g)
    m_hat = m / (1.0 - ADAM_B1 ** ADAM_STEP)
    v_hat = v / (1.0 - ADAM_B2 ** ADAM_STEP)
    delta = -ADAM_LR * (m_hat / (_jnp.sqrt(v_hat) + ADAM_EPS) + ADAM_WD * w)
    return delta, m, v


def reference(x, mem, ffn1_norm, ffn1_w_gate, ffn1_w_up, ffn1_w_down, mix_norm, w_in, conv_w, sg_norm, sg_w, sg_b, fox_b_f, w_branch, w_out, xa_norm, mem_norm, xa_wq, xa_wk, xa_wv, xa_wo, ffn2_norm, ffn2_w_gate, ffn2_w_up, ffn2_w_down, final_norm, loss_target, m_ffn1_norm, m_ffn1_w_gate, m_ffn1_w_up, m_ffn1_w_down, m_mix_norm, m_w_in, m_conv_w, m_sg_norm, m_sg_w, m_sg_b, m_fox_b_f, m_w_branch, m_w_out, m_xa_norm, m_mem_norm, m_xa_wq, m_xa_wk, m_xa_wv, m_xa_wo, m_ffn2_norm, m_ffn2_w_gate, m_ffn2_w_up, m_ffn2_w_down, m_final_norm, v_ffn1_norm, v_ffn1_w_gate, v_ffn1_w_up, v_ffn1_w_down, v_mix_norm, v_w_in, v_conv_w, v_sg_norm, v_sg_w, v_sg_b, v_fox_b_f, v_w_branch, v_w_out, v_xa_norm, v_mem_norm, v_xa_wq, v_xa_wk, v_xa_wv, v_xa_wo, v_ffn2_norm, v_ffn2_w_gate, v_ffn2_w_up, v_ffn2_w_down, v_final_norm):
    given = dict(x=x, mem=mem, ffn1_norm=ffn1_norm, ffn1_w_gate=ffn1_w_gate, ffn1_w_up=ffn1_w_up, ffn1_w_down=ffn1_w_down, mix_norm=mix_norm, w_in=w_in, conv_w=conv_w, sg_norm=sg_norm, sg_w=sg_w, sg_b=sg_b, fox_b_f=fox_b_f, w_branch=w_branch, w_out=w_out, xa_norm=xa_norm, mem_norm=mem_norm, xa_wq=xa_wq, xa_wk=xa_wk, xa_wv=xa_wv, xa_wo=xa_wo, ffn2_norm=ffn2_norm, ffn2_w_gate=ffn2_w_gate, ffn2_w_up=ffn2_w_up, ffn2_w_down=ffn2_w_down, final_norm=final_norm, loss_target=loss_target, m_ffn1_norm=m_ffn1_norm, m_ffn1_w_gate=m_ffn1_w_gate, m_ffn1_w_up=m_ffn1_w_up, m_ffn1_w_down=m_ffn1_w_down, m_mix_norm=m_mix_norm, m_w_in=m_w_in, m_conv_w=m_conv_w, m_sg_norm=m_sg_norm, m_sg_w=m_sg_w, m_sg_b=m_sg_b, m_fox_b_f=m_fox_b_f, m_w_branch=m_w_branch, m_w_out=m_w_out, m_xa_norm=m_xa_norm, m_mem_norm=m_mem_norm, m_xa_wq=m_xa_wq, m_xa_wk=m_xa_wk, m_xa_wv=m_xa_wv, m_xa_wo=m_xa_wo, m_ffn2_norm=m_ffn2_norm, m_ffn2_w_gate=m_ffn2_w_gate, m_ffn2_w_up=m_ffn2_w_up, m_ffn2_w_down=m_ffn2_w_down, m_final_norm=m_final_norm, v_ffn1_norm=v_ffn1_norm, v_ffn1_w_gate=v_ffn1_w_gate, v_ffn1_w_up=v_ffn1_w_up, v_ffn1_w_down=v_ffn1_w_down, v_mix_norm=v_mix_norm, v_w_in=v_w_in, v_conv_w=v_conv_w, v_sg_norm=v_sg_norm, v_sg_w=v_sg_w, v_sg_b=v_sg_b, v_fox_b_f=v_fox_b_f, v_w_branch=v_w_branch, v_w_out=v_w_out, v_xa_norm=v_xa_norm, v_mem_norm=v_mem_norm, v_xa_wq=v_xa_wq, v_xa_wk=v_xa_wk, v_xa_wv=v_xa_wv, v_xa_wo=v_xa_wo, v_ffn2_norm=v_ffn2_norm, v_ffn2_w_gate=v_ffn2_w_gate, v_ffn2_w_up=v_ffn2_w_up, v_ffn2_w_down=v_ffn2_w_down, v_final_norm=v_final_norm)
    weights = {n: given[n] for n in TWIN_WEIGHTS}
    shared = {n: given[n] for n in SHARED_INPUTS}
    per_example = {n: given[n] for n in ['x', 'mem']}
    grad_fn = _jax.value_and_grad(_loss, argnums=(0, 1))

    def one_microbatch(ex, loss_target):
        ex = dict(ex)
        diff = ex.pop(TWIN_DIFF_INPUT)
        return grad_fn(weights, diff, {**shared, **ex}, loss_target)

    if N_MICROBATCH == 1:
        loss, (grad_w, grad_x) = one_microbatch(per_example, given["loss_target"])
    else:
        def body(carry, xs):
            loss_sum, grad_sum = carry
            l_k, (gw_k, gx_k) = one_microbatch(xs[0], xs[1])
            with _jax.named_scope("update"):
                return (loss_sum + l_k, _jax.tree.map(_jnp.add, grad_sum, gw_k)), gx_k

        init = (_jnp.zeros((), _jnp.float32), _jax.tree.map(_jnp.zeros_like, weights))
        (loss, grad_w), grad_x = _jax.lax.scan(body, init, (per_example, given["loss_target"]))
    with _jax.named_scope("update"):
        delta_w, new_m, new_v = {}, {}, {}
        for n in TWIN_WEIGHTS:
            delta_w[n], new_m[n], new_v[n] = _adamw(weights[n], grad_w[n], given["m_" + n], given["v_" + n])
    return (loss, grad_x, *[grad_w[n] for n in TWIN_WEIGHTS], *[delta_w[n] for n in TWIN_WEIGHTS],
            *[new_m[n] for n in TWIN_WEIGHTS], *[new_v[n] for n in TWIN_WEIGHTS])
```

```python
import functools

import jax
import jax.numpy as jnp
from jax import lax
from jax.experimental import pallas as pl
from jax.experimental.pallas import tpu as pltpu

F32 = jnp.float32
BF16 = jnp.bfloat16

N_DEV = 8
RMS_EPS = 1e-6
SEG = 512
FOX_HEADS = 8
FOX_HEAD_DIM = 64
SG_GROUPS = 4
CHUNK = 128
XA_HEADS = 4
N_BRANCH = 3
LANES = 128
VMEM_LIMIT_BYTES = 48 * 1024 * 1024
NEG_BIG = -1e30

ADAM_LR = 0.001
ADAM_B1 = 0.9
ADAM_B2 = 0.999
ADAM_EPS = 1e-08
ADAM_WD = 0.01
ADAM_STEP = 10

_GELU_K = 0.7978845608028654
_GELU_C = 0.044715

MESH_IDS = pl.DeviceIdType.MESH


def _pick(n, candidates):
    for c in candidates:
        if c <= n and n % c == 0:
            return c
    return n


def _params(*sem):
    return pltpu.CompilerParams(dimension_semantics=sem, vmem_limit_bytes=VMEM_LIMIT_BYTES)


def _sig(x):
    return 1.0 / (1.0 + jnp.exp(-x))


def _gelu(x):
    t = jnp.tanh(_GELU_K * (x + _GELU_C * x * x * x))
    return 0.5 * x * (1.0 + t), t


def _gelu_grad(x, t):
    return 0.5 * (1.0 + t) + 0.5 * x * (1.0 - t * t) * _GELU_K * (1.0 + 3.0 * _GELU_C * x * x)


def mm(a, b, *, name, out_dtype=F32, res=None, scale=1.0, tm=512):
    M, K = a.shape
    K2, N = b.shape
    assert K == K2
    tm = _pick(M, (tm, 256, 128))
    tn = _pick(N, (512, 384, 256, 128))
    tk = K if K <= 3072 else _pick(K, (2560, 2048, 1536, 1024, 512, 256, 128))
    nk = K // tk
    has_res = res is not None

    def body(*refs):
        if has_res:
            a_ref, b_ref, r_ref, o_ref = refs[:4]
        else:
            a_ref, b_ref, o_ref = refs[:3]
            r_ref = None

        def finish(acc):
            if scale != 1.0:
                acc = acc * scale
            if has_res:
                acc = r_ref[...] + acc
            o_ref[...] = acc.astype(out_dtype)

        part = jnp.dot(a_ref[...].astype(BF16), b_ref[...].astype(BF16), preferred_element_type=F32)
        if nk == 1:
            finish(part)
        else:
            acc_ref = refs[-1]
            k = pl.program_id(2)

            @pl.when(k == 0)
            def _():
                acc_ref[...] = part

            @pl.when(k > 0)
            def _():
                acc_ref[...] += part

            @pl.when(k == nk - 1)
            def _():
                finish(acc_ref[...])

    in_specs = [pl.BlockSpec((tm, tk), lambda i, j, k: (i, k)), pl.BlockSpec((tk, tn), lambda i, j, k: (k, j))]
    args = [a, b]
    if has_res:
        in_specs.append(pl.BlockSpec((tm, tn), lambda i, j, k: (i, j)))
        args.append(res)
    return pl.pallas_call(
        body, name=name, grid=(M // tm, N // tn, nk), in_specs=in_specs,
        out_specs=pl.BlockSpec((tm, tn), lambda i, j, k: (i, j)),
        out_shape=jax.ShapeDtypeStruct((M, N), out_dtype),
        scratch_shapes=[pltpu.VMEM((tm, tn), F32)] if nk > 1 else [],
        compiler_params=_params("parallel", "parallel", "arbitrary"),
    )(*args)


def mm_tn(a, b, *, name, scale=1.0):
    M, K = a.shape
    M2, N = b.shape
    assert M == M2
    tm = _pick(M, (512, 256, 128))
    tk = _pick(K, (1024, 512, 256, 128))
    tn = _pick(N, (512, 384, 256, 128))
    nm = M // tm

    def body(a_ref, b_ref, o_ref):
        m = pl.program_id(2)
        part = lax.dot_general(a_ref[...].astype(BF16), b_ref[...].astype(BF16), (((0,), (0,)), ((), ())),
                               preferred_element_type=F32)

        @pl.when(m == 0)
        def _():
            o_ref[...] = part

        @pl.when(m > 0)
        def _():
            o_ref[...] += part

        if scale != 1.0:
            @pl.when(m == nm - 1)
            def _():
                o_ref[...] = o_ref[...] * scale

    return pl.pallas_call(
        body, name=name, grid=(K // tk, N // tn, nm),
        in_specs=[pl.BlockSpec((tm, tk), lambda i, j, m: (m, i)), pl.BlockSpec((tm, tn), lambda i, j, m: (m, j))],
        out_specs=pl.BlockSpec((tk, tn), lambda i, j, m: (i, j)),
        out_shape=jax.ShapeDtypeStruct((K, N), F32),
        compiler_params=_params("parallel", "parallel", "arbitrary"),
    )(a, b)


def rms_fwd(x, g, *, name):
    S, D = x.shape
    ts = _pick(S, (512, 256, 128))

    def body(x_ref, g_ref, h_ref):
        xv = x_ref[...]
        r = lax.rsqrt(jnp.mean(xv * xv, axis=-1, keepdims=True) + RMS_EPS)
        h_ref[...] = ((xv * r) * g_ref[...]).astype(BF16)

    return pl.pallas_call(
        body, name=name, grid=(S // ts,),
        in_specs=[pl.BlockSpec((ts, D), lambda i: (i, 0)), pl.BlockSpec((1, D), lambda i: (0, 0))],
        out_specs=pl.BlockSpec((ts, D), lambda i: (i, 0)),
        out_shape=jax.ShapeDtypeStruct((S, D), BF16),
        compiler_params=_params("parallel"),
    )(x, g)


def rms_bwd(x, g, dh, dx_in, *, name):
    S, D = x.shape
    ts = _pick(S, (512, 256, 128))
    has_in = dx_in is not None

    def body(*refs):
        if has_in:
            x_ref, g_ref, dh_ref, di_ref, dx_ref, dg_ref = refs
        else:
            x_ref, g_ref, dh_ref, dx_ref, dg_ref = refs
        xv = x_ref[...]
        dh_v = dh_ref[...]
        r = lax.rsqrt(jnp.mean(xv * xv, axis=-1, keepdims=True) + RMS_EPS)
        xh = xv * r
        gd = dh_v * g_ref[...]
        dx = r * (gd - xh * jnp.mean(gd * xh, axis=-1, keepdims=True))
        if has_in:
            dx = di_ref[...] + dx
        dx_ref[...] = dx
        part = jnp.sum(dh_v * xh, axis=0, keepdims=True)

        @pl.when(pl.program_id(0) == 0)
        def _():
            dg_ref[...] = part

        @pl.when(pl.program_id(0) > 0)
        def _():
            dg_ref[...] += part

    row = pl.BlockSpec((ts, D), lambda i: (i, 0))
    vec = pl.BlockSpec((1, D), lambda i: (0, 0))
    return pl.pallas_call(
        body, name=name, grid=(S // ts,),
        in_specs=[row, vec, row] + ([row] if has_in else []),
        out_specs=[row, vec],
        out_shape=[jax.ShapeDtypeStruct((S, D), F32), jax.ShapeDtypeStruct((1, D), F32)],
        compiler_params=_params("arbitrary"),
    )(*([x, g, dh] + ([dx_in] if has_in else [])))


def final_loss_bwd(x, g, target, *, name):
    S, D = x.shape
    ts = _pick(S, (512, 256, 128))

    def body(x_ref, g_ref, t_ref, dx_ref, dg_ref, ls_ref):
        xv = x_ref[...]
        gv = g_ref[...]
        r = lax.rsqrt(jnp.mean(xv * xv, axis=-1, keepdims=True) + RMS_EPS)
        xh = xv * r
        e = xh * gv - t_ref[...]
        dy = e * (1.0 / D)
        gd = dy * gv
        dx_ref[...] = r * (gd - xh * jnp.mean(gd * xh, axis=-1, keepdims=True))
        dg_part = jnp.sum(dy * xh, axis=0, keepdims=True)
        ls_part = jnp.sum(e * e, axis=0, keepdims=True)

        @pl.when(pl.program_id(0) == 0)
        def _():
            dg_ref[...] = dg_part
            ls_ref[...] = ls_part

        @pl.when(pl.program_id(0) > 0)
        def _():
            dg_ref[...] += dg_part
            ls_ref[...] += ls_part

    row = pl.BlockSpec((ts, D), lambda i: (i, 0))
    vec = pl.BlockSpec((1, D), lambda i: (0, 0))
    return pl.pallas_call(
        body, name=name, grid=(S // ts,), in_specs=[row, vec, row], out_specs=[row, vec, vec],
        out_shape=[jax.ShapeDtypeStruct((S, D), F32), jax.ShapeDtypeStruct((1, D), F32),
                   jax.ShapeDtypeStruct((1, D), F32)],
        compiler_params=_params("arbitrary"),
    )(x, g, target)


def swiglu_fwd(gp, up, *, name):
    S, F = gp.shape
    ts = _pick(S, (256, 128))

    def body(g_ref, u_ref, a_ref):
        gv = g_ref[...]
        a_ref[...] = (gv * _sig(gv) * u_ref[...]).astype(BF16)

    row = pl.BlockSpec((ts, F), lambda i: (i, 0))
    return pl.pallas_call(
        body, name=name, grid=(S // ts,), in_specs=[row, row], out_specs=row,
        out_shape=jax.ShapeDtypeStruct((S, F), BF16), compiler_params=_params("parallel"),
    )(gp, up)


def swiglu_bwd(gp, up, da, *, name):
    S, F = gp.shape
    ts = _pick(S, (256, 128))

    def body(g_ref, u_ref, da_ref, dg_ref, du_ref):
        gv = g_ref[...]
        dav = da_ref[...]
        s = _sig(gv)
        dg_ref[...] = (dav * u_ref[...] * (s * (1.0 + gv * (1.0 - s)))).astype(BF16)
        du_ref[...] = (dav * (gv * s)).astype(BF16)

    row = pl.BlockSpec((ts, F), lambda i: (i, 0))
    return pl.pallas_call(
        body, name=name, grid=(S // ts,), in_specs=[row, row, row], out_specs=[row, row],
        out_shape=[jax.ShapeDtypeStruct((S, F), BF16)] * 2, compiler_params=_params("parallel"),
    )(gp, up, da)


def ffn_fwd(x, w, tag):
    h = rms_fwd(x, w["norm"], name=f"{tag}_rms")
    gp = mm(h, w["w_gate"], name=f"{tag}_gate")
    up = mm(h, w["w_up"], name=f"{tag}_up")
    a = swiglu_fwd(gp, up, name=f"{tag}_act")
    y = mm(a, w["w_down"], name=f"{tag}_down", res=x, scale=0.5)
    return y, (x, h, gp, up, a)


def ffn_bwd(dx, saved, w, tag):
    x, h, gp, up, a = saved
    grads = {"w_down": mm_tn(a, dx, name=f"{tag}_dwd", scale=0.5)}
    da = mm(dx, w["w_down_t"], name=f"{tag}_da", scale=0.5)
    dgp, dup = swiglu_bwd(gp, up, da, name=f"{tag}_dact")
    grads["w_gate"] = mm_tn(h, dgp, name=f"{tag}_dwg")
    grads["w_up"] = mm_tn(h, dup, name=f"{tag}_dwu")
    dh = mm(dgp, w["w_gate_t"], name=f"{tag}_dh1")
    dh = mm(dup, w["w_up_t"], name=f"{tag}_dh2", res=dh)
    dx, grads["norm"] = rms_bwd(x, w["norm"], dh, dx, name=f"{tag}_drms")
    return dx, grads


SEG_AB, SEG_AC, SEG_AH, SEG_U, SEG_V, SEG_FQ, SEG_FK, SEG_FV, SEG_FL = range(9)
N_SEG = 9


def _seg_block(D, seg):
    return 3 * D // SEG + seg


def _shift_down(z, prev8, n, rows):
    out = pltpu.roll(z, n, 0)
    for r in range(n):
        out = jnp.where(rows == r, prev8[8 - n + r:8 - n + r + 1, :], out)
    return out


def _shift_up(z, next8, n, rows, ts):
    out = pltpu.roll(z, ts - n, 0)
    for r in range(n):
        out = jnp.where(rows == ts - n + r, next8[r:r + 1, :], out)
    return out


def conv_fwd(proj, conv_w, D, *, name):
    S = proj.shape[0]
    ts = _pick(S, (512, 256, 128))
    b0 = _seg_block(D, 0)

    def body(ab_ref, ac_ref, ah_ref, pc_ref, ph_ref, w_ref, y_ref):
        i = pl.program_id(0)
        rows = lax.broadcasted_iota(jnp.int32, (ts, 1), 0)
        z = ac_ref[...] * ah_ref[...]
        zp = pc_ref[...] * ph_ref[...] * (i > 0).astype(F32)
        w = w_ref[...]
        y = w[0:1, :] * _shift_down(z, zp, 2, rows) + w[1:2, :] * _shift_down(z, zp, 1, rows) + w[2:3, :] * z
        y_ref[...] = (ab_ref[...] * y).astype(BF16)

    def seg(s):
        return pl.BlockSpec((ts, SEG), lambda i: (i, b0 + s))

    def prev(s):
        return pl.BlockSpec((8, SEG), lambda i: (jnp.maximum(i * (ts // 8) - 1, 0), b0 + s))

    return pl.pallas_call(
        body, name=name, grid=(S // ts,),
        in_specs=[seg(SEG_AB), seg(SEG_AC), seg(SEG_AH), prev(SEG_AC), prev(SEG_AH),
                  pl.BlockSpec((3, SEG), lambda i: (0, 0))],
        out_specs=pl.BlockSpec((ts, SEG), lambda i: (i, 0)),
        out_shape=jax.ShapeDtypeStruct((S, SEG), BF16), compiler_params=_params("parallel"),
    )(proj, proj, proj, proj, proj, conv_w)


def conv_bwd(proj, conv_w, dy, D, *, name):
    S = proj.shape[0]
    ts = _pick(S, (512, 256, 128))
    nt = S // ts
    b0 = _seg_block(D, 0)

    def body(ab_ref, ac_ref, ah_ref, pc_ref, ph_ref, nb_ref, dy_ref, ndy_ref, w_ref,
             dab_ref, dac_ref, dah_ref, dw_ref):
        i = pl.program_id(0)
        rows = lax.broadcasted_iota(jnp.int32, (ts, 1), 0)
        ab, ac, ah = ab_ref[...], ac_ref[...], ah_ref[...]
        z = ac * ah
        zp = pc_ref[...] * ph_ref[...] * (i > 0).astype(F32)
        w = w_ref[...]
        z1 = _shift_down(z, zp, 1, rows)
        z2 = _shift_down(z, zp, 2, rows)
        y = w[0:1, :] * z2 + w[1:2, :] * z1 + w[2:3, :] * z
        dyv = dy_ref[...]
        dab_ref[...] = (dyv * y).astype(BF16)
        dyy = dyv * ab
        nyy = ndy_ref[...] * nb_ref[...] * (i < nt - 1).astype(F32)
        dz = (w[2:3, :] * dyy + w[1:2, :] * _shift_up(dyy, nyy, 1, rows, ts)
              + w[0:1, :] * _shift_up(dyy, nyy, 2, rows, ts))
        dac_ref[...] = (dz * ah).astype(BF16)
        dah_ref[...] = (dz * ac).astype(BF16)
        parts = [jnp.sum(dyy * zz, axis=0, keepdims=True) for zz in (z2, z1, z)]

        @pl.when(i == 0)
        def _():
            for k in range(3):
                dw_ref[k:k + 1, :] = parts[k]

        @pl.when(i > 0)
        def _():
            for k in range(3):
                dw_ref[k:k + 1, :] += parts[k]

    def seg(s):
        return pl.BlockSpec((ts, SEG), lambda i: (i, b0 + s))

    def prev(s):
        return pl.BlockSpec((8, SEG), lambda i: (jnp.maximum(i * (ts // 8) - 1, 0), b0 + s))

    nxt_row = lambda i: jnp.minimum((i + 1) * (ts // 8), S // 8 - 1)
    out_row = pl.BlockSpec((ts, SEG), lambda i: (i, 0))
    return pl.pallas_call(
        body, name=name, grid=(nt,),
        in_specs=[seg(SEG_AB), seg(SEG_AC), seg(SEG_AH), prev(SEG_AC), prev(SEG_AH),
                  pl.BlockSpec((8, SEG), lambda i: (nxt_row(i), b0 + SEG_AB)),
                  out_row, pl.BlockSpec((8, SEG), lambda i: (nxt_row(i), 0)),
                  pl.BlockSpec((3, SEG), lambda i: (0, 0))],
        out_specs=[out_row, out_row, out_row, pl.BlockSpec((3, SEG), lambda i: (0, 0))],
        out_shape=[jax.ShapeDtypeStruct((S, SEG), BF16)] * 3 + [jax.ShapeDtypeStruct((3, SEG), F32)],
        compiler_params=_params("arbitrary"),
    )(proj, proj, proj, proj, proj, proj, dy, dy, conv_w)


def _tril_mask():
    r = lax.broadcasted_iota(jnp.int32, (CHUNK, CHUNK), 0)
    c = lax.broadcasted_iota(jnp.int32, (CHUNK, CHUNK), 1)
    return c <= r


def sg_fwd(proj, sg_norm, sg_w, sg_bt, D, *, name):
    S = proj.shape[0]
    ts = _pick(S, (512, 256, 128))
    b0 = _seg_block(D, 0)

    def body(u_ref, v_ref, gs_ref, w_ref, b_ref, y_ref):
        ug, _ = _gelu(u_ref[...])
        vg, _ = _gelu(v_ref[...])
        vn = ((vg * lax.rsqrt(jnp.mean(vg * vg, axis=-1, keepdims=True) + RMS_EPS)) * gs_ref[...]).astype(BF16)
        mask = _tril_mask()
        for g in range(SG_GROUPS):
            wg = jnp.where(mask, w_ref[g], 0.0).astype(BF16)
            cols = slice(g * CHUNK, (g + 1) * CHUNK)
            for n in range(ts // CHUNK):
                rws = slice(n * CHUNK, (n + 1) * CHUNK)
                sv = jnp.dot(wg, vn[rws, cols], preferred_element_type=F32) + b_ref[g]
                y_ref[rws, cols] = (ug[rws, cols] * sv).astype(BF16)

    seg = lambda s: pl.BlockSpec((ts, SEG), lambda i: (i, b0 + s))
    return pl.pallas_call(
        body, name=name, grid=(S // ts,),
        in_specs=[seg(SEG_U), seg(SEG_V), pl.BlockSpec((1, SEG), lambda i: (0, 0)),
                  pl.BlockSpec((SG_GROUPS, CHUNK, CHUNK), lambda i: (0, 0, 0)),
                  pl.BlockSpec((SG_GROUPS, CHUNK, 1), lambda i: (0, 0, 0))],
        out_specs=pl.BlockSpec((ts, SEG), lambda i: (i, 0)),
        out_shape=jax.ShapeDtypeStruct((S, SEG), BF16), compiler_params=_params("parallel"),
    )(proj, proj, sg_norm, sg_w, sg_bt)


def sg_bwd(proj, sg_norm, sg_w, sg_bt, dy, D, *, name):
    S = proj.shape[0]
    ts = _pick(S, (512, 256, 128))
    nt = S // ts
    b0 = _seg_block(D, 0)

    def body(u_ref, v_ref, dy_ref, gs_ref, w_ref, b_ref, du_ref, dv_ref, dw_ref, db_ref, dgs_ref, dvn_sc):
        i = pl.program_id(0)
        uv, vv, dyv = u_ref[...], v_ref[...], dy_ref[...]
        ug, ut = _gelu(uv)
        vg, vt = _gelu(vv)
        r = lax.rsqrt(jnp.mean(vg * vg, axis=-1, keepdims=True) + RMS_EPS)
        vh = vg * r
        gs = gs_ref[...]
        vn = (vh * gs).astype(BF16)
        dsv = dyv * ug
        dsv_b = dsv.astype(BF16)
        mask = _tril_mask()

        @pl.when(i == 0)
        def _():
            dw_ref[...] = jnp.zeros_like(dw_ref)
            db_ref[...] = jnp.zeros_like(db_ref)

        for g in range(SG_GROUPS):
            wg = jnp.where(mask, w_ref[g], 0.0).astype(BF16)
            cols = slice(g * CHUNK, (g + 1) * CHUNK)
            dw_acc = jnp.zeros((CHUNK, CHUNK), F32)
            db_acc = jnp.zeros((CHUNK, 1), F32)
            for n in range(ts // CHUNK):
                rws = slice(n * CHUNK, (n + 1) * CHUNK)
                vblk = vn[rws, cols]
                sv = jnp.dot(wg, vblk, preferred_element_type=F32) + b_ref[g]
                du_ref[rws, cols] = (dyv[rws, cols] * sv * _gelu_grad(uv[rws, cols], ut[rws, cols])).astype(BF16)
                dblk = dsv_b[rws, cols]
                dvn_sc[rws, cols] = lax.dot_general(wg, dblk, (((0,), (0,)), ((), ())), preferred_element_type=F32)
                dw_acc = dw_acc + lax.dot_general(dblk, vblk, (((1,), (1,)), ((), ())), preferred_element_type=F32)
                db_acc = db_acc + jnp.sum(dsv[rws, cols], axis=1, keepdims=True)
            dw_ref[g] += jnp.where(mask, dw_acc, 0.0)
            db_ref[g] += db_acc

        dvn = dvn_sc[...]
        gd = dvn * gs
        dvg = r * (gd - vh * jnp.mean(gd * vh, axis=-1, keepdims=True))
        dv_ref[...] = (dvg * _gelu_grad(vv, vt)).astype(BF16)
        dgs_part = jnp.sum(dvn * vh, axis=0, keepdims=True)

        @pl.when(i == 0)
        def _():
            dgs_ref[...] = dgs_part

        @pl.when(i > 0)
        def _():
            dgs_ref[...] += dgs_part

    seg = lambda s: pl.BlockSpec((ts, SEG), lambda i: (i, b0 + s))
    row = pl.BlockSpec((ts, SEG), lambda i: (i, 0))
    wspec = pl.BlockSpec((SG_GROUPS, CHUNK, CHUNK), lambda i: (0, 0, 0))
    bspec = pl.BlockSpec((SG_GROUPS, CHUNK, 1), lambda i: (0, 0, 0))
    vec = pl.BlockSpec((1, SEG), lambda i: (0, 0))
    return pl.pallas_call(
        body, name=name, grid=(nt,),
        in_specs=[seg(SEG_U), seg(SEG_V), row, vec, wspec, bspec],
        out_specs=[row, row, wspec, bspec, vec],
        out_shape=[jax.ShapeDtypeStruct((S, SEG), BF16)] * 2
        + [jax.ShapeDtypeStruct((SG_GROUPS, CHUNK, CHUNK), F32), jax.ShapeDtypeStruct((SG_GROUPS, CHUNK, 1), F32),
           jax.ShapeDtypeStruct((1, SEG), F32)],
        scratch_shapes=[pltpu.VMEM((ts, SEG), F32)],
        compiler_params=_params("arbitrary"),
    )(proj, proj, dy, sg_norm, sg_w, sg_bt)


def _log_sigmoid(x):
    return jnp.minimum(x, 0.0) - jnp.log(1.0 + jnp.exp(-jnp.abs(x)))


def fox_cumlog(proj, b_f, D, *, name):
    S = proj.shape[0]
    ts = _pick(S, (512, 256, 128))
    blk = (3 * D + SEG_FL * SEG) // LANES

    def body(f_ref, b_ref, c_ref, carry):
        i = pl.program_id(0)

        @pl.when(i == 0)
        def _():
            carry[...] = jnp.zeros_like(carry)

        rows = lax.broadcasted_iota(jnp.int32, (ts, 1), 0)
        acc = _log_sigmoid(f_ref[...] + b_ref[...])
        d = 1
        while d < ts:
            acc = acc + jnp.where(rows >= d, pltpu.roll(acc, d, 0), 0.0)
            d *= 2
        acc = acc + carry[...]
        c_ref[...] = acc
        carry[...] = acc[ts - 1:ts, :]

    return pl.pallas_call(
        body, name=name, grid=(S // ts,),
        in_specs=[pl.BlockSpec((ts, LANES), lambda i: (i, blk)), pl.BlockSpec((1, LANES), lambda i: (0, 0))],
        out_specs=pl.BlockSpec((ts, LANES), lambda i: (i, 0)),
        out_shape=jax.ShapeDtypeStruct((S, LANES), F32),
        scratch_shapes=[pltpu.VMEM((1, LANES), F32)],
        compiler_params=_params("arbitrary"),
    )(proj, b_f)


def fox_dlogit(proj, b_f, dc, D, *, name):
    S = proj.shape[0]
    ts = _pick(S, (512, 256, 128))
    nt = S // ts
    blk = (3 * D + SEG_FL * SEG) // LANES

    def body(f_ref, b_ref, dc_ref, df_ref, db_ref, carry):
        i = pl.program_id(0)

        @pl.when(i == 0)
        def _():
            carry[...] = jnp.zeros_like(carry)

        rows = lax.broadcasted_iota(jnp.int32, (ts, 1), 0)
        acc = dc_ref[...]
        d = 1
        while d < ts:
            acc = acc + jnp.where(rows < ts - d, pltpu.roll(acc, ts - d, 0), 0.0)
            d *= 2
        acc = acc + carry[...]
        carry[...] = acc[0:1, :]
        df = acc * _sig(-(f_ref[...] + b_ref[...]))
        df_ref[...] = jnp.zeros_like(df_ref)
        df_ref[:, 0:LANES] = df.astype(BF16)
        part = jnp.sum(df, axis=0, keepdims=True)

        @pl.when(i == 0)
        def _():
            db_ref[...] = part

        @pl.when(i > 0)
        def _():
            db_ref[...] += part

    rev = lambda i: nt - 1 - i
    return pl.pallas_call(
        body, name=name, grid=(nt,),
        in_specs=[pl.BlockSpec((ts, LANES), lambda i: (rev(i), blk)), pl.BlockSpec((1, LANES), lambda i: (0, 0)),
                  pl.BlockSpec((ts, LANES), lambda i: (rev(i), 0))],
        out_specs=[pl.BlockSpec((ts, SEG), lambda i: (rev(i), 0)), pl.BlockSpec((1, LANES), lambda i: (0, 0))],
        out_shape=[jax.ShapeDtypeStruct((S, SEG), BF16), jax.ShapeDtypeStruct((1, LANES), F32)],
        scratch_shapes=[pltpu.VMEM((1, LANES), F32)],
        compiler_params=_params("arbitrary"),
    )(proj, b_f, dc)


def _fox_tile(S):
    return min(512, max(128, S // 4))


def _causal_mask(t):
    r = lax.broadcasted_iota(jnp.int32, (t, t), 0)
    c = lax.broadcasted_iota(jnp.int32, (t, t), 1)
    return c <= r


def fox_fwd(q, k, v, c_col, c_row, *, name):
    H, S, Dh = q.shape
    t = _fox_tile(S)
    n = S // t

    def body(q_ref, k_ref, v_ref, cq_ref, ck_ref, o_ref, lse_ref, m_sc, l_sc, acc_sc):
        i, j = pl.program_id(1), pl.program_id(2)

        @pl.when(j == 0)
        def _():
            m_sc[...] = jnp.full_like(m_sc, NEG_BIG)
            l_sc[...] = jnp.zeros_like(l_sc)
            acc_sc[...] = jnp.zeros_like(acc_sc)

        def step(masked):
            s = lax.dot_general(q_ref[0], k_ref[0], (((1,), (1,)), ((), ())), preferred_element_type=F32)
            s = s + cq_ref[0] - ck_ref[0]
            if masked:
                s = jnp.where(_causal_mask(t), s, NEG_BIG)
            m_prev = m_sc[...]
            m_new = jnp.maximum(m_prev, jnp.max(s, axis=-1, keepdims=True))
            alpha = jnp.exp(m_prev - m_new)
            p = jnp.exp(s - m_new)
            l_sc[...] = alpha * l_sc[...] + jnp.sum(p, axis=-1, keepdims=True)
            acc_sc[...] = alpha * acc_sc[...] + jnp.dot(p.astype(BF16), v_ref[0], preferred_element_type=F32)
            m_sc[...] = m_new

        @pl.when(j < i)
        def _():
            step(False)

        @pl.when(j == i)
        def _():
            step(True)
            o_ref[0] = acc_sc[...] / l_sc[...]
            lse_ref[0] = m_sc[...] + jnp.log(l_sc[...])

    qs = pl.BlockSpec((1, t, Dh), lambda h, i, j: (h, i, 0))
    ks = pl.BlockSpec((1, t, Dh), lambda h, i, j: (h, jnp.minimum(j, i), 0))
    col = pl.BlockSpec((1, t, 1), lambda h, i, j: (h, i, 0))
    return pl.pallas_call(
        body, name=name, grid=(H, n, n),
        in_specs=[qs, ks, ks, col, pl.BlockSpec((1, 1, t), lambda h, i, j: (h, 0, jnp.minimum(j, i)))],
        out_specs=[qs, col],
        out_shape=[jax.ShapeDtypeStruct((H, S, Dh), F32), jax.ShapeDtypeStruct((H, S, 1), F32)],
        scratch_shapes=[pltpu.VMEM((t, 1), F32), pltpu.VMEM((t, 1), F32), pltpu.VMEM((t, Dh), F32)],
        compiler_params=_params("parallel", "parallel", "arbitrary"),
    )(q, k, v, c_col, c_row)


def fox_delta(do, o, *, name):
    H, S, Dh = o.shape
    t = _pick(S, (2048, 1024, 512, 256, 128))

    def body(do_ref, o_ref, d_ref):
        d_ref[0] = jnp.sum(do_ref[0] * o_ref[0], axis=-1, keepdims=True)

    blk = pl.BlockSpec((1, t, Dh), lambda h, i: (h, i, 0))
    return pl.pallas_call(
        body, name=name, grid=(H, S // t), in_specs=[blk, blk],
        out_specs=pl.BlockSpec((1, t, 1), lambda h, i: (h, i, 0)),
        out_shape=jax.ShapeDtypeStruct((H, S, 1), F32), compiler_params=_params("parallel", "parallel"),
    )(do, o)


def _fox_probs(q_ref, k_ref, cq_ref, ck_ref, lse_ref, masked, t):
    s = lax.dot_general(q_ref[0], k_ref[0], (((1,), (1,)), ((), ())), preferred_element_type=F32)
    s = s + cq_ref[0] - ck_ref[0]
    p = jnp.exp(s - lse_ref[0])
    if masked:
        p = jnp.where(_causal_mask(t), p, 0.0)
    return p


def fox_bwd_dq(q, k, v, do, c_col, c_row, lse, delta, scale, *, name):
    H, S, Dh = q.shape
    t = _fox_tile(S)
    n = S // t

    def body(q_ref, k_ref, v_ref, do_ref, cq_ref, ck_ref, lse_ref, dl_ref, dq_ref, dr_ref, acc_sc, row_sc):
        i, j = pl.program_id(1), pl.program_id(2)

        @pl.when(j == 0)
        def _():
            acc_sc[...] = jnp.zeros_like(acc_sc)
            row_sc[...] = jnp.zeros_like(row_sc)

        def step(masked):
            p = _fox_probs(q_ref, k_ref, cq_ref, ck_ref, lse_ref, masked, t)
            dp = lax.dot_general(do_ref[0], v_ref[0], (((1,), (1,)), ((), ())), preferred_element_type=F32)
            ds = p * (dp - dl_ref[0])
            acc_sc[...] += jnp.dot(ds.astype(BF16), k_ref[0], preferred_element_type=F32)
            row_sc[...] += jnp.sum(ds, axis=-1, keepdims=True)

        @pl.when(j < i)
        def _():
            step(False)

        @pl.when(j == i)
        def _():
            step(True)
            dq_ref[0] = acc_sc[...] * scale
            dr_ref[0] = row_sc[...]

    qs = pl.BlockSpec((1, t, Dh), lambda h, i, j: (h, i, 0))
    ks = pl.BlockSpec((1, t, Dh), lambda h, i, j: (h, jnp.minimum(j, i), 0))
    col = pl.BlockSpec((1, t, 1), lambda h, i, j: (h, i, 0))
    return pl.pallas_call(
        body, name=name, grid=(H, n, n),
        in_specs=[qs, ks, ks, qs, col, pl.BlockSpec((1, 1, t), lambda h, i, j: (h, 0, jnp.minimum(j, i))), col, col],
        out_specs=[qs, col],
        out_shape=[jax.ShapeDtypeStruct((H, S, Dh), F32), jax.ShapeDtypeStruct((H, S, 1), F32)],
        scratch_shapes=[pltpu.VMEM((t, Dh), F32), pltpu.VMEM((t, 1), F32)],
        compiler_params=_params("parallel", "parallel", "arbitrary"),
    )(q, k, v, do, c_col, c_row, lse, delta)


def fox_bwd_dkv(q, k, v, do, c_col, c_row, lse, delta, *, name):
    H, S, Dh = q.shape
    t = _fox_tile(S)
    n = S // t

    def body(q_ref, k_ref, v_ref, do_ref, cq_ref, ck_ref, lse_ref, dl_ref, dk_ref, dv_ref, dc_ref,
             dk_sc, dv_sc, dc_sc):
        j, i = pl.program_id(1), pl.program_id(2)

        @pl.when(i == 0)
        def _():
            dk_sc[...] = jnp.zeros_like(dk_sc)
            dv_sc[...] = jnp.zeros_like(dv_sc)
            dc_sc[...] = jnp.zeros_like(dc_sc)

        def step(masked):
            p = _fox_probs(q_ref, k_ref, cq_ref, ck_ref, lse_ref, masked, t)
            dob = do_ref[0]
            dp = lax.dot_general(dob, v_ref[0], (((1,), (1,)), ((), ())), preferred_element_type=F32)
            ds = p * (dp - dl_ref[0])
            dv_sc[...] += lax.dot_general(p.astype(BF16), dob, (((0,), (0,)), ((), ())), preferred_element_type=F32)
            dk_sc[...] += lax.dot_general(ds.astype(BF16), q_ref[0], (((0,), (0,)), ((), ())),
                                          preferred_element_type=F32)
            dc_sc[...] -= jnp.sum(ds, axis=0, keepdims=True)

        @pl.when(i > j)
        def _():
            step(False)

        @pl.when(i == j)
        def _():
            step(True)

        @pl.when(i == n - 1)
        def _():
            dk_ref[0] = dk_sc[...]
            dv_ref[0] = dv_sc[...]
            dc_ref[0] = dc_sc[...]

    qs = pl.BlockSpec((1, t, Dh), lambda h, j, i: (h, jnp.maximum(i, j), 0))
    ks = pl.BlockSpec((1, t, Dh), lambda h, j, i: (h, j, 0))
    qcol = pl.BlockSpec((1, t, 1), lambda h, j, i: (h, jnp.maximum(i, j), 0))
    krow = pl.BlockSpec((1, 1, t), lambda h, j, i: (h, 0, j))
    return pl.pallas_call(
        body, name=name, grid=(H, n, n),
        in_specs=[qs, ks, ks, qs, qcol, krow, qcol, qcol],
        out_specs=[ks, ks, krow],
        out_shape=[jax.ShapeDtypeStruct((H, S, Dh), F32)] * 2 + [jax.ShapeDtypeStruct((H, 1, S), F32)],
        scratch_shapes=[pltpu.VMEM((t, Dh), F32), pltpu.VMEM((t, Dh), F32), pltpu.VMEM((1, t), F32)],
        compiler_params=_params("parallel", "parallel", "arbitrary"),
    )(q, k, v, do, c_col, c_row, lse, delta)


def merge_fwd(proj, branches, D, *, name):
    S = proj.shape[0]
    ts = _pick(S, (256, 128))

    def body(g0, g1, g2, b0, b1, b2, o_ref):
        acc = _sig(g0[...]) * b0[...] + _sig(g1[...]) * b1[...] + _sig(g2[...]) * b2[...]
        o_ref[...] = acc.astype(BF16)

    gate = lambda n: pl.BlockSpec((ts, D), lambda i: (i, n))
    row = pl.BlockSpec((ts, D), lambda i: (i, 0))
    return pl.pallas_call(
        body, name=name, grid=(S // ts,), in_specs=[gate(0), gate(1), gate(2), row, row, row], out_specs=row,
        out_shape=jax.ShapeDtypeStruct((S, D), BF16), compiler_params=_params("parallel"),
    )(proj, proj, proj, *branches)


def merge_bwd(proj, branches, dm, D, *, name):
    S = proj.shape[0]
    ts = _pick(S, (256, 128))

    def body(g0, g1, g2, b0, b1, b2, dm_ref, db0, db1, db2, dg0, dg1, dg2):
        dmv = dm_ref[...]
        for g_ref, b_ref, db_ref, dg_ref in ((g0, b0, db0, dg0), (g1, b1, db1, dg1), (g2, b2, db2, dg2)):
            s = _sig(g_ref[...])
            db_ref[...] = (dmv * s).astype(BF16)
            dg_ref[...] = (dmv * b_ref[...] * (s * (1.0 - s))).astype(BF16)

    gate = lambda n: pl.BlockSpec((ts, D), lambda i: (i, n))
    row = pl.BlockSpec((ts, D), lambda i: (i, 0))
    return pl.pallas_call(
        body, name=name, grid=(S // ts,), in_specs=[gate(0), gate(1), gate(2), row, row, row, row],
        out_specs=[row] * 6, out_shape=[jax.ShapeDtypeStruct((S, D), BF16)] * 6,
        compiler_params=_params("parallel"),
    )(proj, proj, proj, *branches, dm)


def xa_fwd(q, k, v, *, name):
    S, D = q.shape
    M = k.shape[0]
    dh = D // XA_HEADS
    scale = dh ** -0.5
    t = _pick(S, (512, 256, 128))

    def body(q_ref, k_ref, v_ref, o_ref):
        for h in range(XA_HEADS):
            cols = slice(h * dh, (h + 1) * dh)
            s = lax.dot_general(q_ref[:, cols], k_ref[:, cols], (((1,), (1,)), ((), ())),
                                preferred_element_type=F32) * scale
            p = jnp.exp(s - jnp.max(s, axis=-1, keepdims=True))
            p = p / jnp.sum(p, axis=-1, keepdims=True)
            o_ref[:, cols] = jnp.dot(p.astype(BF16), v_ref[:, cols], preferred_element_type=F32).astype(BF16)

    row = pl.BlockSpec((t, D), lambda i: (i, 0))
    full = pl.BlockSpec((M, D), lambda i: (0, 0))
    return pl.pallas_call(
        body, name=name, grid=(S // t,), in_specs=[row, full, full], out_specs=row,
        out_shape=jax.ShapeDtypeStruct((S, D), BF16), compiler_params=_params("parallel"),
    )(q, k, v)


def xa_bwd(q, k, v, do, *, name):
    S, D = q.shape
    M = k.shape[0]
    dh = D // XA_HEADS
    scale = dh ** -0.5
    t = _pick(S, (512, 256, 128))

    def body(q_ref, k_ref, v_ref, do_ref, dq_ref, dk_ref, dv_ref):
        i = pl.program_id(0)

        @pl.when(i == 0)
        def _():
            dk_ref[...] = jnp.zeros_like(dk_ref)
            dv_ref[...] = jnp.zeros_like(dv_ref)

        for h in range(XA_HEADS):
            cols = slice(h * dh, (h + 1) * dh)
            qh, kh, vh = q_ref[:, cols], k_ref[:, cols], v_ref[:, cols]
            dob = do_ref[:, cols].astype(BF16)
            s = lax.dot_general(qh, kh, (((1,), (1,)), ((), ())), preferred_element_type=F32) * scale
            p = jnp.exp(s - jnp.max(s, axis=-1, keepdims=True))
            p = p / jnp.sum(p, axis=-1, keepdims=True)
            dp = lax.dot_general(dob, vh, (((1,), (1,)), ((), ())), preferred_element_type=F32)
            ds = (p * (dp - jnp.sum(p * dp, axis=-1, keepdims=True)) * scale).astype(BF16)
            dq_ref[:, cols] = jnp.dot(ds, kh, preferred_element_type=F32).astype(BF16)
            dk_ref[:, cols] += lax.dot_general(ds, qh, (((0,), (0,)), ((), ())), preferred_element_type=F32)
            dv_ref[:, cols] += lax.dot_general(p.astype(BF16), dob, (((0,), (0,)), ((), ())),
                                               preferred_element_type=F32)

    row = pl.BlockSpec((t, D), lambda i: (i, 0))
    full = pl.BlockSpec((M, D), lambda i: (0, 0))
    return pl.pallas_call(
        body, name=name, grid=(S // t,), in_specs=[row, full, full, row], out_specs=[row, full, full],
        out_shape=[jax.ShapeDtypeStruct((S, D), BF16), jax.ShapeDtypeStruct((M, D), F32),
                   jax.ShapeDtypeStruct((M, D), F32)],
        compiler_params=_params("arbitrary"),
    )(q, k, v, do)


def _heads(a, scale=None):
    S = a.shape[0]
    if scale is not None:
        a = a * scale
    return a.astype(BF16).reshape(S, FOX_HEADS, FOX_HEAD_DIM).transpose(1, 0, 2)


def _unheads(a):
    return a.transpose(1, 0, 2).reshape(a.shape[1], FOX_HEADS * FOX_HEAD_DIM)


def mixer_fwd(x, w, tag):
    S, D = x.shape
    h = rms_fwd(x, w["mix_norm"], name=f"{tag}_rms")
    proj = mm(h, w["w_in"], name=f"{tag}_proj")
    y_a = conv_fwd(proj, w["conv_w"], D, name=f"{tag}_conv")
    y_b = sg_fwd(proj, w["sg_norm"], w["sg_w"], w["sg_bt"], D, name=f"{tag}_sg")
    c = fox_cumlog(proj, w["fox_b_f"], D, name=f"{tag}_cumlog")
    b0 = 3 * D
    scale = FOX_HEAD_DIM ** -0.5
    qh = _heads(proj[:, b0 + SEG_FQ * SEG:b0 + (SEG_FQ + 1) * SEG], scale)
    kh = _heads(proj[:, b0 + SEG_FK * SEG:b0 + (SEG_FK + 1) * SEG])
    vh = _heads(proj[:, b0 + SEG_FV * SEG:b0 + (SEG_FV + 1) * SEG])
    ch = c[:, :FOX_HEADS].T
    c_col, c_row = ch[:, :, None], ch[:, None, :]
    o, lse = fox_fwd(qh, kh, vh, c_col, c_row, name=f"{tag}_fox")
    y_c = _unheads(o).astype(BF16)
    ys = (y_a, y_b, y_c)
    branches = [mm(ys[n], w["w_branch"][n], name=f"{tag}_branch{n}") for n in range(N_BRANCH)]
    merged = merge_fwd(proj, branches, D, name=f"{tag}_merge")
    y = mm(merged, w["w_out"], name=f"{tag}_out", res=x)
    return y, (x, h, proj, ys, qh, kh, vh, c_col, c_row, o, lse, branches, merged)


def mixer_bwd(dx, saved, w, tag):
    x, h, proj, ys, qh, kh, vh, c_col, c_row, o, lse, branches, merged = saved
    S, D = x.shape
    grads = {"w_out": mm_tn(merged, dx, name=f"{tag}_dwout")}
    dmerged = mm(dx, w["w_out_t"], name=f"{tag}_dmerged")
    outs = merge_bwd(proj, branches, dmerged, D, name=f"{tag}_dmerge")
    dbr, dgl = outs[:3], outs[3:]
    grads["w_branch"] = jnp.stack([mm_tn(ys[n], dbr[n], name=f"{tag}_dwbr{n}") for n in range(N_BRANCH)])
    dys = [mm(dbr[n], w["w_branch_t"][n], name=f"{tag}_dy{n}") for n in range(N_BRANCH)]
    d_ab, d_ac, d_ah, grads["conv_w"] = conv_bwd(proj, w["conv_w"], dys[0], D, name=f"{tag}_dconv")
    d_u, d_v, grads["sg_w"], d_sgb, grads["sg_norm"] = sg_bwd(
        proj, w["sg_norm"], w["sg_w"], w["sg_bt"], dys[1], D, name=f"{tag}_dsg")
    grads["sg_b"] = d_sgb[:, :, 0]
    do = dys[2].astype(BF16).reshape(S, FOX_HEADS, FOX_HEAD_DIM).transpose(1, 0, 2)
    delta = fox_delta(do, o, name=f"{tag}_delta")
    scale = FOX_HEAD_DIM ** -0.5
    dq, dc_q = fox_bwd_dq(qh, kh, vh, do, c_col, c_row, lse, delta, scale, name=f"{tag}_dq")
    dk, dv, dc_k = fox_bwd_dkv(qh, kh, vh, do, c_col, c_row, lse, delta, name=f"{tag}_dkv")
    dc_rows = jnp.pad((dc_q[:, :, 0] + dc_k[:, 0, :]).T, ((0, 0), (0, LANES - FOX_HEADS)))
    d_fl, d_bf = fox_dlogit(proj, w["fox_b_f"], dc_rows, D, name=f"{tag}_dflogit")
    grads["fox_b_f"] = d_bf[0, :FOX_HEADS]
    dproj = jnp.concatenate(
        list(dgl) + [d_ab, d_ac, d_ah, d_u, d_v, _unheads(dq).astype(BF16), _unheads(dk).astype(BF16),
                     _unheads(dv).astype(BF16), d_fl], axis=1)
    grads["w_in"] = mm_tn(h, dproj, name=f"{tag}_dwin")
    dh = mm(dproj, w["w_in_t"], name=f"{tag}_dh")
    dx, grads["mix_norm"] = rms_bwd(x, w["mix_norm"], dh, dx, name=f"{tag}_drms")
    return dx, grads


def xattn_fwd(x, mem, w, tag):
    h = rms_fwd(x, w["xa_norm"], name=f"{tag}_rms")
    m = rms_fwd(mem, w["mem_norm"], name=f"{tag}_mrms")
    q = mm(h, w["xa_wq"], name=f"{tag}_q", out_dtype=BF16)
    k = mm(m, w["xa_wk"], name=f"{tag}_k", out_dtype=BF16)
    v = mm(m, w["xa_wv"], name=f"{tag}_v", out_dtype=BF16)
    o = xa_fwd(q, k, v, name=f"{tag}_attn")
    y = mm(o, w["xa_wo"], name=f"{tag}_o", res=x)
    return y, (x, h, m, q, k, v, o)


def xattn_bwd(dx, mem, saved, w, tag):
    x, h, m, q, k, v, o = saved
    grads = {"xa_wo": mm_tn(o, dx, name=f"{tag}_dwo")}
    do = mm(dx, w["xa_wo_t"], name=f"{tag}_do")
    dq, dk, dv = xa_bwd(q, k, v, do, name=f"{tag}_dattn")
    grads["xa_wq"] = mm_tn(h, dq, name=f"{tag}_dwq")
    grads["xa_wk"] = mm_tn(m, dk, name=f"{tag}_dwk")
    grads["xa_wv"] = mm_tn(m, dv, name=f"{tag}_dwv")
    dh = mm(dq, w["xa_wq_t"], name=f"{tag}_dh")
    dm = mm(dk, w["xa_wk_t"], name=f"{tag}_dm1")
    dm = mm(dv, w["xa_wv_t"], name=f"{tag}_dm2", res=dm)
    _, grads["mem_norm"] = rms_bwd(mem, w["mem_norm"], dm, None, name=f"{tag}_dmrms")
    dx, grads["xa_norm"] = rms_bwd(x, w["xa_norm"], dh, dx, name=f"{tag}_drms")
    return dx, grads


def _mesh_pos():
    return lax.axis_index("x"), lax.axis_index("y"), lax.axis_index("c")


def _flip(v, bit):
    return 1 - v if bit else v


def all_gather(x, *, name):
    R = x.shape[0]

    def body(x_ref, out_ref, send_sems, recv_sems, local_sem):
        mx, my, mc = _mesh_pos()
        me, sibling = (mx, my, mc), (mx, my, 1 - mc)
        chips = [(1 - mx, my), (mx, 1 - my), (1 - mx, 1 - my)]

        def slot(px, py, pc):
            return out_ref.at[4 * px + 2 * py + pc]

        def copy(k, block, to, src=None):
            return pltpu.make_async_remote_copy(
                src_ref=slot(*block) if src is None else src, dst_ref=slot(*block),
                send_sem=send_sems.at[k], recv_sem=recv_sems.at[k], device_id=to, device_id_type=MESH_IDS)

        mine = pltpu.make_async_copy(x_ref, slot(*me), local_sem)
        mine.start()
        first = [copy(0, me, sibling, src=x_ref)]
        first += [copy(1 + j, me, (*chip, mc), src=x_ref) for j, chip in enumerate(chips)]
        for cp in first:
            cp.start()
        passed = [copy(4 + j, (*chip, mc), sibling) for j, chip in enumerate(chips)]
        for j, chip in enumerate(chips):
            copy(1 + j, (*chip, mc), me).wait_recv()
            passed[j].start()
        copy(0, sibling, me).wait_recv()
        for j, chip in enumerate(chips):
            copy(4 + j, (*chip, 1 - mc), me).wait_recv()
        for cp in first + passed:
            cp.wait_send()
        mine.wait()

    return pl.pallas_call(
        body, name=name, out_shape=jax.ShapeDtypeStruct((N_DEV, R, LANES), x.dtype),
        in_specs=[pl.BlockSpec(memory_space=pl.ANY)], out_specs=pl.BlockSpec(memory_space=pl.ANY),
        scratch_shapes=[pltpu.SemaphoreType.DMA((7,)), pltpu.SemaphoreType.DMA((7,)), pltpu.SemaphoreType.DMA],
    )(x)


def all_to_all(g, *, name):
    R = g.shape[1]

    def body(g_ref, out_ref, send_sems, recv_sems, local_sem):
        mx, my, mc = _mesh_pos()
        me = 4 * mx + 2 * my + mc
        mine = pltpu.make_async_copy(g_ref.at[me], out_ref.at[me], local_sem)
        mine.start()
        copies = []
        for k in range(1, N_DEV):
            peer = (_flip(mx, k & 4), _flip(my, k & 2), _flip(mc, k & 1))
            peer_slot = 4 * peer[0] + 2 * peer[1] + peer[2]
            send = pltpu.make_async_remote_copy(
                src_ref=g_ref.at[peer_slot], dst_ref=out_ref.at[me], send_sem=send_sems.at[k - 1],
                recv_sem=recv_sems.at[k - 1], device_id=peer, device_id_type=MESH_IDS)
            arrive = pltpu.make_async_remote_copy(
                src_ref=g_ref.at[peer_slot], dst_ref=out_ref.at[peer_slot], send_sem=send_sems.at[k - 1],
                recv_sem=recv_sems.at[k - 1], device_id=peer, device_id_type=MESH_IDS)
            send.start()
            copies.append((send, arrive))
        for send, arrive in copies:
            arrive.wait_recv()
        for send, arrive in copies:
            send.wait_send()
        mine.wait()

    return pl.pallas_call(
        body, name=name, out_shape=jax.ShapeDtypeStruct(g.shape, g.dtype),
        in_specs=[pl.BlockSpec(memory_space=pl.ANY)], out_specs=pl.BlockSpec(memory_space=pl.ANY),
        scratch_shapes=[pltpu.SemaphoreType.DMA((7,)), pltpu.SemaphoreType.DMA((7,)), pltpu.SemaphoreType.DMA],
    )(g)


def reduce_adamw(parts, w, m, v, *, name):
    R = w.shape[0]
    tr = _pick(R, (1024, 512, 256, 128, 64, 32, 16, 8))
    bc1 = 1.0 - ADAM_B1 ** ADAM_STEP
    bc2 = 1.0 - ADAM_B2 ** ADAM_STEP

    def body(p_ref, w_ref, m_ref, v_ref, g_ref, d_ref, nm_ref, nv_ref):
        g = p_ref[0].astype(F32)
        for d in range(1, N_DEV):
            g = g + p_ref[d].astype(F32)
        nm = ADAM_B1 * m_ref[...] + (1.0 - ADAM_B1) * g
        nv = ADAM_B2 * v_ref[...] + (1.0 - ADAM_B2) * (g * g)
        m_hat = nm / bc1
        v_hat = nv / bc2
        g_ref[...] = g
        d_ref[...] = -ADAM_LR * (m_hat / (jnp.sqrt(v_hat) + ADAM_EPS) + ADAM_WD * w_ref[...])
        nm_ref[...] = nm
        nv_ref[...] = nv

    row = pl.BlockSpec((tr, LANES), lambda i: (i, 0))
    return pl.pallas_call(
        body, name=name, grid=(R // tr,),
        in_specs=[pl.BlockSpec((N_DEV, tr, LANES), lambda i: (0, i, 0)), row, row, row],
        out_specs=[row] * 4, out_shape=[jax.ShapeDtypeStruct((R, LANES), F32)] * 4,
        compiler_params=_params("parallel"),
    )(parts, w, m, v)


SHARDED = {
    "ffn1_w_gate": 2, "ffn1_w_up": 2, "ffn1_w_down": 1, "w_in": 2, "conv_w": 2, "w_branch": 3, "w_out": 1,
    "xa_wq": 1, "xa_wk": 1, "xa_wv": 1, "xa_wo": 1, "ffn2_w_gate": 2, "ffn2_w_up": 2, "ffn2_w_down": 1,
}
REPLICATED = ("ffn1_norm", "mix_norm", "sg_norm", "sg_w", "sg_b", "fox_b_f", "xa_norm", "mem_norm", "ffn2_norm",
              "final_norm")
WEIGHTS = ("ffn1_norm", "ffn1_w_gate", "ffn1_w_up", "ffn1_w_down", "mix_norm", "w_in", "conv_w", "sg_norm", "sg_w",
           "sg_b", "fox_b_f", "w_branch", "w_out", "xa_norm", "mem_norm", "xa_wq", "xa_wk", "xa_wv", "xa_wo",
           "ffn2_norm", "ffn2_w_gate", "ffn2_w_up", "ffn2_w_down", "final_norm")
PACK_ROWS = 1024


def _pack(arrays, dtype):
    flat = jnp.concatenate([a.reshape(-1).astype(dtype) for a in arrays])
    n = flat.shape[0]
    unit = PACK_ROWS * LANES
    total = -(-n // unit) * unit
    return jnp.pad(flat, (0, total - n)).reshape(total // LANES, LANES)


def _pack_dev_major(arrays, dtype):
    flat = jnp.concatenate([a.reshape(N_DEV, -1).astype(dtype) for a in arrays], axis=1)
    n = flat.shape[1]
    unit = PACK_ROWS * LANES
    total = -(-n // unit) * unit
    return jnp.pad(flat, ((0, 0), (0, total - n))).reshape(N_DEV, total // LANES, LANES)


def _unpack(buf, shapes, lead=()):
    flat = buf.reshape(lead + (-1,))
    out, off = [], 0
    for shp in shapes:
        n = 1
        for s in shp:
            n *= s
        out.append(flat[..., off:off + n].reshape(lead + tuple(shp)))
        off += n
    return out


def _to_dev_major(full, axis):
    shp = full.shape
    a = full.reshape(shp[:axis] + (N_DEV, shp[axis] // N_DEV) + shp[axis + 1:])
    return jnp.moveaxis(a, axis, 0)


def _from_dev_major(a, axis):
    a = jnp.moveaxis(a, 0, axis)
    shp = a.shape
    return a.reshape(shp[:axis] + (shp[axis] * shp[axis + 1],) + shp[axis + 2:])


def _relayout_w_in(w_in, D):
    main = 8 * SEG
    pad = jnp.zeros((w_in.shape[0], SEG - FOX_HEADS), w_in.dtype)
    return jnp.concatenate([w_in[:, main + FOX_HEADS:], w_in[:, :main], w_in[:, main:main + FOX_HEADS], pad], axis=1)


def _unlayout_w_in(g, D):
    return jnp.concatenate([g[:, 3 * D:3 * D + 8 * SEG], g[:, 3 * D + 8 * SEG:3 * D + 8 * SEG + FOX_HEADS],
                            g[:, :3 * D]], axis=1)


def _layer_weights(full, rep, l, D):
    t = lambda a: a.T
    w_in = _relayout_w_in(full["w_in"][l], D)
    ffn = {}
    for tag in ("ffn1", "ffn2"):
        ffn[tag] = {"norm": rep[f"{tag}_norm"][l][None, :]}
        for n in ("w_gate", "w_up", "w_down"):
            ffn[tag][n] = full[f"{tag}_{n}"][l]
            ffn[tag][n + "_t"] = t(full[f"{tag}_{n}"][l])
    mix = {
        "mix_norm": rep["mix_norm"][l][None, :], "w_in": w_in, "w_in_t": t(w_in),
        "conv_w": full["conv_w"][l], "sg_norm": rep["sg_norm"][l][None, :], "sg_w": rep["sg_w"][l],
        "sg_bt": rep["sg_b"][l][:, :, None],
        "fox_b_f": jnp.pad(rep["fox_b_f"][l], (0, LANES - FOX_HEADS))[None, :],
        "w_branch": full["w_branch"][l], "w_branch_t": jnp.swapaxes(full["w_branch"][l], 1, 2),
        "w_out": full["w_out"][l], "w_out_t": t(full["w_out"][l]),
    }
    xa = {"xa_norm": rep["xa_norm"][l][None, :], "mem_norm": rep["mem_norm"][l][None, :]}
    for n in ("xa_wq", "xa_wk", "xa_wv", "xa_wo"):
        xa[n] = full[n][l]
        xa[n + "_t"] = t(full[n][l])
    return ffn, mix, xa


def kernel(x, mem, ffn1_norm, ffn1_w_gate, ffn1_w_up, ffn1_w_down, mix_norm, w_in, conv_w, sg_norm, sg_w, sg_b, fox_b_f, w_branch, w_out, xa_norm, mem_norm, xa_wq, xa_wk, xa_wv, xa_wo, ffn2_norm, ffn2_w_gate, ffn2_w_up, ffn2_w_down, final_norm, loss_target, m_ffn1_norm, m_ffn1_w_gate, m_ffn1_w_up, m_ffn1_w_down, m_mix_norm, m_w_in, m_conv_w, m_sg_norm, m_sg_w, m_sg_b, m_fox_b_f, m_w_branch, m_w_out, m_xa_norm, m_mem_norm, m_xa_wq, m_xa_wk, m_xa_wv, m_xa_wo, m_ffn2_norm, m_ffn2_w_gate, m_ffn2_w_up, m_ffn2_w_down, m_final_norm, v_ffn1_norm, v_ffn1_w_gate, v_ffn1_w_up, v_ffn1_w_down, v_mix_norm, v_w_in, v_conv_w, v_sg_norm, v_sg_w, v_sg_b, v_fox_b_f, v_w_branch, v_w_out, v_xa_norm, v_mem_norm, v_xa_wq, v_xa_wk, v_xa_wv, v_xa_wo, v_ffn2_norm, v_ffn2_w_gate, v_ffn2_w_up, v_ffn2_w_down, v_final_norm):
    args = locals()
    wts = {n: args[n] for n in WEIGHTS}
    mom = {n: args["m_" + n] for n in WEIGHTS}
    var = {n: args["v_" + n] for n in WEIGHTS}
    depth = ffn1_norm.shape[0]
    S, D = x.shape[1], x.shape[2]
    xs, ms, tgt = x[0], mem[0], loss_target[0]
    sharded = list(SHARDED)
    local_shapes = [wts[n].shape for n in sharded]

    gathered = all_gather(_pack([wts[n] for n in sharded], BF16), name="gather_weights")
    full = {n: _from_dev_major(a, SHARDED[n])
            for n, a in zip(sharded, _unpack(gathered, local_shapes, lead=(N_DEV,)))}
    conv_parts = _unpack(all_gather(_pack([conv_w], F32), name="gather_conv_w"), [conv_w.shape], lead=(N_DEV,))[0]
    full["conv_w"] = _from_dev_major(conv_parts, SHARDED["conv_w"])
    rep = {n: wts[n] for n in REPLICATED}
    layers = [_layer_weights(full, rep, l, D) for l in range(depth)]

    saved = []
    h = xs
    for l, (ffn, mix, xa) in enumerate(layers):
        h, s1 = ffn_fwd(h, ffn["ffn1"], f"l{l}_ffn1")
        h, s2 = mixer_fwd(h, mix, f"l{l}_mix")
        h, s3 = xattn_fwd(h, ms, xa, f"l{l}_xa")
        h, s4 = ffn_fwd(h, ffn["ffn2"], f"l{l}_ffn2")
        saved.append((s1, s2, s3, s4))
    dx, d_final, loss_cols = final_loss_bwd(h, final_norm[None, :], tgt, name="final_loss")
    loss = lax.psum(0.5 * jnp.sum(loss_cols) / D, ("x", "y", "c"))

    per_layer = []
    for l in reversed(range(depth)):
        ffn, mix, xa = layers[l]
        s1, s2, s3, s4 = saved[l]
        g = {}
        dx, g4 = ffn_bwd(dx, s4, ffn["ffn2"], f"l{l}_ffn2")
        dx, g3 = xattn_bwd(dx, ms, s3, xa, f"l{l}_xa")
        dx, g2 = mixer_bwd(dx, s2, mix, f"l{l}_mix")
        dx, g1 = ffn_bwd(dx, s1, ffn["ffn1"], f"l{l}_ffn1")
        for tag, gg in (("ffn1", g1), ("ffn2", g4)):
            for n in ("w_gate", "w_up", "w_down"):
                g[f"{tag}_{n}"] = gg[n]
            g[f"{tag}_norm"] = gg["norm"][0]
        g.update(g3)
        g["xa_norm"], g["mem_norm"] = g3["xa_norm"][0], g3["mem_norm"][0]
        g.update({k: v for k, v in g2.items() if k != "w_in"})
        g["w_in"] = _unlayout_w_in(g2["w_in"], D)
        g["mix_norm"], g["sg_norm"] = g2["mix_norm"][0], g2["sg_norm"][0]
        per_layer.append(g)
    per_layer.reverse()
    grads = {n: jnp.stack([per_layer[l][n] for l in range(depth)]) for n in WEIGHTS if n != "final_norm"}
    grads["final_norm"] = d_final[0]

    send = _pack_dev_major([_to_dev_major(grads[n], SHARDED[n]) for n in sharded], BF16)
    parts = all_to_all(send, name="scatter_grads")
    outs = reduce_adamw(parts, _pack([wts[n] for n in sharded], F32), _pack([mom[n] for n in sharded], F32),
                        _pack([var[n] for n in sharded], F32), name="adamw_sharded")
    res = {k: dict(zip(sharded, _unpack(o, local_shapes))) for k, o in zip(("g", "d", "m", "v"), outs)}

    small = list(REPLICATED)
    small_shapes = [wts[n].shape for n in small]
    parts = all_gather(_pack([grads[n] for n in small], F32), name="gather_small_grads")
    outs = reduce_adamw(parts, _pack([wts[n] for n in small], F32), _pack([mom[n] for n in small], F32),
                        _pack([var[n] for n in small], F32), name="adamw_replicated")
    for k, o in zip(("g", "d", "m", "v"), outs):
        res[k].update(dict(zip(small, _unpack(o, small_shapes))))

    return (loss, dx[None], *[res["g"][n] for n in WEIGHTS], *[res["d"][n] for n in WEIGHTS],
            *[res["m"][n] for n in WEIGHTS], *[res["v"][n] for n in WEIGHTS])
```

```python
import functools

import jax
import jax.numpy as jnp
from jax import lax
from jax.experimental import pallas as pl
from jax.experimental.pallas import tpu as pltpu

F32 = jnp.float32
BF16 = jnp.bfloat16

N_DEV = 8
RMS_EPS = 1e-6
SEG = 512
FOX_HEADS = 8
FOX_HEAD_DIM = 64
SG_GROUPS = 4
CHUNK = 128
XA_HEADS = 4
N_BRANCH = 3
LANES = 128
VMEM_LIMIT_BYTES = 48 * 1024 * 1024
NEG_BIG = -1e30

ADAM_LR = 0.001
ADAM_B1 = 0.9
ADAM_B2 = 0.999
ADAM_EPS = 1e-08
ADAM_WD = 0.01
ADAM_STEP = 10

_GELU_K = 0.7978845608028654
_GELU_C = 0.044715

MESH_IDS = pl.DeviceIdType.MESH


def _pick(n, candidates):
    for c in candidates:
        if c <= n and n % c == 0:
            return c
    return n


def _params(*sem):
    return pltpu.CompilerParams(dimension_semantics=sem, vmem_limit_bytes=VMEM_LIMIT_BYTES)


def _sig(x):
    return 1.0 / (1.0 + jnp.exp(-x))


def _gelu(x):
    t = jnp.tanh(_GELU_K * (x + _GELU_C * x * x * x))
    return 0.5 * x * (1.0 + t), t


def _gelu_grad(x, t):
    return 0.5 * (1.0 + t) + 0.5 * x * (1.0 - t * t) * _GELU_K * (1.0 + 3.0 * _GELU_C * x * x)


def mm(a, b, *, name, out_dtype=F32, res=None, scale=1.0, tm=512):
    M, K = a.shape
    K2, N = b.shape
    assert K == K2
    tm = _pick(M, (tm, 256, 128))
    tn = _pick(N, (512, 384, 256, 128))
    tk = K if K <= 3072 else _pick(K, (2560, 2048, 1536, 1024, 512, 256, 128))
    nk = K // tk
    has_res = res is not None

    def body(*refs):
        if has_res:
            a_ref, b_ref, r_ref, o_ref = refs[:4]
        else:
            a_ref, b_ref, o_ref = refs[:3]
            r_ref = None

        def finish(acc):
            if scale != 1.0:
                acc = acc * scale
            if has_res:
                acc = r_ref[...] + acc
            o_ref[...] = acc.astype(out_dtype)

        part = jnp.dot(a_ref[...].astype(BF16), b_ref[...].astype(BF16), preferred_element_type=F32)
        if nk == 1:
            finish(part)
        else:
            acc_ref = refs[-1]
            k = pl.program_id(2)

            @pl.when(k == 0)
            def _():
                acc_ref[...] = part

            @pl.when(k > 0)
            def _():
                acc_ref[...] += part

            @pl.when(k == nk - 1)
            def _():
                finish(acc_ref[...])

    in_specs = [pl.BlockSpec((tm, tk), lambda i, j, k: (i, k)), pl.BlockSpec((tk, tn), lambda i, j, k: (k, j))]
    args = [a, b]
    if has_res:
        in_specs.append(pl.BlockSpec((tm, tn), lambda i, j, k: (i, j)))
        args.append(res)
    return pl.pallas_call(
        body, name=name, grid=(M // tm, N // tn, nk), in_specs=in_specs,
        out_specs=pl.BlockSpec((tm, tn), lambda i, j, k: (i, j)),
        out_shape=jax.ShapeDtypeStruct((M, N), out_dtype),
        scratch_shapes=[pltpu.VMEM((tm, tn), F32)] if nk > 1 else [],
        compiler_params=_params("parallel", "parallel", "arbitrary"),
    )(*args)


def mm_tn(a, b, *, name, scale=1.0):
    M, K = a.shape
    M2, N = b.shape
    assert M == M2
    tm = _pick(M, (512, 256, 128))
    tk = _pick(K, (1024, 512, 256, 128))
    tn = _pick(N, (512, 384, 256, 128))
    nm = M // tm

    def body(a_ref, b_ref, o_ref):
        m = pl.program_id(2)
        part = lax.dot_general(a_ref[...].astype(BF16), b_ref[...].astype(BF16), (((0,), (0,)), ((), ())),
                               preferred_element_type=F32)

        @pl.when(m == 0)
        def _():
            o_ref[...] = part

        @pl.when(m > 0)
        def _():
            o_ref[...] += part

        if scale != 1.0:
            @pl.when(m == nm - 1)
            def _():
                o_ref[...] = o_ref[...] * scale

    return pl.pallas_call(
        body, name=name, grid=(K // tk, N // tn, nm),
        in_specs=[pl.BlockSpec((tm, tk), lambda i, j, m: (m, i)), pl.BlockSpec((tm, tn), lambda i, j, m: (m, j))],
        out_specs=pl.BlockSpec((tk, tn), lambda i, j, m: (i, j)),
        out_shape=jax.ShapeDtypeStruct((K, N), F32),
        compiler_params=_params("parallel", "parallel", "arbitrary"),
    )(a, b)


def rms_fwd(x, g, *, name):
    S, D = x.shape
    ts = _pick(S, (512, 256, 128))

    def body(x_ref, g_ref, h_ref):
        xv = x_ref[...]
        r = lax.rsqrt(jnp.mean(xv * xv, axis=-1, keepdims=True) + RMS_EPS)
        h_ref[...] = ((xv * r) * g_ref[...]).astype(BF16)

    return pl.pallas_call(
        body, name=name, grid=(S // ts,),
        in_specs=[pl.BlockSpec((ts, D), lambda i: (i, 0)), pl.BlockSpec((1, D), lambda i: (0, 0))],
        out_specs=pl.BlockSpec((ts, D), lambda i: (i, 0)),
        out_shape=jax.ShapeDtypeStruct((S, D), BF16),
        compiler_params=_params("parallel"),
    )(x, g)


def rms_bwd(x, g, dh, dx_in, *, name):
    S, D = x.shape
    ts = _pick(S, (512, 256, 128))
    has_in = dx_in is not None

    def body(*refs):
        if has_in:
            x_ref, g_ref, dh_ref, di_ref, dx_ref, dg_ref = refs
        else:
            x_ref, g_ref, dh_ref, dx_ref, dg_ref = refs
        xv = x_ref[...]
        dh_v = dh_ref[...]
        r = lax.rsqrt(jnp.mean(xv * xv, axis=-1, keepdims=True) + RMS_EPS)
        xh = xv * r
        gd = dh_v * g_ref[...]
        dx = r * (gd - xh * jnp.mean(gd * xh, axis=-1, keepdims=True))
        if has_in:
            dx = di_ref[...] + dx
        dx_ref[...] = dx
        part = jnp.sum(dh_v * xh, axis=0, keepdims=True)

        @pl.when(pl.program_id(0) == 0)
        def _():
            dg_ref[...] = part

        @pl.when(pl.program_id(0) > 0)
        def _():
            dg_ref[...] += part

    row = pl.BlockSpec((ts, D), lambda i: (i, 0))
    vec = pl.BlockSpec((1, D), lambda i: (0, 0))
    return pl.pallas_call(
        body, name=name, grid=(S // ts,),
        in_specs=[row, vec, row] + ([row] if has_in else []),
        out_specs=[row, vec],
        out_shape=[jax.ShapeDtypeStruct((S, D), F32), jax.ShapeDtypeStruct((1, D), F32)],
        compiler_params=_params("arbitrary"),
    )(*([x, g, dh] + ([dx_in] if has_in else [])))


def final_loss_bwd(x, g, target, *, name):
    S, D = x.shape
    ts = _pick(S, (512, 256, 128))

    def body(x_ref, g_ref, t_ref, dx_ref, dg_ref, ls_ref):
        xv = x_ref[...]
        gv = g_ref[...]
        r = lax.rsqrt(jnp.mean(xv * xv, axis=-1, keepdims=True) + RMS_EPS)
        xh = xv * r
        e = xh * gv - t_ref[...]
        dy = e * (1.0 / D)
        gd = dy * gv
        dx_ref[...] = r * (gd - xh * jnp.mean(gd * xh, axis=-1, keepdims=True))
        dg_part = jnp.sum(dy * xh, axis=0, keepdims=True)
        ls_part = jnp.sum(e * e, axis=0, keepdims=True)

        @pl.when(pl.program_id(0) == 0)
        def _():
            dg_ref[...] = dg_part
            ls_ref[...] = ls_part

        @pl.when(pl.program_id(0) > 0)
        def _():
            dg_ref[...] += dg_part
            ls_ref[...] += ls_part

    row = pl.BlockSpec((ts, D), lambda i: (i, 0))
    vec = pl.BlockSpec((1, D), lambda i: (0, 0))
    return pl.pallas_call(
        body, name=name, grid=(S // ts,), in_specs=[row, vec, row], out_specs=[row, vec, vec],
        out_shape=[jax.ShapeDtypeStruct((S, D), F32), jax.ShapeDtypeStruct((1, D), F32),
                   jax.ShapeDtypeStruct((1, D), F32)],
        compiler_params=_params("arbitrary"),
    )(x, g, target)


def swiglu_fwd(gp, up, *, name):
    S, F = gp.shape
    ts = _pick(S, (256, 128))

    def body(g_ref, u_ref, a_ref):
        gv = g_ref[...]
        a_ref[...] = (gv * _sig(gv) * u_ref[...]).astype(BF16)

    row = pl.BlockSpec((ts, F), lambda i: (i, 0))
    return pl.pallas_call(
        body, name=name, grid=(S // ts,), in_specs=[row, row], out_specs=row,
        out_shape=jax.ShapeDtypeStruct((S, F), BF16), compiler_params=_params("parallel"),
    )(gp, up)


def swiglu_bwd(gp, up, da, *, name):
    S, F = gp.shape
    ts = _pick(S, (256, 128))

    def body(g_ref, u_ref, da_ref, dg_ref, du_ref):
        gv = g_ref[...]
        dav = da_ref[...]
        s = _sig(gv)
        dg_ref[...] = (dav * u_ref[...] * (s * (1.0 + gv * (1.0 - s)))).astype(BF16)
        du_ref[...] = (dav * (gv * s)).astype(BF16)

    row = pl.BlockSpec((ts, F), lambda i: (i, 0))
    return pl.pallas_call(
        body, name=name, grid=(S // ts,), in_specs=[row, row, row], out_specs=[row, row],
        out_shape=[jax.ShapeDtypeStruct((S, F), BF16)] * 2, compiler_params=_params("parallel"),
    )(gp, up, da)


def ffn_fwd(x, w, tag):
    h = rms_fwd(x, w["norm"], name=f"{tag}_rms")
    gp = mm(h, w["w_gate"], name=f"{tag}_gate")
    up = mm(h, w["w_up"], name=f"{tag}_up")
    a = swiglu_fwd(gp, up, name=f"{tag}_act")
    y = mm(a, w["w_down"], name=f"{tag}_down", res=x, scale=0.5)
    return y, (x, h, gp, up, a)


def ffn_bwd(dx, saved, w, tag):
    x, h, gp, up, a = saved
    grads = {"w_down": mm_tn(a, dx, name=f"{tag}_dwd", scale=0.5)}
    da = mm(dx, w["w_down_t"], name=f"{tag}_da", scale=0.5)
    dgp, dup = swiglu_bwd(gp, up, da, name=f"{tag}_dact")
    grads["w_gate"] = mm_tn(h, dgp, name=f"{tag}_dwg")
    grads["w_up"] = mm_tn(h, dup, name=f"{tag}_dwu")
    dh = mm(dgp, w["w_gate_t"], name=f"{tag}_dh1")
    dh = mm(dup, w["w_up_t"], name=f"{tag}_dh2", res=dh)
    dx, grads["norm"] = rms_bwd(x, w["norm"], dh, dx, name=f"{tag}_drms")
    return dx, grads


SEG_AB, SEG_AC, SEG_AH, SEG_U, SEG_V, SEG_FQ, SEG_FK, SEG_FV, SEG_FL = range(9)
N_SEG = 9


def _seg_block(D, seg):
    return 3 * D // SEG + seg


def _shift_down(z, prev8, n, rows):
    out = pltpu.roll(z, n, 0)
    for r in range(n):
        out = jnp.where(rows == r, prev8[8 - n + r:8 - n + r + 1, :], out)
    return out


def _shift_up(z, next8, n, rows, ts):
    out = pltpu.roll(z, ts - n, 0)
    for r in range(n):
        out = jnp.where(rows == ts - n + r, next8[r:r + 1, :], out)
    return out


def conv_fwd(proj, conv_w, D, *, name):
    S = proj.shape[0]
    ts = _pick(S, (512, 256, 128))
    b0 = _seg_block(D, 0)

    def body(ab_ref, ac_ref, ah_ref, pc_ref, ph_ref, w_ref, y_ref):
        i = pl.program_id(0)
        rows = lax.broadcasted_iota(jnp.int32, (ts, 1), 0)
        z = ac_ref[...] * ah_ref[...]
        zp = pc_ref[...] * ph_ref[...] * (i > 0).astype(F32)
        w = w_ref[...]
        y = w[0:1, :] * _shift_down(z, zp, 2, rows) + w[1:2, :] * _shift_down(z, zp, 1, rows) + w[2:3, :] * z
        y_ref[...] = (ab_ref[...] * y).astype(BF16)

    def seg(s):
        return pl.BlockSpec((ts, SEG), lambda i: (i, b0 + s))

    def prev(s):
        return pl.BlockSpec((8, SEG), lambda i: (jnp.maximum(i * (ts // 8) - 1, 0), b0 + s))

    return pl.pallas_call(
        body, name=name, grid=(S // ts,),
        in_specs=[seg(SEG_AB), seg(SEG_AC), seg(SEG_AH), prev(SEG_AC), prev(SEG_AH),
                  pl.BlockSpec((3, SEG), lambda i: (0, 0))],
        out_specs=pl.BlockSpec((ts, SEG), lambda i: (i, 0)),
        out_shape=jax.ShapeDtypeStruct((S, SEG), BF16), compiler_params=_params("parallel"),
    )(proj, proj, proj, proj, proj, conv_w)


def conv_bwd(proj, conv_w, dy, D, *, name):
    S = proj.shape[0]
    ts = _pick(S, (512, 256, 128))
    nt = S // ts
    b0 = _seg_block(D, 0)

    def body(ab_ref, ac_ref, ah_ref, pc_ref, ph_ref, nb_ref, dy_ref, ndy_ref, w_ref,
             dab_ref, dac_ref, dah_ref, dw_ref):
        i = pl.program_id(0)
        rows = lax.broadcasted_iota(jnp.int32, (ts, 1), 0)
        ab, ac, ah = ab_ref[...], ac_ref[...], ah_ref[...]
        z = ac * ah
        zp = pc_ref[...] * ph_ref[...] * (i > 0).astype(F32)
        w = w_ref[...]
        z1 = _shift_down(z, zp, 1, rows)
        z2 = _shift_down(z, zp, 2, rows)
        y = w[0:1, :] * z2 + w[1:2, :] * z1 + w[2:3, :] * z
        dyv = dy_ref[...]
        dab_ref[...] = (dyv * y).astype(BF16)
        dyy = dyv * ab
        nyy = ndy_ref[...] * nb_ref[...] * (i < nt - 1).astype(F32)
        dz = (w[2:3, :] * dyy + w[1:2, :] * _shift_up(dyy, nyy, 1, rows, ts)
              + w[0:1, :] * _shift_up(dyy, nyy, 2, rows, ts))
        dac_ref[...] = (dz * ah).astype(BF16)
        dah_ref[...] = (dz * ac).astype(BF16)
        parts = [jnp.sum(dyy * zz, axis=0, keepdims=True) for zz in (z2, z1, z)]

        @pl.when(i == 0)
        def _():
            for k in range(3):
                dw_ref[k:k + 1, :] = parts[k]

        @pl.when(i > 0)
        def _():
            for k in range(3):
                dw_ref[k:k + 1, :] += parts[k]

    def seg(s):
        return pl.BlockSpec((ts, SEG), lambda i: (i, b0 + s))

    def prev(s):
        return pl.BlockSpec((8, SEG), lambda i: (jnp.maximum(i * (ts // 8) - 1, 0), b0 + s))

    nxt_row = lambda i: jnp.minimum((i + 1) * (ts // 8), S // 8 - 1)
    out_row = pl.BlockSpec((ts, SEG), lambda i: (i, 0))
    return pl.pallas_call(
        body, name=name, grid=(nt,),
        in_specs=[seg(SEG_AB), seg(SEG_AC), seg(SEG_AH), prev(SEG_AC), prev(SEG_AH),
                  pl.BlockSpec((8, SEG), lambda i: (nxt_row(i), b0 + SEG_AB)),
                  out_row, pl.BlockSpec((8, SEG), lambda i: (nxt_row(i), 0)),
                  pl.BlockSpec((3, SEG), lambda i: (0, 0))],
        out_specs=[out_row, out_row, out_row, pl.BlockSpec((3, SEG), lambda i: (0, 0))],
        out_shape=[jax.ShapeDtypeStruct((S, SEG), BF16)] * 3 + [jax.ShapeDtypeStruct((3, SEG), F32)],
        compiler_params=_params("arbitrary"),
    )(proj, proj, proj, proj, proj, proj, dy, dy, conv_w)


def _tril_mask():
    r = lax.broadcasted_iota(jnp.int32, (CHUNK, CHUNK), 0)
    c = lax.broadcasted_iota(jnp.int32, (CHUNK, CHUNK), 1)
    return c <= r


def sg_fwd(proj, sg_norm, sg_w, sg_bt, D, *, name):
    S = proj.shape[0]
    ts = _pick(S, (512, 256, 128))
    b0 = _seg_block(D, 0)

    def body(u_ref, v_ref, gs_ref, w_ref, b_ref, y_ref):
        ug, _ = _gelu(u_ref[...])
        vg, _ = _gelu(v_ref[...])
        vn = ((vg * lax.rsqrt(jnp.mean(vg * vg, axis=-1, keepdims=True) + RMS_EPS)) * gs_ref[...]).astype(BF16)
        mask = _tril_mask()
        for g in range(SG_GROUPS):
            wg = jnp.where(mask, w_ref[g], 0.0).astype(BF16)
            cols = slice(g * CHUNK, (g + 1) * CHUNK)
            for n in range(ts // CHUNK):
                rws = slice(n * CHUNK, (n + 1) * CHUNK)
                sv = jnp.dot(wg, vn[rws, cols], preferred_element_type=F32) + b_ref[g]
                y_ref[rws, cols] = (ug[rws, cols] * sv).astype(BF16)

    seg = lambda s: pl.BlockSpec((ts, SEG), lambda i: (i, b0 + s))
    return pl.pallas_call(
        body, name=name, grid=(S // ts,),
        in_specs=[seg(SEG_U), seg(SEG_V), pl.BlockSpec((1, SEG), lambda i: (0, 0)),
                  pl.BlockSpec((SG_GROUPS, CHUNK, CHUNK), lambda i: (0, 0, 0)),
                  pl.BlockSpec((SG_GROUPS, CHUNK, 1), lambda i: (0, 0, 0))],
        out_specs=pl.BlockSpec((ts, SEG), lambda i: (i, 0)),
        out_shape=jax.ShapeDtypeStruct((S, SEG), BF16), compiler_params=_params("parallel"),
    )(proj, proj, sg_norm, sg_w, sg_bt)


def sg_bwd(proj, sg_norm, sg_w, sg_bt, dy, D, *, name):
    S = proj.shape[0]
    ts = _pick(S, (512, 256, 128))
    nt = S // ts
    b0 = _seg_block(D, 0)

    def body(u_ref, v_ref, dy_ref, gs_ref, w_ref, b_ref, du_ref, dv_ref, dw_ref, db_ref, dgs_ref, dvn_sc):
        i = pl.program_id(0)
        uv, vv, dyv = u_ref[...], v_ref[...], dy_ref[...]
        ug, ut = _gelu(uv)
        vg, vt = _gelu(vv)
        r = lax.rsqrt(jnp.mean(vg * vg, axis=-1, keepdims=True) + RMS_EPS)
        vh = vg * r
        gs = gs_ref[...]
        vn = (vh * gs).astype(BF16)
        dsv = dyv * ug
        dsv_b = dsv.astype(BF16)
        mask = _tril_mask()

        @pl.when(i == 0)
        def _():
            dw_ref[...] = jnp.zeros_like(dw_ref)
            db_ref[...] = jnp.zeros_like(db_ref)

        for g in range(SG_GROUPS):
            wg = jnp.where(mask, w_ref[g], 0.0).astype(BF16)
            cols = slice(g * CHUNK, (g + 1) * CHUNK)
            dw_acc = jnp.zeros((CHUNK, CHUNK), F32)
            db_acc = jnp.zeros((CHUNK, 1), F32)
            for n in range(ts // CHUNK):
                rws = slice(n * CHUNK, (n + 1) * CHUNK)
                vblk = vn[rws, cols]
                sv = jnp.dot(wg, vblk, preferred_element_type=F32) + b_ref[g]
                du_ref[rws, cols] = (dyv[rws, cols] * sv * _gelu_grad(uv[rws, cols], ut[rws, cols])).astype(BF16)
                dblk = dsv_b[rws, cols]
                dvn_sc[rws, cols] = lax.dot_general(wg, dblk, (((0,), (0,)), ((), ())), preferred_element_type=F32)
                dw_acc = dw_acc + lax.dot_general(dblk, vblk, (((1,), (1,)), ((), ())), preferred_element_type=F32)
                db_acc = db_acc + jnp.sum(dsv[rws, cols], axis=1, keepdims=True)
            dw_ref[g] += jnp.where(mask, dw_acc, 0.0)
            db_ref[g] += db_acc

        dvn = dvn_sc[...]
        gd = dvn * gs
        dvg = r * (gd - vh * jnp.mean(gd * vh, axis=-1, keepdims=True))
        dv_ref[...] = (dvg * _gelu_grad(vv, vt)).astype(BF16)
        dgs_part = jnp.sum(dvn * vh, axis=0, keepdims=True)

        @pl.when(i == 0)
        def _():
            dgs_ref[...] = dgs_part

        @pl.when(i > 0)
        def _():
            dgs_ref[...] += dgs_part

    seg = lambda s: pl.BlockSpec((ts, SEG), lambda i: (i, b0 + s))
    row = pl.BlockSpec((ts, SEG), lambda i: (i, 0))
    wspec = pl.BlockSpec((SG_GROUPS, CHUNK, CHUNK), lambda i: (0, 0, 0))
    bspec = pl.BlockSpec((SG_GROUPS, CHUNK, 1), lambda i: (0, 0, 0))
    vec = pl.BlockSpec((1, SEG), lambda i: (0, 0))
    return pl.pallas_call(
        body, name=name, grid=(nt,),
        in_specs=[seg(SEG_U), seg(SEG_V), row, vec, wspec, bspec],
        out_specs=[row, row, wspec, bspec, vec],
        out_shape=[jax.ShapeDtypeStruct((S, SEG), BF16)] * 2
        + [jax.ShapeDtypeStruct((SG_GROUPS, CHUNK, CHUNK), F32), jax.ShapeDtypeStruct((SG_GROUPS, CHUNK, 1), F32),
           jax.ShapeDtypeStruct((1, SEG), F32)],
        scratch_shapes=[pltpu.VMEM((ts, SEG), F32)],
        compiler_params=_params("arbitrary"),
    )(proj, proj, dy, sg_norm, sg_w, sg_bt)


def _log_sigmoid(x):
    return jnp.minimum(x, 0.0) - jnp.log(1.0 + jnp.exp(-jnp.abs(x)))


def fox_cumlog(proj, b_f, D, *, name):
    S = proj.shape[0]
    ts = _pick(S, (512, 256, 128))
    blk = (3 * D + SEG_FL * SEG) // LANES

    def body(f_ref, b_ref, c_ref, carry):
        i = pl.program_id(0)

        @pl.when(i == 0)
        def _():
            carry[...] = jnp.zeros_like(carry)

        rows = lax.broadcasted_iota(jnp.int32, (ts, 1), 0)
        acc = _log_sigmoid(f_ref[...] + b_ref[...])
        d = 1
        while d < ts:
            acc = acc + jnp.where(rows >= d, pltpu.roll(acc, d, 0), 0.0)
            d *= 2
        acc = acc + carry[...]
        c_ref[...] = acc
        carry[...] = acc[ts - 1:ts, :]

    return pl.pallas_call(
        body, name=name, grid=(S // ts,),
        in_specs=[pl.BlockSpec((ts, LANES), lambda i: (i, blk)), pl.BlockSpec((1, LANES), lambda i: (0, 0))],
        out_specs=pl.BlockSpec((ts, LANES), lambda i: (i, 0)),
        out_shape=jax.ShapeDtypeStruct((S, LANES), F32),
        scratch_shapes=[pltpu.VMEM((1, LANES), F32)],
        compiler_params=_params("arbitrary"),
    )(proj, b_f)


def fox_dlogit(proj, b_f, dc, D, *, name):
    S = proj.shape[0]
    ts = _pick(S, (512, 256, 128))
    nt = S // ts
    blk = (3 * D + SEG_FL * SEG) // LANES

    def body(f_ref, b_ref, dc_ref, df_ref, db_ref, carry):
        i = pl.program_id(0)

        @pl.when(i == 0)
        def _():
            carry[...] = jnp.zeros_like(carry)

        rows = lax.broadcasted_iota(jnp.int32, (ts, 1), 0)
        acc = dc_ref[...]
        d = 1
        while d < ts:
            acc = acc + jnp.where(rows < ts - d, pltpu.roll(acc, ts - d, 0), 0.0)
            d *= 2
        acc = acc + carry[...]
        carry[...] = acc[0:1, :]
        df = acc * _sig(-(f_ref[...] + b_ref[...]))
        df_ref[...] = jnp.zeros_like(df_ref)
        df_ref[:, 0:LANES] = df.astype(BF16)
        part = jnp.sum(df, axis=0, keepdims=True)

        @pl.when(i == 0)
        def _():
            db_ref[...] = part

        @pl.when(i > 0)
        def _():
            db_ref[...] += part

    rev = lambda i: nt - 1 - i
    return pl.pallas_call(
        body, name=name, grid=(nt,),
        in_specs=[pl.BlockSpec((ts, LANES), lambda i: (rev(i), blk)), pl.BlockSpec((1, LANES), lambda i: (0, 0)),
                  pl.BlockSpec((ts, LANES), lambda i: (rev(i), 0))],
        out_specs=[pl.BlockSpec((ts, SEG), lambda i: (rev(i), 0)), pl.BlockSpec((1, LANES), lambda i: (0, 0))],
        out_shape=[jax.ShapeDtypeStruct((S, SEG), BF16), jax.ShapeDtypeStruct((1, LANES), F32)],
        scratch_shapes=[pltpu.VMEM((1, LANES), F32)],
        compiler_params=_params("arbitrary"),
    )(proj, b_f, dc)


def _fox_tile(S):
    return min(512, max(128, S // 4))


def _causal_mask(t, keys_on_rows=False):
    r = lax.broadcasted_iota(jnp.int32, (t, t), 0)
    c = lax.broadcasted_iota(jnp.int32, (t, t), 1)
    return (r <= c) if keys_on_rows else (c <= r)


FOX_PAD = LANES
COL_C = FOX_HEAD_DIM
COL_ROWSUM = FOX_HEAD_DIM + 3
COL_L = FOX_HEAD_DIM
FOX_GROUP = 2
_NT = (((1,), (1,)), ((), ()))


def fox_operands(q, k, v, c, scale):
    S = q.shape[0]
    H = FOX_HEADS

    def heads(a):
        return a.astype(BF16).reshape(S, H, FOX_HEAD_DIM).transpose(1, 0, 2)

    ch = c.T[:, :, None]
    c_hi = lax.reduce_precision(ch, 8, 7)
    r1 = ch - c_hi
    c_mid = lax.reduce_precision(r1, 8, 7)
    c_hi, c_mid, c_lo = c_hi.astype(BF16), c_mid.astype(BF16), (r1 - c_mid).astype(BF16)
    one = jnp.ones((H, S, 1), BF16)

    def pad(parts):
        used = sum(p.shape[-1] for p in parts)
        return jnp.concatenate(parts + [jnp.zeros((H, S, FOX_PAD - used), BF16)], axis=-1)

    return (pad([heads(q * scale), one, one, one]), pad([heads(k), -c_hi, -c_mid, -c_lo, one]), pad([heads(v), one]))


def fox_fwd(q, k, v, *, name):
    H, S, W = q.shape
    t = _fox_tile(S)
    n = S // t
    G = FOX_GROUP

    def body(q_ref, k_ref, v_ref, o_ref, lse_ref, m_sc, acc_sc):
        i, j = pl.program_id(1), pl.program_id(2)

        @pl.when(j == 0)
        def _():
            m_sc[...] = jnp.full_like(m_sc, NEG_BIG)
            acc_sc[...] = jnp.zeros_like(acc_sc)

        def step(masked):
            for g in range(G):
                s = lax.dot_general(q_ref[g], k_ref[g], _NT, preferred_element_type=F32)
                if masked:
                    s = jnp.where(_causal_mask(t), s, NEG_BIG)
                m_prev = m_sc[g]
                m_new = jnp.maximum(m_prev, jnp.max(s, axis=-1, keepdims=True))
                p = jnp.exp(s - m_new)
                acc_sc[g] = jnp.exp(m_prev - m_new) * acc_sc[g] + jnp.dot(p.astype(BF16), v_ref[g],
                                                                          preferred_element_type=F32)
                m_sc[g] = m_new

        @pl.when(j < i)
        def _():
            step(False)

        @pl.when(j == i)
        def _():
            step(True)
            for g in range(G):
                acc = acc_sc[g]
                l = acc[:, COL_L:COL_L + 1]
                o_ref[g] = acc / l
                lse_ref[g] = m_sc[g] + jnp.log(l)

    qs = pl.BlockSpec((G, t, W), lambda h, i, j: (h, i, 0))
    ks = pl.BlockSpec((G, t, W), lambda h, i, j: (h, jnp.minimum(j, i), 0))
    col = pl.BlockSpec((G, t, 1), lambda h, i, j: (h, i, 0))
    return pl.pallas_call(
        body, name=name, grid=(H // G, n, n), in_specs=[qs, ks, ks], out_specs=[qs, col],
        out_shape=[jax.ShapeDtypeStruct((H, S, W), F32), jax.ShapeDtypeStruct((H, S, 1), F32)],
        scratch_shapes=[pltpu.VMEM((G, t, 1), F32), pltpu.VMEM((G, t, W), F32)],
        compiler_params=_params("parallel", "parallel", "arbitrary"),
    )(q, k, v)


def fox_delta(do, o, *, name):
    H, S, Dh = o.shape
    t = _pick(S, (2048, 1024, 512, 256, 128))

    def body(do_ref, o_ref, d_ref):
        d_ref[0] = jnp.sum(do_ref[0] * o_ref[0], axis=-1, keepdims=True)

    blk = pl.BlockSpec((1, t, Dh), lambda h, i: (h, i, 0))
    return pl.pallas_call(
        body, name=name, grid=(H, S // t), in_specs=[blk, blk],
        out_specs=pl.BlockSpec((1, t, 1), lambda h, i: (h, i, 0)),
        out_shape=jax.ShapeDtypeStruct((H, S, 1), F32), compiler_params=_params("parallel", "parallel"),
    )(do, o)


def fox_bwd_dq(q, k, v, do, lse, delta, *, name):
    H, S, W = q.shape
    t = _fox_tile(S)
    n = S // t
    G = FOX_GROUP

    def body(q_ref, k_ref, v_ref, do_ref, lse_ref, dl_ref, dq_ref, acc_sc):
        i, j = pl.program_id(1), pl.program_id(2)

        @pl.when(j == 0)
        def _():
            acc_sc[...] = jnp.zeros_like(acc_sc)

        def step(masked):
            for g in range(G):
                s = lax.dot_general(q_ref[g], k_ref[g], _NT, preferred_element_type=F32)
                p = jnp.exp(s - lse_ref[g])
                if masked:
                    p = jnp.where(_causal_mask(t), p, 0.0)
                dp = lax.dot_general(do_ref[g], v_ref[g], _NT, preferred_element_type=F32)
                ds = p * (dp - dl_ref[g])
                acc_sc[g] += jnp.dot(ds.astype(BF16), k_ref[g], preferred_element_type=F32)

        @pl.when(j < i)
        def _():
            step(False)

        @pl.when(j == i)
        def _():
            step(True)
            dq_ref[...] = acc_sc[...]

    qs = pl.BlockSpec((G, t, W), lambda h, i, j: (h, i, 0))
    ks = pl.BlockSpec((G, t, W), lambda h, i, j: (h, jnp.minimum(j, i), 0))
    col = pl.BlockSpec((G, t, 1), lambda h, i, j: (h, i, 0))
    return pl.pallas_call(
        body, name=name, grid=(H // G, n, n), in_specs=[qs, ks, ks, qs, col, col], out_specs=qs,
        out_shape=jax.ShapeDtypeStruct((H, S, W), F32),
        scratch_shapes=[pltpu.VMEM((G, t, W), F32)],
        compiler_params=_params("parallel", "parallel", "arbitrary"),
    )(q, k, v, do, lse, delta)


def fox_bwd_dkv(q, k, v, do, lse_row, delta_row, *, name):
    H, S, W = q.shape
    t = _fox_tile(S)
    n = S // t
    G = FOX_GROUP

    def body(q_ref, k_ref, v_ref, do_ref, lse_ref, dl_ref, dk_ref, dv_ref, dk_sc, dv_sc):
        j, i = pl.program_id(1), pl.program_id(2)

        @pl.when(i == 0)
        def _():
            dk_sc[...] = jnp.zeros_like(dk_sc)
            dv_sc[...] = jnp.zeros_like(dv_sc)

        def step(masked):
            for g in range(G):
                st = lax.dot_general(k_ref[g], q_ref[g], _NT, preferred_element_type=F32)
                pt = jnp.exp(st - lse_ref[g])
                if masked:
                    pt = jnp.where(_causal_mask(t, keys_on_rows=True), pt, 0.0)
                dpt = lax.dot_general(v_ref[g], do_ref[g], _NT, preferred_element_type=F32)
                dst = pt * (dpt - dl_ref[g])
                dv_sc[g] += jnp.dot(pt.astype(BF16), do_ref[g], preferred_element_type=F32)
                dk_sc[g] += jnp.dot(dst.astype(BF16), q_ref[g], preferred_element_type=F32)

        @pl.when(i > j)
        def _():
            step(False)

        @pl.when(i == j)
        def _():
            step(True)

        @pl.when(i == n - 1)
        def _():
            dk_ref[...] = dk_sc[...]
            dv_ref[...] = dv_sc[...]

    qs = pl.BlockSpec((G, t, W), lambda h, j, i: (h, jnp.maximum(i, j), 0))
    ks = pl.BlockSpec((G, t, W), lambda h, j, i: (h, j, 0))
    qrow = pl.BlockSpec((G, 1, t), lambda h, j, i: (h, 0, jnp.maximum(i, j)))
    return pl.pallas_call(
        body, name=name, grid=(H // G, n, n), in_specs=[qs, ks, ks, qs, qrow, qrow], out_specs=[ks, ks],
        out_shape=[jax.ShapeDtypeStruct((H, S, W), F32)] * 2,
        scratch_shapes=[pltpu.VMEM((G, t, W), F32), pltpu.VMEM((G, t, W), F32)],
        compiler_params=_params("parallel", "parallel", "arbitrary"),
    )(q, k, v, do, lse_row, delta_row)


def merge_fwd(proj, branches, D, *, name):
    S = proj.shape[0]
    ts = _pick(S, (256, 128))

    def body(g0, g1, g2, b0, b1, b2, o_ref):
        acc = _sig(g0[...]) * b0[...] + _sig(g1[...]) * b1[...] + _sig(g2[...]) * b2[...]
        o_ref[...] = acc.astype(BF16)

    gate = lambda n: pl.BlockSpec((ts, D), lambda i: (i, n))
    row = pl.BlockSpec((ts, D), lambda i: (i, 0))
    return pl.pallas_call(
        body, name=name, grid=(S // ts,), in_specs=[gate(0), gate(1), gate(2), row, row, row], out_specs=row,
        out_shape=jax.ShapeDtypeStruct((S, D), BF16), compiler_params=_params("parallel"),
    )(proj, proj, proj, *branches)


def merge_bwd(proj, branches, dm, D, *, name):
    S = proj.shape[0]
    ts = _pick(S, (256, 128))

    def body(g0, g1, g2, b0, b1, b2, dm_ref, db0, db1, db2, dg0, dg1, dg2):
        dmv = dm_ref[...]
        for g_ref, b_ref, db_ref, dg_ref in ((g0, b0, db0, dg0), (g1, b1, db1, dg1), (g2, b2, db2, dg2)):
            s = _sig(g_ref[...])
            db_ref[...] = (dmv * s).astype(BF16)
            dg_ref[...] = (dmv * b_ref[...] * (s * (1.0 - s))).astype(BF16)

    gate = lambda n: pl.BlockSpec((ts, D), lambda i: (i, n))
    row = pl.BlockSpec((ts, D), lambda i: (i, 0))
    return pl.pallas_call(
        body, name=name, grid=(S // ts,), in_specs=[gate(0), gate(1), gate(2), row, row, row, row],
        out_specs=[row] * 6, out_shape=[jax.ShapeDtypeStruct((S, D), BF16)] * 6,
        compiler_params=_params("parallel"),
    )(proj, proj, proj, *branches, dm)


def xa_fwd(q, k, v, *, name):
    S, D = q.shape
    M = k.shape[0]
    dh = D // XA_HEADS
    scale = dh ** -0.5
    t = _pick(S, (512, 256, 128))

    def body(q_ref, k_ref, v_ref, o_ref):
        for h in range(XA_HEADS):
            cols = slice(h * dh, (h + 1) * dh)
            s = lax.dot_general(q_ref[:, cols], k_ref[:, cols], (((1,), (1,)), ((), ())),
                                preferred_element_type=F32) * scale
            p = jnp.exp(s - jnp.max(s, axis=-1, keepdims=True))
            p = p / jnp.sum(p, axis=-1, keepdims=True)
            o_ref[:, cols] = jnp.dot(p.astype(BF16), v_ref[:, cols], preferred_element_type=F32).astype(BF16)

    row = pl.BlockSpec((t, D), lambda i: (i, 0))
    full = pl.BlockSpec((M, D), lambda i: (0, 0))
    return pl.pallas_call(
        body, name=name, grid=(S // t,), in_specs=[row, full, full], out_specs=row,
        out_shape=jax.ShapeDtypeStruct((S, D), BF16), compiler_params=_params("parallel"),
    )(q, k, v)


def xa_bwd(q, k, v, do, *, name):
    S, D = q.shape
    M = k.shape[0]
    dh = D // XA_HEADS
    scale = dh ** -0.5
    t = _pick(S, (512, 256, 128))

    def body(q_ref, k_ref, v_ref, do_ref, dq_ref, dk_ref, dv_ref):
        i = pl.program_id(0)

        @pl.when(i == 0)
        def _():
            dk_ref[...] = jnp.zeros_like(dk_ref)
            dv_ref[...] = jnp.zeros_like(dv_ref)

        for h in range(XA_HEADS):
            cols = slice(h * dh, (h + 1) * dh)
            qh, kh, vh = q_ref[:, cols], k_ref[:, cols], v_ref[:, cols]
            dob = do_ref[:, cols].astype(BF16)
            s = lax.dot_general(qh, kh, (((1,), (1,)), ((), ())), preferred_element_type=F32) * scale
            p = jnp.exp(s - jnp.max(s, axis=-1, keepdims=True))
            p = p / jnp.sum(p, axis=-1, keepdims=True)
            dp = lax.dot_general(dob, vh, (((1,), (1,)), ((), ())), preferred_element_type=F32)
            ds = (p * (dp - jnp.sum(p * dp, axis=-1, keepdims=True)) * scale).astype(BF16)
            dq_ref[:, cols] = jnp.dot(ds, kh, preferred_element_type=F32).astype(BF16)
            dk_ref[:, cols] += lax.dot_general(ds, qh, (((0,), (0,)), ((), ())), preferred_element_type=F32)
            dv_ref[:, cols] += lax.dot_general(p.astype(BF16), dob, (((0,), (0,)), ((), ())),
                                               preferred_element_type=F32)

    row = pl.BlockSpec((t, D), lambda i: (i, 0))
    full = pl.BlockSpec((M, D), lambda i: (0, 0))
    return pl.pallas_call(
        body, name=name, grid=(S // t,), in_specs=[row, full, full, row], out_specs=[row, full, full],
        out_shape=[jax.ShapeDtypeStruct((S, D), BF16), jax.ShapeDtypeStruct((M, D), F32),
                   jax.ShapeDtypeStruct((M, D), F32)],
        compiler_params=_params("arbitrary"),
    )(q, k, v, do)


def _unheads(a):
    return a[:, :, :FOX_HEAD_DIM].transpose(1, 0, 2).reshape(a.shape[1], FOX_HEADS * FOX_HEAD_DIM)


def mixer_fwd(x, w, tag):
    S, D = x.shape
    h = rms_fwd(x, w["mix_norm"], name=f"{tag}_rms")
    proj = mm(h, w["w_in"], name=f"{tag}_proj")
    y_a = conv_fwd(proj, w["conv_w"], D, name=f"{tag}_conv")
    y_b = sg_fwd(proj, w["sg_norm"], w["sg_w"], w["sg_bt"], D, name=f"{tag}_sg")
    c = fox_cumlog(proj, w["fox_b_f"], D, name=f"{tag}_cumlog")
    seg = lambda s: proj[:, 3 * D + s * SEG:3 * D + (s + 1) * SEG]
    qh, kh, vh = fox_operands(seg(SEG_FQ), seg(SEG_FK), seg(SEG_FV), c[:, :FOX_HEADS], FOX_HEAD_DIM ** -0.5)
    o, lse = fox_fwd(qh, kh, vh, name=f"{tag}_fox")
    y_c = _unheads(o).astype(BF16)
    ys = (y_a, y_b, y_c)
    branches = [mm(ys[n], w["w_branch"][n], name=f"{tag}_branch{n}") for n in range(N_BRANCH)]
    merged = merge_fwd(proj, branches, D, name=f"{tag}_merge")
    y = mm(merged, w["w_out"], name=f"{tag}_out", res=x)
    return y, (x, h, proj, ys, qh, kh, vh, o, lse, branches, merged)


def mixer_bwd(dx, saved, w, tag):
    x, h, proj, ys, qh, kh, vh, o, lse, branches, merged = saved
    S, D = x.shape
    grads = {"w_out": mm_tn(merged, dx, name=f"{tag}_dwout")}
    dmerged = mm(dx, w["w_out_t"], name=f"{tag}_dmerged")
    outs = merge_bwd(proj, branches, dmerged, D, name=f"{tag}_dmerge")
    dbr, dgl = outs[:3], outs[3:]
    grads["w_branch"] = jnp.stack([mm_tn(ys[n], dbr[n], name=f"{tag}_dwbr{n}") for n in range(N_BRANCH)])
    dys = [mm(dbr[n], w["w_branch_t"][n], name=f"{tag}_dy{n}") for n in range(N_BRANCH)]
    d_ab, d_ac, d_ah, grads["conv_w"] = conv_bwd(proj, w["conv_w"], dys[0], D, name=f"{tag}_dconv")
    d_u, d_v, grads["sg_w"], d_sgb, grads["sg_norm"] = sg_bwd(
        proj, w["sg_norm"], w["sg_w"], w["sg_bt"], dys[1], D, name=f"{tag}_dsg")
    grads["sg_b"] = d_sgb[:, :, 0]
    do = dys[2].astype(BF16).reshape(S, FOX_HEADS, FOX_HEAD_DIM).transpose(1, 0, 2)
    do = jnp.pad(do, ((0, 0), (0, 0), (0, FOX_PAD - FOX_HEAD_DIM)))
    delta = fox_delta(do, o, name=f"{tag}_delta")
    dq = fox_bwd_dq(qh, kh, vh, do, lse, delta, name=f"{tag}_dq")
    dk, dv = fox_bwd_dkv(qh, kh, vh, do, lse.reshape(FOX_HEADS, 1, S), delta.reshape(FOX_HEADS, 1, S),
                         name=f"{tag}_dkv")
    dc_rows = jnp.pad((dq[:, :, COL_ROWSUM] - dk[:, :, COL_C]).T, ((0, 0), (0, LANES - FOX_HEADS)))
    d_fl, d_bf = fox_dlogit(proj, w["fox_b_f"], dc_rows, D, name=f"{tag}_dflogit")
    grads["fox_b_f"] = d_bf[0, :FOX_HEADS]
    dproj = jnp.concatenate(
        list(dgl) + [d_ab, d_ac, d_ah, d_u, d_v, (_unheads(dq) * FOX_HEAD_DIM ** -0.5).astype(BF16),
                     _unheads(dk).astype(BF16), _unheads(dv).astype(BF16), d_fl], axis=1)
    grads["w_in"] = mm_tn(h, dproj, name=f"{tag}_dwin")
    dh = mm(dproj, w["w_in_t"], name=f"{tag}_dh")
    dx, grads["mix_norm"] = rms_bwd(x, w["mix_norm"], dh, dx, name=f"{tag}_drms")
    return dx, grads


def xattn_fwd(x, mem, w, tag):
    h = rms_fwd(x, w["xa_norm"], name=f"{tag}_rms")
    m = rms_fwd(mem, w["mem_norm"], name=f"{tag}_mrms")
    q = mm(h, w["xa_wq"], name=f"{tag}_q", out_dtype=BF16)
    k = mm(m, w["xa_wk"], name=f"{tag}_k", out_dtype=BF16)
    v = mm(m, w["xa_wv"], name=f"{tag}_v", out_dtype=BF16)
    o = xa_fwd(q, k, v, name=f"{tag}_attn")
    y = mm(o, w["xa_wo"], name=f"{tag}_o", res=x)
    return y, (x, h, m, q, k, v, o)


def xattn_bwd(dx, mem, saved, w, tag):
    x, h, m, q, k, v, o = saved
    grads = {"xa_wo": mm_tn(o, dx, name=f"{tag}_dwo")}
    do = mm(dx, w["xa_wo_t"], name=f"{tag}_do")
    dq, dk, dv = xa_bwd(q, k, v, do, name=f"{tag}_dattn")
    grads["xa_wq"] = mm_tn(h, dq, name=f"{tag}_dwq")
    grads["xa_wk"] = mm_tn(m, dk, name=f"{tag}_dwk")
    grads["xa_wv"] = mm_tn(m, dv, name=f"{tag}_dwv")
    dh = mm(dq, w["xa_wq_t"], name=f"{tag}_dh")
    dm = mm(dk, w["xa_wk_t"], name=f"{tag}_dm1")
    dm = mm(dv, w["xa_wv_t"], name=f"{tag}_dm2", res=dm)
    _, grads["mem_norm"] = rms_bwd(mem, w["mem_norm"], dm, None, name=f"{tag}_dmrms")
    dx, grads["xa_norm"] = rms_bwd(x, w["xa_norm"], dh, dx, name=f"{tag}_drms")
    return dx, grads


def _mesh_pos():
    return lax.axis_index("x"), lax.axis_index("y"), lax.axis_index("c")


def _flip(v, bit):
    return 1 - v if bit else v


def all_gather(xs, *, name):
    n = len(xs)

    def body(*refs):
        x_refs, out_refs = refs[:n], refs[n:2 * n]
        send_sems, recv_sems, local_sems = refs[2 * n:]
        mx, my, mc = _mesh_pos()
        me, sibling = (mx, my, mc), (mx, my, 1 - mc)
        chips = [(1 - mx, my), (mx, 1 - my), (1 - mx, 1 - my)]

        def copy(a, k, block, to, from_input=False):
            slot = out_refs[a].at[4 * block[0] + 2 * block[1] + block[2]]
            return pltpu.make_async_remote_copy(
                src_ref=x_refs[a] if from_input else slot, dst_ref=slot,
                send_sem=send_sems.at[a, k], recv_sem=recv_sems.at[a, k], device_id=to, device_id_type=MESH_IDS)

        started = []
        for a in range(n):
            mine = pltpu.make_async_copy(x_refs[a], out_refs[a].at[4 * mx + 2 * my + mc], local_sems.at[a])
            mine.start()
            started.append(mine)
        sends = []
        for a in range(n):
            sends.append(copy(a, 0, me, sibling, from_input=True))
            sends += [copy(a, 1 + j, me, (*chip, mc), from_input=True) for j, chip in enumerate(chips)]
        for cp in sends:
            cp.start()
        for j, chip in enumerate(chips):
            for a in range(n):
                copy(a, 1 + j, (*chip, mc), me).wait_recv()
                onward = copy(a, 4 + j, (*chip, mc), sibling)
                onward.start()
                sends.append(onward)
        for a in range(n):
            copy(a, 0, sibling, me).wait_recv()
            for j, chip in enumerate(chips):
                copy(a, 4 + j, (*chip, 1 - mc), me).wait_recv()
        for cp in sends:
            cp.wait_send()
        for mine in started:
            mine.wait()

    any_spec = pl.BlockSpec(memory_space=pl.ANY)
    return pl.pallas_call(
        body, name=name, out_shape=[jax.ShapeDtypeStruct((N_DEV,) + x.shape, x.dtype) for x in xs],
        in_specs=[any_spec] * n, out_specs=[any_spec] * n,
        scratch_shapes=[pltpu.SemaphoreType.DMA((n, 7)), pltpu.SemaphoreType.DMA((n, 7)),
                        pltpu.SemaphoreType.DMA((n,))],
    )(*xs)


def all_to_all(gs, *, name):
    n = len(gs)

    def body(*refs):
        g_refs, out_refs = refs[:n], refs[n:2 * n]
        send_sems, recv_sems, local_sems = refs[2 * n:]
        mx, my, mc = _mesh_pos()
        me = 4 * mx + 2 * my + mc
        started, copies = [], []
        for a in range(n):
            mine = pltpu.make_async_copy(g_refs[a].at[me], out_refs[a].at[me], local_sems.at[a])
            mine.start()
            started.append(mine)
        for k in range(1, N_DEV):
            peer = (_flip(mx, k & 4), _flip(my, k & 2), _flip(mc, k & 1))
            peer_slot = 4 * peer[0] + 2 * peer[1] + peer[2]
            for a in range(n):
                sems = dict(send_sem=send_sems.at[a, k - 1], recv_sem=recv_sems.at[a, k - 1], device_id=peer,
                            device_id_type=MESH_IDS)
                send = pltpu.make_async_remote_copy(src_ref=g_refs[a].at[peer_slot], dst_ref=out_refs[a].at[me], **sems)
                arrive = pltpu.make_async_remote_copy(src_ref=g_refs[a].at[peer_slot],
                                                      dst_ref=out_refs[a].at[peer_slot], **sems)
                send.start()
                copies.append((send, arrive))
        for send, arrive in copies:
            arrive.wait_recv()
        for send, arrive in copies:
            send.wait_send()
        for mine in started:
            mine.wait()

    any_spec = pl.BlockSpec(memory_space=pl.ANY)
    return pl.pallas_call(
        body, name=name, out_shape=[jax.ShapeDtypeStruct(g.shape, g.dtype) for g in gs],
        in_specs=[any_spec] * n, out_specs=[any_spec] * n,
        scratch_shapes=[pltpu.SemaphoreType.DMA((n, 7)), pltpu.SemaphoreType.DMA((n, 7)),
                        pltpu.SemaphoreType.DMA((n,))],
    )(*gs)


ADAM_BLOCK_ELEMS = 256 * 1024


def reduce_adamw(parts, row0, w, m, v, *, name):
    R, C = w.shape
    n_parts = parts.shape[0]
    tr = R
    for cand in (512, 256, 128, 64, 32, 16):
        if R % cand == 0 and row0 % cand == 0 and cand * C <= ADAM_BLOCK_ELEMS:
            tr = cand
            break
    assert row0 % tr == 0 and (tr % 16 == 0 or (row0 == 0 and parts.shape[1] == R))
    bc1 = 1.0 - ADAM_B1 ** ADAM_STEP
    bc2 = 1.0 - ADAM_B2 ** ADAM_STEP

    def body(p_ref, w_ref, m_ref, v_ref, g_ref, d_ref, nm_ref, nv_ref):
        g = p_ref[0].astype(F32)
        for d in range(1, n_parts):
            g = g + p_ref[d].astype(F32)
        nm = ADAM_B1 * m_ref[...] + (1.0 - ADAM_B1) * g
        nv = ADAM_B2 * v_ref[...] + (1.0 - ADAM_B2) * (g * g)
        m_hat = nm / bc1
        v_hat = nv / bc2
        g_ref[...] = g
        d_ref[...] = -ADAM_LR * (m_hat / (jnp.sqrt(v_hat) + ADAM_EPS) + ADAM_WD * w_ref[...])
        nm_ref[...] = nm
        nv_ref[...] = nv

    row = pl.BlockSpec((tr, C), lambda i: (i, 0))
    blk0 = row0 // tr
    return pl.pallas_call(
        body, name=name, grid=(R // tr,),
        in_specs=[pl.BlockSpec((n_parts, tr, C), lambda i: (0, blk0 + i, 0)), row, row, row],
        out_specs=[row] * 4, out_shape=[jax.ShapeDtypeStruct((R, C), F32)] * 4,
        compiler_params=_params("parallel"),
    )(parts, w, m, v)


SHARDED = {
    "ffn1_w_gate": 2, "ffn1_w_up": 2, "ffn1_w_down": 1, "w_in": 2, "conv_w": 2, "w_branch": 3, "w_out": 1,
    "xa_wq": 1, "xa_wk": 1, "xa_wv": 1, "xa_wo": 1, "ffn2_w_gate": 2, "ffn2_w_up": 2, "ffn2_w_down": 1,
}
GROUPS = (("ffn1_w_gate", "ffn1_w_up", "ffn2_w_gate", "ffn2_w_up"),
          ("ffn1_w_down", "ffn2_w_down", "w_out", "xa_wq", "xa_wk", "xa_wv", "xa_wo"),
          ("w_in",), ("w_branch",))
REPLICATED = ("ffn1_norm", "mix_norm", "sg_norm", "sg_w", "sg_b", "fox_b_f", "xa_norm", "mem_norm", "ffn2_norm",
              "final_norm")
WEIGHTS = ("ffn1_norm", "ffn1_w_gate", "ffn1_w_up", "ffn1_w_down", "mix_norm", "w_in", "conv_w", "sg_norm", "sg_w",
           "sg_b", "fox_b_f", "w_branch", "w_out", "xa_norm", "mem_norm", "xa_wq", "xa_wk", "xa_wv", "xa_wo",
           "ffn2_norm", "ffn2_w_gate", "ffn2_w_up", "ffn2_w_down", "final_norm")
PACK_ROWS = 1024


def _rows(a):
    return a.reshape(-1, a.shape[-1])


def _pack(arrays, dtype):
    flat = jnp.concatenate([a.reshape(-1).astype(dtype) for a in arrays])
    n = flat.shape[0]
    unit = PACK_ROWS * LANES
    total = -(-n // unit) * unit
    return jnp.pad(flat, (0, total - n)).reshape(total // LANES, LANES)


def _unpack(buf, shapes):
    flat = buf.reshape(-1)
    out, off = [], 0
    for shp in shapes:
        n = 1
        for s in shp:
            n *= s
        out.append(flat[off:off + n].reshape(tuple(shp)))
        off += n
    return out


def _to_dev_major(full, axis):
    shp = full.shape
    a = full.reshape(shp[:axis] + (N_DEV, shp[axis] // N_DEV) + shp[axis + 1:])
    return jnp.moveaxis(a, axis, 0)


def _from_dev_major(a, axis):
    a = jnp.moveaxis(a, 0, axis)
    shp = a.shape
    return a.reshape(shp[:axis] + (shp[axis] * shp[axis + 1],) + shp[axis + 2:])


def _relayout_w_in(w_in, D):
    main = 8 * SEG
    pad = jnp.zeros((w_in.shape[0], SEG - FOX_HEADS), w_in.dtype)
    return jnp.concatenate([w_in[:, main + FOX_HEADS:], w_in[:, :main], w_in[:, main:main + FOX_HEADS], pad], axis=1)


def _unlayout_w_in(g, D):
    return jnp.concatenate([g[:, 3 * D:3 * D + 8 * SEG], g[:, 3 * D + 8 * SEG:3 * D + 8 * SEG + FOX_HEADS],
                            g[:, :3 * D]], axis=1)


def _layer_weights(full, rep, l, D):
    t = lambda a: a.T
    w_in = _relayout_w_in(full["w_in"][l], D)
    ffn = {}
    for tag in ("ffn1", "ffn2"):
        ffn[tag] = {"norm": rep[f"{tag}_norm"][l][None, :]}
        for n in ("w_gate", "w_up", "w_down"):
            ffn[tag][n] = full[f"{tag}_{n}"][l]
            ffn[tag][n + "_t"] = t(full[f"{tag}_{n}"][l])
    mix = {
        "mix_norm": rep["mix_norm"][l][None, :], "w_in": w_in, "w_in_t": t(w_in),
        "conv_w": full["conv_w"][l], "sg_norm": rep["sg_norm"][l][None, :], "sg_w": rep["sg_w"][l],
        "sg_bt": rep["sg_b"][l][:, :, None],
        "fox_b_f": jnp.pad(rep["fox_b_f"][l], (0, LANES - FOX_HEADS))[None, :],
        "w_branch": full["w_branch"][l], "w_branch_t": jnp.swapaxes(full["w_branch"][l], 1, 2),
        "w_out": full["w_out"][l], "w_out_t": t(full["w_out"][l]),
    }
    xa = {"xa_norm": rep["xa_norm"][l][None, :], "mem_norm": rep["mem_norm"][l][None, :]}
    for n in ("xa_wq", "xa_wk", "xa_wv", "xa_wo"):
        xa[n] = full[n][l]
        xa[n + "_t"] = t(full[n][l])
    return ffn, mix, xa


def kernel(x, mem, ffn1_norm, ffn1_w_gate, ffn1_w_up, ffn1_w_down, mix_norm, w_in, conv_w, sg_norm, sg_w, sg_b, fox_b_f, w_branch, w_out, xa_norm, mem_norm, xa_wq, xa_wk, xa_wv, xa_wo, ffn2_norm, ffn2_w_gate, ffn2_w_up, ffn2_w_down, final_norm, loss_target, m_ffn1_norm, m_ffn1_w_gate, m_ffn1_w_up, m_ffn1_w_down, m_mix_norm, m_w_in, m_conv_w, m_sg_norm, m_sg_w, m_sg_b, m_fox_b_f, m_w_branch, m_w_out, m_xa_norm, m_mem_norm, m_xa_wq, m_xa_wk, m_xa_wv, m_xa_wo, m_ffn2_norm, m_ffn2_w_gate, m_ffn2_w_up, m_ffn2_w_down, m_final_norm, v_ffn1_norm, v_ffn1_w_gate, v_ffn1_w_up, v_ffn1_w_down, v_mix_norm, v_w_in, v_conv_w, v_sg_norm, v_sg_w, v_sg_b, v_fox_b_f, v_w_branch, v_w_out, v_xa_norm, v_mem_norm, v_xa_wq, v_xa_wk, v_xa_wv, v_xa_wo, v_ffn2_norm, v_ffn2_w_gate, v_ffn2_w_up, v_ffn2_w_down, v_final_norm):
    args = locals()
    wts = {n: args[n] for n in WEIGHTS}
    mom = {n: args["m_" + n] for n in WEIGHTS}
    var = {n: args["v_" + n] for n in WEIGHTS}
    depth = ffn1_norm.shape[0]
    S, D = x.shape[1], x.shape[2]
    xs, ms, tgt = x[0], mem[0], loss_target[0]

    outs = all_gather([jnp.concatenate([_rows(wts[n]).astype(BF16) for n in grp]) for grp in GROUPS]
                      + [_rows(conv_w)], name="gather_weights")
    full, row0 = {}, {}
    for grp, got in zip(GROUPS + (("conv_w",),), outs):
        off = 0
        for n in grp:
            rows = _rows(wts[n]).shape[0]
            row0[n] = off
            full[n] = _from_dev_major(got[:, off:off + rows].reshape((N_DEV,) + wts[n].shape), SHARDED[n])
            off += rows
    rep = {n: wts[n] for n in REPLICATED}
    layers = [_layer_weights(full, rep, l, D) for l in range(depth)]

    saved = []
    h = xs
    for l, (ffn, mix, xa) in enumerate(layers):
        h, s1 = ffn_fwd(h, ffn["ffn1"], f"l{l}_ffn1")
        h, s2 = mixer_fwd(h, mix, f"l{l}_mix")
        h, s3 = xattn_fwd(h, ms, xa, f"l{l}_xa")
        h, s4 = ffn_fwd(h, ffn["ffn2"], f"l{l}_ffn2")
        saved.append((s1, s2, s3, s4))
    dx, d_final, loss_cols = final_loss_bwd(h, final_norm[None, :], tgt, name="final_loss")
    loss = lax.psum(0.5 * jnp.sum(loss_cols) / D, ("x", "y", "c"))

    per_layer = []
    for l in reversed(range(depth)):
        ffn, mix, xa = layers[l]
        s1, s2, s3, s4 = saved[l]
        g = {}
        dx, g4 = ffn_bwd(dx, s4, ffn["ffn2"], f"l{l}_ffn2")
        dx, g3 = xattn_bwd(dx, ms, s3, xa, f"l{l}_xa")
        dx, g2 = mixer_bwd(dx, s2, mix, f"l{l}_mix")
        dx, g1 = ffn_bwd(dx, s1, ffn["ffn1"], f"l{l}_ffn1")
        for tag, gg in (("ffn1", g1), ("ffn2", g4)):
            for n in ("w_gate", "w_up", "w_down"):
                g[f"{tag}_{n}"] = gg[n]
            g[f"{tag}_norm"] = gg["norm"][0]
        g.update(g3)
        g["xa_norm"], g["mem_norm"] = g3["xa_norm"][0], g3["mem_norm"][0]
        g.update({k: v for k, v in g2.items() if k != "w_in"})
        g["w_in"] = _unlayout_w_in(g2["w_in"], D)
        g["mix_norm"], g["sg_norm"] = g2["mix_norm"][0], g2["sg_norm"][0]
        per_layer.append(g)
    per_layer.reverse()
    grads = {n: jnp.stack([per_layer[l][n] for l in range(depth)]) for n in WEIGHTS if n != "final_norm"}
    grads["final_norm"] = d_final[0]

    def dev_major_rows(n):
        a = _to_dev_major(grads[n], SHARDED[n])
        return a.reshape(N_DEV, -1, a.shape[-1]).astype(BF16)

    parts = all_to_all([jnp.concatenate([dev_major_rows(n) for n in grp], axis=1) for grp in GROUPS],
                       name="scatter_grads")
    res = {k: {} for k in ("g", "d", "m", "v")}

    def adamw(n, parts_n, off):
        outs = reduce_adamw(parts_n, off, _rows(wts[n]), _rows(mom[n]), _rows(var[n]), name=f"adamw_{n}")
        for k, o in zip(("g", "d", "m", "v"), outs):
            res[k][n] = o.reshape(wts[n].shape)

    for grp, got in zip(GROUPS, parts):
        for n in grp:
            adamw(n, got, row0[n])

    small = list(REPLICATED)
    shapes = [wts[n].shape for n in small]
    conv_zero = jnp.zeros(grads["conv_w"].shape, F32)
    parts = all_gather([_pack([grads[n] for n in small] + [grads["conv_w"]], F32)], name="gather_small_grads")[0]
    outs = reduce_adamw(parts, 0, _pack([wts[n] for n in small] + [conv_zero], F32),
                        _pack([mom[n] for n in small] + [conv_zero], F32),
                        _pack([var[n] for n in small] + [conv_zero], F32), name="adamw_replicated")
    for k, o in zip(("g", "d", "m", "v"), outs):
        res[k].update(dict(zip(small, _unpack(o, shapes + [conv_zero.shape])[:-1])))
    conv_g = _unpack(outs[0], shapes + [conv_zero.shape])[-1]
    me = 4 * lax.axis_index("x") + 2 * lax.axis_index("y") + lax.axis_index("c")
    width = conv_w.shape[-1]
    conv_g = lax.dynamic_slice_in_dim(conv_g, me * width, width, axis=2)
    adamw("conv_w", _rows(conv_g)[None], 0)

    return (loss, dx[None], *[res["g"][n] for n in WEIGHTS], *[res["d"][n] for n in WEIGHTS],
            *[res["m"][n] for n in WEIGHTS], *[res["v"][n] for n in WEIGHTS])
```

```python
import functools

import jax
import jax.numpy as jnp
from jax import lax
from jax.experimental import pallas as pl
from jax.experimental.pallas import tpu as pltpu

F32 = jnp.float32
BF16 = jnp.bfloat16

N_DEV = 8
RMS_EPS = 1e-6
SEG = 512
FOX_HEADS = 8
FOX_HEAD_DIM = 64
SG_GROUPS = 4
CHUNK = 128
XA_HEADS = 4
N_BRANCH = 3
LANES = 128
VMEM_LIMIT_BYTES = 48 * 1024 * 1024
NEG_BIG = -1e30

ADAM_LR = 0.001
ADAM_B1 = 0.9
ADAM_B2 = 0.999
ADAM_EPS = 1e-08
ADAM_WD = 0.01
ADAM_STEP = 10

_GELU_K = 0.7978845608028654
_GELU_C = 0.044715

MESH_IDS = pl.DeviceIdType.MESH


def _pick(n, candidates):
    for c in candidates:
        if c <= n and n % c == 0:
            return c
    return n


def _params(*sem):
    return pltpu.CompilerParams(dimension_semantics=sem, vmem_limit_bytes=VMEM_LIMIT_BYTES)


def _sig(x):
    return 1.0 / (1.0 + jnp.exp(-x))


def _gelu(x):
    t = jnp.tanh(_GELU_K * (x + _GELU_C * x * x * x))
    return 0.5 * x * (1.0 + t), t


def _gelu_grad(x, t):
    return 0.5 * (1.0 + t) + 0.5 * x * (1.0 - t * t) * _GELU_K * (1.0 + 3.0 * _GELU_C * x * x)


_WIDE_TILES = (1536, 1408, 1280, 1024, 768, 512, 384, 256, 128)
MM_VMEM_BUDGET = 36 * 1024 * 1024
MM_ACC_BYTES = 6 * 1024 * 1024 + 512 * 1024


def mm(a, b, *, name, out_dtype=F32, res=None, scale=1.0, extras=(), epilogue=None, out_dtypes=None, tm=1024):
    M, K = a.shape
    K2, N = b.shape
    assert K == K2
    custom = epilogue is not None
    if not custom:
        extras = () if res is None else (res,)
        out_dtypes = (out_dtype,)

        def epilogue(acc, *ex):
            if scale != 1.0:
                acc = acc * scale
            return ((ex[0] + acc) if ex else acc,)

    n_ex, n_out = len(extras), len(out_dtypes)
    tn = _pick(N, _WIDE_TILES)
    tk = K if K <= 3072 else _pick(K, (2560, 2048, 1536, 1024, 512, 256, 128))
    nk = K // tk
    tile_bytes = sum(e.dtype.itemsize for e in extras) + sum(jnp.dtype(d).itemsize for d in out_dtypes)
    for tm in (tm, 512, 256, 128):
        blocks = 2 * (tm * tk * a.dtype.itemsize + tk * tn * 2 + tm * tn * tile_bytes)
        if M % tm == 0 and blocks + (tm * tn * 4 if nk > 1 else 0) <= MM_VMEM_BUDGET:
            break
    else:
        tm = M

    def body(*refs):
        a_ref, b_ref = refs[:2]
        ex_refs = refs[2:2 + n_ex]
        o_refs = refs[2 + n_ex:2 + n_ex + n_out]

        def finish(acc):
            for o_ref, val, dt in zip(o_refs, epilogue(acc, *[r[...] for r in ex_refs]), out_dtypes, strict=True):
                o_ref[...] = val.astype(dt)

        part = jnp.dot(a_ref[...].astype(BF16), b_ref[...].astype(BF16), preferred_element_type=F32)
        if nk == 1:
            finish(part)
        else:
            acc_ref = refs[-1]
            k = pl.program_id(2)

            @pl.when(k == 0)
            def _():
                acc_ref[...] = part

            @pl.when(k > 0)
            def _():
                acc_ref[...] += part

            @pl.when(k == nk - 1)
            def _():
                finish(acc_ref[...])

    tile = pl.BlockSpec((tm, tn), lambda i, j, k: (i, j))
    outs = pl.pallas_call(
        body, name=name, grid=(M // tm, N // tn, nk),
        in_specs=[pl.BlockSpec((tm, tk), lambda i, j, k: (i, k)), pl.BlockSpec((tk, tn), lambda i, j, k: (k, j))]
        + [tile] * n_ex,
        out_specs=[tile] * n_out, out_shape=[jax.ShapeDtypeStruct((M, N), d) for d in out_dtypes],
        scratch_shapes=[pltpu.VMEM((tm, tn), F32)] if nk > 1 else [],
        compiler_params=_params("parallel", "parallel", "arbitrary"),
    )(a, b, *extras)
    return tuple(outs) if custom else outs[0]


def mm_tn(a, b, *, name, scale=1.0):
    M, K = a.shape
    M2, N = b.shape
    assert M == M2
    tm = _pick(M, (1024, 512, 256, 128))
    tk = _pick(K, (1408, 1024, 512, 256, 128))
    tn = next((c for c in _WIDE_TILES if N % c == 0 and tk * c * 4 <= MM_ACC_BYTES), N)
    nm = M // tm

    def body(a_ref, b_ref, o_ref):
        m = pl.program_id(2)
        part = lax.dot_general(a_ref[...].astype(BF16), b_ref[...].astype(BF16), (((0,), (0,)), ((), ())),
                               preferred_element_type=F32)

        @pl.when(m == 0)
        def _():
            o_ref[...] = part

        @pl.when(m > 0)
        def _():
            o_ref[...] += part

        if scale != 1.0:
            @pl.when(m == nm - 1)
            def _():
                o_ref[...] = o_ref[...] * scale

    return pl.pallas_call(
        body, name=name, grid=(K // tk, N // tn, nm),
        in_specs=[pl.BlockSpec((tm, tk), lambda i, j, m: (m, i)), pl.BlockSpec((tm, tn), lambda i, j, m: (m, j))],
        out_specs=pl.BlockSpec((tk, tn), lambda i, j, m: (i, j)),
        out_shape=jax.ShapeDtypeStruct((K, N), F32),
        compiler_params=_params("parallel", "parallel", "arbitrary"),
    )(a, b)


def rms_fwd(x, g, *, name):
    S, D = x.shape
    ts = _pick(S, (512, 256, 128))

    def body(x_ref, g_ref, h_ref):
        xv = x_ref[...]
        r = lax.rsqrt(jnp.mean(xv * xv, axis=-1, keepdims=True) + RMS_EPS)
        h_ref[...] = ((xv * r) * g_ref[...]).astype(BF16)

    return pl.pallas_call(
        body, name=name, grid=(S // ts,),
        in_specs=[pl.BlockSpec((ts, D), lambda i: (i, 0)), pl.BlockSpec((1, D), lambda i: (0, 0))],
        out_specs=pl.BlockSpec((ts, D), lambda i: (i, 0)),
        out_shape=jax.ShapeDtypeStruct((S, D), BF16),
        compiler_params=_params("parallel"),
    )(x, g)


def rms_bwd(x, g, dh, dx_in, *, name):
    S, D = x.shape
    ts = _pick(S, (512, 256, 128))
    has_in = dx_in is not None

    def body(*refs):
        if has_in:
            x_ref, g_ref, dh_ref, di_ref, dx_ref, dg_ref = refs
        else:
            x_ref, g_ref, dh_ref, dx_ref, dg_ref = refs
        xv = x_ref[...]
        dh_v = dh_ref[...]
        r = lax.rsqrt(jnp.mean(xv * xv, axis=-1, keepdims=True) + RMS_EPS)
        xh = xv * r
        gd = dh_v * g_ref[...]
        dx = r * (gd - xh * jnp.mean(gd * xh, axis=-1, keepdims=True))
        if has_in:
            dx = di_ref[...] + dx
        dx_ref[...] = dx
        part = jnp.sum(dh_v * xh, axis=0, keepdims=True)

        @pl.when(pl.program_id(0) == 0)
        def _():
            dg_ref[...] = part

        @pl.when(pl.program_id(0) > 0)
        def _():
            dg_ref[...] += part

    row = pl.BlockSpec((ts, D), lambda i: (i, 0))
    vec = pl.BlockSpec((1, D), lambda i: (0, 0))
    return pl.pallas_call(
        body, name=name, grid=(S // ts,),
        in_specs=[row, vec, row] + ([row] if has_in else []),
        out_specs=[row, vec],
        out_shape=[jax.ShapeDtypeStruct((S, D), F32), jax.ShapeDtypeStruct((1, D), F32)],
        compiler_params=_params("arbitrary"),
    )(*([x, g, dh] + ([dx_in] if has_in else [])))


def final_loss_bwd(x, g, target, *, name):
    S, D = x.shape
    ts = _pick(S, (512, 256, 128))

    def body(x_ref, g_ref, t_ref, dx_ref, dg_ref, ls_ref):
        xv = x_ref[...]
        gv = g_ref[...]
        r = lax.rsqrt(jnp.mean(xv * xv, axis=-1, keepdims=True) + RMS_EPS)
        xh = xv * r
        e = xh * gv - t_ref[...]
        dy = e * (1.0 / D)
        gd = dy * gv
        dx_ref[...] = r * (gd - xh * jnp.mean(gd * xh, axis=-1, keepdims=True))
        dg_part = jnp.sum(dy * xh, axis=0, keepdims=True)
        ls_part = jnp.sum(e * e, axis=0, keepdims=True)

        @pl.when(pl.program_id(0) == 0)
        def _():
            dg_ref[...] = dg_part
            ls_ref[...] = ls_part

        @pl.when(pl.program_id(0) > 0)
        def _():
            dg_ref[...] += dg_part
            ls_ref[...] += ls_part

    row = pl.BlockSpec((ts, D), lambda i: (i, 0))
    vec = pl.BlockSpec((1, D), lambda i: (0, 0))
    return pl.pallas_call(
        body, name=name, grid=(S // ts,), in_specs=[row, vec, row], out_specs=[row, vec, vec],
        out_shape=[jax.ShapeDtypeStruct((S, D), F32), jax.ShapeDtypeStruct((1, D), F32),
                   jax.ShapeDtypeStruct((1, D), F32)],
        compiler_params=_params("arbitrary"),
    )(x, g, target)


def _swiglu(up, gp):
    return up, gp * _sig(gp) * up


def _swiglu_grad(da, gp, up):
    da = da * 0.5
    s = _sig(gp)
    return da * up * (s * (1.0 + gp * (1.0 - s))), da * (gp * s)


def ffn_fwd(x, w, tag):
    h = rms_fwd(x, w["norm"], name=f"{tag}_rms")
    gp = mm(h, w["w_gate"], name=f"{tag}_gate")
    up, a = mm(h, w["w_up"], name=f"{tag}_up", extras=(gp,), epilogue=_swiglu, out_dtypes=(F32, BF16))
    y = mm(a, w["w_down"], name=f"{tag}_down", res=x, scale=0.5)
    return y, (x, h, gp, up, a)


def ffn_bwd(dx, saved, w, tag):
    x, h, gp, up, a = saved
    grads = {"w_down": mm_tn(a, dx, name=f"{tag}_dwd", scale=0.5)}
    dgp, dup = mm(dx, w["w_down_t"], name=f"{tag}_da", extras=(gp, up), epilogue=_swiglu_grad,
                  out_dtypes=(BF16, BF16))
    grads["w_gate"] = mm_tn(h, dgp, name=f"{tag}_dwg")
    grads["w_up"] = mm_tn(h, dup, name=f"{tag}_dwu")
    dh = mm(dgp, w["w_gate_t"], name=f"{tag}_dh1")
    dh = mm(dup, w["w_up_t"], name=f"{tag}_dh2", res=dh)
    dx, grads["norm"] = rms_bwd(x, w["norm"], dh, dx, name=f"{tag}_drms")
    return dx, grads


SEG_AB, SEG_AC, SEG_AH, SEG_U, SEG_V, SEG_FQ, SEG_FK, SEG_FV, SEG_FL = range(9)
N_SEG = 9


def _seg_block(D, seg):
    return 3 * D // SEG + seg


def _shift_down(z, prev8, n, rows):
    out = pltpu.roll(z, n, 0)
    for r in range(n):
        out = jnp.where(rows == r, prev8[8 - n + r:8 - n + r + 1, :], out)
    return out


def _shift_up(z, next8, n, rows, ts):
    out = pltpu.roll(z, ts - n, 0)
    for r in range(n):
        out = jnp.where(rows == ts - n + r, next8[r:r + 1, :], out)
    return out


def conv_fwd(proj, conv_w, D, *, name):
    S = proj.shape[0]
    ts = _pick(S, (512, 256, 128))
    b0 = _seg_block(D, 0)

    def body(ab_ref, ac_ref, ah_ref, pc_ref, ph_ref, w_ref, y_ref):
        i = pl.program_id(0)
        rows = lax.broadcasted_iota(jnp.int32, (ts, 1), 0)
        z = ac_ref[...] * ah_ref[...]
        zp = pc_ref[...] * ph_ref[...] * (i > 0).astype(F32)
        w = w_ref[...]
        y = w[0:1, :] * _shift_down(z, zp, 2, rows) + w[1:2, :] * _shift_down(z, zp, 1, rows) + w[2:3, :] * z
        y_ref[...] = (ab_ref[...] * y).astype(BF16)

    def seg(s):
        return pl.BlockSpec((ts, SEG), lambda i: (i, b0 + s))

    def prev(s):
        return pl.BlockSpec((8, SEG), lambda i: (jnp.maximum(i * (ts // 8) - 1, 0), b0 + s))

    return pl.pallas_call(
        body, name=name, grid=(S // ts,),
        in_specs=[seg(SEG_AB), seg(SEG_AC), seg(SEG_AH), prev(SEG_AC), prev(SEG_AH),
                  pl.BlockSpec((3, SEG), lambda i: (0, 0))],
        out_specs=pl.BlockSpec((ts, SEG), lambda i: (i, 0)),
        out_shape=jax.ShapeDtypeStruct((S, SEG), BF16), compiler_params=_params("parallel"),
    )(proj, proj, proj, proj, proj, conv_w)


def conv_bwd(proj, conv_w, dy, D, *, name):
    S = proj.shape[0]
    ts = _pick(S, (512, 256, 128))
    nt = S // ts
    b0 = _seg_block(D, 0)

    def body(ab_ref, ac_ref, ah_ref, pc_ref, ph_ref, nb_ref, dy_ref, ndy_ref, w_ref,
             dab_ref, dac_ref, dah_ref, dw_ref):
        i = pl.program_id(0)
        rows = lax.broadcasted_iota(jnp.int32, (ts, 1), 0)
        ab, ac, ah = ab_ref[...], ac_ref[...], ah_ref[...]
        z = ac * ah
        zp = pc_ref[...] * ph_ref[...] * (i > 0).astype(F32)
        w = w_ref[...]
        z1 = _shift_down(z, zp, 1, rows)
        z2 = _shift_down(z, zp, 2, rows)
        y = w[0:1, :] * z2 + w[1:2, :] * z1 + w[2:3, :] * z
        dyv = dy_ref[...]
        dab_ref[...] = (dyv * y).astype(BF16)
        dyy = dyv * ab
        nyy = ndy_ref[...] * nb_ref[...] * (i < nt - 1).astype(F32)
        dz = (w[2:3, :] * dyy + w[1:2, :] * _shift_up(dyy, nyy, 1, rows, ts)
              + w[0:1, :] * _shift_up(dyy, nyy, 2, rows, ts))
        dac_ref[...] = (dz * ah).astype(BF16)
        dah_ref[...] = (dz * ac).astype(BF16)
        parts = [jnp.sum(dyy * zz, axis=0, keepdims=True) for zz in (z2, z1, z)]

        @pl.when(i == 0)
        def _():
            for k in range(3):
                dw_ref[k:k + 1, :] = parts[k]

        @pl.when(i > 0)
        def _():
            for k in range(3):
                dw_ref[k:k + 1, :] += parts[k]

    def seg(s):
        return pl.BlockSpec((ts, SEG), lambda i: (i, b0 + s))

    def prev(s):
        return pl.BlockSpec((8, SEG), lambda i: (jnp.maximum(i * (ts // 8) - 1, 0), b0 + s))

    nxt_row = lambda i: jnp.minimum((i + 1) * (ts // 8), S // 8 - 1)
    out_row = pl.BlockSpec((ts, SEG), lambda i: (i, 0))
    return pl.pallas_call(
        body, name=name, grid=(nt,),
        in_specs=[seg(SEG_AB), seg(SEG_AC), seg(SEG_AH), prev(SEG_AC), prev(SEG_AH),
                  pl.BlockSpec((8, SEG), lambda i: (nxt_row(i), b0 + SEG_AB)),
                  out_row, pl.BlockSpec((8, SEG), lambda i: (nxt_row(i), 0)),
                  pl.BlockSpec((3, SEG), lambda i: (0, 0))],
        out_specs=[out_row, out_row, out_row, pl.BlockSpec((3, SEG), lambda i: (0, 0))],
        out_shape=[jax.ShapeDtypeStruct((S, SEG), BF16)] * 3 + [jax.ShapeDtypeStruct((3, SEG), F32)],
        compiler_params=_params("arbitrary"),
    )(proj, proj, proj, proj, proj, proj, dy, dy, conv_w)


def _tril_mask():
    r = lax.broadcasted_iota(jnp.int32, (CHUNK, CHUNK), 0)
    c = lax.broadcasted_iota(jnp.int32, (CHUNK, CHUNK), 1)
    return c <= r


def sg_fwd(proj, sg_norm, sg_w, sg_bt, D, *, name):
    S = proj.shape[0]
    ts = _pick(S, (512, 256, 128))
    b0 = _seg_block(D, 0)

    def body(u_ref, v_ref, gs_ref, w_ref, b_ref, y_ref):
        ug, _ = _gelu(u_ref[...])
        vg, _ = _gelu(v_ref[...])
        vn = ((vg * lax.rsqrt(jnp.mean(vg * vg, axis=-1, keepdims=True) + RMS_EPS)) * gs_ref[...]).astype(BF16)
        mask = _tril_mask()
        for g in range(SG_GROUPS):
            wg = jnp.where(mask, w_ref[g], 0.0).astype(BF16)
            cols = slice(g * CHUNK, (g + 1) * CHUNK)
            for n in range(ts // CHUNK):
                rws = slice(n * CHUNK, (n + 1) * CHUNK)
                sv = jnp.dot(wg, vn[rws, cols], preferred_element_type=F32) + b_ref[g]
                y_ref[rws, cols] = (ug[rws, cols] * sv).astype(BF16)

    seg = lambda s: pl.BlockSpec((ts, SEG), lambda i: (i, b0 + s))
    return pl.pallas_call(
        body, name=name, grid=(S // ts,),
        in_specs=[seg(SEG_U), seg(SEG_V), pl.BlockSpec((1, SEG), lambda i: (0, 0)),
                  pl.BlockSpec((SG_GROUPS, CHUNK, CHUNK), lambda i: (0, 0, 0)),
                  pl.BlockSpec((SG_GROUPS, CHUNK, 1), lambda i: (0, 0, 0))],
        out_specs=pl.BlockSpec((ts, SEG), lambda i: (i, 0)),
        out_shape=jax.ShapeDtypeStruct((S, SEG), BF16), compiler_params=_params("parallel"),
    )(proj, proj, sg_norm, sg_w, sg_bt)


def sg_bwd(proj, sg_norm, sg_w, sg_bt, dy, D, *, name):
    S = proj.shape[0]
    ts = _pick(S, (512, 256, 128))
    nt = S // ts
    b0 = _seg_block(D, 0)

    def body(u_ref, v_ref, dy_ref, gs_ref, w_ref, b_ref, du_ref, dv_ref, dw_ref, db_ref, dgs_ref, dvn_sc):
        i = pl.program_id(0)
        uv, vv, dyv = u_ref[...], v_ref[...], dy_ref[...]
        ug, ut = _gelu(uv)
        vg, vt = _gelu(vv)
        r = lax.rsqrt(jnp.mean(vg * vg, axis=-1, keepdims=True) + RMS_EPS)
        vh = vg * r
        gs = gs_ref[...]
        vn = (vh * gs).astype(BF16)
        dsv = dyv * ug
        dsv_b = dsv.astype(BF16)
        mask = _tril_mask()

        @pl.when(i == 0)
        def _():
            dw_ref[...] = jnp.zeros_like(dw_ref)
            db_ref[...] = jnp.zeros_like(db_ref)

        for g in range(SG_GROUPS):
            wg = jnp.where(mask, w_ref[g], 0.0).astype(BF16)
            cols = slice(g * CHUNK, (g + 1) * CHUNK)
            dw_acc = jnp.zeros((CHUNK, CHUNK), F32)
            db_acc = jnp.zeros((CHUNK, 1), F32)
            for n in range(ts // CHUNK):
                rws = slice(n * CHUNK, (n + 1) * CHUNK)
                vblk = vn[rws, cols]
                sv = jnp.dot(wg, vblk, preferred_element_type=F32) + b_ref[g]
                du_ref[rws, cols] = (dyv[rws, cols] * sv * _gelu_grad(uv[rws, cols], ut[rws, cols])).astype(BF16)
                dblk = dsv_b[rws, cols]
                dvn_sc[rws, cols] = lax.dot_general(wg, dblk, (((0,), (0,)), ((), ())), preferred_element_type=F32)
                dw_acc = dw_acc + lax.dot_general(dblk, vblk, (((1,), (1,)), ((), ())), preferred_element_type=F32)
                db_acc = db_acc + jnp.sum(dsv[rws, cols], axis=1, keepdims=True)
            dw_ref[g] += jnp.where(mask, dw_acc, 0.0)
            db_ref[g] += db_acc

        dvn = dvn_sc[...]
        gd = dvn * gs
        dvg = r * (gd - vh * jnp.mean(gd * vh, axis=-1, keepdims=True))
        dv_ref[...] = (dvg * _gelu_grad(vv, vt)).astype(BF16)
        dgs_part = jnp.sum(dvn * vh, axis=0, keepdims=True)

        @pl.when(i == 0)
        def _():
            dgs_ref[...] = dgs_part

        @pl.when(i > 0)
        def _():
            dgs_ref[...] += dgs_part

    seg = lambda s: pl.BlockSpec((ts, SEG), lambda i: (i, b0 + s))
    row = pl.BlockSpec((ts, SEG), lambda i: (i, 0))
    wspec = pl.BlockSpec((SG_GROUPS, CHUNK, CHUNK), lambda i: (0, 0, 0))
    bspec = pl.BlockSpec((SG_GROUPS, CHUNK, 1), lambda i: (0, 0, 0))
    vec = pl.BlockSpec((1, SEG), lambda i: (0, 0))
    return pl.pallas_call(
        body, name=name, grid=(nt,),
        in_specs=[seg(SEG_U), seg(SEG_V), row, vec, wspec, bspec],
        out_specs=[row, row, wspec, bspec, vec],
        out_shape=[jax.ShapeDtypeStruct((S, SEG), BF16)] * 2
        + [jax.ShapeDtypeStruct((SG_GROUPS, CHUNK, CHUNK), F32), jax.ShapeDtypeStruct((SG_GROUPS, CHUNK, 1), F32),
           jax.ShapeDtypeStruct((1, SEG), F32)],
        scratch_shapes=[pltpu.VMEM((ts, SEG), F32)],
        compiler_params=_params("arbitrary"),
    )(proj, proj, dy, sg_norm, sg_w, sg_bt)


def _log_sigmoid(x):
    return jnp.minimum(x, 0.0) - jnp.log(1.0 + jnp.exp(-jnp.abs(x)))


def fox_cumlog(proj, b_f, D, *, name):
    S = proj.shape[0]
    ts = _pick(S, (512, 256, 128))
    blk = (3 * D + SEG_FL * SEG) // LANES

    def body(f_ref, b_ref, c_ref, carry):
        i = pl.program_id(0)

        @pl.when(i == 0)
        def _():
            carry[...] = jnp.zeros_like(carry)

        rows = lax.broadcasted_iota(jnp.int32, (ts, 1), 0)
        acc = _log_sigmoid(f_ref[...] + b_ref[...])
        d = 1
        while d < ts:
            acc = acc + jnp.where(rows >= d, pltpu.roll(acc, d, 0), 0.0)
            d *= 2
        acc = acc + carry[...]
        c_ref[...] = acc
        carry[...] = acc[ts - 1:ts, :]

    return pl.pallas_call(
        body, name=name, grid=(S // ts,),
        in_specs=[pl.BlockSpec((ts, LANES), lambda i: (i, blk)), pl.BlockSpec((1, LANES), lambda i: (0, 0))],
        out_specs=pl.BlockSpec((ts, LANES), lambda i: (i, 0)),
        out_shape=jax.ShapeDtypeStruct((S, LANES), F32),
        scratch_shapes=[pltpu.VMEM((1, LANES), F32)],
        compiler_params=_params("arbitrary"),
    )(proj, b_f)


def fox_dlogit(proj, b_f, dc, D, *, name):
    S = proj.shape[0]
    ts = _pick(S, (512, 256, 128))
    nt = S // ts
    blk = (3 * D + SEG_FL * SEG) // LANES

    def body(f_ref, b_ref, dc_ref, df_ref, db_ref, carry):
        i = pl.program_id(0)

        @pl.when(i == 0)
        def _():
            carry[...] = jnp.zeros_like(carry)

        rows = lax.broadcasted_iota(jnp.int32, (ts, 1), 0)
        acc = dc_ref[...]
        d = 1
        while d < ts:
            acc = acc + jnp.where(rows < ts - d, pltpu.roll(acc, ts - d, 0), 0.0)
            d *= 2
        acc = acc + carry[...]
        carry[...] = acc[0:1, :]
        df = acc * _sig(-(f_ref[...] + b_ref[...]))
        df_ref[...] = jnp.zeros_like(df_ref)
        df_ref[:, 0:LANES] = df.astype(BF16)
        part = jnp.sum(df, axis=0, keepdims=True)

        @pl.when(i == 0)
        def _():
            db_ref[...] = part

        @pl.when(i > 0)
        def _():
            db_ref[...] += part

    rev = lambda i: nt - 1 - i
    return pl.pallas_call(
        body, name=name, grid=(nt,),
        in_specs=[pl.BlockSpec((ts, LANES), lambda i: (rev(i), blk)), pl.BlockSpec((1, LANES), lambda i: (0, 0)),
                  pl.BlockSpec((ts, LANES), lambda i: (rev(i), 0))],
        out_specs=[pl.BlockSpec((ts, SEG), lambda i: (rev(i), 0)), pl.BlockSpec((1, LANES), lambda i: (0, 0))],
        out_shape=[jax.ShapeDtypeStruct((S, SEG), BF16), jax.ShapeDtypeStruct((1, LANES), F32)],
        scratch_shapes=[pltpu.VMEM((1, LANES), F32)],
        compiler_params=_params("arbitrary"),
    )(proj, b_f, dc)


def _fox_tile(S):
    return min(512, max(128, S // 4))


def _causal_mask(t, keys_on_rows=False):
    r = lax.broadcasted_iota(jnp.int32, (t, t), 0)
    c = lax.broadcasted_iota(jnp.int32, (t, t), 1)
    return (r <= c) if keys_on_rows else (c <= r)


FOX_PAD = LANES
COL_C = FOX_HEAD_DIM
COL_ROWSUM = FOX_HEAD_DIM + 3
COL_L = FOX_HEAD_DIM
FOX_GROUP = 8
_NT = (((1,), (1,)), ((), ()))


def _head_pair(ref, a):
    x = ref[...]
    return x if a == 0 else pltpu.roll(x, FOX_HEAD_DIM, 1)


def fox_pack(proj, c, D, scale, *, name):
    S = proj.shape[0]
    ts = _pick(S, (512, 256, 128))
    base = 3 * D // LANES
    per_seg = SEG // LANES

    def body(q_ref, k_ref, v_ref, c_ref, qo, ko, vo):
        hp = pl.program_id(1)
        lane = lax.broadcasted_iota(jnp.int32, (ts, LANES), 1)
        is_val = lane < FOX_HEAD_DIM
        cv = c_ref[...]
        for a in range(2):
            ch = jnp.sum(jnp.where(lane == 2 * hp + a, cv, 0.0), axis=1, keepdims=True)
            c_hi = ch.astype(BF16).astype(F32)
            r1 = ch - c_hi
            c_mid = r1.astype(BF16).astype(F32)
            c_lo = r1 - c_mid
            qo[a] = jnp.where(is_val, _head_pair(q_ref, a) * scale,
                              jnp.where(lane < COL_C + 3, 1.0, 0.0)).astype(BF16)
            extra = jnp.where(lane == COL_C, -c_hi, jnp.where(lane == COL_C + 1, -c_mid, jnp.where(
                lane == COL_C + 2, -c_lo, jnp.where(lane == COL_ROWSUM, 1.0, 0.0))))
            ko[a] = jnp.where(is_val, _head_pair(k_ref, a), extra).astype(BF16)
            vo[a] = jnp.where(is_val, _head_pair(v_ref, a), jnp.where(lane == COL_L, 1.0, 0.0)).astype(BF16)

    seg = lambda s: pl.BlockSpec((ts, LANES), lambda i, hp: (i, base + s * per_seg + hp))
    out = pl.BlockSpec((2, ts, FOX_PAD), lambda i, hp: (hp, i, 0))
    return pl.pallas_call(
        body, name=name, grid=(S // ts, FOX_HEADS // 2),
        in_specs=[seg(SEG_FQ), seg(SEG_FK), seg(SEG_FV), pl.BlockSpec((ts, LANES), lambda i, hp: (i, 0))],
        out_specs=[out] * 3, out_shape=[jax.ShapeDtypeStruct((FOX_HEADS, S, FOX_PAD), BF16)] * 3,
        compiler_params=_params("parallel", "parallel"),
    )(proj, proj, proj, c)


def heads_pack(x, *, name):
    S = x.shape[0]
    ts = _pick(S, (512, 256, 128))

    def body(x_ref, o_ref):
        lane = lax.broadcasted_iota(jnp.int32, (ts, LANES), 1)
        for a in range(2):
            o_ref[a] = jnp.where(lane < FOX_HEAD_DIM, _head_pair(x_ref, a), 0.0).astype(BF16)

    return pl.pallas_call(
        body, name=name, grid=(S // ts, FOX_HEADS // 2),
        in_specs=[pl.BlockSpec((ts, LANES), lambda i, hp: (i, hp))],
        out_specs=pl.BlockSpec((2, ts, FOX_PAD), lambda i, hp: (hp, i, 0)),
        out_shape=jax.ShapeDtypeStruct((FOX_HEADS, S, FOX_PAD), BF16), compiler_params=_params("parallel", "parallel"),
    )(x)


def heads_unpack(x, scale, *, name):
    S = x.shape[1]
    ts = _pick(S, (512, 256, 128))

    def body(x_ref, o_ref):
        lane = lax.broadcasted_iota(jnp.int32, (ts, LANES), 1)
        both = jnp.where(lane < FOX_HEAD_DIM, x_ref[0], pltpu.roll(x_ref[1], FOX_HEAD_DIM, 1))
        o_ref[...] = (both * scale).astype(BF16)

    return pl.pallas_call(
        body, name=name, grid=(S // ts, FOX_HEADS // 2),
        in_specs=[pl.BlockSpec((2, ts, FOX_PAD), lambda i, hp: (hp, i, 0))],
        out_specs=pl.BlockSpec((ts, LANES), lambda i, hp: (i, hp)),
        out_shape=jax.ShapeDtypeStruct((S, FOX_HEADS * FOX_HEAD_DIM), BF16),
        compiler_params=_params("parallel", "parallel"),
    )(x)


def fox_fwd(q, k, v, *, name):
    H, S, W = q.shape
    t = _fox_tile(S)
    n = S // t
    G = FOX_GROUP

    def body(q_ref, k_ref, v_ref, o_ref, lse_ref, m_sc, acc_sc):
        i, j = pl.program_id(1), pl.program_id(2)

        @pl.when(j == 0)
        def _():
            m_sc[...] = jnp.full_like(m_sc, NEG_BIG)
            acc_sc[...] = jnp.zeros_like(acc_sc)

        def step(masked):
            for g in range(G):
                s = lax.dot_general(q_ref[g], k_ref[g], _NT, preferred_element_type=F32)
                if masked:
                    s = jnp.where(_causal_mask(t), s, NEG_BIG)
                m_prev = m_sc[g]
                m_new = jnp.maximum(m_prev, jnp.max(s, axis=-1, keepdims=True))
                p = jnp.exp(s - m_new)
                acc_sc[g] = jnp.exp(m_prev - m_new) * acc_sc[g] + jnp.dot(p.astype(BF16), v_ref[g],
                                                                          preferred_element_type=F32)
                m_sc[g] = m_new

        @pl.when(j < i)
        def _():
            step(False)

        @pl.when(j == i)
        def _():
            step(True)
            for g in range(G):
                acc = acc_sc[g]
                l = acc[:, COL_L:COL_L + 1]
                o_ref[g] = acc / l
                lse_ref[g] = m_sc[g] + jnp.log(l)

    qs = pl.BlockSpec((G, t, W), lambda h, i, j: (h, i, 0))
    ks = pl.BlockSpec((G, t, W), lambda h, i, j: (h, jnp.minimum(j, i), 0))
    col = pl.BlockSpec((G, t, 1), lambda h, i, j: (h, i, 0))
    return pl.pallas_call(
        body, name=name, grid=(H // G, n, n), in_specs=[qs, ks, ks], out_specs=[qs, col],
        out_shape=[jax.ShapeDtypeStruct((H, S, W), F32), jax.ShapeDtypeStruct((H, S, 1), F32)],
        scratch_shapes=[pltpu.VMEM((G, t, 1), F32), pltpu.VMEM((G, t, W), F32)],
        compiler_params=_params("parallel", "parallel", "arbitrary"),
    )(q, k, v)


def fox_delta(do, o, *, name):
    H, S, Dh = o.shape
    t = _pick(S, (2048, 1024, 512, 256, 128))

    def body(do_ref, o_ref, d_ref):
        d_ref[0] = jnp.sum(do_ref[0] * o_ref[0], axis=-1, keepdims=True)

    blk = pl.BlockSpec((1, t, Dh), lambda h, i: (h, i, 0))
    return pl.pallas_call(
        body, name=name, grid=(H, S // t), in_specs=[blk, blk],
        out_specs=pl.BlockSpec((1, t, 1), lambda h, i: (h, i, 0)),
        out_shape=jax.ShapeDtypeStruct((H, S, 1), F32), compiler_params=_params("parallel", "parallel"),
    )(do, o)


def fox_bwd_dq(q, k, v, do, lse, delta, *, name):
    H, S, W = q.shape
    t = _fox_tile(S)
    n = S // t
    G = FOX_GROUP

    def body(q_ref, k_ref, v_ref, do_ref, lse_ref, dl_ref, dq_ref, acc_sc):
        i, j = pl.program_id(1), pl.program_id(2)

        @pl.when(j == 0)
        def _():
            acc_sc[...] = jnp.zeros_like(acc_sc)

        def step(masked):
            for g in range(G):
                s = lax.dot_general(q_ref[g], k_ref[g], _NT, preferred_element_type=F32)
                p = jnp.exp(s - lse_ref[g])
                if masked:
                    p = jnp.where(_causal_mask(t), p, 0.0)
                dp = lax.dot_general(do_ref[g], v_ref[g], _NT, preferred_element_type=F32)
                ds = p * (dp - dl_ref[g])
                acc_sc[g] += jnp.dot(ds.astype(BF16), k_ref[g], preferred_element_type=F32)

        @pl.when(j < i)
        def _():
            step(False)

        @pl.when(j == i)
        def _():
            step(True)
            dq_ref[...] = acc_sc[...]

    qs = pl.BlockSpec((G, t, W), lambda h, i, j: (h, i, 0))
    ks = pl.BlockSpec((G, t, W), lambda h, i, j: (h, jnp.minimum(j, i), 0))
    col = pl.BlockSpec((G, t, 1), lambda h, i, j: (h, i, 0))
    return pl.pallas_call(
        body, name=name, grid=(H // G, n, n), in_specs=[qs, ks, ks, qs, col, col], out_specs=qs,
        out_shape=jax.ShapeDtypeStruct((H, S, W), F32),
        scratch_shapes=[pltpu.VMEM((G, t, W), F32)],
        compiler_params=_params("parallel", "parallel", "arbitrary"),
    )(q, k, v, do, lse, delta)


def fox_bwd_dkv(q, k, v, do, lse_row, delta_row, *, name):
    H, S, W = q.shape
    t = _fox_tile(S)
    n = S // t
    G = FOX_GROUP

    def body(q_ref, k_ref, v_ref, do_ref, lse_ref, dl_ref, dk_ref, dv_ref, dk_sc, dv_sc):
        j, i = pl.program_id(1), pl.program_id(2)

        @pl.when(i == 0)
        def _():
            dk_sc[...] = jnp.zeros_like(dk_sc)
            dv_sc[...] = jnp.zeros_like(dv_sc)

        def step(masked):
            for g in range(G):
                st = lax.dot_general(k_ref[g], q_ref[g], _NT, preferred_element_type=F32)
                pt = jnp.exp(st - lse_ref[g])
                if masked:
                    pt = jnp.where(_causal_mask(t, keys_on_rows=True), pt, 0.0)
                dpt = lax.dot_general(v_ref[g], do_ref[g], _NT, preferred_element_type=F32)
                dst = pt * (dpt - dl_ref[g])
                dv_sc[g] += jnp.dot(pt.astype(BF16), do_ref[g], preferred_element_type=F32)
                dk_sc[g] += jnp.dot(dst.astype(BF16), q_ref[g], preferred_element_type=F32)

        @pl.when(i > j)
        def _():
            step(False)

        @pl.when(i == j)
        def _():
            step(True)

        @pl.when(i == n - 1)
        def _():
            dk_ref[...] = dk_sc[...]
            dv_ref[...] = dv_sc[...]

    qs = pl.BlockSpec((G, t, W), lambda h, j, i: (h, jnp.maximum(i, j), 0))
    ks = pl.BlockSpec((G, t, W), lambda h, j, i: (h, j, 0))
    qrow = pl.BlockSpec((G, 1, t), lambda h, j, i: (h, 0, jnp.maximum(i, j)))
    return pl.pallas_call(
        body, name=name, grid=(H // G, n, n), in_specs=[qs, ks, ks, qs, qrow, qrow], out_specs=[ks, ks],
        out_shape=[jax.ShapeDtypeStruct((H, S, W), F32)] * 2,
        scratch_shapes=[pltpu.VMEM((G, t, W), F32), pltpu.VMEM((G, t, W), F32)],
        compiler_params=_params("parallel", "parallel", "arbitrary"),
    )(q, k, v, do, lse_row, delta_row)


def merge_fwd(proj, branches, D, *, name):
    S = proj.shape[0]
    ts = _pick(S, (256, 128))

    def body(g0, g1, g2, b0, b1, b2, o_ref):
        acc = _sig(g0[...]) * b0[...] + _sig(g1[...]) * b1[...] + _sig(g2[...]) * b2[...]
        o_ref[...] = acc.astype(BF16)

    gate = lambda n: pl.BlockSpec((ts, D), lambda i: (i, n))
    row = pl.BlockSpec((ts, D), lambda i: (i, 0))
    return pl.pallas_call(
        body, name=name, grid=(S // ts,), in_specs=[gate(0), gate(1), gate(2), row, row, row], out_specs=row,
        out_shape=jax.ShapeDtypeStruct((S, D), BF16), compiler_params=_params("parallel"),
    )(proj, proj, proj, *branches)


def merge_bwd(proj, branches, dm, D, *, name):
    S = proj.shape[0]
    ts = _pick(S, (256, 128))

    def body(g0, g1, g2, b0, b1, b2, dm_ref, db0, db1, db2, dg0, dg1, dg2):
        dmv = dm_ref[...]
        for g_ref, b_ref, db_ref, dg_ref in ((g0, b0, db0, dg0), (g1, b1, db1, dg1), (g2, b2, db2, dg2)):
            s = _sig(g_ref[...])
            db_ref[...] = (dmv * s).astype(BF16)
            dg_ref[...] = (dmv * b_ref[...] * (s * (1.0 - s))).astype(BF16)

    gate = lambda n: pl.BlockSpec((ts, D), lambda i: (i, n))
    row = pl.BlockSpec((ts, D), lambda i: (i, 0))
    return pl.pallas_call(
        body, name=name, grid=(S // ts,), in_specs=[gate(0), gate(1), gate(2), row, row, row, row],
        out_specs=[row] * 6, out_shape=[jax.ShapeDtypeStruct((S, D), BF16)] * 6,
        compiler_params=_params("parallel"),
    )(proj, proj, proj, *branches, dm)


def xa_fwd(q, k, v, *, name):
    S, D = q.shape
    M = k.shape[0]
    dh = D // XA_HEADS
    scale = dh ** -0.5
    t = _pick(S, (512, 256, 128))

    def body(q_ref, k_ref, v_ref, o_ref):
        for h in range(XA_HEADS):
            cols = slice(h * dh, (h + 1) * dh)
            s = lax.dot_general(q_ref[:, cols], k_ref[:, cols], (((1,), (1,)), ((), ())),
                                preferred_element_type=F32) * scale
            p = jnp.exp(s - jnp.max(s, axis=-1, keepdims=True))
            p = p / jnp.sum(p, axis=-1, keepdims=True)
            o_ref[:, cols] = jnp.dot(p.astype(BF16), v_ref[:, cols], preferred_element_type=F32).astype(BF16)

    row = pl.BlockSpec((t, D), lambda i: (i, 0))
    full = pl.BlockSpec((M, D), lambda i: (0, 0))
    return pl.pallas_call(
        body, name=name, grid=(S // t,), in_specs=[row, full, full], out_specs=row,
        out_shape=jax.ShapeDtypeStruct((S, D), BF16), compiler_params=_params("parallel"),
    )(q, k, v)


def xa_bwd(q, k, v, do, *, name):
    S, D = q.shape
    M = k.shape[0]
    dh = D // XA_HEADS
    scale = dh ** -0.5
    t = _pick(S, (512, 256, 128))

    def body(q_ref, k_ref, v_ref, do_ref, dq_ref, dk_ref, dv_ref):
        i = pl.program_id(0)

        @pl.when(i == 0)
        def _():
            dk_ref[...] = jnp.zeros_like(dk_ref)
            dv_ref[...] = jnp.zeros_like(dv_ref)

        for h in range(XA_HEADS):
            cols = slice(h * dh, (h + 1) * dh)
            qh, kh, vh = q_ref[:, cols], k_ref[:, cols], v_ref[:, cols]
            dob = do_ref[:, cols].astype(BF16)
            s = lax.dot_general(qh, kh, (((1,), (1,)), ((), ())), preferred_element_type=F32) * scale
            p = jnp.exp(s - jnp.max(s, axis=-1, keepdims=True))
            p = p / jnp.sum(p, axis=-1, keepdims=True)
            dp = lax.dot_general(dob, vh, (((1,), (1,)), ((), ())), preferred_element_type=F32)
            ds = (p * (dp - jnp.sum(p * dp, axis=-1, keepdims=True)) * scale).astype(BF16)
            dq_ref[:, cols] = jnp.dot(ds, kh, preferred_element_type=F32).astype(BF16)
            dk_ref[:, cols] += lax.dot_general(ds, qh, (((0,), (0,)), ((), ())), preferred_element_type=F32)
            dv_ref[:, cols] += lax.dot_general(p.astype(BF16), dob, (((0,), (0,)), ((), ())),
                                               preferred_element_type=F32)

    row = pl.BlockSpec((t, D), lambda i: (i, 0))
    full = pl.BlockSpec((M, D), lambda i: (0, 0))
    return pl.pallas_call(
        body, name=name, grid=(S // t,), in_specs=[row, full, full, row], out_specs=[row, full, full],
        out_shape=[jax.ShapeDtypeStruct((S, D), BF16), jax.ShapeDtypeStruct((M, D), F32),
                   jax.ShapeDtypeStruct((M, D), F32)],
        compiler_params=_params("arbitrary"),
    )(q, k, v, do)


def mixer_fwd(x, w, tag):
    S, D = x.shape
    h = rms_fwd(x, w["mix_norm"], name=f"{tag}_rms")
    proj = mm(h, w["w_in"], name=f"{tag}_proj")
    y_a = conv_fwd(proj, w["conv_w"], D, name=f"{tag}_conv")
    y_b = sg_fwd(proj, w["sg_norm"], w["sg_w"], w["sg_bt"], D, name=f"{tag}_sg")
    c = fox_cumlog(proj, w["fox_b_f"], D, name=f"{tag}_cumlog")
    qh, kh, vh = fox_pack(proj, c, D, FOX_HEAD_DIM ** -0.5, name=f"{tag}_foxpack")
    o, lse = fox_fwd(qh, kh, vh, name=f"{tag}_fox")
    y_c = heads_unpack(o, 1.0, name=f"{tag}_foxout")
    ys = (y_a, y_b, y_c)
    branches = [mm(ys[n], w["w_branch"][n], name=f"{tag}_branch{n}") for n in range(N_BRANCH)]
    merged = merge_fwd(proj, branches, D, name=f"{tag}_merge")
    y = mm(merged, w["w_out"], name=f"{tag}_out", res=x)
    return y, (x, h, proj, ys, qh, kh, vh, o, lse, branches, merged)


def mixer_bwd(dx, saved, w, tag):
    x, h, proj, ys, qh, kh, vh, o, lse, branches, merged = saved
    S, D = x.shape
    grads = {"w_out": mm_tn(merged, dx, name=f"{tag}_dwout")}
    dmerged = mm(dx, w["w_out_t"], name=f"{tag}_dmerged")
    outs = merge_bwd(proj, branches, dmerged, D, name=f"{tag}_dmerge")
    dbr, dgl = outs[:3], outs[3:]
    grads["w_branch"] = jnp.stack([mm_tn(ys[n], dbr[n], name=f"{tag}_dwbr{n}") for n in range(N_BRANCH)])
    dys = [mm(dbr[n], w["w_branch_t"][n], name=f"{tag}_dy{n}") for n in range(N_BRANCH)]
    d_ab, d_ac, d_ah, grads["conv_w"] = conv_bwd(proj, w["conv_w"], dys[0], D, name=f"{tag}_dconv")
    d_u, d_v, grads["sg_w"], d_sgb, grads["sg_norm"] = sg_bwd(
        proj, w["sg_norm"], w["sg_w"], w["sg_bt"], dys[1], D, name=f"{tag}_dsg")
    grads["sg_b"] = d_sgb[:, :, 0]
    do = heads_pack(dys[2], name=f"{tag}_dopack")
    delta = fox_delta(do, o, name=f"{tag}_delta")
    dq = fox_bwd_dq(qh, kh, vh, do, lse, delta, name=f"{tag}_dq")
    dk, dv = fox_bwd_dkv(qh, kh, vh, do, lse.reshape(FOX_HEADS, 1, S), delta.reshape(FOX_HEADS, 1, S),
                         name=f"{tag}_dkv")
    dc_rows = jnp.pad((dq[:, :, COL_ROWSUM] - dk[:, :, COL_C]).T, ((0, 0), (0, LANES - FOX_HEADS)))
    d_fl, d_bf = fox_dlogit(proj, w["fox_b_f"], dc_rows, D, name=f"{tag}_dflogit")
    grads["fox_b_f"] = d_bf[0, :FOX_HEADS]
    dproj = jnp.concatenate(
        list(dgl) + [d_ab, d_ac, d_ah, d_u, d_v, heads_unpack(dq, FOX_HEAD_DIM ** -0.5, name=f"{tag}_dqout"),
                     heads_unpack(dk, 1.0, name=f"{tag}_dkout"), heads_unpack(dv, 1.0, name=f"{tag}_dvout"), d_fl],
        axis=1)
    grads["w_in"] = mm_tn(h, dproj, name=f"{tag}_dwin")
    dh = mm(dproj, w["w_in_t"], name=f"{tag}_dh")
    dx, grads["mix_norm"] = rms_bwd(x, w["mix_norm"], dh, dx, name=f"{tag}_drms")
    return dx, grads


def xattn_fwd(x, mem, w, tag):
    h = rms_fwd(x, w["xa_norm"], name=f"{tag}_rms")
    m = rms_fwd(mem, w["mem_norm"], name=f"{tag}_mrms")
    q = mm(h, w["xa_wq"], name=f"{tag}_q", out_dtype=BF16)
    k = mm(m, w["xa_wk"], name=f"{tag}_k", out_dtype=BF16)
    v = mm(m, w["xa_wv"], name=f"{tag}_v", out_dtype=BF16)
    o = xa_fwd(q, k, v, name=f"{tag}_attn")
    y = mm(o, w["xa_wo"], name=f"{tag}_o", res=x)
    return y, (x, h, m, q, k, v, o)


def xattn_bwd(dx, mem, saved, w, tag):
    x, h, m, q, k, v, o = saved
    grads = {"xa_wo": mm_tn(o, dx, name=f"{tag}_dwo")}
    do = mm(dx, w["xa_wo_t"], name=f"{tag}_do")
    dq, dk, dv = xa_bwd(q, k, v, do, name=f"{tag}_dattn")
    grads["xa_wq"] = mm_tn(h, dq, name=f"{tag}_dwq")
    grads["xa_wk"] = mm_tn(m, dk, name=f"{tag}_dwk")
    grads["xa_wv"] = mm_tn(m, dv, name=f"{tag}_dwv")
    dh = mm(dq, w["xa_wq_t"], name=f"{tag}_dh")
    dm = mm(dk, w["xa_wk_t"], name=f"{tag}_dm1")
    dm = mm(dv, w["xa_wv_t"], name=f"{tag}_dm2", res=dm)
    _, grads["mem_norm"] = rms_bwd(mem, w["mem_norm"], dm, None, name=f"{tag}_dmrms")
    dx, grads["xa_norm"] = rms_bwd(x, w["xa_norm"], dh, dx, name=f"{tag}_drms")
    return dx, grads


def _mesh_pos():
    return lax.axis_index("x"), lax.axis_index("y"), lax.axis_index("c")


def _flip(v, bit):
    return 1 - v if bit else v


def all_gather(xs, *, name):
    n = len(xs)

    def body(*refs):
        x_refs, out_refs = refs[:n], refs[n:2 * n]
        send_sems, recv_sems, local_sems = refs[2 * n:]
        mx, my, mc = _mesh_pos()
        me, sibling = (mx, my, mc), (mx, my, 1 - mc)
        chips = [(1 - mx, my), (mx, 1 - my), (1 - mx, 1 - my)]

        def copy(a, k, block, to, from_input=False):
            slot = out_refs[a].at[4 * block[0] + 2 * block[1] + block[2]]
            return pltpu.make_async_remote_copy(
                src_ref=x_refs[a] if from_input else slot, dst_ref=slot,
                send_sem=send_sems.at[a, k], recv_sem=recv_sems.at[a, k], device_id=to, device_id_type=MESH_IDS)

        started = []
        for a in range(n):
            mine = pltpu.make_async_copy(x_refs[a], out_refs[a].at[4 * mx + 2 * my + mc], local_sems.at[a])
            mine.start()
            started.append(mine)
        sends = []
        for a in range(n):
            sends.append(copy(a, 0, me, sibling, from_input=True))
            sends += [copy(a, 1 + j, me, (*chip, mc), from_input=True) for j, chip in enumerate(chips)]
        for cp in sends:
            cp.start()
        for j, chip in enumerate(chips):
            for a in range(n):
                copy(a, 1 + j, (*chip, mc), me).wait_recv()
                onward = copy(a, 4 + j, (*chip, mc), sibling)
                onward.start()
                sends.append(onward)
        for a in range(n):
            copy(a, 0, sibling, me).wait_recv()
            for j, chip in enumerate(chips):
                copy(a, 4 + j, (*chip, 1 - mc), me).wait_recv()
        for cp in sends:
            cp.wait_send()
        for mine in started:
            mine.wait()

    any_spec = pl.BlockSpec(memory_space=pl.ANY)
    return pl.pallas_call(
        body, name=name, out_shape=[jax.ShapeDtypeStruct((N_DEV,) + x.shape, x.dtype) for x in xs],
        in_specs=[any_spec] * n, out_specs=[any_spec] * n,
        scratch_shapes=[pltpu.SemaphoreType.DMA((n, 7)), pltpu.SemaphoreType.DMA((n, 7)),
                        pltpu.SemaphoreType.DMA((n,))],
    )(*xs)


def all_to_all(gs, *, name):
    n = len(gs)

    def body(*refs):
        g_refs, out_refs = refs[:n], refs[n:2 * n]
        send_sems, recv_sems, local_sems = refs[2 * n:]
        mx, my, mc = _mesh_pos()
        me = 4 * mx + 2 * my + mc
        started, copies = [], []
        for a in range(n):
            mine = pltpu.make_async_copy(g_refs[a].at[me], out_refs[a].at[me], local_sems.at[a])
            mine.start()
            started.append(mine)
        for k in range(1, N_DEV):
            peer = (_flip(mx, k & 4), _flip(my, k & 2), _flip(mc, k & 1))
            peer_slot = 4 * peer[0] + 2 * peer[1] + peer[2]
            for a in range(n):
                sems = dict(send_sem=send_sems.at[a, k - 1], recv_sem=recv_sems.at[a, k - 1], device_id=peer,
                            device_id_type=MESH_IDS)
                send = pltpu.make_async_remote_copy(src_ref=g_refs[a].at[peer_slot], dst_ref=out_refs[a].at[me], **sems)
                arrive = pltpu.make_async_remote_copy(src_ref=g_refs[a].at[peer_slot],
                                                      dst_ref=out_refs[a].at[peer_slot], **sems)
                send.start()
                copies.append((send, arrive))
        for send, arrive in copies:
            arrive.wait_recv()
        for send, arrive in copies:
            send.wait_send()
        for mine in started:
            mine.wait()

    any_spec = pl.BlockSpec(memory_space=pl.ANY)
    return pl.pallas_call(
        body, name=name, out_shape=[jax.ShapeDtypeStruct(g.shape, g.dtype) for g in gs],
        in_specs=[any_spec] * n, out_specs=[any_spec] * n,
        scratch_shapes=[pltpu.SemaphoreType.DMA((n, 7)), pltpu.SemaphoreType.DMA((n, 7)),
                        pltpu.SemaphoreType.DMA((n,))],
    )(*gs)


ADAM_BLOCK_ELEMS = 256 * 1024


def reduce_adamw(parts, row0, w, m, v, *, name):
    R, C = w.shape
    n_parts = parts.shape[0]
    tr = R
    for cand in (512, 256, 128, 64, 32, 16):
        if R % cand == 0 and row0 % cand == 0 and cand * C <= ADAM_BLOCK_ELEMS:
            tr = cand
            break
    assert row0 % tr == 0 and (tr % 16 == 0 or (row0 == 0 and parts.shape[1] == R))
    bc1 = 1.0 - ADAM_B1 ** ADAM_STEP
    bc2 = 1.0 - ADAM_B2 ** ADAM_STEP

    def body(p_ref, w_ref, m_ref, v_ref, g_ref, d_ref, nm_ref, nv_ref):
        g = p_ref[0].astype(F32)
        for d in range(1, n_parts):
            g = g + p_ref[d].astype(F32)
        nm = ADAM_B1 * m_ref[...] + (1.0 - ADAM_B1) * g
        nv = ADAM_B2 * v_ref[...] + (1.0 - ADAM_B2) * (g * g)
        m_hat = nm / bc1
        v_hat = nv / bc2
        g_ref[...] = g
        d_ref[...] = -ADAM_LR * (m_hat / (jnp.sqrt(v_hat) + ADAM_EPS) + ADAM_WD * w_ref[...])
        nm_ref[...] = nm
        nv_ref[...] = nv

    row = pl.BlockSpec((tr, C), lambda i: (i, 0))
    blk0 = row0 // tr
    return pl.pallas_call(
        body, name=name, grid=(R // tr,),
        in_specs=[pl.BlockSpec((n_parts, tr, C), lambda i: (0, blk0 + i, 0)), row, row, row],
        out_specs=[row] * 4, out_shape=[jax.ShapeDtypeStruct((R, C), F32)] * 4,
        compiler_params=_params("parallel"),
    )(parts, w, m, v)


SHARDED = {
    "ffn1_w_gate": 2, "ffn1_w_up": 2, "ffn1_w_down": 1, "w_in": 2, "conv_w": 2, "w_branch": 3, "w_out": 1,
    "xa_wq": 1, "xa_wk": 1, "xa_wv": 1, "xa_wo": 1, "ffn2_w_gate": 2, "ffn2_w_up": 2, "ffn2_w_down": 1,
}
GROUPS = (("ffn1_w_gate", "ffn1_w_up", "ffn2_w_gate", "ffn2_w_up"),
          ("ffn1_w_down", "ffn2_w_down", "w_out", "xa_wq", "xa_wk", "xa_wv", "xa_wo"),
          ("w_in",), ("w_branch",))
REPLICATED = ("ffn1_norm", "mix_norm", "sg_norm", "sg_w", "sg_b", "fox_b_f", "xa_norm", "mem_norm", "ffn2_norm",
              "final_norm")
WEIGHTS = ("ffn1_norm", "ffn1_w_gate", "ffn1_w_up", "ffn1_w_down", "mix_norm", "w_in", "conv_w", "sg_norm", "sg_w",
           "sg_b", "fox_b_f", "w_branch", "w_out", "xa_norm", "mem_norm", "xa_wq", "xa_wk", "xa_wv", "xa_wo",
           "ffn2_norm", "ffn2_w_gate", "ffn2_w_up", "ffn2_w_down", "final_norm")
PACK_ROWS = 1024


def _rows(a):
    return a.reshape(-1, a.shape[-1])


def _pack(arrays, dtype):
    flat = jnp.concatenate([a.reshape(-1).astype(dtype) for a in arrays])
    n = flat.shape[0]
    unit = PACK_ROWS * LANES
    total = -(-n // unit) * unit
    return jnp.pad(flat, (0, total - n)).reshape(total // LANES, LANES)


def _unpack(buf, shapes):
    flat = buf.reshape(-1)
    out, off = [], 0
    for shp in shapes:
        n = 1
        for s in shp:
            n *= s
        out.append(flat[off:off + n].reshape(tuple(shp)))
        off += n
    return out


def _to_dev_major(full, axis):
    shp = full.shape
    a = full.reshape(shp[:axis] + (N_DEV, shp[axis] // N_DEV) + shp[axis + 1:])
    return jnp.moveaxis(a, axis, 0)


def _from_dev_major(a, axis):
    a = jnp.moveaxis(a, 0, axis)
    shp = a.shape
    return a.reshape(shp[:axis] + (shp[axis] * shp[axis + 1],) + shp[axis + 2:])


def _relayout_w_in(w_in, D):
    main = 8 * SEG
    pad = jnp.zeros((w_in.shape[0], SEG - FOX_HEADS), w_in.dtype)
    return jnp.concatenate([w_in[:, main + FOX_HEADS:], w_in[:, :main], w_in[:, main:main + FOX_HEADS], pad], axis=1)


def _unlayout_w_in(g, D):
    return jnp.concatenate([g[:, 3 * D:3 * D + 8 * SEG], g[:, 3 * D + 8 * SEG:3 * D + 8 * SEG + FOX_HEADS],
                            g[:, :3 * D]], axis=1)


def _layer_weights(full, rep, l, D):
    t = lambda a: a.T
    w_in = _relayout_w_in(full["w_in"][l], D)
    ffn = {}
    for tag in ("ffn1", "ffn2"):
        ffn[tag] = {"norm": rep[f"{tag}_norm"][l][None, :]}
        for n in ("w_gate", "w_up", "w_down"):
            ffn[tag][n] = full[f"{tag}_{n}"][l]
            ffn[tag][n + "_t"] = t(full[f"{tag}_{n}"][l])
    mix = {
        "mix_norm": rep["mix_norm"][l][None, :], "w_in": w_in, "w_in_t": t(w_in),
        "conv_w": full["conv_w"][l], "sg_norm": rep["sg_norm"][l][None, :], "sg_w": rep["sg_w"][l],
        "sg_bt": rep["sg_b"][l][:, :, None],
        "fox_b_f": jnp.pad(rep["fox_b_f"][l], (0, LANES - FOX_HEADS))[None, :],
        "w_branch": full["w_branch"][l], "w_branch_t": jnp.swapaxes(full["w_branch"][l], 1, 2),
        "w_out": full["w_out"][l], "w_out_t": t(full["w_out"][l]),
    }
    xa = {"xa_norm": rep["xa_norm"][l][None, :], "mem_norm": rep["mem_norm"][l][None, :]}
    for n in ("xa_wq", "xa_wk", "xa_wv", "xa_wo"):
        xa[n] = full[n][l]
        xa[n + "_t"] = t(full[n][l])
    return ffn, mix, xa


def kernel(x, mem, ffn1_norm, ffn1_w_gate, ffn1_w_up, ffn1_w_down, mix_norm, w_in, conv_w, sg_norm, sg_w, sg_b, fox_b_f, w_branch, w_out, xa_norm, mem_norm, xa_wq, xa_wk, xa_wv, xa_wo, ffn2_norm, ffn2_w_gate, ffn2_w_up, ffn2_w_down, final_norm, loss_target, m_ffn1_norm, m_ffn1_w_gate, m_ffn1_w_up, m_ffn1_w_down, m_mix_norm, m_w_in, m_conv_w, m_sg_norm, m_sg_w, m_sg_b, m_fox_b_f, m_w_branch, m_w_out, m_xa_norm, m_mem_norm, m_xa_wq, m_xa_wk, m_xa_wv, m_xa_wo, m_ffn2_norm, m_ffn2_w_gate, m_ffn2_w_up, m_ffn2_w_down, m_final_norm, v_ffn1_norm, v_ffn1_w_gate, v_ffn1_w_up, v_ffn1_w_down, v_mix_norm, v_w_in, v_conv_w, v_sg_norm, v_sg_w, v_sg_b, v_fox_b_f, v_w_branch, v_w_out, v_xa_norm, v_mem_norm, v_xa_wq, v_xa_wk, v_xa_wv, v_xa_wo, v_ffn2_norm, v_ffn2_w_gate, v_ffn2_w_up, v_ffn2_w_down, v_final_norm):
    args = locals()
    wts = {n: args[n] for n in WEIGHTS}
    mom = {n: args["m_" + n] for n in WEIGHTS}
    var = {n: args["v_" + n] for n in WEIGHTS}
    depth = ffn1_norm.shape[0]
    S, D = x.shape[1], x.shape[2]
    xs, ms, tgt = x[0], mem[0], loss_target[0]

    outs = all_gather([jnp.concatenate([_rows(wts[n]).astype(BF16) for n in grp]) for grp in GROUPS]
                      + [_rows(conv_w)], name="gather_weights")
    full, row0 = {}, {}
    for grp, got in zip(GROUPS + (("conv_w",),), outs):
        off = 0
        for n in grp:
            rows = _rows(wts[n]).shape[0]
            row0[n] = off
            full[n] = _from_dev_major(got[:, off:off + rows].reshape((N_DEV,) + wts[n].shape), SHARDED[n])
            off += rows
    rep = {n: wts[n] for n in REPLICATED}
    layers = [_layer_weights(full, rep, l, D) for l in range(depth)]

    saved = []
    h = xs
    for l, (ffn, mix, xa) in enumerate(layers):
        h, s1 = ffn_fwd(h, ffn["ffn1"], f"l{l}_ffn1")
        h, s2 = mixer_fwd(h, mix, f"l{l}_mix")
        h, s3 = xattn_fwd(h, ms, xa, f"l{l}_xa")
        h, s4 = ffn_fwd(h, ffn["ffn2"], f"l{l}_ffn2")
        saved.append((s1, s2, s3, s4))
    dx, d_final, loss_cols = final_loss_bwd(h, final_norm[None, :], tgt, name="final_loss")
    loss = lax.psum(0.5 * jnp.sum(loss_cols) / D, ("x", "y", "c"))

    per_layer = []
    for l in reversed(range(depth)):
        ffn, mix, xa = layers[l]
        s1, s2, s3, s4 = saved[l]
        g = {}
        dx, g4 = ffn_bwd(dx, s4, ffn["ffn2"], f"l{l}_ffn2")
        dx, g3 = xattn_bwd(dx, ms, s3, xa, f"l{l}_xa")
        dx, g2 = mixer_bwd(dx, s2, mix, f"l{l}_mix")
        dx, g1 = ffn_bwd(dx, s1, ffn["ffn1"], f"l{l}_ffn1")
        for tag, gg in (("ffn1", g1), ("ffn2", g4)):
            for n in ("w_gate", "w_up", "w_down"):
                g[f"{tag}_{n}"] = gg[n]
            g[f"{tag}_norm"] = gg["norm"][0]
        g.update(g3)
        g["xa_norm"], g["mem_norm"] = g3["xa_norm"][0], g3["mem_norm"][0]
        g.update({k: v for k, v in g2.items() if k != "w_in"})
        g["w_in"] = _unlayout_w_in(g2["w_in"], D)
        g["mix_norm"], g["sg_norm"] = g2["mix_norm"][0], g2["sg_norm"][0]
        per_layer.append(g)
    per_layer.reverse()
    grads = {n: jnp.stack([per_layer[l][n] for l in range(depth)]) for n in WEIGHTS if n != "final_norm"}
    grads["final_norm"] = d_final[0]

    def dev_major_rows(n):
        a = _to_dev_major(grads[n], SHARDED[n])
        return a.reshape(N_DEV, -1, a.shape[-1]).astype(BF16)

    parts = all_to_all([jnp.concatenate([dev_major_rows(n) for n in grp], axis=1) for grp in GROUPS],
                       name="scatter_grads")
    res = {k: {} for k in ("g", "d", "m", "v")}

    def adamw(n, parts_n, off):
        outs = reduce_adamw(parts_n, off, _rows(wts[n]), _rows(mom[n]), _rows(var[n]), name=f"adamw_{n}")
        for k, o in zip(("g", "d", "m", "v"), outs):
            res[k][n] = o.reshape(wts[n].shape)

    for grp, got in zip(GROUPS, parts):
        for n in grp:
            adamw(n, got, row0[n])

    small = list(REPLICATED)
    shapes = [wts[n].shape for n in small]
    conv_zero = jnp.zeros(grads["conv_w"].shape, F32)
    parts = all_gather([_pack([grads[n] for n in small] + [grads["conv_w"]], F32)], name="gather_small_grads")[0]
    outs = reduce_adamw(parts, 0, _pack([wts[n] for n in small] + [conv_zero], F32),
                        _pack([mom[n] for n in small] + [conv_zero], F32),
                        _pack([var[n] for n in small] + [conv_zero], F32), name="adamw_replicated")
    for k, o in zip(("g", "d", "m", "v"), outs):
        res[k].update(dict(zip(small, _unpack(o, shapes + [conv_zero.shape])[:-1])))
    conv_g = _unpack(outs[0], shapes + [conv_zero.shape])[-1]
    me = 4 * lax.axis_index("x") + 2 * lax.axis_index("y") + lax.axis_index("c")
    width = conv_w.shape[-1]
    conv_g = lax.dynamic_slice_in_dim(conv_g, me * width, width, axis=2)
    adamw("conv_w", _rows(conv_g)[None], 0)

    return (loss, dx[None], *[res["g"][n] for n in WEIGHTS], *[res["d"][n] for n in WEIGHTS],
            *[res["m"][n] for n in WEIGHTS], *[res["v"][n] for n in WEIGHTS])
```

```python
import functools

import jax
import jax.numpy as jnp
from jax import lax
from jax.experimental import pallas as pl
from jax.experimental.pallas import tpu as pltpu

F32 = jnp.float32
BF16 = jnp.bfloat16

N_DEV = 8
RMS_EPS = 1e-6
SEG = 512
FOX_HEADS = 8
FOX_HEAD_DIM = 64
SG_GROUPS = 4
CHUNK = 128
XA_HEADS = 4
N_BRANCH = 3
LANES = 128
VMEM_LIMIT_BYTES = 48 * 1024 * 1024
NEG_BIG = -1e30

ADAM_LR = 0.001
ADAM_B1 = 0.9
ADAM_B2 = 0.999
ADAM_EPS = 1e-08
ADAM_WD = 0.01
ADAM_STEP = 10

_GELU_K = 0.7978845608028654
_GELU_C = 0.044715

MESH_IDS = pl.DeviceIdType.MESH


def _pick(n, candidates):
    for c in candidates:
        if c <= n and n % c == 0:
            return c
    return n


def _params(*sem):
    return pltpu.CompilerParams(dimension_semantics=sem, vmem_limit_bytes=VMEM_LIMIT_BYTES)


def _sig(x):
    return 1.0 / (1.0 + jnp.exp(-x))


def _gelu(x):
    t = jnp.tanh(_GELU_K * (x + _GELU_C * x * x * x))
    return 0.5 * x * (1.0 + t), t


def _gelu_grad(x, t):
    return 0.5 * (1.0 + t) + 0.5 * x * (1.0 - t * t) * _GELU_K * (1.0 + 3.0 * _GELU_C * x * x)


_WIDE_TILES = (1536, 1408, 1280, 1024, 768, 512, 384, 256, 128)
MM_VMEM_BUDGET = 36 * 1024 * 1024
MM_ACC_BYTES = 6 * 1024 * 1024 + 512 * 1024


def mm(a, b, *, name, out_dtype=F32, res=None, scale=1.0, extras=(), epilogue=None, out_dtypes=None, tm=1024):
    M, K = a.shape
    K2, N = b.shape
    assert K == K2
    custom = epilogue is not None
    if not custom:
        extras = () if res is None else (res,)
        out_dtypes = (out_dtype,)

        def epilogue(acc, *ex):
            if scale != 1.0:
                acc = acc * scale
            return ((ex[0] + acc) if ex else acc,)

    n_ex, n_out = len(extras), len(out_dtypes)
    tn = _pick(N, _WIDE_TILES)
    tk = K if K <= 3072 else _pick(K, (2560, 2048, 1536, 1024, 512, 256, 128))
    nk = K // tk
    tile_bytes = sum(e.dtype.itemsize for e in extras) + sum(jnp.dtype(d).itemsize for d in out_dtypes)
    for tm in (tm, 512, 256, 128):
        blocks = 2 * (tm * tk * a.dtype.itemsize + tk * tn * 2 + tm * tn * tile_bytes)
        if M % tm == 0 and blocks + (tm * tn * 4 if nk > 1 else 0) <= MM_VMEM_BUDGET:
            break
    else:
        tm = M

    def body(*refs):
        a_ref, b_ref = refs[:2]
        ex_refs = refs[2:2 + n_ex]
        o_refs = refs[2 + n_ex:2 + n_ex + n_out]

        def finish(acc):
            for o_ref, val, dt in zip(o_refs, epilogue(acc, *[r[...] for r in ex_refs]), out_dtypes, strict=True):
                o_ref[...] = val.astype(dt)

        part = jnp.dot(a_ref[...].astype(BF16), b_ref[...].astype(BF16), preferred_element_type=F32)
        if nk == 1:
            finish(part)
        else:
            acc_ref = refs[-1]
            k = pl.program_id(2)

            @pl.when(k == 0)
            def _():
                acc_ref[...] = part

            @pl.when(k > 0)
            def _():
                acc_ref[...] += part

            @pl.when(k == nk - 1)
            def _():
                finish(acc_ref[...])

    tile = pl.BlockSpec((tm, tn), lambda i, j, k: (i, j))
    outs = pl.pallas_call(
        body, name=name, grid=(M // tm, N // tn, nk),
        in_specs=[pl.BlockSpec((tm, tk), lambda i, j, k: (i, k)), pl.BlockSpec((tk, tn), lambda i, j, k: (k, j))]
        + [tile] * n_ex,
        out_specs=[tile] * n_out, out_shape=[jax.ShapeDtypeStruct((M, N), d) for d in out_dtypes],
        scratch_shapes=[pltpu.VMEM((tm, tn), F32)] if nk > 1 else [],
        compiler_params=_params("parallel", "parallel", "arbitrary"),
    )(a, b, *extras)
    return tuple(outs) if custom else outs[0]


def mm_tn(a, b, *, name, scale=1.0):
    M, K = a.shape
    M2, N = b.shape
    assert M == M2
    tm = _pick(M, (1024, 512, 256, 128))
    tk = _pick(K, (1408, 1024, 512, 256, 128))
    tn = next((c for c in _WIDE_TILES if N % c == 0 and tk * c * 4 <= MM_ACC_BYTES), N)
    nm = M // tm

    def body(a_ref, b_ref, o_ref):
        m = pl.program_id(2)
        part = lax.dot_general(a_ref[...].astype(BF16), b_ref[...].astype(BF16), (((0,), (0,)), ((), ())),
                               preferred_element_type=F32)

        @pl.when(m == 0)
        def _():
            o_ref[...] = part

        @pl.when(m > 0)
        def _():
            o_ref[...] += part

        if scale != 1.0:
            @pl.when(m == nm - 1)
            def _():
                o_ref[...] = o_ref[...] * scale

    return pl.pallas_call(
        body, name=name, grid=(K // tk, N // tn, nm),
        in_specs=[pl.BlockSpec((tm, tk), lambda i, j, m: (m, i)), pl.BlockSpec((tm, tn), lambda i, j, m: (m, j))],
        out_specs=pl.BlockSpec((tk, tn), lambda i, j, m: (i, j)),
        out_shape=jax.ShapeDtypeStruct((K, N), F32),
        compiler_params=_params("parallel", "parallel", "arbitrary"),
    )(a, b)


def rms_fwd(x, g, *, name):
    S, D = x.shape
    ts = _pick(S, (512, 256, 128))

    def body(x_ref, g_ref, h_ref):
        xv = x_ref[...]
        r = lax.rsqrt(jnp.mean(xv * xv, axis=-1, keepdims=True) + RMS_EPS)
        h_ref[...] = ((xv * r) * g_ref[...]).astype(BF16)

    return pl.pallas_call(
        body, name=name, grid=(S // ts,),
        in_specs=[pl.BlockSpec((ts, D), lambda i: (i, 0)), pl.BlockSpec((1, D), lambda i: (0, 0))],
        out_specs=pl.BlockSpec((ts, D), lambda i: (i, 0)),
        out_shape=jax.ShapeDtypeStruct((S, D), BF16),
        compiler_params=_params("parallel"),
    )(x, g)


def rms_bwd(x, g, dh, dx_in, *, name):
    S, D = x.shape
    ts = _pick(S, (512, 256, 128))
    has_in = dx_in is not None

    def body(*refs):
        if has_in:
            x_ref, g_ref, dh_ref, di_ref, dx_ref, dg_ref = refs
        else:
            x_ref, g_ref, dh_ref, dx_ref, dg_ref = refs
        xv = x_ref[...]
        dh_v = dh_ref[...]
        r = lax.rsqrt(jnp.mean(xv * xv, axis=-1, keepdims=True) + RMS_EPS)
        xh = xv * r
        gd = dh_v * g_ref[...]
        dx = r * (gd - xh * jnp.mean(gd * xh, axis=-1, keepdims=True))
        if has_in:
            dx = di_ref[...] + dx
        dx_ref[...] = dx
        part = jnp.sum(dh_v * xh, axis=0, keepdims=True)

        @pl.when(pl.program_id(0) == 0)
        def _():
            dg_ref[...] = part

        @pl.when(pl.program_id(0) > 0)
        def _():
            dg_ref[...] += part

    row = pl.BlockSpec((ts, D), lambda i: (i, 0))
    vec = pl.BlockSpec((1, D), lambda i: (0, 0))
    return pl.pallas_call(
        body, name=name, grid=(S // ts,),
        in_specs=[row, vec, row] + ([row] if has_in else []),
        out_specs=[row, vec],
        out_shape=[jax.ShapeDtypeStruct((S, D), F32), jax.ShapeDtypeStruct((1, D), F32)],
        compiler_params=_params("arbitrary"),
    )(*([x, g, dh] + ([dx_in] if has_in else [])))


def final_loss_bwd(x, g, target, *, name):
    S, D = x.shape
    ts = _pick(S, (512, 256, 128))

    def body(x_ref, g_ref, t_ref, dx_ref, dg_ref, ls_ref):
        xv = x_ref[...]
        gv = g_ref[...]
        r = lax.rsqrt(jnp.mean(xv * xv, axis=-1, keepdims=True) + RMS_EPS)
        xh = xv * r
        e = xh * gv - t_ref[...]
        dy = e * (1.0 / D)
        gd = dy * gv
        dx_ref[...] = r * (gd - xh * jnp.mean(gd * xh, axis=-1, keepdims=True))
        dg_part = jnp.sum(dy * xh, axis=0, keepdims=True)
        ls_part = jnp.sum(e * e, axis=0, keepdims=True)

        @pl.when(pl.program_id(0) == 0)
        def _():
            dg_ref[...] = dg_part
            ls_ref[...] = ls_part

        @pl.when(pl.program_id(0) > 0)
        def _():
            dg_ref[...] += dg_part
            ls_ref[...] += ls_part

    row = pl.BlockSpec((ts, D), lambda i: (i, 0))
    vec = pl.BlockSpec((1, D), lambda i: (0, 0))
    return pl.pallas_call(
        body, name=name, grid=(S // ts,), in_specs=[row, vec, row], out_specs=[row, vec, vec],
        out_shape=[jax.ShapeDtypeStruct((S, D), F32), jax.ShapeDtypeStruct((1, D), F32),
                   jax.ShapeDtypeStruct((1, D), F32)],
        compiler_params=_params("arbitrary"),
    )(x, g, target)


def _swiglu(up, gp):
    return up, gp * _sig(gp) * up


def _swiglu_grad(da, gp, up):
    da = da * 0.5
    s = _sig(gp)
    return da * up * (s * (1.0 + gp * (1.0 - s))), da * (gp * s)


def ffn_fwd(x, w, tag):
    h = rms_fwd(x, w["norm"], name=f"{tag}_rms")
    gp = mm(h, w["w_gate"], name=f"{tag}_gate")
    up, a = mm(h, w["w_up"], name=f"{tag}_up", extras=(gp,), epilogue=_swiglu, out_dtypes=(F32, BF16))
    y = mm(a, w["w_down"], name=f"{tag}_down", res=x, scale=0.5)
    return y, (x, h, gp, up, a)


def ffn_bwd(dx, saved, w, tag):
    x, h, gp, up, a = saved
    grads = {"w_down": mm_tn(a, dx, name=f"{tag}_dwd", scale=0.5)}
    dgp, dup = mm(dx, w["w_down_t"], name=f"{tag}_da", extras=(gp, up), epilogue=_swiglu_grad,
                  out_dtypes=(BF16, BF16))
    grads["w_gate"] = mm_tn(h, dgp, name=f"{tag}_dwg")
    grads["w_up"] = mm_tn(h, dup, name=f"{tag}_dwu")
    dh = mm(dgp, w["w_gate_t"], name=f"{tag}_dh1")
    dh = mm(dup, w["w_up_t"], name=f"{tag}_dh2", res=dh)
    dx, grads["norm"] = rms_bwd(x, w["norm"], dh, dx, name=f"{tag}_drms")
    return dx, grads


SEG_AB, SEG_AC, SEG_AH, SEG_U, SEG_V, SEG_FQ, SEG_FK, SEG_FV, SEG_FL = range(9)
N_SEG = 9


def _seg_block(D, seg):
    return 3 * D // SEG + seg


def _shift_down(z, prev8, n, rows):
    out = pltpu.roll(z, n, 0)
    for r in range(n):
        out = jnp.where(rows == r, prev8[8 - n + r:8 - n + r + 1, :], out)
    return out


def _shift_up(z, next8, n, rows, ts):
    out = pltpu.roll(z, ts - n, 0)
    for r in range(n):
        out = jnp.where(rows == ts - n + r, next8[r:r + 1, :], out)
    return out


def conv_fwd(proj, conv_w, D, *, name):
    S = proj.shape[0]
    ts = _pick(S, (512, 256, 128))
    b0 = _seg_block(D, 0)

    def body(ab_ref, ac_ref, ah_ref, pc_ref, ph_ref, w_ref, y_ref):
        i = pl.program_id(0)
        rows = lax.broadcasted_iota(jnp.int32, (ts, 1), 0)
        z = ac_ref[...] * ah_ref[...]
        zp = pc_ref[...] * ph_ref[...] * (i > 0).astype(F32)
        w = w_ref[...]
        y = w[0:1, :] * _shift_down(z, zp, 2, rows) + w[1:2, :] * _shift_down(z, zp, 1, rows) + w[2:3, :] * z
        y_ref[...] = (ab_ref[...] * y).astype(BF16)

    def seg(s):
        return pl.BlockSpec((ts, SEG), lambda i: (i, b0 + s))

    def prev(s):
        return pl.BlockSpec((8, SEG), lambda i: (jnp.maximum(i * (ts // 8) - 1, 0), b0 + s))

    return pl.pallas_call(
        body, name=name, grid=(S // ts,),
        in_specs=[seg(SEG_AB), seg(SEG_AC), seg(SEG_AH), prev(SEG_AC), prev(SEG_AH),
                  pl.BlockSpec((3, SEG), lambda i: (0, 0))],
        out_specs=pl.BlockSpec((ts, SEG), lambda i: (i, 0)),
        out_shape=jax.ShapeDtypeStruct((S, SEG), BF16), compiler_params=_params("parallel"),
    )(proj, proj, proj, proj, proj, conv_w)


def conv_bwd(proj, conv_w, dy, D, *, name):
    S = proj.shape[0]
    ts = _pick(S, (512, 256, 128))
    nt = S // ts
    b0 = _seg_block(D, 0)

    def body(ab_ref, ac_ref, ah_ref, pc_ref, ph_ref, nb_ref, dy_ref, ndy_ref, w_ref,
             dab_ref, dac_ref, dah_ref, dw_ref):
        i = pl.program_id(0)
        rows = lax.broadcasted_iota(jnp.int32, (ts, 1), 0)
        ab, ac, ah = ab_ref[...], ac_ref[...], ah_ref[...]
        z = ac * ah
        zp = pc_ref[...] * ph_ref[...] * (i > 0).astype(F32)
        w = w_ref[...]
        z1 = _shift_down(z, zp, 1, rows)
        z2 = _shift_down(z, zp, 2, rows)
        y = w[0:1, :] * z2 + w[1:2, :] * z1 + w[2:3, :] * z
        dyv = dy_ref[...]
        dab_ref[...] = (dyv * y).astype(BF16)
        dyy = dyv * ab
        nyy = ndy_ref[...] * nb_ref[...] * (i < nt - 1).astype(F32)
        dz = (w[2:3, :] * dyy + w[1:2, :] * _shift_up(dyy, nyy, 1, rows, ts)
              + w[0:1, :] * _shift_up(dyy, nyy, 2, rows, ts))
        dac_ref[...] = (dz * ah).astype(BF16)
        dah_ref[...] = (dz * ac).astype(BF16)
        parts = [jnp.sum(dyy * zz, axis=0, keepdims=True) for zz in (z2, z1, z)]

        @pl.when(i == 0)
        def _():
            for k in range(3):
                dw_ref[k:k + 1, :] = parts[k]

        @pl.when(i > 0)
        def _():
            for k in range(3):
                dw_ref[k:k + 1, :] += parts[k]

    def seg(s):
        return pl.BlockSpec((ts, SEG), lambda i: (i, b0 + s))

    def prev(s):
        return pl.BlockSpec((8, SEG), lambda i: (jnp.maximum(i * (ts // 8) - 1, 0), b0 + s))

    nxt_row = lambda i: jnp.minimum((i + 1) * (ts // 8), S // 8 - 1)
    out_row = pl.BlockSpec((ts, SEG), lambda i: (i, 0))
    return pl.pallas_call(
        body, name=name, grid=(nt,),
        in_specs=[seg(SEG_AB), seg(SEG_AC), seg(SEG_AH), prev(SEG_AC), prev(SEG_AH),
                  pl.BlockSpec((8, SEG), lambda i: (nxt_row(i), b0 + SEG_AB)),
                  out_row, pl.BlockSpec((8, SEG), lambda i: (nxt_row(i), 0)),
                  pl.BlockSpec((3, SEG), lambda i: (0, 0))],
        out_specs=[out_row, out_row, out_row, pl.BlockSpec((3, SEG), lambda i: (0, 0))],
        out_shape=[jax.ShapeDtypeStruct((S, SEG), BF16)] * 3 + [jax.ShapeDtypeStruct((3, SEG), F32)],
        compiler_params=_params("arbitrary"),
    )(proj, proj, proj, proj, proj, proj, dy, dy, conv_w)


def _tril_mask():
    r = lax.broadcasted_iota(jnp.int32, (CHUNK, CHUNK), 0)
    c = lax.broadcasted_iota(jnp.int32, (CHUNK, CHUNK), 1)
    return c <= r


def sg_fwd(proj, sg_norm, sg_w, sg_bt, D, *, name):
    S = proj.shape[0]
    ts = _pick(S, (512, 256, 128))
    b0 = _seg_block(D, 0)

    def body(u_ref, v_ref, gs_ref, w_ref, b_ref, y_ref):
        ug, _ = _gelu(u_ref[...])
        vg, _ = _gelu(v_ref[...])
        vn = ((vg * lax.rsqrt(jnp.mean(vg * vg, axis=-1, keepdims=True) + RMS_EPS)) * gs_ref[...]).astype(BF16)
        mask = _tril_mask()
        for g in range(SG_GROUPS):
            wg = jnp.where(mask, w_ref[g], 0.0).astype(BF16)
            cols = slice(g * CHUNK, (g + 1) * CHUNK)
            for n in range(ts // CHUNK):
                rws = slice(n * CHUNK, (n + 1) * CHUNK)
                sv = jnp.dot(wg, vn[rws, cols], preferred_element_type=F32) + b_ref[g]
                y_ref[rws, cols] = (ug[rws, cols] * sv).astype(BF16)

    seg = lambda s: pl.BlockSpec((ts, SEG), lambda i: (i, b0 + s))
    return pl.pallas_call(
        body, name=name, grid=(S // ts,),
        in_specs=[seg(SEG_U), seg(SEG_V), pl.BlockSpec((1, SEG), lambda i: (0, 0)),
                  pl.BlockSpec((SG_GROUPS, CHUNK, CHUNK), lambda i: (0, 0, 0)),
                  pl.BlockSpec((SG_GROUPS, CHUNK, 1), lambda i: (0, 0, 0))],
        out_specs=pl.BlockSpec((ts, SEG), lambda i: (i, 0)),
        out_shape=jax.ShapeDtypeStruct((S, SEG), BF16), compiler_params=_params("parallel"),
    )(proj, proj, sg_norm, sg_w, sg_bt)


def sg_bwd(proj, sg_norm, sg_w, sg_bt, dy, D, *, name):
    S = proj.shape[0]
    ts = _pick(S, (512, 256, 128))
    nt = S // ts
    b0 = _seg_block(D, 0)

    def body(u_ref, v_ref, dy_ref, gs_ref, w_ref, b_ref, du_ref, dv_ref, dw_ref, db_ref, dgs_ref, dvn_sc):
        i = pl.program_id(0)
        uv, vv, dyv = u_ref[...], v_ref[...], dy_ref[...]
        ug, ut = _gelu(uv)
        vg, vt = _gelu(vv)
        r = lax.rsqrt(jnp.mean(vg * vg, axis=-1, keepdims=True) + RMS_EPS)
        vh = vg * r
        gs = gs_ref[...]
        vn = (vh * gs).astype(BF16)
        dsv = dyv * ug
        dsv_b = dsv.astype(BF16)
        mask = _tril_mask()

        @pl.when(i == 0)
        def _():
            dw_ref[...] = jnp.zeros_like(dw_ref)
            db_ref[...] = jnp.zeros_like(db_ref)

        for g in range(SG_GROUPS):
            wg = jnp.where(mask, w_ref[g], 0.0).astype(BF16)
            cols = slice(g * CHUNK, (g + 1) * CHUNK)
            dw_acc = jnp.zeros((CHUNK, CHUNK), F32)
            db_acc = jnp.zeros((CHUNK, 1), F32)
            for n in range(ts // CHUNK):
                rws = slice(n * CHUNK, (n + 1) * CHUNK)
                vblk = vn[rws, cols]
                sv = jnp.dot(wg, vblk, preferred_element_type=F32) + b_ref[g]
                du_ref[rws, cols] = (dyv[rws, cols] * sv * _gelu_grad(uv[rws, cols], ut[rws, cols])).astype(BF16)
                dblk = dsv_b[rws, cols]
                dvn_sc[rws, cols] = lax.dot_general(wg, dblk, (((0,), (0,)), ((), ())), preferred_element_type=F32)
                dw_acc = dw_acc + lax.dot_general(dblk, vblk, (((1,), (1,)), ((), ())), preferred_element_type=F32)
                db_acc = db_acc + jnp.sum(dsv[rws, cols], axis=1, keepdims=True)
            dw_ref[g] += jnp.where(mask, dw_acc, 0.0)
            db_ref[g] += db_acc

        dvn = dvn_sc[...]
        gd = dvn * gs
        dvg = r * (gd - vh * jnp.mean(gd * vh, axis=-1, keepdims=True))
        dv_ref[...] = (dvg * _gelu_grad(vv, vt)).astype(BF16)
        dgs_part = jnp.sum(dvn * vh, axis=0, keepdims=True)

        @pl.when(i == 0)
        def _():
            dgs_ref[...] = dgs_part

        @pl.when(i > 0)
        def _():
            dgs_ref[...] += dgs_part

    seg = lambda s: pl.BlockSpec((ts, SEG), lambda i: (i, b0 + s))
    row = pl.BlockSpec((ts, SEG), lambda i: (i, 0))
    wspec = pl.BlockSpec((SG_GROUPS, CHUNK, CHUNK), lambda i: (0, 0, 0))
    bspec = pl.BlockSpec((SG_GROUPS, CHUNK, 1), lambda i: (0, 0, 0))
    vec = pl.BlockSpec((1, SEG), lambda i: (0, 0))
    return pl.pallas_call(
        body, name=name, grid=(nt,),
        in_specs=[seg(SEG_U), seg(SEG_V), row, vec, wspec, bspec],
        out_specs=[row, row, wspec, bspec, vec],
        out_shape=[jax.ShapeDtypeStruct((S, SEG), BF16)] * 2
        + [jax.ShapeDtypeStruct((SG_GROUPS, CHUNK, CHUNK), F32), jax.ShapeDtypeStruct((SG_GROUPS, CHUNK, 1), F32),
           jax.ShapeDtypeStruct((1, SEG), F32)],
        scratch_shapes=[pltpu.VMEM((ts, SEG), F32)],
        compiler_params=_params("arbitrary"),
    )(proj, proj, dy, sg_norm, sg_w, sg_bt)


def _log_sigmoid(x):
    return jnp.minimum(x, 0.0) - jnp.log(1.0 + jnp.exp(-jnp.abs(x)))


def fox_cumlog(proj, b_f, D, *, name):
    S = proj.shape[0]
    ts = _pick(S, (512, 256, 128))
    blk = (3 * D + SEG_FL * SEG) // LANES

    def body(f_ref, b_ref, c_ref, carry):
        i = pl.program_id(0)

        @pl.when(i == 0)
        def _():
            carry[...] = jnp.zeros_like(carry)

        rows = lax.broadcasted_iota(jnp.int32, (ts, 1), 0)
        acc = _log_sigmoid(f_ref[...] + b_ref[...])
        d = 1
        while d < ts:
            acc = acc + jnp.where(rows >= d, pltpu.roll(acc, d, 0), 0.0)
            d *= 2
        acc = acc + carry[...]
        c_ref[...] = acc
        carry[...] = acc[ts - 1:ts, :]

    return pl.pallas_call(
        body, name=name, grid=(S // ts,),
        in_specs=[pl.BlockSpec((ts, LANES), lambda i: (i, blk)), pl.BlockSpec((1, LANES), lambda i: (0, 0))],
        out_specs=pl.BlockSpec((ts, LANES), lambda i: (i, 0)),
        out_shape=jax.ShapeDtypeStruct((S, LANES), F32),
        scratch_shapes=[pltpu.VMEM((1, LANES), F32)],
        compiler_params=_params("arbitrary"),
    )(proj, b_f)


def fox_dlogit(proj, b_f, dc, D, *, name):
    S = proj.shape[0]
    ts = _pick(S, (512, 256, 128))
    nt = S // ts
    blk = (3 * D + SEG_FL * SEG) // LANES

    def body(f_ref, b_ref, dc_ref, df_ref, db_ref, carry):
        i = pl.program_id(0)

        @pl.when(i == 0)
        def _():
            carry[...] = jnp.zeros_like(carry)

        rows = lax.broadcasted_iota(jnp.int32, (ts, 1), 0)
        acc = dc_ref[...]
        d = 1
        while d < ts:
            acc = acc + jnp.where(rows < ts - d, pltpu.roll(acc, ts - d, 0), 0.0)
            d *= 2
        acc = acc + carry[...]
        carry[...] = acc[0:1, :]
        df = acc * _sig(-(f_ref[...] + b_ref[...]))
        df_ref[...] = jnp.zeros_like(df_ref)
        df_ref[:, 0:LANES] = df.astype(BF16)
        part = jnp.sum(df, axis=0, keepdims=True)

        @pl.when(i == 0)
        def _():
            db_ref[...] = part

        @pl.when(i > 0)
        def _():
            db_ref[...] += part

    rev = lambda i: nt - 1 - i
    return pl.pallas_call(
        body, name=name, grid=(nt,),
        in_specs=[pl.BlockSpec((ts, LANES), lambda i: (rev(i), blk)), pl.BlockSpec((1, LANES), lambda i: (0, 0)),
                  pl.BlockSpec((ts, LANES), lambda i: (rev(i), 0))],
        out_specs=[pl.BlockSpec((ts, SEG), lambda i: (rev(i), 0)), pl.BlockSpec((1, LANES), lambda i: (0, 0))],
        out_shape=[jax.ShapeDtypeStruct((S, SEG), BF16), jax.ShapeDtypeStruct((1, LANES), F32)],
        scratch_shapes=[pltpu.VMEM((1, LANES), F32)],
        compiler_params=_params("arbitrary"),
    )(proj, b_f, dc)


def _fox_tile(S):
    return min(512, max(128, S // 4))


def _causal_mask(t, keys_on_rows=False):
    r = lax.broadcasted_iota(jnp.int32, (t, t), 0)
    c = lax.broadcasted_iota(jnp.int32, (t, t), 1)
    return (r <= c) if keys_on_rows else (c <= r)


FOX_PAD = LANES
COL_C = FOX_HEAD_DIM
COL_ROWSUM = FOX_HEAD_DIM + 3
COL_L = FOX_HEAD_DIM
FOX_GROUP = 8
_NT = (((1,), (1,)), ((), ()))


def _head_pair(ref, a):
    x = ref[...]
    return x if a == 0 else pltpu.roll(x, FOX_HEAD_DIM, 1)


def fox_pack(proj, c, D, scale, *, name):
    S = proj.shape[0]
    ts = _pick(S, (512, 256, 128))
    base = 3 * D // LANES
    per_seg = SEG // LANES

    def body(q_ref, k_ref, v_ref, c_ref, qo, ko, vo, qto, vto):
        hp = pl.program_id(1)
        lane = lax.broadcasted_iota(jnp.int32, (ts, LANES), 1)
        is_val = lane < FOX_HEAD_DIM
        cv = c_ref[...]
        for a in range(2):
            ch = jnp.sum(jnp.where(lane == 2 * hp + a, cv, 0.0), axis=1, keepdims=True)
            c_hi = ch.astype(BF16).astype(F32)
            r1 = ch - c_hi
            c_mid = r1.astype(BF16).astype(F32)
            c_lo = r1 - c_mid
            qa = jnp.where(is_val, _head_pair(q_ref, a) * scale, jnp.where(lane < COL_C + 3, 1.0, 0.0))
            extra = jnp.where(lane == COL_C, -c_hi, jnp.where(lane == COL_C + 1, -c_mid, jnp.where(
                lane == COL_C + 2, -c_lo, jnp.where(lane == COL_ROWSUM, 1.0, 0.0))))
            va = jnp.where(is_val, _head_pair(v_ref, a), jnp.where(lane == COL_L, 1.0, 0.0))
            qo[a] = qa.astype(BF16)
            ko[a] = jnp.where(is_val, _head_pair(k_ref, a), extra).astype(BF16)
            vo[a] = va.astype(BF16)
            qto[a] = qa.T.astype(BF16)
            vto[a] = va.T.astype(BF16)

    seg = lambda s: pl.BlockSpec((ts, LANES), lambda i, hp: (i, base + s * per_seg + hp))
    out = pl.BlockSpec((2, ts, FOX_PAD), lambda i, hp: (hp, i, 0))
    out_t = pl.BlockSpec((2, FOX_PAD, ts), lambda i, hp: (hp, 0, i))
    return pl.pallas_call(
        body, name=name, grid=(S // ts, FOX_HEADS // 2),
        in_specs=[seg(SEG_FQ), seg(SEG_FK), seg(SEG_FV), pl.BlockSpec((ts, LANES), lambda i, hp: (i, 0))],
        out_specs=[out] * 3 + [out_t] * 2,
        out_shape=[jax.ShapeDtypeStruct((FOX_HEADS, S, FOX_PAD), BF16)] * 3
        + [jax.ShapeDtypeStruct((FOX_HEADS, FOX_PAD, S), BF16)] * 2,
        compiler_params=_params("parallel", "parallel"),
    )(proj, proj, proj, c)


def heads_pack(x, *, name):
    S = x.shape[0]
    ts = _pick(S, (512, 256, 128))

    def body(x_ref, o_ref):
        lane = lax.broadcasted_iota(jnp.int32, (ts, LANES), 1)
        for a in range(2):
            o_ref[a] = jnp.where(lane < FOX_HEAD_DIM, _head_pair(x_ref, a), 0.0).astype(BF16)

    return pl.pallas_call(
        body, name=name, grid=(S // ts, FOX_HEADS // 2),
        in_specs=[pl.BlockSpec((ts, LANES), lambda i, hp: (i, hp))],
        out_specs=pl.BlockSpec((2, ts, FOX_PAD), lambda i, hp: (hp, i, 0)),
        out_shape=jax.ShapeDtypeStruct((FOX_HEADS, S, FOX_PAD), BF16), compiler_params=_params("parallel", "parallel"),
    )(x)


def heads_unpack(x, scale, *, name):
    S = x.shape[1]
    ts = _pick(S, (512, 256, 128))

    def body(x_ref, o_ref):
        lane = lax.broadcasted_iota(jnp.int32, (ts, LANES), 1)
        both = jnp.where(lane < FOX_HEAD_DIM, x_ref[0], pltpu.roll(x_ref[1], FOX_HEAD_DIM, 1))
        o_ref[...] = (both * scale).astype(BF16)

    return pl.pallas_call(
        body, name=name, grid=(S // ts, FOX_HEADS // 2),
        in_specs=[pl.BlockSpec((2, ts, FOX_PAD), lambda i, hp: (hp, i, 0))],
        out_specs=pl.BlockSpec((ts, LANES), lambda i, hp: (i, hp)),
        out_shape=jax.ShapeDtypeStruct((S, FOX_HEADS * FOX_HEAD_DIM), BF16),
        compiler_params=_params("parallel", "parallel"),
    )(x)


FOX_FWD_GROUP = 2


def _fox_fwd_tile(S):
    return min(1024, max(128, S // 4))


def fox_fwd(qt, k, vt, *, name):
    H, W, S = qt.shape
    t = _fox_fwd_tile(S)
    n = S // t
    G = FOX_FWD_GROUP

    def body(qt_ref, k_ref, vt_ref, ot_ref, lse_ref, m_sc, acc_sc):
        i, j = pl.program_id(1), pl.program_id(2)

        @pl.when(j == 0)
        def _():
            m_sc[...] = jnp.full_like(m_sc, NEG_BIG)
            acc_sc[...] = jnp.zeros_like(acc_sc)

        def step(masked):
            for g in range(G):
                st = jnp.dot(k_ref[g], qt_ref[g], preferred_element_type=F32)
                if masked:
                    st = jnp.where(_causal_mask(t, keys_on_rows=True), st, NEG_BIG)
                m_prev = m_sc[g]
                m_new = jnp.maximum(m_prev, jnp.max(st, axis=0, keepdims=True))
                pt = jnp.exp(st - m_new)
                acc_sc[g] = jnp.exp(m_prev - m_new) * acc_sc[g] + jnp.dot(vt_ref[g], pt.astype(BF16),
                                                                          preferred_element_type=F32)
                m_sc[g] = m_new

        @pl.when(j < i)
        def _():
            step(False)

        @pl.when(j == i)
        def _():
            step(True)
            for g in range(G):
                acc = acc_sc[g]
                l = acc[COL_L:COL_L + 1, :]
                ot_ref[g] = acc / l
                lse_ref[g] = m_sc[g] + jnp.log(l)

    qs = pl.BlockSpec((G, W, t), lambda h, i, j: (h, 0, i))
    ks = pl.BlockSpec((G, t, W), lambda h, i, j: (h, jnp.minimum(j, i), 0))
    vs = pl.BlockSpec((G, W, t), lambda h, i, j: (h, 0, jnp.minimum(j, i)))
    row = pl.BlockSpec((G, 1, t), lambda h, i, j: (h, 0, i))
    return pl.pallas_call(
        body, name=name, grid=(H // G, n, n), in_specs=[qs, ks, vs], out_specs=[qs, row],
        out_shape=[jax.ShapeDtypeStruct((H, W, S), F32), jax.ShapeDtypeStruct((H, 1, S), F32)],
        scratch_shapes=[pltpu.VMEM((G, 1, t), F32), pltpu.VMEM((G, W, t), F32)],
        compiler_params=_params("parallel", "parallel", "arbitrary"),
    )(qt, k, vt)


def fox_dc(dq, dk, *, name):
    H, S, W = dq.shape
    ts = _pick(S, (512, 256, 128))

    def body(dq_ref, dk_ref, o_ref):
        lane = lax.broadcasted_iota(jnp.int32, (ts, LANES), 1)
        acc = jnp.zeros((ts, LANES), F32)
        for h in range(H):
            d = dq_ref[h][:, COL_ROWSUM:COL_ROWSUM + 1] - dk_ref[h][:, COL_C:COL_C + 1]
            acc = jnp.where(lane == h, d, acc)
        o_ref[...] = acc

    blk = pl.BlockSpec((H, ts, W), lambda i: (0, i, 0))
    return pl.pallas_call(
        body, name=name, grid=(S // ts,), in_specs=[blk, blk],
        out_specs=pl.BlockSpec((ts, LANES), lambda i: (i, 0)),
        out_shape=jax.ShapeDtypeStruct((S, LANES), F32), compiler_params=_params("parallel"),
    )(dq, dk)


def heads_unpack_t(xt, *, name):
    S = xt.shape[2]
    ts = _pick(S, (512, 256, 128))

    def body(x_ref, y_ref, o_ref):
        lane = lax.broadcasted_iota(jnp.int32, (ts, LANES), 1)
        x0, x1 = x_ref[0].T, x_ref[1].T
        o_ref[0] = x0
        o_ref[1] = x1
        y_ref[...] = jnp.where(lane < FOX_HEAD_DIM, x0, pltpu.roll(x1, FOX_HEAD_DIM, 1)).astype(BF16)

    return pl.pallas_call(
        body, name=name, grid=(S // ts, FOX_HEADS // 2),
        in_specs=[pl.BlockSpec((2, FOX_PAD, ts), lambda i, hp: (hp, 0, i))],
        out_specs=[pl.BlockSpec((ts, LANES), lambda i, hp: (i, hp)),
                   pl.BlockSpec((2, ts, FOX_PAD), lambda i, hp: (hp, i, 0))],
        out_shape=[jax.ShapeDtypeStruct((S, FOX_HEADS * FOX_HEAD_DIM), BF16),
                   jax.ShapeDtypeStruct((FOX_HEADS, S, FOX_PAD), F32)],
        compiler_params=_params("parallel", "parallel"),
    )(xt)


def fox_delta(do, o, *, name):
    H, S, Dh = o.shape
    t = _pick(S, (2048, 1024, 512, 256, 128))

    def body(do_ref, o_ref, d_ref):
        d_ref[0] = jnp.sum(do_ref[0] * o_ref[0], axis=-1, keepdims=True)

    blk = pl.BlockSpec((1, t, Dh), lambda h, i: (h, i, 0))
    return pl.pallas_call(
        body, name=name, grid=(H, S // t), in_specs=[blk, blk],
        out_specs=pl.BlockSpec((1, t, 1), lambda h, i: (h, i, 0)),
        out_shape=jax.ShapeDtypeStruct((H, S, 1), F32), compiler_params=_params("parallel", "parallel"),
    )(do, o)


def fox_bwd_dq(q, k, v, do, lse, delta, *, name):
    H, S, W = q.shape
    t = _fox_tile(S)
    n = S // t
    G = FOX_GROUP

    def body(q_ref, k_ref, v_ref, do_ref, lse_ref, dl_ref, dq_ref, acc_sc):
        i, j = pl.program_id(1), pl.program_id(2)

        @pl.when(j == 0)
        def _():
            acc_sc[...] = jnp.zeros_like(acc_sc)

        def step(masked):
            for g in range(G):
                s = lax.dot_general(q_ref[g], k_ref[g], _NT, preferred_element_type=F32)
                p = jnp.exp(s - lse_ref[g])
                if masked:
                    p = jnp.where(_causal_mask(t), p, 0.0)
                dp = lax.dot_general(do_ref[g], v_ref[g], _NT, preferred_element_type=F32)
                ds = p * (dp - dl_ref[g])
                acc_sc[g] += jnp.dot(ds.astype(BF16), k_ref[g], preferred_element_type=F32)

        @pl.when(j < i)
        def _():
            step(False)

        @pl.when(j == i)
        def _():
            step(True)
            dq_ref[...] = acc_sc[...]

    qs = pl.BlockSpec((G, t, W), lambda h, i, j: (h, i, 0))
    ks = pl.BlockSpec((G, t, W), lambda h, i, j: (h, jnp.minimum(j, i), 0))
    col = pl.BlockSpec((G, t, 1), lambda h, i, j: (h, i, 0))
    return pl.pallas_call(
        body, name=name, grid=(H // G, n, n), in_specs=[qs, ks, ks, qs, col, col], out_specs=qs,
        out_shape=jax.ShapeDtypeStruct((H, S, W), F32),
        scratch_shapes=[pltpu.VMEM((G, t, W), F32)],
        compiler_params=_params("parallel", "parallel", "arbitrary"),
    )(q, k, v, do, lse, delta)


def fox_bwd_dkv(q, k, v, do, lse_row, delta_row, *, name):
    H, S, W = q.shape
    t = _fox_tile(S)
    n = S // t
    G = FOX_GROUP

    def body(q_ref, k_ref, v_ref, do_ref, lse_ref, dl_ref, dk_ref, dv_ref, dk_sc, dv_sc):
        j, i = pl.program_id(1), pl.program_id(2)

        @pl.when(i == 0)
        def _():
            dk_sc[...] = jnp.zeros_like(dk_sc)
            dv_sc[...] = jnp.zeros_like(dv_sc)

        def step(masked):
            for g in range(G):
                st = lax.dot_general(k_ref[g], q_ref[g], _NT, preferred_element_type=F32)
                pt = jnp.exp(st - lse_ref[g])
                if masked:
                    pt = jnp.where(_causal_mask(t, keys_on_rows=True), pt, 0.0)
                dpt = lax.dot_general(v_ref[g], do_ref[g], _NT, preferred_element_type=F32)
                dst = pt * (dpt - dl_ref[g])
                dv_sc[g] += jnp.dot(pt.astype(BF16), do_ref[g], preferred_element_type=F32)
                dk_sc[g] += jnp.dot(dst.astype(BF16), q_ref[g], preferred_element_type=F32)

        @pl.when(i > j)
        def _():
            step(False)

        @pl.when(i == j)
        def _():
            step(True)

        @pl.when(i == n - 1)
        def _():
            dk_ref[...] = dk_sc[...]
            dv_ref[...] = dv_sc[...]

    qs = pl.BlockSpec((G, t, W), lambda h, j, i: (h, jnp.maximum(i, j), 0))
    ks = pl.BlockSpec((G, t, W), lambda h, j, i: (h, j, 0))
    qrow = pl.BlockSpec((G, 1, t), lambda h, j, i: (h, 0, jnp.maximum(i, j)))
    return pl.pallas_call(
        body, name=name, grid=(H // G, n, n), in_specs=[qs, ks, ks, qs, qrow, qrow], out_specs=[ks, ks],
        out_shape=[jax.ShapeDtypeStruct((H, S, W), F32)] * 2,
        scratch_shapes=[pltpu.VMEM((G, t, W), F32), pltpu.VMEM((G, t, W), F32)],
        compiler_params=_params("parallel", "parallel", "arbitrary"),
    )(q, k, v, do, lse_row, delta_row)


def merge_fwd(proj, branches, D, *, name):
    S = proj.shape[0]
    ts = _pick(S, (256, 128))

    def body(g0, g1, g2, b0, b1, b2, o_ref):
        acc = _sig(g0[...]) * b0[...] + _sig(g1[...]) * b1[...] + _sig(g2[...]) * b2[...]
        o_ref[...] = acc.astype(BF16)

    gate = lambda n: pl.BlockSpec((ts, D), lambda i: (i, n))
    row = pl.BlockSpec((ts, D), lambda i: (i, 0))
    return pl.pallas_call(
        body, name=name, grid=(S // ts,), in_specs=[gate(0), gate(1), gate(2), row, row, row], out_specs=row,
        out_shape=jax.ShapeDtypeStruct((S, D), BF16), compiler_params=_params("parallel"),
    )(proj, proj, proj, *branches)


def merge_bwd(proj, branches, dm, D, *, name):
    S = proj.shape[0]
    ts = _pick(S, (256, 128))

    def body(g0, g1, g2, b0, b1, b2, dm_ref, db0, db1, db2, dg0, dg1, dg2):
        dmv = dm_ref[...]
        for g_ref, b_ref, db_ref, dg_ref in ((g0, b0, db0, dg0), (g1, b1, db1, dg1), (g2, b2, db2, dg2)):
            s = _sig(g_ref[...])
            db_ref[...] = (dmv * s).astype(BF16)
            dg_ref[...] = (dmv * b_ref[...] * (s * (1.0 - s))).astype(BF16)

    gate = lambda n: pl.BlockSpec((ts, D), lambda i: (i, n))
    row = pl.BlockSpec((ts, D), lambda i: (i, 0))
    return pl.pallas_call(
        body, name=name, grid=(S // ts,), in_specs=[gate(0), gate(1), gate(2), row, row, row, row],
        out_specs=[row] * 6, out_shape=[jax.ShapeDtypeStruct((S, D), BF16)] * 6,
        compiler_params=_params("parallel"),
    )(proj, proj, proj, *branches, dm)


def xa_fwd(q, k, v, *, name):
    S, D = q.shape
    M = k.shape[0]
    dh = D // XA_HEADS
    scale = dh ** -0.5
    t = _pick(S, (512, 256, 128))

    def body(q_ref, k_ref, v_ref, o_ref):
        for h in range(XA_HEADS):
            cols = slice(h * dh, (h + 1) * dh)
            s = lax.dot_general(q_ref[:, cols], k_ref[:, cols], (((1,), (1,)), ((), ())),
                                preferred_element_type=F32) * scale
            p = jnp.exp(s - jnp.max(s, axis=-1, keepdims=True))
            p = p / jnp.sum(p, axis=-1, keepdims=True)
            o_ref[:, cols] = jnp.dot(p.astype(BF16), v_ref[:, cols], preferred_element_type=F32).astype(BF16)

    row = pl.BlockSpec((t, D), lambda i: (i, 0))
    full = pl.BlockSpec((M, D), lambda i: (0, 0))
    return pl.pallas_call(
        body, name=name, grid=(S // t,), in_specs=[row, full, full], out_specs=row,
        out_shape=jax.ShapeDtypeStruct((S, D), BF16), compiler_params=_params("parallel"),
    )(q, k, v)


def xa_bwd(q, k, v, do, *, name):
    S, D = q.shape
    M = k.shape[0]
    dh = D // XA_HEADS
    scale = dh ** -0.5
    t = _pick(S, (512, 256, 128))

    def body(q_ref, k_ref, v_ref, do_ref, dq_ref, dk_ref, dv_ref):
        i = pl.program_id(0)

        @pl.when(i == 0)
        def _():
            dk_ref[...] = jnp.zeros_like(dk_ref)
            dv_ref[...] = jnp.zeros_like(dv_ref)

        for h in range(XA_HEADS):
            cols = slice(h * dh, (h + 1) * dh)
            qh, kh, vh = q_ref[:, cols], k_ref[:, cols], v_ref[:, cols]
            dob = do_ref[:, cols].astype(BF16)
            s = lax.dot_general(qh, kh, (((1,), (1,)), ((), ())), preferred_element_type=F32) * scale
            p = jnp.exp(s - jnp.max(s, axis=-1, keepdims=True))
            p = p / jnp.sum(p, axis=-1, keepdims=True)
            dp = lax.dot_general(dob, vh, (((1,), (1,)), ((), ())), preferred_element_type=F32)
            ds = (p * (dp - jnp.sum(p * dp, axis=-1, keepdims=True)) * scale).astype(BF16)
            dq_ref[:, cols] = jnp.dot(ds, kh, preferred_element_type=F32).astype(BF16)
            dk_ref[:, cols] += lax.dot_general(ds, qh, (((0,), (0,)), ((), ())), preferred_element_type=F32)
            dv_ref[:, cols] += lax.dot_general(p.astype(BF16), dob, (((0,), (0,)), ((), ())),
                                               preferred_element_type=F32)

    row = pl.BlockSpec((t, D), lambda i: (i, 0))
    full = pl.BlockSpec((M, D), lambda i: (0, 0))
    return pl.pallas_call(
        body, name=name, grid=(S // t,), in_specs=[row, full, full, row], out_specs=[row, full, full],
        out_shape=[jax.ShapeDtypeStruct((S, D), BF16), jax.ShapeDtypeStruct((M, D), F32),
                   jax.ShapeDtypeStruct((M, D), F32)],
        compiler_params=_params("arbitrary"),
    )(q, k, v, do)


def mixer_fwd(x, w, tag):
    S, D = x.shape
    h = rms_fwd(x, w["mix_norm"], name=f"{tag}_rms")
    proj = mm(h, w["w_in"], name=f"{tag}_proj")
    y_a = conv_fwd(proj, w["conv_w"], D, name=f"{tag}_conv")
    y_b = sg_fwd(proj, w["sg_norm"], w["sg_w"], w["sg_bt"], D, name=f"{tag}_sg")
    c = fox_cumlog(proj, w["fox_b_f"], D, name=f"{tag}_cumlog")
    qh, kh, vh, qt, vt = fox_pack(proj, c, D, FOX_HEAD_DIM ** -0.5, name=f"{tag}_foxpack")
    ot, lse = fox_fwd(qt, kh, vt, name=f"{tag}_fox")
    y_c, o = heads_unpack_t(ot, name=f"{tag}_foxout")
    ys = (y_a, y_b, y_c)
    branches = [mm(ys[n], w["w_branch"][n], name=f"{tag}_branch{n}") for n in range(N_BRANCH)]
    merged = merge_fwd(proj, branches, D, name=f"{tag}_merge")
    y = mm(merged, w["w_out"], name=f"{tag}_out", res=x)
    return y, (x, h, proj, ys, qh, kh, vh, o, lse, branches, merged)


def mixer_bwd(dx, saved, w, tag):
    x, h, proj, ys, qh, kh, vh, o, lse, branches, merged = saved
    S, D = x.shape
    grads = {"w_out": mm_tn(merged, dx, name=f"{tag}_dwout")}
    dmerged = mm(dx, w["w_out_t"], name=f"{tag}_dmerged")
    outs = merge_bwd(proj, branches, dmerged, D, name=f"{tag}_dmerge")
    dbr, dgl = outs[:3], outs[3:]
    grads["w_branch"] = jnp.stack([mm_tn(ys[n], dbr[n], name=f"{tag}_dwbr{n}") for n in range(N_BRANCH)])
    dys = [mm(dbr[n], w["w_branch_t"][n], name=f"{tag}_dy{n}") for n in range(N_BRANCH)]
    d_ab, d_ac, d_ah, grads["conv_w"] = conv_bwd(proj, w["conv_w"], dys[0], D, name=f"{tag}_dconv")
    d_u, d_v, grads["sg_w"], d_sgb, grads["sg_norm"] = sg_bwd(
        proj, w["sg_norm"], w["sg_w"], w["sg_bt"], dys[1], D, name=f"{tag}_dsg")
    grads["sg_b"] = d_sgb[:, :, 0]
    do = heads_pack(dys[2], name=f"{tag}_dopack")
    delta = fox_delta(do, o, name=f"{tag}_delta")
    dq = fox_bwd_dq(qh, kh, vh, do, lse.reshape(FOX_HEADS, S, 1), delta, name=f"{tag}_dq")
    dk, dv = fox_bwd_dkv(qh, kh, vh, do, lse, delta.reshape(FOX_HEADS, 1, S), name=f"{tag}_dkv")
    dc_rows = fox_dc(dq, dk, name=f"{tag}_dc")
    d_fl, d_bf = fox_dlogit(proj, w["fox_b_f"], dc_rows, D, name=f"{tag}_dflogit")
    grads["fox_b_f"] = d_bf[0, :FOX_HEADS]
    dproj = jnp.concatenate(
        list(dgl) + [d_ab, d_ac, d_ah, d_u, d_v, heads_unpack(dq, FOX_HEAD_DIM ** -0.5, name=f"{tag}_dqout"),
                     heads_unpack(dk, 1.0, name=f"{tag}_dkout"), heads_unpack(dv, 1.0, name=f"{tag}_dvout"), d_fl],
        axis=1)
    grads["w_in"] = mm_tn(h, dproj, name=f"{tag}_dwin")
    dh = mm(dproj, w["w_in_t"], name=f"{tag}_dh")
    dx, grads["mix_norm"] = rms_bwd(x, w["mix_norm"], dh, dx, name=f"{tag}_drms")
    return dx, grads


def xattn_fwd(x, mem, w, tag):
    h = rms_fwd(x, w["xa_norm"], name=f"{tag}_rms")
    m = rms_fwd(mem, w["mem_norm"], name=f"{tag}_mrms")
    q = mm(h, w["xa_wq"], name=f"{tag}_q", out_dtype=BF16)
    k = mm(m, w["xa_wk"], name=f"{tag}_k", out_dtype=BF16)
    v = mm(m, w["xa_wv"], name=f"{tag}_v", out_dtype=BF16)
    o = xa_fwd(q, k, v, name=f"{tag}_attn")
    y = mm(o, w["xa_wo"], name=f"{tag}_o", res=x)
    return y, (x, h, m, q, k, v, o)


def xattn_bwd(dx, mem, saved, w, tag):
    x, h, m, q, k, v, o = saved
    grads = {"xa_wo": mm_tn(o, dx, name=f"{tag}_dwo")}
    do = mm(dx, w["xa_wo_t"], name=f"{tag}_do")
    dq, dk, dv = xa_bwd(q, k, v, do, name=f"{tag}_dattn")
    grads["xa_wq"] = mm_tn(h, dq, name=f"{tag}_dwq")
    grads["xa_wk"] = mm_tn(m, dk, name=f"{tag}_dwk")
    grads["xa_wv"] = mm_tn(m, dv, name=f"{tag}_dwv")
    dh = mm(dq, w["xa_wq_t"], name=f"{tag}_dh")
    dm = mm(dk, w["xa_wk_t"], name=f"{tag}_dm1")
    dm = mm(dv, w["xa_wv_t"], name=f"{tag}_dm2", res=dm)
    _, grads["mem_norm"] = rms_bwd(mem, w["mem_norm"], dm, None, name=f"{tag}_dmrms")
    dx, grads["xa_norm"] = rms_bwd(x, w["xa_norm"], dh, dx, name=f"{tag}_drms")
    return dx, grads


def _mesh_pos():
    return lax.axis_index("x"), lax.axis_index("y"), lax.axis_index("c")


def _flip(v, bit):
    return 1 - v if bit else v


def all_gather(xs, *, name):
    n = len(xs)

    def body(*refs):
        x_refs, out_refs = refs[:n], refs[n:2 * n]
        send_sems, recv_sems, local_sems = refs[2 * n:]
        mx, my, mc = _mesh_pos()
        me, sibling = (mx, my, mc), (mx, my, 1 - mc)
        chips = [(1 - mx, my), (mx, 1 - my), (1 - mx, 1 - my)]

        def copy(a, k, block, to, from_input=False):
            slot = out_refs[a].at[4 * block[0] + 2 * block[1] + block[2]]
            return pltpu.make_async_remote_copy(
                src_ref=x_refs[a] if from_input else slot, dst_ref=slot,
                send_sem=send_sems.at[a, k], recv_sem=recv_sems.at[a, k], device_id=to, device_id_type=MESH_IDS)

        started = []
        for a in range(n):
            mine = pltpu.make_async_copy(x_refs[a], out_refs[a].at[4 * mx + 2 * my + mc], local_sems.at[a])
            mine.start()
            started.append(mine)
        sends = []
        for a in range(n):
            sends.append(copy(a, 0, me, sibling, from_input=True))
            sends += [copy(a, 1 + j, me, (*chip, mc), from_input=True) for j, chip in enumerate(chips)]
        for cp in sends:
            cp.start()
        for j, chip in enumerate(chips):
            for a in range(n):
                copy(a, 1 + j, (*chip, mc), me).wait_recv()
                onward = copy(a, 4 + j, (*chip, mc), sibling)
                onward.start()
                sends.append(onward)
        for a in range(n):
            copy(a, 0, sibling, me).wait_recv()
            for j, chip in enumerate(chips):
                copy(a, 4 + j, (*chip, 1 - mc), me).wait_recv()
        for cp in sends:
            cp.wait_send()
        for mine in started:
            mine.wait()

    any_spec = pl.BlockSpec(memory_space=pl.ANY)
    return pl.pallas_call(
        body, name=name, out_shape=[jax.ShapeDtypeStruct((N_DEV,) + x.shape, x.dtype) for x in xs],
        in_specs=[any_spec] * n, out_specs=[any_spec] * n,
        scratch_shapes=[pltpu.SemaphoreType.DMA((n, 7)), pltpu.SemaphoreType.DMA((n, 7)),
                        pltpu.SemaphoreType.DMA((n,))],
    )(*xs)


def all_to_all(gs, *, name):
    n = len(gs)

    def body(*refs):
        g_refs, out_refs = refs[:n], refs[n:2 * n]
        send_sems, recv_sems, local_sems = refs[2 * n:]
        mx, my, mc = _mesh_pos()
        me = 4 * mx + 2 * my + mc
        started, copies = [], []
        for a in range(n):
            mine = pltpu.make_async_copy(g_refs[a].at[me], out_refs[a].at[me], local_sems.at[a])
            mine.start()
            started.append(mine)
        for k in range(1, N_DEV):
            peer = (_flip(mx, k & 4), _flip(my, k & 2), _flip(mc, k & 1))
            peer_slot = 4 * peer[0] + 2 * peer[1] + peer[2]
            for a in range(n):
                sems = dict(send_sem=send_sems.at[a, k - 1], recv_sem=recv_sems.at[a, k - 1], device_id=peer,
                            device_id_type=MESH_IDS)
                send = pltpu.make_async_remote_copy(src_ref=g_refs[a].at[peer_slot], dst_ref=out_refs[a].at[me], **sems)
                arrive = pltpu.make_async_remote_copy(src_ref=g_refs[a].at[peer_slot],
                                                      dst_ref=out_refs[a].at[peer_slot], **sems)
                send.start()
                copies.append((send, arrive))
        for send, arrive in copies:
            arrive.wait_recv()
        for send, arrive in copies:
            send.wait_send()
        for mine in started:
            mine.wait()

    any_spec = pl.BlockSpec(memory_space=pl.ANY)
    return pl.pallas_call(
        body, name=name, out_shape=[jax.ShapeDtypeStruct(g.shape, g.dtype) for g in gs],
        in_specs=[any_spec] * n, out_specs=[any_spec] * n,
        scratch_shapes=[pltpu.SemaphoreType.DMA((n, 7)), pltpu.SemaphoreType.DMA((n, 7)),
                        pltpu.SemaphoreType.DMA((n,))],
    )(*gs)


ADAM_BLOCK_ELEMS = 256 * 1024


def reduce_adamw(parts, row0, w, m, v, *, name):
    R, C = w.shape
    n_parts = parts.shape[0]
    tr = R
    for cand in (512, 256, 128, 64, 32, 16):
        if R % cand == 0 and row0 % cand == 0 and cand * C <= ADAM_BLOCK_ELEMS:
            tr = cand
            break
    assert row0 % tr == 0 and (tr % 16 == 0 or (row0 == 0 and parts.shape[1] == R))
    bc1 = 1.0 - ADAM_B1 ** ADAM_STEP
    bc2 = 1.0 - ADAM_B2 ** ADAM_STEP

    def body(p_ref, w_ref, m_ref, v_ref, g_ref, d_ref, nm_ref, nv_ref):
        g = p_ref[0].astype(F32)
        for d in range(1, n_parts):
            g = g + p_ref[d].astype(F32)
        nm = ADAM_B1 * m_ref[...] + (1.0 - ADAM_B1) * g
        nv = ADAM_B2 * v_ref[...] + (1.0 - ADAM_B2) * (g * g)
        m_hat = nm / bc1
        v_hat = nv / bc2
        g_ref[...] = g
        d_ref[...] = -ADAM_LR * (m_hat / (jnp.sqrt(v_hat) + ADAM_EPS) + ADAM_WD * w_ref[...])
        nm_ref[...] = nm
        nv_ref[...] = nv

    row = pl.BlockSpec((tr, C), lambda i: (i, 0))
    blk0 = row0 // tr
    return pl.pallas_call(
        body, name=name, grid=(R // tr,),
        in_specs=[pl.BlockSpec((n_parts, tr, C), lambda i: (0, blk0 + i, 0)), row, row, row],
        out_specs=[row] * 4, out_shape=[jax.ShapeDtypeStruct((R, C), F32)] * 4,
        compiler_params=_params("parallel"),
    )(parts, w, m, v)


SHARDED = {
    "ffn1_w_gate": 2, "ffn1_w_up": 2, "ffn1_w_down": 1, "w_in": 2, "conv_w": 2, "w_branch": 3, "w_out": 1,
    "xa_wq": 1, "xa_wk": 1, "xa_wv": 1, "xa_wo": 1, "ffn2_w_gate": 2, "ffn2_w_up": 2, "ffn2_w_down": 1,
}
GROUPS = (("ffn1_w_gate", "ffn1_w_up", "ffn2_w_gate", "ffn2_w_up"),
          ("ffn1_w_down", "ffn2_w_down", "w_out", "xa_wq", "xa_wk", "xa_wv", "xa_wo"),
          ("w_in",), ("w_branch",))
REPLICATED = ("ffn1_norm", "mix_norm", "sg_norm", "sg_w", "sg_b", "fox_b_f", "xa_norm", "mem_norm", "ffn2_norm",
              "final_norm")
WEIGHTS = ("ffn1_norm", "ffn1_w_gate", "ffn1_w_up", "ffn1_w_down", "mix_norm", "w_in", "conv_w", "sg_norm", "sg_w",
           "sg_b", "fox_b_f", "w_branch", "w_out", "xa_norm", "mem_norm", "xa_wq", "xa_wk", "xa_wv", "xa_wo",
           "ffn2_norm", "ffn2_w_gate", "ffn2_w_up", "ffn2_w_down", "final_norm")
PACK_ROWS = 1024


def _rows(a):
    return a.reshape(-1, a.shape[-1])


def _pack(arrays, dtype):
    flat = jnp.concatenate([a.reshape(-1).astype(dtype) for a in arrays])
    n = flat.shape[0]
    unit = PACK_ROWS * LANES
    total = -(-n // unit) * unit
    return jnp.pad(flat, (0, total - n)).reshape(total // LANES, LANES)


def _unpack(buf, shapes):
    flat = buf.reshape(-1)
    out, off = [], 0
    for shp in shapes:
        n = 1
        for s in shp:
            n *= s
        out.append(flat[off:off + n].reshape(tuple(shp)))
        off += n
    return out


def _to_dev_major(full, axis):
    shp = full.shape
    a = full.reshape(shp[:axis] + (N_DEV, shp[axis] // N_DEV) + shp[axis + 1:])
    return jnp.moveaxis(a, axis, 0)


def _from_dev_major(a, axis):
    a = jnp.moveaxis(a, 0, axis)
    shp = a.shape
    return a.reshape(shp[:axis] + (shp[axis] * shp[axis + 1],) + shp[axis + 2:])


def _relayout_w_in(w_in, D):
    main = 8 * SEG
    pad = jnp.zeros((w_in.shape[0], SEG - FOX_HEADS), w_in.dtype)
    return jnp.concatenate([w_in[:, main + FOX_HEADS:], w_in[:, :main], w_in[:, main:main + FOX_HEADS], pad], axis=1)


def _unlayout_w_in(g, D):
    return jnp.concatenate([g[:, 3 * D:3 * D + 8 * SEG], g[:, 3 * D + 8 * SEG:3 * D + 8 * SEG + FOX_HEADS],
                            g[:, :3 * D]], axis=1)


def _layer_weights(full, rep, l, D):
    t = lambda a: a.T
    w_in = _relayout_w_in(full["w_in"][l], D)
    ffn = {}
    for tag in ("ffn1", "ffn2"):
        ffn[tag] = {"norm": rep[f"{tag}_norm"][l][None, :]}
        for n in ("w_gate", "w_up", "w_down"):
            ffn[tag][n] = full[f"{tag}_{n}"][l]
            ffn[tag][n + "_t"] = t(full[f"{tag}_{n}"][l])
    mix = {
        "mix_norm": rep["mix_norm"][l][None, :], "w_in": w_in, "w_in_t": t(w_in),
        "conv_w": full["conv_w"][l], "sg_norm": rep["sg_norm"][l][None, :], "sg_w": rep["sg_w"][l],
        "sg_bt": rep["sg_b"][l][:, :, None],
        "fox_b_f": jnp.pad(rep["fox_b_f"][l], (0, LANES - FOX_HEADS))[None, :],
        "w_branch": full["w_branch"][l], "w_branch_t": jnp.swapaxes(full["w_branch"][l], 1, 2),
        "w_out": full["w_out"][l], "w_out_t": t(full["w_out"][l]),
    }
    xa = {"xa_norm": rep["xa_norm"][l][None, :], "mem_norm": rep["mem_norm"][l][None, :]}
    for n in ("xa_wq", "xa_wk", "xa_wv", "xa_wo"):
        xa[n] = full[n][l]
        xa[n + "_t"] = t(full[n][l])
    return ffn, mix, xa


def kernel(x, mem, ffn1_norm, ffn1_w_gate, ffn1_w_up, ffn1_w_down, mix_norm, w_in, conv_w, sg_norm, sg_w, sg_b, fox_b_f, w_branch, w_out, xa_norm, mem_norm, xa_wq, xa_wk, xa_wv, xa_wo, ffn2_norm, ffn2_w_gate, ffn2_w_up, ffn2_w_down, final_norm, loss_target, m_ffn1_norm, m_ffn1_w_gate, m_ffn1_w_up, m_ffn1_w_down, m_mix_norm, m_w_in, m_conv_w, m_sg_norm, m_sg_w, m_sg_b, m_fox_b_f, m_w_branch, m_w_out, m_xa_norm, m_mem_norm, m_xa_wq, m_xa_wk, m_xa_wv, m_xa_wo, m_ffn2_norm, m_ffn2_w_gate, m_ffn2_w_up, m_ffn2_w_down, m_final_norm, v_ffn1_norm, v_ffn1_w_gate, v_ffn1_w_up, v_ffn1_w_down, v_mix_norm, v_w_in, v_conv_w, v_sg_norm, v_sg_w, v_sg_b, v_fox_b_f, v_w_branch, v_w_out, v_xa_norm, v_mem_norm, v_xa_wq, v_xa_wk, v_xa_wv, v_xa_wo, v_ffn2_norm, v_ffn2_w_gate, v_ffn2_w_up, v_ffn2_w_down, v_final_norm):
    args = locals()
    wts = {n: args[n] for n in WEIGHTS}
    mom = {n: args["m_" + n] for n in WEIGHTS}
    var = {n: args["v_" + n] for n in WEIGHTS}
    depth = ffn1_norm.shape[0]
    S, D = x.shape[1], x.shape[2]
    xs, ms, tgt = x[0], mem[0], loss_target[0]

    outs = all_gather([jnp.concatenate([_rows(wts[n]).astype(BF16) for n in grp]) for grp in GROUPS]
                      + [_rows(conv_w)], name="gather_weights")
    full, row0 = {}, {}
    for grp, got in zip(GROUPS + (("conv_w",),), outs):
        off = 0
        for n in grp:
            rows = _rows(wts[n]).shape[0]
            row0[n] = off
            full[n] = _from_dev_major(got[:, off:off + rows].reshape((N_DEV,) + wts[n].shape), SHARDED[n])
            off += rows
    rep = {n: wts[n] for n in REPLICATED}
    layers = [_layer_weights(full, rep, l, D) for l in range(depth)]

    saved = []
    h = xs
    for l, (ffn, mix, xa) in enumerate(layers):
        h, s1 = ffn_fwd(h, ffn["ffn1"], f"l{l}_ffn1")
        h, s2 = mixer_fwd(h, mix, f"l{l}_mix")
        h, s3 = xattn_fwd(h, ms, xa, f"l{l}_xa")
        h, s4 = ffn_fwd(h, ffn["ffn2"], f"l{l}_ffn2")
        saved.append((s1, s2, s3, s4))
    dx, d_final, loss_cols = final_loss_bwd(h, final_norm[None, :], tgt, name="final_loss")
    loss = lax.psum(0.5 * jnp.sum(loss_cols) / D, ("x", "y", "c"))

    per_layer = []
    for l in reversed(range(depth)):
        ffn, mix, xa = layers[l]
        s1, s2, s3, s4 = saved[l]
        g = {}
        dx, g4 = ffn_bwd(dx, s4, ffn["ffn2"], f"l{l}_ffn2")
        dx, g3 = xattn_bwd(dx, ms, s3, xa, f"l{l}_xa")
        dx, g2 = mixer_bwd(dx, s2, mix, f"l{l}_mix")
        dx, g1 = ffn_bwd(dx, s1, ffn["ffn1"], f"l{l}_ffn1")
        for tag, gg in (("ffn1", g1), ("ffn2", g4)):
            for n in ("w_gate", "w_up", "w_down"):
                g[f"{tag}_{n}"] = gg[n]
            g[f"{tag}_norm"] = gg["norm"][0]
        g.update(g3)
        g["xa_norm"], g["mem_norm"] = g3["xa_norm"][0], g3["mem_norm"][0]
        g.update({k: v for k, v in g2.items() if k != "w_in"})
        g["w_in"] = _unlayout_w_in(g2["w_in"], D)
        g["mix_norm"], g["sg_norm"] = g2["mix_norm"][0], g2["sg_norm"][0]
        per_layer.append(g)
    per_layer.reverse()
    grads = {n: jnp.stack([per_layer[l][n] for l in range(depth)]) for n in WEIGHTS if n != "final_norm"}
    grads["final_norm"] = d_final[0]

    def dev_major_rows(n):
        a = _to_dev_major(grads[n], SHARDED[n])
        return a.reshape(N_DEV, -1, a.shape[-1]).astype(BF16)

    parts = all_to_all([jnp.concatenate([dev_major_rows(n) for n in grp], axis=1) for grp in GROUPS],
                       name="scatter_grads")
    res = {k: {} for k in ("g", "d", "m", "v")}

    def adamw(n, parts_n, off):
        outs = reduce_adamw(parts_n, off, _rows(wts[n]), _rows(mom[n]), _rows(var[n]), name=f"adamw_{n}")
        for k, o in zip(("g", "d", "m", "v"), outs):
            res[k][n] = o.reshape(wts[n].shape)

    for grp, got in zip(GROUPS, parts):
        for n in grp:
            adamw(n, got, row0[n])

    small = list(REPLICATED)
    shapes = [wts[n].shape for n in small]
    conv_zero = jnp.zeros(grads["conv_w"].shape, F32)
    parts = all_gather([_pack([grads[n] for n in small] + [grads["conv_w"]], F32)], name="gather_small_grads")[0]
    outs = reduce_adamw(parts, 0, _pack([wts[n] for n in small] + [conv_zero], F32),
                        _pack([mom[n] for n in small] + [conv_zero], F32),
                        _pack([var[n] for n in small] + [conv_zero], F32), name="adamw_replicated")
    for k, o in zip(("g", "d", "m", "v"), outs):
        res[k].update(dict(zip(small, _unpack(o, shapes + [conv_zero.shape])[:-1])))
    conv_g = _unpack(outs[0], shapes + [conv_zero.shape])[-1]
    me = 4 * lax.axis_index("x") + 2 * lax.axis_index("y") + lax.axis_index("c")
    width = conv_w.shape[-1]
    conv_g = lax.dynamic_slice_in_dim(conv_g, me * width, width, axis=2)
    adamw("conv_w", _rows(conv_g)[None], 0)

    return (loss, dx[None], *[res["g"][n] for n in WEIGHTS], *[res["d"][n] for n in WEIGHTS],
            *[res["m"][n] for n in WEIGHTS], *[res["v"][n] for n in WEIGHTS])
```

```python
import functools

import jax
import jax.numpy as jnp
from jax import lax
from jax.experimental import pallas as pl
from jax.experimental.pallas import tpu as pltpu

F32 = jnp.float32
BF16 = jnp.bfloat16

N_DEV = 8
RMS_EPS = 1e-6
SEG = 512
FOX_HEADS = 8
FOX_HEAD_DIM = 64
SG_GROUPS = 4
CHUNK = 128
XA_HEADS = 4
N_BRANCH = 3
LANES = 128
VMEM_LIMIT_BYTES = 48 * 1024 * 1024
NEG_BIG = -1e30

ADAM_LR = 0.001
ADAM_B1 = 0.9
ADAM_B2 = 0.999
ADAM_EPS = 1e-08
ADAM_WD = 0.01
ADAM_STEP = 10

_GELU_K = 0.7978845608028654
_GELU_C = 0.044715

MESH_IDS = pl.DeviceIdType.MESH


def _pick(n, candidates):
    for c in candidates:
        if c <= n and n % c == 0:
            return c
    return n


def _params(*sem):
    return pltpu.CompilerParams(dimension_semantics=sem, vmem_limit_bytes=VMEM_LIMIT_BYTES)


def _sig(x):
    return 1.0 / (1.0 + jnp.exp(-x))


def _gelu(x):
    t = jnp.tanh(_GELU_K * (x + _GELU_C * x * x * x))
    return 0.5 * x * (1.0 + t), t


def _gelu_grad(x, t):
    return 0.5 * (1.0 + t) + 0.5 * x * (1.0 - t * t) * _GELU_K * (1.0 + 3.0 * _GELU_C * x * x)


_WIDE_TILES = (1536, 1408, 1280, 1024, 768, 512, 384, 256, 128)
MM_VMEM_BUDGET = 36 * 1024 * 1024
MM_ACC_BYTES = 6 * 1024 * 1024 + 512 * 1024


def mm(a, b, *, name, out_dtype=F32, res=None, scale=1.0, extras=(), epilogue=None, out_dtypes=None, tm=1024):
    M, K = a.shape
    K2, N = b.shape
    assert K == K2
    custom = epilogue is not None
    if not custom:
        extras = () if res is None else (res,)
        out_dtypes = (out_dtype,)

        def epilogue(acc, *ex):
            if scale != 1.0:
                acc = acc * scale
            return ((ex[0] + acc) if ex else acc,)

    n_ex, n_out = len(extras), len(out_dtypes)
    tn = _pick(N, _WIDE_TILES)
    tk = K if K <= 3072 else _pick(K, (2560, 2048, 1536, 1024, 512, 256, 128))
    nk = K // tk
    tile_bytes = sum(e.dtype.itemsize for e in extras) + sum(jnp.dtype(d).itemsize for d in out_dtypes)
    for tm in (tm, 512, 256, 128):
        blocks = 2 * (tm * tk * a.dtype.itemsize + tk * tn * 2 + tm * tn * tile_bytes)
        if M % tm == 0 and blocks + (tm * tn * 4 if nk > 1 else 0) <= MM_VMEM_BUDGET:
            break
    else:
        tm = M

    def body(*refs):
        a_ref, b_ref = refs[:2]
        ex_refs = refs[2:2 + n_ex]
        o_refs = refs[2 + n_ex:2 + n_ex + n_out]

        def finish(acc):
            for o_ref, val, dt in zip(o_refs, epilogue(acc, *[r[...] for r in ex_refs]), out_dtypes, strict=True):
                o_ref[...] = val.astype(dt)

        part = jnp.dot(a_ref[...].astype(BF16), b_ref[...].astype(BF16), preferred_element_type=F32)
        if nk == 1:
            finish(part)
        else:
            acc_ref = refs[-1]
            k = pl.program_id(2)

            @pl.when(k == 0)
            def _():
                acc_ref[...] = part

            @pl.when(k > 0)
            def _():
                acc_ref[...] += part

            @pl.when(k == nk - 1)
            def _():
                finish(acc_ref[...])

    tile = pl.BlockSpec((tm, tn), lambda i, j, k: (i, j))
    outs = pl.pallas_call(
        body, name=name, grid=(M // tm, N // tn, nk),
        in_specs=[pl.BlockSpec((tm, tk), lambda i, j, k: (i, k)), pl.BlockSpec((tk, tn), lambda i, j, k: (k, j))]
        + [tile] * n_ex,
        out_specs=[tile] * n_out, out_shape=[jax.ShapeDtypeStruct((M, N), d) for d in out_dtypes],
        scratch_shapes=[pltpu.VMEM((tm, tn), F32)] if nk > 1 else [],
        compiler_params=_params("parallel", "parallel", "arbitrary"),
    )(a, b, *extras)
    return tuple(outs) if custom else outs[0]


def mm_tn(a, b, *, name, scale=1.0):
    M, K = a.shape
    M2, N = b.shape
    assert M == M2
    tm = _pick(M, (1024, 512, 256, 128))
    tk = _pick(K, (1408, 1024, 512, 256, 128))
    tn = next((c for c in _WIDE_TILES if N % c == 0 and tk * c * 4 <= MM_ACC_BYTES), N)
    nm = M // tm

    def body(a_ref, b_ref, o_ref):
        m = pl.program_id(2)
        part = lax.dot_general(a_ref[...].astype(BF16), b_ref[...].astype(BF16), (((0,), (0,)), ((), ())),
                               preferred_element_type=F32)

        @pl.when(m == 0)
        def _():
            o_ref[...] = part

        @pl.when(m > 0)
        def _():
            o_ref[...] += part

        if scale != 1.0:
            @pl.when(m == nm - 1)
            def _():
                o_ref[...] = o_ref[...] * scale

    return pl.pallas_call(
        body, name=name, grid=(K // tk, N // tn, nm),
        in_specs=[pl.BlockSpec((tm, tk), lambda i, j, m: (m, i)), pl.BlockSpec((tm, tn), lambda i, j, m: (m, j))],
        out_specs=pl.BlockSpec((tk, tn), lambda i, j, m: (i, j)),
        out_shape=jax.ShapeDtypeStruct((K, N), F32),
        compiler_params=_params("parallel", "parallel", "arbitrary"),
    )(a, b)


def rms_fwd(x, g, *, name):
    S, D = x.shape
    ts = _pick(S, (512, 256, 128))

    def body(x_ref, g_ref, h_ref):
        xv = x_ref[...]
        r = lax.rsqrt(jnp.mean(xv * xv, axis=-1, keepdims=True) + RMS_EPS)
        h_ref[...] = ((xv * r) * g_ref[...]).astype(BF16)

    return pl.pallas_call(
        body, name=name, grid=(S // ts,),
        in_specs=[pl.BlockSpec((ts, D), lambda i: (i, 0)), pl.BlockSpec((1, D), lambda i: (0, 0))],
        out_specs=pl.BlockSpec((ts, D), lambda i: (i, 0)),
        out_shape=jax.ShapeDtypeStruct((S, D), BF16),
        compiler_params=_params("parallel"),
    )(x, g)


def rms_bwd(x, g, dh, dx_in, *, name):
    S, D = x.shape
    ts = _pick(S, (512, 256, 128))
    has_in = dx_in is not None

    def body(*refs):
        if has_in:
            x_ref, g_ref, dh_ref, di_ref, dx_ref, dg_ref = refs
        else:
            x_ref, g_ref, dh_ref, dx_ref, dg_ref = refs
        xv = x_ref[...]
        dh_v = dh_ref[...]
        r = lax.rsqrt(jnp.mean(xv * xv, axis=-1, keepdims=True) + RMS_EPS)
        xh = xv * r
        gd = dh_v * g_ref[...]
        dx = r * (gd - xh * jnp.mean(gd * xh, axis=-1, keepdims=True))
        if has_in:
            dx = di_ref[...] + dx
        dx_ref[...] = dx
        part = jnp.sum(dh_v * xh, axis=0, keepdims=True)

        @pl.when(pl.program_id(0) == 0)
        def _():
            dg_ref[...] = part

        @pl.when(pl.program_id(0) > 0)
        def _():
            dg_ref[...] += part

    row = pl.BlockSpec((ts, D), lambda i: (i, 0))
    vec = pl.BlockSpec((1, D), lambda i: (0, 0))
    return pl.pallas_call(
        body, name=name, grid=(S // ts,),
        in_specs=[row, vec, row] + ([row] if has_in else []),
        out_specs=[row, vec],
        out_shape=[jax.ShapeDtypeStruct((S, D), F32), jax.ShapeDtypeStruct((1, D), F32)],
        compiler_params=_params("arbitrary"),
    )(*([x, g, dh] + ([dx_in] if has_in else [])))


def final_loss_bwd(x, g, target, *, name):
    S, D = x.shape
    ts = _pick(S, (512, 256, 128))

    def body(x_ref, g_ref, t_ref, dx_ref, dg_ref, ls_ref):
        xv = x_ref[...]
        gv = g_ref[...]
        r = lax.rsqrt(jnp.mean(xv * xv, axis=-1, keepdims=True) + RMS_EPS)
        xh = xv * r
        e = xh * gv - t_ref[...]
        dy = e * (1.0 / D)
        gd = dy * gv
        dx_ref[...] = r * (gd - xh * jnp.mean(gd * xh, axis=-1, keepdims=True))
        dg_part = jnp.sum(dy * xh, axis=0, keepdims=True)
        ls_part = jnp.sum(e * e, axis=0, keepdims=True)

        @pl.when(pl.program_id(0) == 0)
        def _():
            dg_ref[...] = dg_part
            ls_ref[...] = ls_part

        @pl.when(pl.program_id(0) > 0)
        def _():
            dg_ref[...] += dg_part
            ls_ref[...] += ls_part

    row = pl.BlockSpec((ts, D), lambda i: (i, 0))
    vec = pl.BlockSpec((1, D), lambda i: (0, 0))
    return pl.pallas_call(
        body, name=name, grid=(S // ts,), in_specs=[row, vec, row], out_specs=[row, vec, vec],
        out_shape=[jax.ShapeDtypeStruct((S, D), F32), jax.ShapeDtypeStruct((1, D), F32),
                   jax.ShapeDtypeStruct((1, D), F32)],
        compiler_params=_params("arbitrary"),
    )(x, g, target)


def _swiglu(up, gp):
    gp = gp.astype(F32)
    return up, gp * _sig(gp) * up


def _swiglu_grad(da, gp, up):
    da = da * 0.5
    gp, up = gp.astype(F32), up.astype(F32)
    s = _sig(gp)
    return da * up * (s * (1.0 + gp * (1.0 - s))), da * (gp * s)


def ffn_fwd(x, w, tag):
    h = rms_fwd(x, w["norm"], name=f"{tag}_rms")
    gp = mm(h, w["w_gate"], name=f"{tag}_gate", out_dtype=BF16)
    up, a = mm(h, w["w_up"], name=f"{tag}_up", extras=(gp,), epilogue=_swiglu, out_dtypes=(BF16, BF16))
    y = mm(a, w["w_down"], name=f"{tag}_down", res=x, scale=0.5)
    return y, (x, h, gp, up, a)


def ffn_bwd(dx, saved, w, tag):
    x, h, gp, up, a = saved
    grads = {"w_down": mm_tn(a, dx, name=f"{tag}_dwd", scale=0.5)}
    dgp, dup = mm(dx, w["w_down_t"], name=f"{tag}_da", extras=(gp, up), epilogue=_swiglu_grad,
                  out_dtypes=(BF16, BF16))
    grads["w_gate"] = mm_tn(h, dgp, name=f"{tag}_dwg")
    grads["w_up"] = mm_tn(h, dup, name=f"{tag}_dwu")
    dh = mm(dgp, w["w_gate_t"], name=f"{tag}_dh1")
    dh = mm(dup, w["w_up_t"], name=f"{tag}_dh2", res=dh)
    dx, grads["norm"] = rms_bwd(x, w["norm"], dh, dx, name=f"{tag}_drms")
    return dx, grads


SEG_AB, SEG_AC, SEG_AH, SEG_U, SEG_V, SEG_FQ, SEG_FK, SEG_FV, SEG_FL = range(9)
N_SEG = 9


def _seg_block(D, seg):
    return 3 * D // SEG + seg


def _shift_down(z, prev8, n, rows):
    out = pltpu.roll(z, n, 0)
    for r in range(n):
        out = jnp.where(rows == r, prev8[8 - n + r:8 - n + r + 1, :], out)
    return out


def _shift_up(z, next8, n, rows, ts):
    out = pltpu.roll(z, ts - n, 0)
    for r in range(n):
        out = jnp.where(rows == ts - n + r, next8[r:r + 1, :], out)
    return out


def conv_fwd(proj, conv_w, D, *, name):
    S = proj.shape[0]
    ts = _pick(S, (512, 256, 128))
    b0 = _seg_block(D, 0)

    def body(ab_ref, ac_ref, ah_ref, pc_ref, ph_ref, w_ref, y_ref):
        i = pl.program_id(0)
        rows = lax.broadcasted_iota(jnp.int32, (ts, 1), 0)
        z = ac_ref[...] * ah_ref[...]
        zp = pc_ref[...] * ph_ref[...] * (i > 0).astype(F32)
        w = w_ref[...]
        y = w[0:1, :] * _shift_down(z, zp, 2, rows) + w[1:2, :] * _shift_down(z, zp, 1, rows) + w[2:3, :] * z
        y_ref[...] = (ab_ref[...] * y).astype(BF16)

    def seg(s):
        return pl.BlockSpec((ts, SEG), lambda i: (i, b0 + s))

    def prev(s):
        return pl.BlockSpec((8, SEG), lambda i: (jnp.maximum(i * (ts // 8) - 1, 0), b0 + s))

    return pl.pallas_call(
        body, name=name, grid=(S // ts,),
        in_specs=[seg(SEG_AB), seg(SEG_AC), seg(SEG_AH), prev(SEG_AC), prev(SEG_AH),
                  pl.BlockSpec((3, SEG), lambda i: (0, 0))],
        out_specs=pl.BlockSpec((ts, SEG), lambda i: (i, 0)),
        out_shape=jax.ShapeDtypeStruct((S, SEG), BF16), compiler_params=_params("parallel"),
    )(proj, proj, proj, proj, proj, conv_w)


def conv_bwd(proj, conv_w, dy, D, *, name):
    S = proj.shape[0]
    ts = _pick(S, (512, 256, 128))
    nt = S // ts
    b0 = _seg_block(D, 0)

    def body(ab_ref, ac_ref, ah_ref, pc_ref, ph_ref, nb_ref, dy_ref, ndy_ref, w_ref,
             dab_ref, dac_ref, dah_ref, dw_ref):
        i = pl.program_id(0)
        rows = lax.broadcasted_iota(jnp.int32, (ts, 1), 0)
        ab, ac, ah = ab_ref[...], ac_ref[...], ah_ref[...]
        z = ac * ah
        zp = pc_ref[...] * ph_ref[...] * (i > 0).astype(F32)
        w = w_ref[...]
        z1 = _shift_down(z, zp, 1, rows)
        z2 = _shift_down(z, zp, 2, rows)
        y = w[0:1, :] * z2 + w[1:2, :] * z1 + w[2:3, :] * z
        dyv = dy_ref[...]
        dab_ref[...] = (dyv * y).astype(BF16)
        dyy = dyv * ab
        nyy = ndy_ref[...] * nb_ref[...] * (i < nt - 1).astype(F32)
        dz = (w[2:3, :] * dyy + w[1:2, :] * _shift_up(dyy, nyy, 1, rows, ts)
              + w[0:1, :] * _shift_up(dyy, nyy, 2, rows, ts))
        dac_ref[...] = (dz * ah).astype(BF16)
        dah_ref[...] = (dz * ac).astype(BF16)
        parts = [jnp.sum(dyy * zz, axis=0, keepdims=True) for zz in (z2, z1, z)]

        @pl.when(i == 0)
        def _():
            for k in range(3):
                dw_ref[k:k + 1, :] = parts[k]

        @pl.when(i > 0)
        def _():
            for k in range(3):
                dw_ref[k:k + 1, :] += parts[k]

    def seg(s):
        return pl.BlockSpec((ts, SEG), lambda i: (i, b0 + s))

    def prev(s):
        return pl.BlockSpec((8, SEG), lambda i: (jnp.maximum(i * (ts // 8) - 1, 0), b0 + s))

    nxt_row = lambda i: jnp.minimum((i + 1) * (ts // 8), S // 8 - 1)
    out_row = pl.BlockSpec((ts, SEG), lambda i: (i, 0))
    return pl.pallas_call(
        body, name=name, grid=(nt,),
        in_specs=[seg(SEG_AB), seg(SEG_AC), seg(SEG_AH), prev(SEG_AC), prev(SEG_AH),
                  pl.BlockSpec((8, SEG), lambda i: (nxt_row(i), b0 + SEG_AB)),
                  out_row, pl.BlockSpec((8, SEG), lambda i: (nxt_row(i), 0)),
                  pl.BlockSpec((3, SEG), lambda i: (0, 0))],
        out_specs=[out_row, out_row, out_row, pl.BlockSpec((3, SEG), lambda i: (0, 0))],
        out_shape=[jax.ShapeDtypeStruct((S, SEG), BF16)] * 3 + [jax.ShapeDtypeStruct((3, SEG), F32)],
        compiler_params=_params("arbitrary"),
    )(proj, proj, proj, proj, proj, proj, dy, dy, conv_w)


def _tril_mask():
    r = lax.broadcasted_iota(jnp.int32, (CHUNK, CHUNK), 0)
    c = lax.broadcasted_iota(jnp.int32, (CHUNK, CHUNK), 1)
    return c <= r


def sg_fwd(proj, sg_norm, sg_w, sg_bt, D, *, name):
    S = proj.shape[0]
    ts = _pick(S, (512, 256, 128))
    b0 = _seg_block(D, 0)

    def body(u_ref, v_ref, gs_ref, w_ref, b_ref, y_ref):
        ug, _ = _gelu(u_ref[...])
        vg, _ = _gelu(v_ref[...])
        vn = ((vg * lax.rsqrt(jnp.mean(vg * vg, axis=-1, keepdims=True) + RMS_EPS)) * gs_ref[...]).astype(BF16)
        mask = _tril_mask()
        for g in range(SG_GROUPS):
            wg = jnp.where(mask, w_ref[g], 0.0).astype(BF16)
            cols = slice(g * CHUNK, (g + 1) * CHUNK)
            for n in range(ts // CHUNK):
                rws = slice(n * CHUNK, (n + 1) * CHUNK)
                sv = jnp.dot(wg, vn[rws, cols], preferred_element_type=F32) + b_ref[g]
                y_ref[rws, cols] = (ug[rws, cols] * sv).astype(BF16)

    seg = lambda s: pl.BlockSpec((ts, SEG), lambda i: (i, b0 + s))
    return pl.pallas_call(
        body, name=name, grid=(S // ts,),
        in_specs=[seg(SEG_U), seg(SEG_V), pl.BlockSpec((1, SEG), lambda i: (0, 0)),
                  pl.BlockSpec((SG_GROUPS, CHUNK, CHUNK), lambda i: (0, 0, 0)),
                  pl.BlockSpec((SG_GROUPS, CHUNK, 1), lambda i: (0, 0, 0))],
        out_specs=pl.BlockSpec((ts, SEG), lambda i: (i, 0)),
        out_shape=jax.ShapeDtypeStruct((S, SEG), BF16), compiler_params=_params("parallel"),
    )(proj, proj, sg_norm, sg_w, sg_bt)


def sg_bwd(proj, sg_norm, sg_w, sg_bt, dy, D, *, name):
    S = proj.shape[0]
    ts = _pick(S, (512, 256, 128))
    nt = S // ts
    b0 = _seg_block(D, 0)

    def body(u_ref, v_ref, dy_ref, gs_ref, w_ref, b_ref, du_ref, dv_ref, dw_ref, db_ref, dgs_ref, dvn_sc):
        i = pl.program_id(0)
        uv, vv, dyv = u_ref[...], v_ref[...], dy_ref[...]
        ug, ut = _gelu(uv)
        vg, vt = _gelu(vv)
        r = lax.rsqrt(jnp.mean(vg * vg, axis=-1, keepdims=True) + RMS_EPS)
        vh = vg * r
        gs = gs_ref[...]
        vn = (vh * gs).astype(BF16)
        dsv = dyv * ug
        dsv_b = dsv.astype(BF16)
        mask = _tril_mask()

        @pl.when(i == 0)
        def _():
            dw_ref[...] = jnp.zeros_like(dw_ref)
            db_ref[...] = jnp.zeros_like(db_ref)

        for g in range(SG_GROUPS):
            wg = jnp.where(mask, w_ref[g], 0.0).astype(BF16)
            cols = slice(g * CHUNK, (g + 1) * CHUNK)
            dw_acc = jnp.zeros((CHUNK, CHUNK), F32)
            db_acc = jnp.zeros((CHUNK, 1), F32)
            for n in range(ts // CHUNK):
                rws = slice(n * CHUNK, (n + 1) * CHUNK)
                vblk = vn[rws, cols]
                sv = jnp.dot(wg, vblk, preferred_element_type=F32) + b_ref[g]
                du_ref[rws, cols] = (dyv[rws, cols] * sv * _gelu_grad(uv[rws, cols], ut[rws, cols])).astype(BF16)
                dblk = dsv_b[rws, cols]
                dvn_sc[rws, cols] = lax.dot_general(wg, dblk, (((0,), (0,)), ((), ())), preferred_element_type=F32)
                dw_acc = dw_acc + lax.dot_general(dblk, vblk, (((1,), (1,)), ((), ())), preferred_element_type=F32)
                db_acc = db_acc + jnp.sum(dsv[rws, cols], axis=1, keepdims=True)
            dw_ref[g] += jnp.where(mask, dw_acc, 0.0)
            db_ref[g] += db_acc

        dvn = dvn_sc[...]
        gd = dvn * gs
        dvg = r * (gd - vh * jnp.mean(gd * vh, axis=-1, keepdims=True))
        dv_ref[...] = (dvg * _gelu_grad(vv, vt)).astype(BF16)
        dgs_part = jnp.sum(dvn * vh, axis=0, keepdims=True)

        @pl.when(i == 0)
        def _():
            dgs_ref[...] = dgs_part

        @pl.when(i > 0)
        def _():
            dgs_ref[...] += dgs_part

    seg = lambda s: pl.BlockSpec((ts, SEG), lambda i: (i, b0 + s))
    row = pl.BlockSpec((ts, SEG), lambda i: (i, 0))
    wspec = pl.BlockSpec((SG_GROUPS, CHUNK, CHUNK), lambda i: (0, 0, 0))
    bspec = pl.BlockSpec((SG_GROUPS, CHUNK, 1), lambda i: (0, 0, 0))
    vec = pl.BlockSpec((1, SEG), lambda i: (0, 0))
    return pl.pallas_call(
        body, name=name, grid=(nt,),
        in_specs=[seg(SEG_U), seg(SEG_V), row, vec, wspec, bspec],
        out_specs=[row, row, wspec, bspec, vec],
        out_shape=[jax.ShapeDtypeStruct((S, SEG), BF16)] * 2
        + [jax.ShapeDtypeStruct((SG_GROUPS, CHUNK, CHUNK), F32), jax.ShapeDtypeStruct((SG_GROUPS, CHUNK, 1), F32),
           jax.ShapeDtypeStruct((1, SEG), F32)],
        scratch_shapes=[pltpu.VMEM((ts, SEG), F32)],
        compiler_params=_params("arbitrary"),
    )(proj, proj, dy, sg_norm, sg_w, sg_bt)


def _log_sigmoid(x):
    return jnp.minimum(x, 0.0) - jnp.log(1.0 + jnp.exp(-jnp.abs(x)))


def fox_cumlog(proj, b_f, D, *, name):
    S = proj.shape[0]
    ts = _pick(S, (512, 256, 128))
    blk = (3 * D + SEG_FL * SEG) // LANES

    def body(f_ref, b_ref, c_ref, carry):
        i = pl.program_id(0)

        @pl.when(i == 0)
        def _():
            carry[...] = jnp.zeros_like(carry)

        rows = lax.broadcasted_iota(jnp.int32, (ts, 1), 0)
        acc = _log_sigmoid(f_ref[...] + b_ref[...])
        d = 1
        while d < ts:
            acc = acc + jnp.where(rows >= d, pltpu.roll(acc, d, 0), 0.0)
            d *= 2
        acc = acc + carry[...]
        c_ref[...] = acc
        carry[...] = acc[ts - 1:ts, :]

    return pl.pallas_call(
        body, name=name, grid=(S // ts,),
        in_specs=[pl.BlockSpec((ts, LANES), lambda i: (i, blk)), pl.BlockSpec((1, LANES), lambda i: (0, 0))],
        out_specs=pl.BlockSpec((ts, LANES), lambda i: (i, 0)),
        out_shape=jax.ShapeDtypeStruct((S, LANES), F32),
        scratch_shapes=[pltpu.VMEM((1, LANES), F32)],
        compiler_params=_params("arbitrary"),
    )(proj, b_f)


def fox_dlogit(proj, b_f, dc, D, *, name):
    S = proj.shape[0]
    ts = _pick(S, (512, 256, 128))
    nt = S // ts
    blk = (3 * D + SEG_FL * SEG) // LANES

    def body(f_ref, b_ref, dc_ref, df_ref, db_ref, carry):
        i = pl.program_id(0)

        @pl.when(i == 0)
        def _():
            carry[...] = jnp.zeros_like(carry)

        rows = lax.broadcasted_iota(jnp.int32, (ts, 1), 0)
        acc = dc_ref[...]
        d = 1
        while d < ts:
            acc = acc + jnp.where(rows < ts - d, pltpu.roll(acc, ts - d, 0), 0.0)
            d *= 2
        acc = acc + carry[...]
        carry[...] = acc[0:1, :]
        df = acc * _sig(-(f_ref[...] + b_ref[...]))
        df_ref[...] = jnp.zeros_like(df_ref)
        df_ref[:, 0:LANES] = df.astype(BF16)
        part = jnp.sum(df, axis=0, keepdims=True)

        @pl.when(i == 0)
        def _():
            db_ref[...] = part

        @pl.when(i > 0)
        def _():
            db_ref[...] += part

    rev = lambda i: nt - 1 - i
    return pl.pallas_call(
        body, name=name, grid=(nt,),
        in_specs=[pl.BlockSpec((ts, LANES), lambda i: (rev(i), blk)), pl.BlockSpec((1, LANES), lambda i: (0, 0)),
                  pl.BlockSpec((ts, LANES), lambda i: (rev(i), 0))],
        out_specs=[pl.BlockSpec((ts, SEG), lambda i: (rev(i), 0)), pl.BlockSpec((1, LANES), lambda i: (0, 0))],
        out_shape=[jax.ShapeDtypeStruct((S, SEG), BF16), jax.ShapeDtypeStruct((1, LANES), F32)],
        scratch_shapes=[pltpu.VMEM((1, LANES), F32)],
        compiler_params=_params("arbitrary"),
    )(proj, b_f, dc)


def _causal_mask(t, keys_on_rows=True):
    r = lax.broadcasted_iota(jnp.int32, (t, t), 0)
    c = lax.broadcasted_iota(jnp.int32, (t, t), 1)
    return (r <= c) if keys_on_rows else (c <= r)


FOX_PAD = LANES
COL_C = FOX_HEAD_DIM
COL_ROWSUM = FOX_HEAD_DIM + 3
COL_L = FOX_HEAD_DIM


def _head_pair(ref, a):
    x = ref[...]
    return x if a == 0 else pltpu.roll(x, FOX_HEAD_DIM, 1)


def fox_pack(proj, c, D, scale, *, name):
    S = proj.shape[0]
    ts = _pick(S, (512, 256, 128))
    base = 3 * D // LANES
    per_seg = SEG // LANES

    def body(q_ref, k_ref, v_ref, c_ref, qo, ko, vo, qto, vto):
        hp = pl.program_id(1)
        lane = lax.broadcasted_iota(jnp.int32, (ts, LANES), 1)
        is_val = lane < FOX_HEAD_DIM
        cv = c_ref[...]
        for a in range(2):
            ch = jnp.sum(jnp.where(lane == 2 * hp + a, cv, 0.0), axis=1, keepdims=True)
            c_hi = ch.astype(BF16).astype(F32)
            r1 = ch - c_hi
            c_mid = r1.astype(BF16).astype(F32)
            c_lo = r1 - c_mid
            qa = jnp.where(is_val, _head_pair(q_ref, a) * scale, jnp.where(lane < COL_C + 3, 1.0, 0.0))
            extra = jnp.where(lane == COL_C, -c_hi, jnp.where(lane == COL_C + 1, -c_mid, jnp.where(
                lane == COL_C + 2, -c_lo, jnp.where(lane == COL_ROWSUM, 1.0, 0.0))))
            va = jnp.where(is_val, _head_pair(v_ref, a), jnp.where(lane == COL_L, 1.0, 0.0))
            qo[a] = qa.astype(BF16)
            ko[a] = jnp.where(is_val, _head_pair(k_ref, a), extra).astype(BF16)
            vo[a] = va.astype(BF16)
            qto[a] = qa.T.astype(BF16)
            vto[a] = va.T.astype(BF16)

    seg = lambda s: pl.BlockSpec((ts, LANES), lambda i, hp: (i, base + s * per_seg + hp))
    out = pl.BlockSpec((2, ts, FOX_PAD), lambda i, hp: (hp, i, 0))
    out_t = pl.BlockSpec((2, FOX_PAD, ts), lambda i, hp: (hp, 0, i))
    return pl.pallas_call(
        body, name=name, grid=(S // ts, FOX_HEADS // 2),
        in_specs=[seg(SEG_FQ), seg(SEG_FK), seg(SEG_FV), pl.BlockSpec((ts, LANES), lambda i, hp: (i, 0))],
        out_specs=[out] * 3 + [out_t] * 2,
        out_shape=[jax.ShapeDtypeStruct((FOX_HEADS, S, FOX_PAD), BF16)] * 3
        + [jax.ShapeDtypeStruct((FOX_HEADS, FOX_PAD, S), BF16)] * 2,
        compiler_params=_params("parallel", "parallel"),
    )(proj, proj, proj, c)


def heads_pack(x, *, name):
    S = x.shape[0]
    ts = _pick(S, (512, 256, 128))

    def body(x_ref, o_ref, ot_ref):
        lane = lax.broadcasted_iota(jnp.int32, (ts, LANES), 1)
        for a in range(2):
            xa = jnp.where(lane < FOX_HEAD_DIM, _head_pair(x_ref, a), 0.0)
            o_ref[a] = xa.astype(BF16)
            ot_ref[a] = xa.T.astype(BF16)

    return pl.pallas_call(
        body, name=name, grid=(S // ts, FOX_HEADS // 2),
        in_specs=[pl.BlockSpec((ts, LANES), lambda i, hp: (i, hp))],
        out_specs=[pl.BlockSpec((2, ts, FOX_PAD), lambda i, hp: (hp, i, 0)),
                   pl.BlockSpec((2, FOX_PAD, ts), lambda i, hp: (hp, 0, i))],
        out_shape=[jax.ShapeDtypeStruct((FOX_HEADS, S, FOX_PAD), BF16),
                   jax.ShapeDtypeStruct((FOX_HEADS, FOX_PAD, S), BF16)],
        compiler_params=_params("parallel", "parallel"),
    )(x)


def heads_unpack(x, scale, *, name):
    S = x.shape[1]
    ts = _pick(S, (512, 256, 128))

    def body(x_ref, o_ref):
        lane = lax.broadcasted_iota(jnp.int32, (ts, LANES), 1)
        both = jnp.where(lane < FOX_HEAD_DIM, x_ref[0], pltpu.roll(x_ref[1], FOX_HEAD_DIM, 1))
        o_ref[...] = (both * scale).astype(BF16)

    return pl.pallas_call(
        body, name=name, grid=(S // ts, FOX_HEADS // 2),
        in_specs=[pl.BlockSpec((2, ts, FOX_PAD), lambda i, hp: (hp, i, 0))],
        out_specs=pl.BlockSpec((ts, LANES), lambda i, hp: (i, hp)),
        out_shape=jax.ShapeDtypeStruct((S, FOX_HEADS * FOX_HEAD_DIM), BF16),
        compiler_params=_params("parallel", "parallel"),
    )(x)


FOX_FWD_GROUP = 2


def _fox_fwd_tile(S):
    return min(1024, max(128, S // 4))


def fox_fwd(qt, k, vt, *, name):
    H, W, S = qt.shape
    t = _fox_fwd_tile(S)
    n = S // t
    G = FOX_FWD_GROUP

    def body(qt_ref, k_ref, vt_ref, ot_ref, lse_ref, m_sc, acc_sc):
        i, j = pl.program_id(1), pl.program_id(2)

        @pl.when(j == 0)
        def _():
            m_sc[...] = jnp.full_like(m_sc, NEG_BIG)
            acc_sc[...] = jnp.zeros_like(acc_sc)

        def step(masked):
            for g in range(G):
                st = jnp.dot(k_ref[g], qt_ref[g], preferred_element_type=F32)
                if masked:
                    st = jnp.where(_causal_mask(t, keys_on_rows=True), st, NEG_BIG)
                m_prev = m_sc[g]
                m_new = jnp.maximum(m_prev, jnp.max(st, axis=0, keepdims=True))
                pt = jnp.exp(st - m_new)
                acc_sc[g] = jnp.exp(m_prev - m_new) * acc_sc[g] + jnp.dot(vt_ref[g], pt.astype(BF16),
                                                                          preferred_element_type=F32)
                m_sc[g] = m_new

        @pl.when(j < i)
        def _():
            step(False)

        @pl.when(j == i)
        def _():
            step(True)
            for g in range(G):
                acc = acc_sc[g]
                l = acc[COL_L:COL_L + 1, :]
                ot_ref[g] = acc / l
                lse_ref[g] = m_sc[g] + jnp.log(l)

    qs = pl.BlockSpec((G, W, t), lambda h, i, j: (h, 0, i))
    ks = pl.BlockSpec((G, t, W), lambda h, i, j: (h, jnp.minimum(j, i), 0))
    vs = pl.BlockSpec((G, W, t), lambda h, i, j: (h, 0, jnp.minimum(j, i)))
    row = pl.BlockSpec((G, 1, t), lambda h, i, j: (h, 0, i))
    return pl.pallas_call(
        body, name=name, grid=(H // G, n, n), in_specs=[qs, ks, vs], out_specs=[qs, row],
        out_shape=[jax.ShapeDtypeStruct((H, W, S), F32), jax.ShapeDtypeStruct((H, 1, S), F32)],
        scratch_shapes=[pltpu.VMEM((G, 1, t), F32), pltpu.VMEM((G, W, t), F32)],
        compiler_params=_params("parallel", "parallel", "arbitrary"),
    )(qt, k, vt)


def fox_dc(dq, dk, *, name):
    H, S, W = dq.shape
    ts = _pick(S, (512, 256, 128))

    def body(dq_ref, dk_ref, o_ref):
        lane = lax.broadcasted_iota(jnp.int32, (ts, LANES), 1)
        acc = jnp.zeros((ts, LANES), F32)
        for h in range(H):
            d = dq_ref[h][:, COL_ROWSUM:COL_ROWSUM + 1] - dk_ref[h][:, COL_C:COL_C + 1]
            acc = jnp.where(lane == h, d, acc)
        o_ref[...] = acc

    blk = pl.BlockSpec((H, ts, W), lambda i: (0, i, 0))
    return pl.pallas_call(
        body, name=name, grid=(S // ts,), in_specs=[blk, blk],
        out_specs=pl.BlockSpec((ts, LANES), lambda i: (i, 0)),
        out_shape=jax.ShapeDtypeStruct((S, LANES), F32), compiler_params=_params("parallel"),
    )(dq, dk)


def heads_unpack_t(xt, *, name):
    S = xt.shape[2]
    ts = _pick(S, (512, 256, 128))

    def body(x_ref, y_ref, o_ref):
        lane = lax.broadcasted_iota(jnp.int32, (ts, LANES), 1)
        x0, x1 = x_ref[0].T, x_ref[1].T
        o_ref[0] = x0
        o_ref[1] = x1
        y_ref[...] = jnp.where(lane < FOX_HEAD_DIM, x0, pltpu.roll(x1, FOX_HEAD_DIM, 1)).astype(BF16)

    return pl.pallas_call(
        body, name=name, grid=(S // ts, FOX_HEADS // 2),
        in_specs=[pl.BlockSpec((2, FOX_PAD, ts), lambda i, hp: (hp, 0, i))],
        out_specs=[pl.BlockSpec((ts, LANES), lambda i, hp: (i, hp)),
                   pl.BlockSpec((2, ts, FOX_PAD), lambda i, hp: (hp, i, 0))],
        out_shape=[jax.ShapeDtypeStruct((S, FOX_HEADS * FOX_HEAD_DIM), BF16),
                   jax.ShapeDtypeStruct((FOX_HEADS, S, FOX_PAD), F32)],
        compiler_params=_params("parallel", "parallel"),
    )(xt)


def fox_delta(do, o, *, name):
    H, S, Dh = o.shape
    t = _pick(S, (2048, 1024, 512, 256, 128))

    def body(do_ref, o_ref, d_ref):
        d_ref[0] = jnp.sum(do_ref[0] * o_ref[0], axis=-1, keepdims=True)

    blk = pl.BlockSpec((1, t, Dh), lambda h, i: (h, i, 0))
    return pl.pallas_call(
        body, name=name, grid=(H, S // t), in_specs=[blk, blk],
        out_specs=pl.BlockSpec((1, t, 1), lambda h, i: (h, i, 0)),
        out_shape=jax.ShapeDtypeStruct((H, S, 1), F32), compiler_params=_params("parallel", "parallel"),
    )(do, o)


def fox_bwd(qt, q, k, v, dot, do, lse, delta, *, name):
    H, W, S = qt.shape
    t = _fox_fwd_tile(S)
    n = S // t

    def body(qt_ref, q_ref, k_ref, v_ref, dot_ref, do_ref, lse_ref, dl_ref, dq_hbm, dk_ref, dv_ref,
             dq_sc, dk_sc, dv_sc, sem):
        h, j, i = pl.program_id(0), pl.program_id(1), pl.program_id(2)

        @pl.when((j == 0) & (i == 0))
        def _():
            dq_sc[...] = jnp.zeros_like(dq_sc)

        @pl.when(i == 0)
        def _():
            dk_sc[...] = jnp.zeros_like(dk_sc)
            dv_sc[...] = jnp.zeros_like(dv_sc)

        def step(masked):
            st = jnp.dot(k_ref[0], qt_ref[0], preferred_element_type=F32)
            pt = jnp.exp(st - lse_ref[0])
            if masked:
                pt = jnp.where(_causal_mask(t, keys_on_rows=True), pt, 0.0)
            dpt = jnp.dot(v_ref[0], dot_ref[0], preferred_element_type=F32)
            dst = (pt * (dpt - dl_ref[0])).astype(BF16)
            dv_sc[...] += jnp.dot(pt.astype(BF16), do_ref[0], preferred_element_type=F32)
            dk_sc[...] += jnp.dot(dst, q_ref[0], preferred_element_type=F32)
            rows = pl.ds(pl.multiple_of(i * t, t), t)
            dq_sc[rows, :] += lax.dot_general(dst, k_ref[0], (((0,), (0,)), ((), ())), preferred_element_type=F32)

        @pl.when(i > j)
        def _():
            step(False)

        @pl.when(i == j)
        def _():
            step(True)

        @pl.when(i == n - 1)
        def _():
            dk_ref[0] = dk_sc[...]
            dv_ref[0] = dv_sc[...]

        @pl.when((j == n - 1) & (i == n - 1))
        def _():
            out = pltpu.make_async_copy(dq_sc, dq_hbm.at[h], sem)
            out.start()
            out.wait()

    q_t = pl.BlockSpec((1, W, t), lambda h, j, i: (h, 0, jnp.maximum(i, j)))
    q_r = pl.BlockSpec((1, t, W), lambda h, j, i: (h, jnp.maximum(i, j), 0))
    k_r = pl.BlockSpec((1, t, W), lambda h, j, i: (h, j, 0))
    row = pl.BlockSpec((1, 1, t), lambda h, j, i: (h, 0, jnp.maximum(i, j)))
    return pl.pallas_call(
        body, name=name, grid=(H, n, n), in_specs=[q_t, q_r, k_r, k_r, q_t, q_r, row, row],
        out_specs=[pl.BlockSpec(memory_space=pl.ANY), k_r, k_r],
        out_shape=[jax.ShapeDtypeStruct((H, S, W), F32)] * 3,
        scratch_shapes=[pltpu.VMEM((S, W), F32), pltpu.VMEM((t, W), F32), pltpu.VMEM((t, W), F32),
                        pltpu.SemaphoreType.DMA],
        compiler_params=_params("arbitrary", "arbitrary", "arbitrary"),
    )(qt, q, k, v, dot, do, lse, delta)


def merge_fwd(proj, branches, D, *, name):
    S = proj.shape[0]
    ts = _pick(S, (256, 128))

    def body(g0, g1, g2, b0, b1, b2, o_ref):
        acc = (_sig(g0[...]) * b0[...].astype(F32) + _sig(g1[...]) * b1[...].astype(F32)
               + _sig(g2[...]) * b2[...].astype(F32))
        o_ref[...] = acc.astype(BF16)

    gate = lambda n: pl.BlockSpec((ts, D), lambda i: (i, n))
    row = pl.BlockSpec((ts, D), lambda i: (i, 0))
    return pl.pallas_call(
        body, name=name, grid=(S // ts,), in_specs=[gate(0), gate(1), gate(2), row, row, row], out_specs=row,
        out_shape=jax.ShapeDtypeStruct((S, D), BF16), compiler_params=_params("parallel"),
    )(proj, proj, proj, *branches)


def merge_bwd(proj, branches, dm, D, *, name):
    S = proj.shape[0]
    ts = _pick(S, (256, 128))

    def body(g0, g1, g2, b0, b1, b2, dm_ref, db0, db1, db2, dg0, dg1, dg2):
        dmv = dm_ref[...]
        for g_ref, b_ref, db_ref, dg_ref in ((g0, b0, db0, dg0), (g1, b1, db1, dg1), (g2, b2, db2, dg2)):
            s = _sig(g_ref[...])
            db_ref[...] = (dmv * s).astype(BF16)
            dg_ref[...] = (dmv * b_ref[...].astype(F32) * (s * (1.0 - s))).astype(BF16)

    gate = lambda n: pl.BlockSpec((ts, D), lambda i: (i, n))
    row = pl.BlockSpec((ts, D), lambda i: (i, 0))
    return pl.pallas_call(
        body, name=name, grid=(S // ts,), in_specs=[gate(0), gate(1), gate(2), row, row, row, row],
        out_specs=[row] * 6, out_shape=[jax.ShapeDtypeStruct((S, D), BF16)] * 6,
        compiler_params=_params("parallel"),
    )(proj, proj, proj, *branches, dm)


def xa_fwd(q, k, v, *, name):
    S, D = q.shape
    M = k.shape[0]
    dh = D // XA_HEADS
    scale = dh ** -0.5
    t = _pick(S, (512, 256, 128))

    def body(q_ref, k_ref, v_ref, o_ref):
        for h in range(XA_HEADS):
            cols = slice(h * dh, (h + 1) * dh)
            s = lax.dot_general(q_ref[:, cols], k_ref[:, cols], (((1,), (1,)), ((), ())),
                                preferred_element_type=F32) * scale
            p = jnp.exp(s - jnp.max(s, axis=-1, keepdims=True))
            p = p / jnp.sum(p, axis=-1, keepdims=True)
            o_ref[:, cols] = jnp.dot(p.astype(BF16), v_ref[:, cols], preferred_element_type=F32).astype(BF16)

    row = pl.BlockSpec((t, D), lambda i: (i, 0))
    full = pl.BlockSpec((M, D), lambda i: (0, 0))
    return pl.pallas_call(
        body, name=name, grid=(S // t,), in_specs=[row, full, full], out_specs=row,
        out_shape=jax.ShapeDtypeStruct((S, D), BF16), compiler_params=_params("parallel"),
    )(q, k, v)


def xa_bwd(q, k, v, do, *, name):
    S, D = q.shape
    M = k.shape[0]
    dh = D // XA_HEADS
    scale = dh ** -0.5
    t = _pick(S, (512, 256, 128))

    def body(q_ref, k_ref, v_ref, do_ref, dq_ref, dk_ref, dv_ref):
        i = pl.program_id(0)

        @pl.when(i == 0)
        def _():
            dk_ref[...] = jnp.zeros_like(dk_ref)
            dv_ref[...] = jnp.zeros_like(dv_ref)

        for h in range(XA_HEADS):
            cols = slice(h * dh, (h + 1) * dh)
            qh, kh, vh = q_ref[:, cols], k_ref[:, cols], v_ref[:, cols]
            dob = do_ref[:, cols].astype(BF16)
            s = lax.dot_general(qh, kh, (((1,), (1,)), ((), ())), preferred_element_type=F32) * scale
            p = jnp.exp(s - jnp.max(s, axis=-1, keepdims=True))
            p = p / jnp.sum(p, axis=-1, keepdims=True)
            dp = lax.dot_general(dob, vh, (((1,), (1,)), ((), ())), preferred_element_type=F32)
            ds = (p * (dp - jnp.sum(p * dp, axis=-1, keepdims=True)) * scale).astype(BF16)
            dq_ref[:, cols] = jnp.dot(ds, kh, preferred_element_type=F32).astype(BF16)
            dk_ref[:, cols] += lax.dot_general(ds, qh, (((0,), (0,)), ((), ())), preferred_element_type=F32)
            dv_ref[:, cols] += lax.dot_general(p.astype(BF16), dob, (((0,), (0,)), ((), ())),
                                               preferred_element_type=F32)

    row = pl.BlockSpec((t, D), lambda i: (i, 0))
    full = pl.BlockSpec((M, D), lambda i: (0, 0))
    return pl.pallas_call(
        body, name=name, grid=(S // t,), in_specs=[row, full, full, row], out_specs=[row, full, full],
        out_shape=[jax.ShapeDtypeStruct((S, D), BF16), jax.ShapeDtypeStruct((M, D), F32),
                   jax.ShapeDtypeStruct((M, D), F32)],
        compiler_params=_params("arbitrary"),
    )(q, k, v, do)


def mixer_fwd(x, w, tag):
    S, D = x.shape
    h = rms_fwd(x, w["mix_norm"], name=f"{tag}_rms")
    proj = mm(h, w["w_in"], name=f"{tag}_proj")
    y_a = conv_fwd(proj, w["conv_w"], D, name=f"{tag}_conv")
    y_b = sg_fwd(proj, w["sg_norm"], w["sg_w"], w["sg_bt"], D, name=f"{tag}_sg")
    c = fox_cumlog(proj, w["fox_b_f"], D, name=f"{tag}_cumlog")
    qh, kh, vh, qt, vt = fox_pack(proj, c, D, FOX_HEAD_DIM ** -0.5, name=f"{tag}_foxpack")
    ot, lse = fox_fwd(qt, kh, vt, name=f"{tag}_fox")
    y_c, o = heads_unpack_t(ot, name=f"{tag}_foxout")
    ys = (y_a, y_b, y_c)
    branches = [mm(ys[n], w["w_branch"][n], name=f"{tag}_branch{n}", out_dtype=BF16) for n in range(N_BRANCH)]
    merged = merge_fwd(proj, branches, D, name=f"{tag}_merge")
    y = mm(merged, w["w_out"], name=f"{tag}_out", res=x)
    return y, (x, h, proj, ys, qh, kh, vh, qt, o, lse, branches, merged)


def mixer_bwd(dx, saved, w, tag):
    x, h, proj, ys, qh, kh, vh, qt, o, lse, branches, merged = saved
    S, D = x.shape
    grads = {"w_out": mm_tn(merged, dx, name=f"{tag}_dwout")}
    dmerged = mm(dx, w["w_out_t"], name=f"{tag}_dmerged")
    outs = merge_bwd(proj, branches, dmerged, D, name=f"{tag}_dmerge")
    dbr, dgl = outs[:3], outs[3:]
    grads["w_branch"] = jnp.stack([mm_tn(ys[n], dbr[n], name=f"{tag}_dwbr{n}") for n in range(N_BRANCH)])
    dys = [mm(dbr[n], w["w_branch_t"][n], name=f"{tag}_dy{n}") for n in range(N_BRANCH)]
    d_ab, d_ac, d_ah, grads["conv_w"] = conv_bwd(proj, w["conv_w"], dys[0], D, name=f"{tag}_dconv")
    d_u, d_v, grads["sg_w"], d_sgb, grads["sg_norm"] = sg_bwd(
        proj, w["sg_norm"], w["sg_w"], w["sg_bt"], dys[1], D, name=f"{tag}_dsg")
    grads["sg_b"] = d_sgb[:, :, 0]
    do, dot = heads_pack(dys[2], name=f"{tag}_dopack")
    delta = fox_delta(do, o, name=f"{tag}_delta")
    dq, dk, dv = fox_bwd(qt, qh, kh, vh, dot, do, lse, delta.reshape(FOX_HEADS, 1, S), name=f"{tag}_foxbwd")
    dc_rows = fox_dc(dq, dk, name=f"{tag}_dc")
    d_fl, d_bf = fox_dlogit(proj, w["fox_b_f"], dc_rows, D, name=f"{tag}_dflogit")
    grads["fox_b_f"] = d_bf[0, :FOX_HEADS]
    dproj = jnp.concatenate(
        list(dgl) + [d_ab, d_ac, d_ah, d_u, d_v, heads_unpack(dq, FOX_HEAD_DIM ** -0.5, name=f"{tag}_dqout"),
                     heads_unpack(dk, 1.0, name=f"{tag}_dkout"), heads_unpack(dv, 1.0, name=f"{tag}_dvout"), d_fl],
        axis=1)
    grads["w_in"] = mm_tn(h, dproj, name=f"{tag}_dwin")
    dh = mm(dproj, w["w_in_t"], name=f"{tag}_dh")
    dx, grads["mix_norm"] = rms_bwd(x, w["mix_norm"], dh, dx, name=f"{tag}_drms")
    return dx, grads


def xattn_fwd(x, mem, w, tag):
    h = rms_fwd(x, w["xa_norm"], name=f"{tag}_rms")
    m = rms_fwd(mem, w["mem_norm"], name=f"{tag}_mrms")
    q = mm(h, w["xa_wq"], name=f"{tag}_q", out_dtype=BF16)
    k = mm(m, w["xa_wk"], name=f"{tag}_k", out_dtype=BF16)
    v = mm(m, w["xa_wv"], name=f"{tag}_v", out_dtype=BF16)
    o = xa_fwd(q, k, v, name=f"{tag}_attn")
    y = mm(o, w["xa_wo"], name=f"{tag}_o", res=x)
    return y, (x, h, m, q, k, v, o)


def xattn_bwd(dx, mem, saved, w, tag):
    x, h, m, q, k, v, o = saved
    grads = {"xa_wo": mm_tn(o, dx, name=f"{tag}_dwo")}
    do = mm(dx, w["xa_wo_t"], name=f"{tag}_do")
    dq, dk, dv = xa_bwd(q, k, v, do, name=f"{tag}_dattn")
    grads["xa_wq"] = mm_tn(h, dq, name=f"{tag}_dwq")
    grads["xa_wk"] = mm_tn(m, dk, name=f"{tag}_dwk")
    grads["xa_wv"] = mm_tn(m, dv, name=f"{tag}_dwv")
    dh = mm(dq, w["xa_wq_t"], name=f"{tag}_dh")
    dm = mm(dk, w["xa_wk_t"], name=f"{tag}_dm1")
    dm = mm(dv, w["xa_wv_t"], name=f"{tag}_dm2", res=dm)
    _, grads["mem_norm"] = rms_bwd(mem, w["mem_norm"], dm, None, name=f"{tag}_dmrms")
    dx, grads["xa_norm"] = rms_bwd(x, w["xa_norm"], dh, dx, name=f"{tag}_drms")
    return dx, grads


def _mesh_pos():
    return lax.axis_index("x"), lax.axis_index("y"), lax.axis_index("c")


def _flip(v, bit):
    return 1 - v if bit else v


def all_gather(xs, *, name):
    n = len(xs)

    def body(*refs):
        x_refs, out_refs = refs[:n], refs[n:2 * n]
        send_sems, recv_sems, local_sems = refs[2 * n:]
        mx, my, mc = _mesh_pos()
        me, sibling = (mx, my, mc), (mx, my, 1 - mc)
        chips = [(1 - mx, my), (mx, 1 - my), (1 - mx, 1 - my)]

        def copy(a, k, block, to, from_input=False):
            slot = out_refs[a].at[4 * block[0] + 2 * block[1] + block[2]]
            return pltpu.make_async_remote_copy(
                src_ref=x_refs[a] if from_input else slot, dst_ref=slot,
                send_sem=send_sems.at[a, k], recv_sem=recv_sems.at[a, k], device_id=to, device_id_type=MESH_IDS)

        started = []
        for a in range(n):
            mine = pltpu.make_async_copy(x_refs[a], out_refs[a].at[4 * mx + 2 * my + mc], local_sems.at[a])
            mine.start()
            started.append(mine)
        sends = []
        for a in range(n):
            sends.append(copy(a, 0, me, sibling, from_input=True))
            sends += [copy(a, 1 + j, me, (*chip, mc), from_input=True) for j, chip in enumerate(chips)]
        for cp in sends:
            cp.start()
        for j, chip in enumerate(chips):
            for a in range(n):
                copy(a, 1 + j, (*chip, mc), me).wait_recv()
                onward = copy(a, 4 + j, (*chip, mc), sibling)
                onward.start()
                sends.append(onward)
        for a in range(n):
            copy(a, 0, sibling, me).wait_recv()
            for j, chip in enumerate(chips):
                copy(a, 4 + j, (*chip, 1 - mc), me).wait_recv()
        for cp in sends:
            cp.wait_send()
        for mine in started:
            mine.wait()

    any_spec = pl.BlockSpec(memory_space=pl.ANY)
    return pl.pallas_call(
        body, name=name, out_shape=[jax.ShapeDtypeStruct((N_DEV,) + x.shape, x.dtype) for x in xs],
        in_specs=[any_spec] * n, out_specs=[any_spec] * n,
        scratch_shapes=[pltpu.SemaphoreType.DMA((n, 7)), pltpu.SemaphoreType.DMA((n, 7)),
                        pltpu.SemaphoreType.DMA((n,))],
    )(*xs)


def all_to_all(gs, *, name):
    n = len(gs)

    def body(*refs):
        g_refs, out_refs = refs[:n], refs[n:2 * n]
        send_sems, recv_sems, local_sems = refs[2 * n:]
        mx, my, mc = _mesh_pos()
        me = 4 * mx + 2 * my + mc
        started, copies = [], []
        for a in range(n):
            mine = pltpu.make_async_copy(g_refs[a].at[me], out_refs[a].at[me], local_sems.at[a])
            mine.start()
            started.append(mine)
        for k in range(1, N_DEV):
            peer = (_flip(mx, k & 4), _flip(my, k & 2), _flip(mc, k & 1))
            peer_slot = 4 * peer[0] + 2 * peer[1] + peer[2]
            for a in range(n):
                sems = dict(send_sem=send_sems.at[a, k - 1], recv_sem=recv_sems.at[a, k - 1], device_id=peer,
                            device_id_type=MESH_IDS)
                send = pltpu.make_async_remote_copy(src_ref=g_refs[a].at[peer_slot], dst_ref=out_refs[a].at[me], **sems)
                arrive = pltpu.make_async_remote_copy(src_ref=g_refs[a].at[peer_slot],
                                                      dst_ref=out_refs[a].at[peer_slot], **sems)
                send.start()
                copies.append((send, arrive))
        for send, arrive in copies:
            arrive.wait_recv()
        for send, arrive in copies:
            send.wait_send()
        for mine in started:
            mine.wait()

    any_spec = pl.BlockSpec(memory_space=pl.ANY)
    return pl.pallas_call(
        body, name=name, out_shape=[jax.ShapeDtypeStruct(g.shape, g.dtype) for g in gs],
        in_specs=[any_spec] * n, out_specs=[any_spec] * n,
        scratch_shapes=[pltpu.SemaphoreType.DMA((n, 7)), pltpu.SemaphoreType.DMA((n, 7)),
                        pltpu.SemaphoreType.DMA((n,))],
    )(*gs)


ADAM_BLOCK_ELEMS = 256 * 1024


def reduce_adamw(parts, row0, w, m, v, *, name):
    R, C = w.shape
    n_parts = parts.shape[0]
    tr = R
    for cand in (512, 256, 128, 64, 32, 16):
        if R % cand == 0 and row0 % cand == 0 and cand * C <= ADAM_BLOCK_ELEMS:
            tr = cand
            break
    assert row0 % tr == 0 and (tr % 16 == 0 or (row0 == 0 and parts.shape[1] == R))
    bc1 = 1.0 - ADAM_B1 ** ADAM_STEP
    bc2 = 1.0 - ADAM_B2 ** ADAM_STEP

    def body(p_ref, w_ref, m_ref, v_ref, g_ref, d_ref, nm_ref, nv_ref):
        g = p_ref[0].astype(F32)
        for d in range(1, n_parts):
            g = g + p_ref[d].astype(F32)
        nm = ADAM_B1 * m_ref[...] + (1.0 - ADAM_B1) * g
        nv = ADAM_B2 * v_ref[...] + (1.0 - ADAM_B2) * (g * g)
        m_hat = nm / bc1
        v_hat = nv / bc2
        g_ref[...] = g
        d_ref[...] = -ADAM_LR * (m_hat / (jnp.sqrt(v_hat) + ADAM_EPS) + ADAM_WD * w_ref[...])
        nm_ref[...] = nm
        nv_ref[...] = nv

    row = pl.BlockSpec((tr, C), lambda i: (i, 0))
    blk0 = row0 // tr
    return pl.pallas_call(
        body, name=name, grid=(R // tr,),
        in_specs=[pl.BlockSpec((n_parts, tr, C), lambda i: (0, blk0 + i, 0)), row, row, row],
        out_specs=[row] * 4, out_shape=[jax.ShapeDtypeStruct((R, C), F32)] * 4,
        compiler_params=_params("parallel"),
    )(parts, w, m, v)


SHARDED = {
    "ffn1_w_gate": 2, "ffn1_w_up": 2, "ffn1_w_down": 1, "w_in": 2, "conv_w": 2, "w_branch": 3, "w_out": 1,
    "xa_wq": 1, "xa_wk": 1, "xa_wv": 1, "xa_wo": 1, "ffn2_w_gate": 2, "ffn2_w_up": 2, "ffn2_w_down": 1,
}
GROUPS = (("ffn1_w_gate", "ffn1_w_up", "ffn2_w_gate", "ffn2_w_up"),
          ("ffn1_w_down", "ffn2_w_down", "w_out", "xa_wq", "xa_wk", "xa_wv", "xa_wo"),
          ("w_in",), ("w_branch",))
REPLICATED = ("ffn1_norm", "mix_norm", "sg_norm", "sg_w", "sg_b", "fox_b_f", "xa_norm", "mem_norm", "ffn2_norm",
              "final_norm")
WEIGHTS = ("ffn1_norm", "ffn1_w_gate", "ffn1_w_up", "ffn1_w_down", "mix_norm", "w_in", "conv_w", "sg_norm", "sg_w",
           "sg_b", "fox_b_f", "w_branch", "w_out", "xa_norm", "mem_norm", "xa_wq", "xa_wk", "xa_wv", "xa_wo",
           "ffn2_norm", "ffn2_w_gate", "ffn2_w_up", "ffn2_w_down", "final_norm")
PACK_ROWS = 1024


def _rows(a):
    return a.reshape(-1, a.shape[-1])


def _pack(arrays, dtype):
    flat = jnp.concatenate([a.reshape(-1).astype(dtype) for a in arrays])
    n = flat.shape[0]
    unit = PACK_ROWS * LANES
    total = -(-n // unit) * unit
    return jnp.pad(flat, (0, total - n)).reshape(total // LANES, LANES)


def _unpack(buf, shapes):
    flat = buf.reshape(-1)
    out, off = [], 0
    for shp in shapes:
        n = 1
        for s in shp:
            n *= s
        out.append(flat[off:off + n].reshape(tuple(shp)))
        off += n
    return out


def _to_dev_major(full, axis):
    shp = full.shape
    a = full.reshape(shp[:axis] + (N_DEV, shp[axis] // N_DEV) + shp[axis + 1:])
    return jnp.moveaxis(a, axis, 0)


def _from_dev_major(a, axis):
    a = jnp.moveaxis(a, 0, axis)
    shp = a.shape
    return a.reshape(shp[:axis] + (shp[axis] * shp[axis + 1],) + shp[axis + 2:])


def _relayout_w_in(w_in, D):
    main = 8 * SEG
    pad = jnp.zeros((w_in.shape[0], SEG - FOX_HEADS), w_in.dtype)
    return jnp.concatenate([w_in[:, main + FOX_HEADS:], w_in[:, :main], w_in[:, main:main + FOX_HEADS], pad], axis=1)


def _unlayout_w_in(g, D):
    return jnp.concatenate([g[:, 3 * D:3 * D + 8 * SEG], g[:, 3 * D + 8 * SEG:3 * D + 8 * SEG + FOX_HEADS],
                            g[:, :3 * D]], axis=1)


def _layer_weights(full, rep, l, D):
    t = lambda a: a.T
    w_in = _relayout_w_in(full["w_in"][l], D)
    ffn = {}
    for tag in ("ffn1", "ffn2"):
        ffn[tag] = {"norm": rep[f"{tag}_norm"][l][None, :]}
        for n in ("w_gate", "w_up", "w_down"):
            ffn[tag][n] = full[f"{tag}_{n}"][l]
            ffn[tag][n + "_t"] = t(full[f"{tag}_{n}"][l])
    mix = {
        "mix_norm": rep["mix_norm"][l][None, :], "w_in": w_in, "w_in_t": t(w_in),
        "conv_w": full["conv_w"][l], "sg_norm": rep["sg_norm"][l][None, :], "sg_w": rep["sg_w"][l],
        "sg_bt": rep["sg_b"][l][:, :, None],
        "fox_b_f": jnp.pad(rep["fox_b_f"][l], (0, LANES - FOX_HEADS))[None, :],
        "w_branch": full["w_branch"][l], "w_branch_t": jnp.swapaxes(full["w_branch"][l], 1, 2),
        "w_out": full["w_out"][l], "w_out_t": t(full["w_out"][l]),
    }
    xa = {"xa_norm": rep["xa_norm"][l][None, :], "mem_norm": rep["mem_norm"][l][None, :]}
    for n in ("xa_wq", "xa_wk", "xa_wv", "xa_wo"):
        xa[n] = full[n][l]
        xa[n + "_t"] = t(full[n][l])
    return ffn, mix, xa


def kernel(x, mem, ffn1_norm, ffn1_w_gate, ffn1_w_up, ffn1_w_down, mix_norm, w_in, conv_w, sg_norm, sg_w, sg_b, fox_b_f, w_branch, w_out, xa_norm, mem_norm, xa_wq, xa_wk, xa_wv, xa_wo, ffn2_norm, ffn2_w_gate, ffn2_w_up, ffn2_w_down, final_norm, loss_target, m_ffn1_norm, m_ffn1_w_gate, m_ffn1_w_up, m_ffn1_w_down, m_mix_norm, m_w_in, m_conv_w, m_sg_norm, m_sg_w, m_sg_b, m_fox_b_f, m_w_branch, m_w_out, m_xa_norm, m_mem_norm, m_xa_wq, m_xa_wk, m_xa_wv, m_xa_wo, m_ffn2_norm, m_ffn2_w_gate, m_ffn2_w_up, m_ffn2_w_down, m_final_norm, v_ffn1_norm, v_ffn1_w_gate, v_ffn1_w_up, v_ffn1_w_down, v_mix_norm, v_w_in, v_conv_w, v_sg_norm, v_sg_w, v_sg_b, v_fox_b_f, v_w_branch, v_w_out, v_xa_norm, v_mem_norm, v_xa_wq, v_xa_wk, v_xa_wv, v_xa_wo, v_ffn2_norm, v_ffn2_w_gate, v_ffn2_w_up, v_ffn2_w_down, v_final_norm):
    args = locals()
    wts = {n: args[n] for n in WEIGHTS}
    mom = {n: args["m_" + n] for n in WEIGHTS}
    var = {n: args["v_" + n] for n in WEIGHTS}
    depth = ffn1_norm.shape[0]
    S, D = x.shape[1], x.shape[2]
    xs, ms, tgt = x[0], mem[0], loss_target[0]

    outs = all_gather([jnp.concatenate([_rows(wts[n]).astype(BF16) for n in grp]) for grp in GROUPS]
                      + [_rows(conv_w)], name="gather_weights")
    full, row0 = {}, {}
    for grp, got in zip(GROUPS + (("conv_w",),), outs):
        off = 0
        for n in grp:
            rows = _rows(wts[n]).shape[0]
            row0[n] = off
            full[n] = _from_dev_major(got[:, off:off + rows].reshape((N_DEV,) + wts[n].shape), SHARDED[n])
            off += rows
    rep = {n: wts[n] for n in REPLICATED}
    layers = [_layer_weights(full, rep, l, D) for l in range(depth)]

    saved = []
    h = xs
    for l, (ffn, mix, xa) in enumerate(layers):
        h, s1 = ffn_fwd(h, ffn["ffn1"], f"l{l}_ffn1")
        h, s2 = mixer_fwd(h, mix, f"l{l}_mix")
        h, s3 = xattn_fwd(h, ms, xa, f"l{l}_xa")
        h, s4 = ffn_fwd(h, ffn["ffn2"], f"l{l}_ffn2")
        saved.append((s1, s2, s3, s4))
    dx, d_final, loss_cols = final_loss_bwd(h, final_norm[None, :], tgt, name="final_loss")
    loss = lax.psum(0.5 * jnp.sum(loss_cols) / D, ("x", "y", "c"))

    per_layer = []
    for l in reversed(range(depth)):
        ffn, mix, xa = layers[l]
        s1, s2, s3, s4 = saved[l]
        g = {}
        dx, g4 = ffn_bwd(dx, s4, ffn["ffn2"], f"l{l}_ffn2")
        dx, g3 = xattn_bwd(dx, ms, s3, xa, f"l{l}_xa")
        dx, g2 = mixer_bwd(dx, s2, mix, f"l{l}_mix")
        dx, g1 = ffn_bwd(dx, s1, ffn["ffn1"], f"l{l}_ffn1")
        for tag, gg in (("ffn1", g1), ("ffn2", g4)):
            for n in ("w_gate", "w_up", "w_down"):
                g[f"{tag}_{n}"] = gg[n]
            g[f"{tag}_norm"] = gg["norm"][0]
        g.update(g3)
        g["xa_norm"], g["mem_norm"] = g3["xa_norm"][0], g3["mem_norm"][0]
        g.update({k: v for k, v in g2.items() if k != "w_in"})
        g["w_in"] = _unlayout_w_in(g2["w_in"], D)
        g["mix_norm"], g["sg_norm"] = g2["mix_norm"][0], g2["sg_norm"][0]
        per_layer.append(g)
    per_layer.reverse()
    grads = {n: jnp.stack([per_layer[l][n] for l in range(depth)]) for n in WEIGHTS if n != "final_norm"}
    grads["final_norm"] = d_final[0]

    def dev_major_rows(n):
        a = _to_dev_major(grads[n], SHARDED[n])
        return a.reshape(N_DEV, -1, a.shape[-1]).astype(BF16)

    parts = all_to_all([jnp.concatenate([dev_major_rows(n) for n in grp], axis=1) for grp in GROUPS],
                       name="scatter_grads")
    res = {k: {} for k in ("g", "d", "m", "v")}

    def adamw(n, parts_n, off):
        outs = reduce_adamw(parts_n, off, _rows(wts[n]), _rows(mom[n]), _rows(var[n]), name=f"adamw_{n}")
        for k, o in zip(("g", "d", "m", "v"), outs):
            res[k][n] = o.reshape(wts[n].shape)

    for grp, got in zip(GROUPS, parts):
        for n in grp:
            adamw(n, got, row0[n])

    small = list(REPLICATED)
    shapes = [wts[n].shape for n in small]
    conv_zero = jnp.zeros(grads["conv_w"].shape, F32)
    parts = all_gather([_pack([grads[n] for n in small] + [grads["conv_w"]], F32)], name="gather_small_grads")[0]
    outs = reduce_adamw(parts, 0, _pack([wts[n] for n in small] + [conv_zero], F32),
                        _pack([mom[n] for n in small] + [conv_zero], F32),
                        _pack([var[n] for n in small] + [conv_zero], F32), name="adamw_replicated")
    for k, o in zip(("g", "d", "m", "v"), outs):
        res[k].update(dict(zip(small, _unpack(o, shapes + [conv_zero.shape])[:-1])))
    conv_g = _unpack(outs[0], shapes + [conv_zero.shape])[-1]
    me = 4 * lax.axis_index("x") + 2 * lax.axis_index("y") + lax.axis_index("c")
    width = conv_w.shape[-1]
    conv_g = lax.dynamic_slice_in_dim(conv_g, me * width, width, axis=2)
    adamw("conv_w", _rows(conv_g)[None], 0)

    return (loss, dx[None], *[res["g"][n] for n in WEIGHTS], *[res["d"][n] for n in WEIGHTS],
            *[res["m"][n] for n in WEIGHTS], *[res["v"][n] for n in WEIGHTS])
```

```python
import functools

import jax
import jax.numpy as jnp
from jax import lax
from jax.experimental import pallas as pl
from jax.experimental.pallas import tpu as pltpu

F32 = jnp.float32
BF16 = jnp.bfloat16

N_DEV = 8
RMS_EPS = 1e-6
SEG = 512
FOX_HEADS = 8
FOX_HEAD_DIM = 64
SG_GROUPS = 4
CHUNK = 128
XA_HEADS = 4
N_BRANCH = 3
LANES = 128
VMEM_LIMIT_BYTES = 48 * 1024 * 1024
NEG_BIG = -1e30

ADAM_LR = 0.001
ADAM_B1 = 0.9
ADAM_B2 = 0.999
ADAM_EPS = 1e-08
ADAM_WD = 0.01
ADAM_STEP = 10

_GELU_K = 0.7978845608028654
_GELU_C = 0.044715

MESH_IDS = pl.DeviceIdType.MESH


def _pick(n, candidates):
    for c in candidates:
        if c <= n and n % c == 0:
            return c
    return n


def _params(*sem):
    return pltpu.CompilerParams(dimension_semantics=sem, vmem_limit_bytes=VMEM_LIMIT_BYTES)


def _sig(x):
    return 1.0 / (1.0 + jnp.exp(-x))


def _gelu(x):
    t = jnp.tanh(_GELU_K * (x + _GELU_C * x * x * x))
    return 0.5 * x * (1.0 + t), t


def _gelu_grad(x, t):
    return 0.5 * (1.0 + t) + 0.5 * x * (1.0 - t * t) * _GELU_K * (1.0 + 3.0 * _GELU_C * x * x)


_WIDE_TILES = (1536, 1408, 1280, 1024, 768, 512, 384, 256, 128)
MM_VMEM_BUDGET = 36 * 1024 * 1024
MM_ACC_BYTES = 6 * 1024 * 1024 + 512 * 1024


def mm(a, b, *, name, out_dtype=F32, res=None, scale=1.0, extras=(), epilogue=None, out_dtypes=None, tm=1024):
    M, K = a.shape
    K2, N = b.shape
    assert K == K2
    custom = epilogue is not None
    if not custom:
        extras = () if res is None else (res,)
        out_dtypes = (out_dtype,)

        def epilogue(acc, *ex):
            if scale != 1.0:
                acc = acc * scale
            return ((ex[0] + acc) if ex else acc,)

    n_ex, n_out = len(extras), len(out_dtypes)
    tn = _pick(N, _WIDE_TILES)
    tk = K if K <= 3072 else _pick(K, (2560, 2048, 1536, 1024, 512, 256, 128))
    nk = K // tk
    tile_bytes = sum(e.dtype.itemsize for e in extras) + sum(jnp.dtype(d).itemsize for d in out_dtypes)
    for tm in (tm, 512, 256, 128):
        blocks = 2 * (tm * tk * a.dtype.itemsize + tk * tn * 2 + tm * tn * tile_bytes)
        if M % tm == 0 and blocks + (tm * tn * 4 if nk > 1 else 0) <= MM_VMEM_BUDGET:
            break
    else:
        tm = M

    def body(*refs):
        a_ref, b_ref = refs[:2]
        ex_refs = refs[2:2 + n_ex]
        o_refs = refs[2 + n_ex:2 + n_ex + n_out]

        def finish(acc):
            for o_ref, val, dt in zip(o_refs, epilogue(acc, *[r[...] for r in ex_refs]), out_dtypes, strict=True):
                o_ref[...] = val.astype(dt)

        part = jnp.dot(a_ref[...].astype(BF16), b_ref[...].astype(BF16), preferred_element_type=F32)
        if nk == 1:
            finish(part)
        else:
            acc_ref = refs[-1]
            k = pl.program_id(2)

            @pl.when(k == 0)
            def _():
                acc_ref[...] = part

            @pl.when(k > 0)
            def _():
                acc_ref[...] += part

            @pl.when(k == nk - 1)
            def _():
                finish(acc_ref[...])

    tile = pl.BlockSpec((tm, tn), lambda i, j, k: (i, j))
    outs = pl.pallas_call(
        body, name=name, grid=(M // tm, N // tn, nk),
        in_specs=[pl.BlockSpec((tm, tk), lambda i, j, k: (i, k)), pl.BlockSpec((tk, tn), lambda i, j, k: (k, j))]
        + [tile] * n_ex,
        out_specs=[tile] * n_out, out_shape=[jax.ShapeDtypeStruct((M, N), d) for d in out_dtypes],
        scratch_shapes=[pltpu.VMEM((tm, tn), F32)] if nk > 1 else [],
        compiler_params=_params("parallel", "parallel", "arbitrary"),
    )(a, b, *extras)
    return tuple(outs) if custom else outs[0]


def mm_tn(a, b, *, name, scale=1.0):
    M, K = a.shape
    M2, N = b.shape
    assert M == M2
    tm = _pick(M, (1024, 512, 256, 128))
    tk = _pick(K, (1408, 1024, 512, 256, 128))
    tn = next((c for c in _WIDE_TILES if N % c == 0 and tk * c * 4 <= MM_ACC_BYTES), N)
    nm = M // tm

    def body(a_ref, b_ref, o_ref):
        m = pl.program_id(2)
        part = lax.dot_general(a_ref[...].astype(BF16), b_ref[...].astype(BF16), (((0,), (0,)), ((), ())),
                               preferred_element_type=F32)

        @pl.when(m == 0)
        def _():
            o_ref[...] = part

        @pl.when(m > 0)
        def _():
            o_ref[...] += part

        if scale != 1.0:
            @pl.when(m == nm - 1)
            def _():
                o_ref[...] = o_ref[...] * scale

    return pl.pallas_call(
        body, name=name, grid=(K // tk, N // tn, nm),
        in_specs=[pl.BlockSpec((tm, tk), lambda i, j, m: (m, i)), pl.BlockSpec((tm, tn), lambda i, j, m: (m, j))],
        out_specs=pl.BlockSpec((tk, tn), lambda i, j, m: (i, j)),
        out_shape=jax.ShapeDtypeStruct((K, N), F32),
        compiler_params=_params("parallel", "parallel", "arbitrary"),
    )(a, b)


def rms_fwd(x, g, *, name):
    S, D = x.shape
    ts = _pick(S, (512, 256, 128))

    def body(x_ref, g_ref, h_ref):
        xv = x_ref[...]
        r = lax.rsqrt(jnp.mean(xv * xv, axis=-1, keepdims=True) + RMS_EPS)
        h_ref[...] = ((xv * r) * g_ref[...]).astype(BF16)

    return pl.pallas_call(
        body, name=name, grid=(S // ts,),
        in_specs=[pl.BlockSpec((ts, D), lambda i: (i, 0)), pl.BlockSpec((1, D), lambda i: (0, 0))],
        out_specs=pl.BlockSpec((ts, D), lambda i: (i, 0)),
        out_shape=jax.ShapeDtypeStruct((S, D), BF16),
        compiler_params=_params("parallel"),
    )(x, g)


def rms_bwd(x, g, dh, dx_in, *, name):
    S, D = x.shape
    ts = _pick(S, (512, 256, 128))
    has_in = dx_in is not None

    def body(*refs):
        if has_in:
            x_ref, g_ref, dh_ref, di_ref, dx_ref, dg_ref = refs
        else:
            x_ref, g_ref, dh_ref, dx_ref, dg_ref = refs
        xv = x_ref[...]
        dh_v = dh_ref[...]
        r = lax.rsqrt(jnp.mean(xv * xv, axis=-1, keepdims=True) + RMS_EPS)
        xh = xv * r
        gd = dh_v * g_ref[...]
        dx = r * (gd - xh * jnp.mean(gd * xh, axis=-1, keepdims=True))
        if has_in:
            dx = di_ref[...] + dx
        dx_ref[...] = dx
        part = jnp.sum(dh_v * xh, axis=0, keepdims=True)

        @pl.when(pl.program_id(0) == 0)
        def _():
            dg_ref[...] = part

        @pl.when(pl.program_id(0) > 0)
        def _():
            dg_ref[...] += part

    row = pl.BlockSpec((ts, D), lambda i: (i, 0))
    vec = pl.BlockSpec((1, D), lambda i: (0, 0))
    return pl.pallas_call(
        body, name=name, grid=(S // ts,),
        in_specs=[row, vec, row] + ([row] if has_in else []),
        out_specs=[row, vec],
        out_shape=[jax.ShapeDtypeStruct((S, D), F32), jax.ShapeDtypeStruct((1, D), F32)],
        compiler_params=_params("arbitrary"),
    )(*([x, g, dh] + ([dx_in] if has_in else [])))


def final_loss_bwd(x, g, target, *, name):
    S, D = x.shape
    ts = _pick(S, (512, 256, 128))

    def body(x_ref, g_ref, t_ref, dx_ref, dg_ref, ls_ref):
        xv = x_ref[...]
        gv = g_ref[...]
        r = lax.rsqrt(jnp.mean(xv * xv, axis=-1, keepdims=True) + RMS_EPS)
        xh = xv * r
        e = xh * gv - t_ref[...]
        dy = e * (1.0 / D)
        gd = dy * gv
        dx_ref[...] = r * (gd - xh * jnp.mean(gd * xh, axis=-1, keepdims=True))
        dg_part = jnp.sum(dy * xh, axis=0, keepdims=True)
        ls_part = jnp.sum(e * e, axis=0, keepdims=True)

        @pl.when(pl.program_id(0) == 0)
        def _():
            dg_ref[...] = dg_part
            ls_ref[...] = ls_part

        @pl.when(pl.program_id(0) > 0)
        def _():
            dg_ref[...] += dg_part
            ls_ref[...] += ls_part

    row = pl.BlockSpec((ts, D), lambda i: (i, 0))
    vec = pl.BlockSpec((1, D), lambda i: (0, 0))
    return pl.pallas_call(
        body, name=name, grid=(S // ts,), in_specs=[row, vec, row], out_specs=[row, vec, vec],
        out_shape=[jax.ShapeDtypeStruct((S, D), F32), jax.ShapeDtypeStruct((1, D), F32),
                   jax.ShapeDtypeStruct((1, D), F32)],
        compiler_params=_params("arbitrary"),
    )(x, g, target)


def _swiglu(up, gp):
    gp = gp.astype(F32)
    return up, gp * _sig(gp) * up


def _swiglu_grad(da, gp, up):
    da = da * 0.5
    gp, up = gp.astype(F32), up.astype(F32)
    s = _sig(gp)
    return da * up * (s * (1.0 + gp * (1.0 - s))), da * (gp * s)


def ffn_fwd(x, w, tag):
    h = rms_fwd(x, w["norm"], name=f"{tag}_rms")
    gp = mm(h, w["w_gate"], name=f"{tag}_gate", out_dtype=BF16)
    up, a = mm(h, w["w_up"], name=f"{tag}_up", extras=(gp,), epilogue=_swiglu, out_dtypes=(BF16, BF16))
    y = mm(a, w["w_down"], name=f"{tag}_down", res=x, scale=0.5)
    return y, (x, h, gp, up, a)


def ffn_bwd(dx, saved, w, tag):
    x, h, gp, up, a = saved
    grads = {"w_down": mm_tn(a, dx, name=f"{tag}_dwd", scale=0.5)}
    dgp, dup = mm(dx, w["w_down_t"], name=f"{tag}_da", extras=(gp, up), epilogue=_swiglu_grad,
                  out_dtypes=(BF16, BF16))
    grads["w_gate"] = mm_tn(h, dgp, name=f"{tag}_dwg")
    grads["w_up"] = mm_tn(h, dup, name=f"{tag}_dwu")
    dh = mm(dgp, w["w_gate_t"], name=f"{tag}_dh1")
    dh = mm(dup, w["w_up_t"], name=f"{tag}_dh2", res=dh)
    dx, grads["norm"] = rms_bwd(x, w["norm"], dh, dx, name=f"{tag}_drms")
    return dx, grads


SEG_AB, SEG_AC, SEG_AH, SEG_U, SEG_V, SEG_FQ, SEG_FK, SEG_FV, SEG_FL = range(9)
N_SEG = 9


def _seg_block(D, seg):
    return 3 * D // SEG + seg


def _shift_down(z, prev8, n, rows):
    out = pltpu.roll(z, n, 0)
    for r in range(n):
        out = jnp.where(rows == r, prev8[8 - n + r:8 - n + r + 1, :], out)
    return out


def _shift_up(z, next8, n, rows, ts):
    out = pltpu.roll(z, ts - n, 0)
    for r in range(n):
        out = jnp.where(rows == ts - n + r, next8[r:r + 1, :], out)
    return out


def conv_fwd(proj, conv_w, D, *, name):
    S = proj.shape[0]
    ts = _pick(S, (512, 256, 128))
    b0 = _seg_block(D, 0)

    def body(ab_ref, ac_ref, ah_ref, pc_ref, ph_ref, w_ref, y_ref):
        i = pl.program_id(0)
        rows = lax.broadcasted_iota(jnp.int32, (ts, 1), 0)
        z = ac_ref[...] * ah_ref[...]
        zp = pc_ref[...] * ph_ref[...] * (i > 0).astype(F32)
        w = w_ref[...]
        y = w[0:1, :] * _shift_down(z, zp, 2, rows) + w[1:2, :] * _shift_down(z, zp, 1, rows) + w[2:3, :] * z
        y_ref[...] = (ab_ref[...] * y).astype(BF16)

    def seg(s):
        return pl.BlockSpec((ts, SEG), lambda i: (i, b0 + s))

    def prev(s):
        return pl.BlockSpec((8, SEG), lambda i: (jnp.maximum(i * (ts // 8) - 1, 0), b0 + s))

    return pl.pallas_call(
        body, name=name, grid=(S // ts,),
        in_specs=[seg(SEG_AB), seg(SEG_AC), seg(SEG_AH), prev(SEG_AC), prev(SEG_AH),
                  pl.BlockSpec((3, SEG), lambda i: (0, 0))],
        out_specs=pl.BlockSpec((ts, SEG), lambda i: (i, 0)),
        out_shape=jax.ShapeDtypeStruct((S, SEG), BF16), compiler_params=_params("parallel"),
    )(proj, proj, proj, proj, proj, conv_w)


def conv_bwd(proj, conv_w, dy, D, *, name):
    S = proj.shape[0]
    ts = _pick(S, (512, 256, 128))
    nt = S // ts
    b0 = _seg_block(D, 0)

    def body(ab_ref, ac_ref, ah_ref, pc_ref, ph_ref, nb_ref, dy_ref, ndy_ref, w_ref,
             dab_ref, dac_ref, dah_ref, dw_ref):
        i = pl.program_id(0)
        rows = lax.broadcasted_iota(jnp.int32, (ts, 1), 0)
        ab, ac, ah = ab_ref[...], ac_ref[...], ah_ref[...]
        z = ac * ah
        zp = pc_ref[...] * ph_ref[...] * (i > 0).astype(F32)
        w = w_ref[...]
        z1 = _shift_down(z, zp, 1, rows)
        z2 = _shift_down(z, zp, 2, rows)
        y = w[0:1, :] * z2 + w[1:2, :] * z1 + w[2:3, :] * z
        dyv = dy_ref[...]
        dab_ref[...] = (dyv * y).astype(BF16)
        dyy = dyv * ab
        nyy = ndy_ref[...] * nb_ref[...] * (i < nt - 1).astype(F32)
        dz = (w[2:3, :] * dyy + w[1:2, :] * _shift_up(dyy, nyy, 1, rows, ts)
              + w[0:1, :] * _shift_up(dyy, nyy, 2, rows, ts))
        dac_ref[...] = (dz * ah).astype(BF16)
        dah_ref[...] = (dz * ac).astype(BF16)
        parts = [jnp.sum(dyy * zz, axis=0, keepdims=True) for zz in (z2, z1, z)]

        @pl.when(i == 0)
        def _():
            for k in range(3):
                dw_ref[k:k + 1, :] = parts[k]

        @pl.when(i > 0)
        def _():
            for k in range(3):
                dw_ref[k:k + 1, :] += parts[k]

    def seg(s):
        return pl.BlockSpec((ts, SEG), lambda i: (i, b0 + s))

    def prev(s):
        return pl.BlockSpec((8, SEG), lambda i: (jnp.maximum(i * (ts // 8) - 1, 0), b0 + s))

    nxt_row = lambda i: jnp.minimum((i + 1) * (ts // 8), S // 8 - 1)
    out_row = pl.BlockSpec((ts, SEG), lambda i: (i, 0))
    return pl.pallas_call(
        body, name=name, grid=(nt,),
        in_specs=[seg(SEG_AB), seg(SEG_AC), seg(SEG_AH), prev(SEG_AC), prev(SEG_AH),
                  pl.BlockSpec((8, SEG), lambda i: (nxt_row(i), b0 + SEG_AB)),
                  out_row, pl.BlockSpec((8, SEG), lambda i: (nxt_row(i), 0)),
                  pl.BlockSpec((3, SEG), lambda i: (0, 0))],
        out_specs=[out_row, out_row, out_row, pl.BlockSpec((3, SEG), lambda i: (0, 0))],
        out_shape=[jax.ShapeDtypeStruct((S, SEG), BF16)] * 3 + [jax.ShapeDtypeStruct((3, SEG), F32)],
        compiler_params=_params("arbitrary"),
    )(proj, proj, proj, proj, proj, proj, dy, dy, conv_w)


def _tril_mask():
    r = lax.broadcasted_iota(jnp.int32, (CHUNK, CHUNK), 0)
    c = lax.broadcasted_iota(jnp.int32, (CHUNK, CHUNK), 1)
    return c <= r


def sg_fwd(proj, sg_norm, sg_w, sg_bt, D, *, name):
    S = proj.shape[0]
    ts = _pick(S, (512, 256, 128))
    b0 = _seg_block(D, 0)

    def body(u_ref, v_ref, gs_ref, w_ref, b_ref, y_ref):
        ug, _ = _gelu(u_ref[...])
        vg, _ = _gelu(v_ref[...])
        vn = ((vg * lax.rsqrt(jnp.mean(vg * vg, axis=-1, keepdims=True) + RMS_EPS)) * gs_ref[...]).astype(BF16)
        mask = _tril_mask()
        for g in range(SG_GROUPS):
            wg = jnp.where(mask, w_ref[g], 0.0).astype(BF16)
            cols = slice(g * CHUNK, (g + 1) * CHUNK)
            for n in range(ts // CHUNK):
                rws = slice(n * CHUNK, (n + 1) * CHUNK)
                sv = jnp.dot(wg, vn[rws, cols], preferred_element_type=F32) + b_ref[g]
                y_ref[rws, cols] = (ug[rws, cols] * sv).astype(BF16)

    seg = lambda s: pl.BlockSpec((ts, SEG), lambda i: (i, b0 + s))
    return pl.pallas_call(
        body, name=name, grid=(S // ts,),
        in_specs=[seg(SEG_U), seg(SEG_V), pl.BlockSpec((1, SEG), lambda i: (0, 0)),
                  pl.BlockSpec((SG_GROUPS, CHUNK, CHUNK), lambda i: (0, 0, 0)),
                  pl.BlockSpec((SG_GROUPS, CHUNK, 1), lambda i: (0, 0, 0))],
        out_specs=pl.BlockSpec((ts, SEG), lambda i: (i, 0)),
        out_shape=jax.ShapeDtypeStruct((S, SEG), BF16), compiler_params=_params("parallel"),
    )(proj, proj, sg_norm, sg_w, sg_bt)


def sg_bwd(proj, sg_norm, sg_w, sg_bt, dy, D, *, name):
    S = proj.shape[0]
    ts = _pick(S, (512, 256, 128))
    nt = S // ts
    b0 = _seg_block(D, 0)

    def body(u_ref, v_ref, dy_ref, gs_ref, w_ref, b_ref, du_ref, dv_ref, dw_ref, db_ref, dgs_ref, dvn_sc):
        i = pl.program_id(0)
        uv, vv, dyv = u_ref[...], v_ref[...], dy_ref[...]
        ug, ut = _gelu(uv)
        vg, vt = _gelu(vv)
        r = lax.rsqrt(jnp.mean(vg * vg, axis=-1, keepdims=True) + RMS_EPS)
        vh = vg * r
        gs = gs_ref[...]
        vn = (vh * gs).astype(BF16)
        dsv = dyv * ug
        dsv_b = dsv.astype(BF16)
        mask = _tril_mask()

        @pl.when(i == 0)
        def _():
            dw_ref[...] = jnp.zeros_like(dw_ref)
            db_ref[...] = jnp.zeros_like(db_ref)

        for g in range(SG_GROUPS):
            wg = jnp.where(mask, w_ref[g], 0.0).astype(BF16)
            cols = slice(g * CHUNK, (g + 1) * CHUNK)
            dw_acc = jnp.zeros((CHUNK, CHUNK), F32)
            db_acc = jnp.zeros((CHUNK, 1), F32)
            for n in range(ts // CHUNK):
                rws = slice(n * CHUNK, (n + 1) * CHUNK)
                vblk = vn[rws, cols]
                sv = jnp.dot(wg, vblk, preferred_element_type=F32) + b_ref[g]
                du_ref[rws, cols] = (dyv[rws, cols] * sv * _gelu_grad(uv[rws, cols], ut[rws, cols])).astype(BF16)
                dblk = dsv_b[rws, cols]
                dvn_sc[rws, cols] = lax.dot_general(wg, dblk, (((0,), (0,)), ((), ())), preferred_element_type=F32)
                dw_acc = dw_acc + lax.dot_general(dblk, vblk, (((1,), (1,)), ((), ())), preferred_element_type=F32)
                db_acc = db_acc + jnp.sum(dsv[rws, cols], axis=1, keepdims=True)
            dw_ref[g] += jnp.where(mask, dw_acc, 0.0)
            db_ref[g] += db_acc

        dvn = dvn_sc[...]
        gd = dvn * gs
        dvg = r * (gd - vh * jnp.mean(gd * vh, axis=-1, keepdims=True))
        dv_ref[...] = (dvg * _gelu_grad(vv, vt)).astype(BF16)
        dgs_part = jnp.sum(dvn * vh, axis=0, keepdims=True)

        @pl.when(i == 0)
        def _():
            dgs_ref[...] = dgs_part

        @pl.when(i > 0)
        def _():
            dgs_ref[...] += dgs_part

    seg = lambda s: pl.BlockSpec((ts, SEG), lambda i: (i, b0 + s))
    row = pl.BlockSpec((ts, SEG), lambda i: (i, 0))
    wspec = pl.BlockSpec((SG_GROUPS, CHUNK, CHUNK), lambda i: (0, 0, 0))
    bspec = pl.BlockSpec((SG_GROUPS, CHUNK, 1), lambda i: (0, 0, 0))
    vec = pl.BlockSpec((1, SEG), lambda i: (0, 0))
    return pl.pallas_call(
        body, name=name, grid=(nt,),
        in_specs=[seg(SEG_U), seg(SEG_V), row, vec, wspec, bspec],
        out_specs=[row, row, wspec, bspec, vec],
        out_shape=[jax.ShapeDtypeStruct((S, SEG), BF16)] * 2
        + [jax.ShapeDtypeStruct((SG_GROUPS, CHUNK, CHUNK), F32), jax.ShapeDtypeStruct((SG_GROUPS, CHUNK, 1), F32),
           jax.ShapeDtypeStruct((1, SEG), F32)],
        scratch_shapes=[pltpu.VMEM((ts, SEG), F32)],
        compiler_params=_params("arbitrary"),
    )(proj, proj, dy, sg_norm, sg_w, sg_bt)


def _mesh_pos():
    return lax.axis_index("x"), lax.axis_index("y"), lax.axis_index("c")


def _flip(v, bit):
    return 1 - v if bit else v


def _comm_scratch(n):
    return [pltpu.SemaphoreType.DMA((n, 7)), pltpu.SemaphoreType.DMA((n, 7)), pltpu.SemaphoreType.DMA((n,))]


def _gather_plan(refs):
    x_refs, out_refs, send_sems, recv_sems, local_sems = refs
    n = len(x_refs)
    mx, my, mc = _mesh_pos()
    me, sibling = (mx, my, mc), (mx, my, 1 - mc)
    chips = [(1 - mx, my), (mx, 1 - my), (1 - mx, 1 - my)]

    def copy(a, k, block, to, from_input=False):
        slot = out_refs[a].at[4 * block[0] + 2 * block[1] + block[2]]
        return pltpu.make_async_remote_copy(
            src_ref=x_refs[a] if from_input else slot, dst_ref=slot,
            send_sem=send_sems.at[a, k], recv_sem=recv_sems.at[a, k], device_id=to, device_id_type=MESH_IDS)

    local = [pltpu.make_async_copy(x_refs[a], out_refs[a].at[4 * mx + 2 * my + mc], local_sems.at[a])
             for a in range(n)]
    first = []
    for a in range(n):
        first.append(copy(a, 0, me, sibling, from_input=True))
        first += [copy(a, 1 + j, me, (*chip, mc), from_input=True) for j, chip in enumerate(chips)]
    return n, me, sibling, chips, mc, copy, local, first


def gather_start(refs):
    n, me, sibling, chips, mc, copy, local, first = _gather_plan(refs)
    for cp in local + first:
        cp.start()


def gather_finish(refs):
    n, me, sibling, chips, mc, copy, local, first = _gather_plan(refs)
    onward = []
    for j, chip in enumerate(chips):
        for a in range(n):
            copy(a, 1 + j, (*chip, mc), me).wait_recv()
            cp = copy(a, 4 + j, (*chip, mc), sibling)
            cp.start()
            onward.append(cp)
    for a in range(n):
        copy(a, 0, sibling, me).wait_recv()
        for j, chip in enumerate(chips):
            copy(a, 4 + j, (*chip, 1 - mc), me).wait_recv()
    for cp in first + onward:
        cp.wait_send()
    for cp in local:
        cp.wait()


def _scatter_plan(refs):
    g_refs, out_refs, send_sems, recv_sems, local_sems = refs
    n = len(g_refs)
    mx, my, mc = _mesh_pos()
    me = 4 * mx + 2 * my + mc
    local = [pltpu.make_async_copy(g_refs[a].at[me], out_refs[a].at[me], local_sems.at[a]) for a in range(n)]
    sends, arrivals = [], []
    for k in range(1, N_DEV):
        peer = (_flip(mx, k & 4), _flip(my, k & 2), _flip(mc, k & 1))
        peer_slot = 4 * peer[0] + 2 * peer[1] + peer[2]
        for a in range(n):
            sems = dict(send_sem=send_sems.at[a, k - 1], recv_sem=recv_sems.at[a, k - 1], device_id=peer,
                        device_id_type=MESH_IDS)
            sends.append(pltpu.make_async_remote_copy(src_ref=g_refs[a].at[peer_slot], dst_ref=out_refs[a].at[me],
                                                      **sems))
            arrivals.append(pltpu.make_async_remote_copy(src_ref=g_refs[a].at[peer_slot],
                                                         dst_ref=out_refs[a].at[peer_slot], **sems))
    return local, sends, arrivals


def scatter_start(refs):
    local, sends, arrivals = _scatter_plan(refs)
    for cp in local + sends:
        cp.start()


def scatter_finish(refs):
    local, sends, arrivals = _scatter_plan(refs)
    for cp in arrivals:
        cp.wait_recv()
    for cp in sends:
        cp.wait_send()
    for cp in local:
        cp.wait()


def _split_comm_refs(refs, n_in, n_out, n_side):
    ins, side_in = refs[:n_in], refs[n_in:n_in + n_side]
    outs = refs[n_in + n_side:n_in + n_side + n_out]
    side_out = refs[n_in + n_side + n_out:n_in + 2 * n_side + n_out]
    rest = refs[n_in + 2 * n_side + n_out:]
    if n_side == 0:
        return ins + outs + rest, None
    return ins + outs + rest[:-3], (side_in, side_out) + tuple(rest[-3:])


def _log_sigmoid(x):
    return jnp.minimum(x, 0.0) - jnp.log(1.0 + jnp.exp(-jnp.abs(x)))


def fox_cumlog(proj, b_f, D, *, name):
    S = proj.shape[0]
    ts = _pick(S, (512, 256, 128))
    blk = (3 * D + SEG_FL * SEG) // LANES

    def body(f_ref, b_ref, c_ref, carry):
        i = pl.program_id(0)

        @pl.when(i == 0)
        def _():
            carry[...] = jnp.zeros_like(carry)

        rows = lax.broadcasted_iota(jnp.int32, (ts, 1), 0)
        acc = _log_sigmoid(f_ref[...] + b_ref[...])
        d = 1
        while d < ts:
            acc = acc + jnp.where(rows >= d, pltpu.roll(acc, d, 0), 0.0)
            d *= 2
        acc = acc + carry[...]
        c_ref[...] = acc
        carry[...] = acc[ts - 1:ts, :]

    return pl.pallas_call(
        body, name=name, grid=(S // ts,),
        in_specs=[pl.BlockSpec((ts, LANES), lambda i: (i, blk)), pl.BlockSpec((1, LANES), lambda i: (0, 0))],
        out_specs=pl.BlockSpec((ts, LANES), lambda i: (i, 0)),
        out_shape=jax.ShapeDtypeStruct((S, LANES), F32),
        scratch_shapes=[pltpu.VMEM((1, LANES), F32)],
        compiler_params=_params("arbitrary"),
    )(proj, b_f)


def fox_dlogit(proj, b_f, dc, D, *, name):
    S = proj.shape[0]
    ts = _pick(S, (512, 256, 128))
    nt = S // ts
    blk = (3 * D + SEG_FL * SEG) // LANES

    def body(f_ref, b_ref, dc_ref, df_ref, db_ref, carry):
        i = pl.program_id(0)

        @pl.when(i == 0)
        def _():
            carry[...] = jnp.zeros_like(carry)

        rows = lax.broadcasted_iota(jnp.int32, (ts, 1), 0)
        acc = dc_ref[...]
        d = 1
        while d < ts:
            acc = acc + jnp.where(rows < ts - d, pltpu.roll(acc, ts - d, 0), 0.0)
            d *= 2
        acc = acc + carry[...]
        carry[...] = acc[0:1, :]
        df = acc * _sig(-(f_ref[...] + b_ref[...]))
        df_ref[...] = jnp.zeros_like(df_ref)
        df_ref[:, 0:LANES] = df.astype(BF16)
        part = jnp.sum(df, axis=0, keepdims=True)

        @pl.when(i == 0)
        def _():
            db_ref[...] = part

        @pl.when(i > 0)
        def _():
            db_ref[...] += part

    rev = lambda i: nt - 1 - i
    return pl.pallas_call(
        body, name=name, grid=(nt,),
        in_specs=[pl.BlockSpec((ts, LANES), lambda i: (rev(i), blk)), pl.BlockSpec((1, LANES), lambda i: (0, 0)),
                  pl.BlockSpec((ts, LANES), lambda i: (rev(i), 0))],
        out_specs=[pl.BlockSpec((ts, SEG), lambda i: (rev(i), 0)), pl.BlockSpec((1, LANES), lambda i: (0, 0))],
        out_shape=[jax.ShapeDtypeStruct((S, SEG), BF16), jax.ShapeDtypeStruct((1, LANES), F32)],
        scratch_shapes=[pltpu.VMEM((1, LANES), F32)],
        compiler_params=_params("arbitrary"),
    )(proj, b_f, dc)


def _causal_mask(t, keys_on_rows=True):
    r = lax.broadcasted_iota(jnp.int32, (t, t), 0)
    c = lax.broadcasted_iota(jnp.int32, (t, t), 1)
    return (r <= c) if keys_on_rows else (c <= r)


FOX_PAD = LANES
COL_C = FOX_HEAD_DIM
COL_ROWSUM = FOX_HEAD_DIM + 3
COL_L = FOX_HEAD_DIM


def _head_pair(ref, a):
    x = ref[...]
    return x if a == 0 else pltpu.roll(x, FOX_HEAD_DIM, 1)


def fox_pack(proj, c, D, scale, *, name):
    S = proj.shape[0]
    ts = _pick(S, (512, 256, 128))
    base = 3 * D // LANES
    per_seg = SEG // LANES

    def body(q_ref, k_ref, v_ref, c_ref, qo, ko, vo, qto, vto):
        hp = pl.program_id(1)
        lane = lax.broadcasted_iota(jnp.int32, (ts, LANES), 1)
        is_val = lane < FOX_HEAD_DIM
        cv = c_ref[...]
        for a in range(2):
            ch = jnp.sum(jnp.where(lane == 2 * hp + a, cv, 0.0), axis=1, keepdims=True)
            c_hi = ch.astype(BF16).astype(F32)
            r1 = ch - c_hi
            c_mid = r1.astype(BF16).astype(F32)
            c_lo = r1 - c_mid
            qa = jnp.where(is_val, _head_pair(q_ref, a) * scale, jnp.where(lane < COL_C + 3, 1.0, 0.0))
            extra = jnp.where(lane == COL_C, -c_hi, jnp.where(lane == COL_C + 1, -c_mid, jnp.where(
                lane == COL_C + 2, -c_lo, jnp.where(lane == COL_ROWSUM, 1.0, 0.0))))
            va = jnp.where(is_val, _head_pair(v_ref, a), jnp.where(lane == COL_L, 1.0, 0.0))
            qo[a] = qa.astype(BF16)
            ko[a] = jnp.where(is_val, _head_pair(k_ref, a), extra).astype(BF16)
            vo[a] = va.astype(BF16)
            qto[a] = qa.T.astype(BF16)
            vto[a] = va.T.astype(BF16)

    seg = lambda s: pl.BlockSpec((ts, LANES), lambda i, hp: (i, base + s * per_seg + hp))
    out = pl.BlockSpec((2, ts, FOX_PAD), lambda i, hp: (hp, i, 0))
    out_t = pl.BlockSpec((2, FOX_PAD, ts), lambda i, hp: (hp, 0, i))
    return pl.pallas_call(
        body, name=name, grid=(S // ts, FOX_HEADS // 2),
        in_specs=[seg(SEG_FQ), seg(SEG_FK), seg(SEG_FV), pl.BlockSpec((ts, LANES), lambda i, hp: (i, 0))],
        out_specs=[out] * 3 + [out_t] * 2,
        out_shape=[jax.ShapeDtypeStruct((FOX_HEADS, S, FOX_PAD), BF16)] * 3
        + [jax.ShapeDtypeStruct((FOX_HEADS, FOX_PAD, S), BF16)] * 2,
        compiler_params=_params("parallel", "parallel"),
    )(proj, proj, proj, c)


def heads_pack(x, *, name):
    S = x.shape[0]
    ts = _pick(S, (512, 256, 128))

    def body(x_ref, o_ref, ot_ref):
        lane = lax.broadcasted_iota(jnp.int32, (ts, LANES), 1)
        for a in range(2):
            xa = jnp.where(lane < FOX_HEAD_DIM, _head_pair(x_ref, a), 0.0)
            o_ref[a] = xa.astype(BF16)
            ot_ref[a] = xa.T.astype(BF16)

    return pl.pallas_call(
        body, name=name, grid=(S // ts, FOX_HEADS // 2),
        in_specs=[pl.BlockSpec((ts, LANES), lambda i, hp: (i, hp))],
        out_specs=[pl.BlockSpec((2, ts, FOX_PAD), lambda i, hp: (hp, i, 0)),
                   pl.BlockSpec((2, FOX_PAD, ts), lambda i, hp: (hp, 0, i))],
        out_shape=[jax.ShapeDtypeStruct((FOX_HEADS, S, FOX_PAD), BF16),
                   jax.ShapeDtypeStruct((FOX_HEADS, FOX_PAD, S), BF16)],
        compiler_params=_params("parallel", "parallel"),
    )(x)


def heads_unpack(x, scale, *, name):
    S = x.shape[1]
    ts = _pick(S, (512, 256, 128))

    def body(x_ref, o_ref):
        lane = lax.broadcasted_iota(jnp.int32, (ts, LANES), 1)
        both = jnp.where(lane < FOX_HEAD_DIM, x_ref[0], pltpu.roll(x_ref[1], FOX_HEAD_DIM, 1))
        o_ref[...] = (both * scale).astype(BF16)

    return pl.pallas_call(
        body, name=name, grid=(S // ts, FOX_HEADS // 2),
        in_specs=[pl.BlockSpec((2, ts, FOX_PAD), lambda i, hp: (hp, i, 0))],
        out_specs=pl.BlockSpec((ts, LANES), lambda i, hp: (i, hp)),
        out_shape=jax.ShapeDtypeStruct((S, FOX_HEADS * FOX_HEAD_DIM), BF16),
        compiler_params=_params("parallel", "parallel"),
    )(x)


FOX_FWD_GROUP = 1


def _fox_fwd_tile(S):
    return min(2048, max(128, S // 4))


def _fox_bwd_tile(S):
    return min(1024, max(128, S // 4))


def fox_fwd(qt, k, vt, *, name, gather=()):
    H, W, S = qt.shape
    t = _fox_fwd_tile(S)
    n = S // t
    G = FOX_FWD_GROUP
    ns = len(gather)

    def body(*refs):
        (qt_ref, k_ref, vt_ref, ot_ref, lse_ref, m_sc, acc_sc), comm = _split_comm_refs(refs, 3, 2, ns)
        h, i, j = pl.program_id(0), pl.program_id(1), pl.program_id(2)

        if ns:
            @pl.when((h == 0) & (i == 0) & (j == 0))
            def _():
                gather_start(comm)

        @pl.when(j == 0)
        def _():
            m_sc[...] = jnp.full_like(m_sc, NEG_BIG)
            acc_sc[...] = jnp.zeros_like(acc_sc)

        def step(masked):
            for g in range(G):
                st = jnp.dot(k_ref[g], qt_ref[g], preferred_element_type=F32)
                if masked:
                    st = jnp.where(_causal_mask(t, keys_on_rows=True), st, NEG_BIG)
                m_prev = m_sc[g]
                m_new = jnp.maximum(m_prev, jnp.max(st, axis=0, keepdims=True))
                pt = jnp.exp(st - m_new)
                acc_sc[g] = jnp.exp(m_prev - m_new) * acc_sc[g] + jnp.dot(vt_ref[g], pt.astype(BF16),
                                                                          preferred_element_type=F32)
                m_sc[g] = m_new

        @pl.when(j < i)
        def _():
            step(False)

        @pl.when(j == i)
        def _():
            step(True)
            for g in range(G):
                acc = acc_sc[g]
                l = acc[COL_L:COL_L + 1, :]
                ot_ref[g] = acc / l
                lse_ref[g] = m_sc[g] + jnp.log(l)

        if ns:
            @pl.when((h == H // G - 1) & (i == n - 1) & (j == n - 1))
            def _():
                gather_finish(comm)

    qs = pl.BlockSpec((G, W, t), lambda h, i, j: (h, 0, i))
    ks = pl.BlockSpec((G, t, W), lambda h, i, j: (h, jnp.minimum(j, i), 0))
    vs = pl.BlockSpec((G, W, t), lambda h, i, j: (h, 0, jnp.minimum(j, i)))
    row = pl.BlockSpec((G, 1, t), lambda h, i, j: (h, 0, i))
    outs = pl.pallas_call(
        body, name=name, grid=(H // G, n, n), in_specs=[qs, ks, vs] + [ANY_SPEC] * ns,
        out_specs=[qs, row] + [ANY_SPEC] * ns,
        out_shape=[jax.ShapeDtypeStruct((H, W, S), F32), jax.ShapeDtypeStruct((H, 1, S), F32)]
        + [jax.ShapeDtypeStruct((N_DEV,) + x.shape, x.dtype) for x in gather],
        scratch_shapes=[pltpu.VMEM((G, 1, t), F32), pltpu.VMEM((G, W, t), F32)] + (_comm_scratch(ns) if ns else []),
        compiler_params=_params("arbitrary", "arbitrary", "arbitrary") if ns
        else _params("parallel", "parallel", "arbitrary"),
    )(qt, k, vt, *gather)
    return outs[0], outs[1], list(outs[2:])


def fox_dc(dq, dk, *, name):
    H, S, W = dq.shape
    ts = _pick(S, (512, 256, 128))

    def body(dq_ref, dk_ref, o_ref):
        lane = lax.broadcasted_iota(jnp.int32, (ts, LANES), 1)
        acc = jnp.zeros((ts, LANES), F32)
        for h in range(H):
            d = dq_ref[h][:, COL_ROWSUM:COL_ROWSUM + 1] - dk_ref[h][:, COL_C:COL_C + 1]
            acc = jnp.where(lane == h, d, acc)
        o_ref[...] = acc

    blk = pl.BlockSpec((H, ts, W), lambda i: (0, i, 0))
    return pl.pallas_call(
        body, name=name, grid=(S // ts,), in_specs=[blk, blk],
        out_specs=pl.BlockSpec((ts, LANES), lambda i: (i, 0)),
        out_shape=jax.ShapeDtypeStruct((S, LANES), F32), compiler_params=_params("parallel"),
    )(dq, dk)


def heads_unpack_t(xt, *, name):
    S = xt.shape[2]
    ts = _pick(S, (512, 256, 128))

    def body(x_ref, y_ref, o_ref):
        lane = lax.broadcasted_iota(jnp.int32, (ts, LANES), 1)
        x0, x1 = x_ref[0].T, x_ref[1].T
        o_ref[0] = x0
        o_ref[1] = x1
        y_ref[...] = jnp.where(lane < FOX_HEAD_DIM, x0, pltpu.roll(x1, FOX_HEAD_DIM, 1)).astype(BF16)

    return pl.pallas_call(
        body, name=name, grid=(S // ts, FOX_HEADS // 2),
        in_specs=[pl.BlockSpec((2, FOX_PAD, ts), lambda i, hp: (hp, 0, i))],
        out_specs=[pl.BlockSpec((ts, LANES), lambda i, hp: (i, hp)),
                   pl.BlockSpec((2, ts, FOX_PAD), lambda i, hp: (hp, i, 0))],
        out_shape=[jax.ShapeDtypeStruct((S, FOX_HEADS * FOX_HEAD_DIM), BF16),
                   jax.ShapeDtypeStruct((FOX_HEADS, S, FOX_PAD), F32)],
        compiler_params=_params("parallel", "parallel"),
    )(xt)


def fox_delta(do, o, *, name):
    H, S, Dh = o.shape
    t = _pick(S, (2048, 1024, 512, 256, 128))

    def body(do_ref, o_ref, d_ref):
        d_ref[0] = jnp.sum(do_ref[0] * o_ref[0], axis=-1, keepdims=True)

    blk = pl.BlockSpec((1, t, Dh), lambda h, i: (h, i, 0))
    return pl.pallas_call(
        body, name=name, grid=(H, S // t), in_specs=[blk, blk],
        out_specs=pl.BlockSpec((1, t, 1), lambda h, i: (h, i, 0)),
        out_shape=jax.ShapeDtypeStruct((H, S, 1), F32), compiler_params=_params("parallel", "parallel"),
    )(do, o)


def fox_bwd(qt, q, k, v, dot, do, lse, delta, *, name, scatter=()):
    H, W, S = qt.shape
    t = _fox_bwd_tile(S)
    n = S // t
    ns = len(scatter)

    def body(*refs):
        own, comm = _split_comm_refs(refs, 8, 3, ns)
        (qt_ref, q_ref, k_ref, v_ref, dot_ref, do_ref, lse_ref, dl_ref, dq_hbm, dk_ref, dv_ref,
         dq_sc, dk_sc, dv_sc, sem) = own
        h, j, i = pl.program_id(0), pl.program_id(1), pl.program_id(2)

        if ns:
            @pl.when((h == 0) & (j == 0) & (i == 0))
            def _():
                scatter_start(comm)

        @pl.when((j == 0) & (i == 0))
        def _():
            dq_sc[...] = jnp.zeros_like(dq_sc)

        @pl.when(i == 0)
        def _():
            dk_sc[...] = jnp.zeros_like(dk_sc)
            dv_sc[...] = jnp.zeros_like(dv_sc)

        def step(masked):
            st = jnp.dot(k_ref[0], qt_ref[0], preferred_element_type=F32)
            pt = jnp.exp(st - lse_ref[0])
            if masked:
                pt = jnp.where(_causal_mask(t, keys_on_rows=True), pt, 0.0)
            dpt = jnp.dot(v_ref[0], dot_ref[0], preferred_element_type=F32)
            dst = (pt * (dpt - dl_ref[0])).astype(BF16)
            dv_sc[...] += jnp.dot(pt.astype(BF16), do_ref[0], preferred_element_type=F32)
            dk_sc[...] += jnp.dot(dst, q_ref[0], preferred_element_type=F32)
            rows = pl.ds(pl.multiple_of(i * t, t), t)
            dq_sc[rows, :] += lax.dot_general(dst, k_ref[0], (((0,), (0,)), ((), ())), preferred_element_type=F32)

        @pl.when(i > j)
        def _():
            step(False)

        @pl.when(i == j)
        def _():
            step(True)

        @pl.when(i == n - 1)
        def _():
            dk_ref[0] = dk_sc[...]
            dv_ref[0] = dv_sc[...]

        @pl.when((j == n - 1) & (i == n - 1))
        def _():
            out = pltpu.make_async_copy(dq_sc, dq_hbm.at[h], sem)
            out.start()
            out.wait()

        if ns:
            @pl.when((h == H - 1) & (j == n - 1) & (i == n - 1))
            def _():
                scatter_finish(comm)

    q_t = pl.BlockSpec((1, W, t), lambda h, j, i: (h, 0, jnp.maximum(i, j)))
    q_r = pl.BlockSpec((1, t, W), lambda h, j, i: (h, jnp.maximum(i, j), 0))
    k_r = pl.BlockSpec((1, t, W), lambda h, j, i: (h, j, 0))
    row = pl.BlockSpec((1, 1, t), lambda h, j, i: (h, 0, jnp.maximum(i, j)))
    outs = pl.pallas_call(
        body, name=name, grid=(H, n, n), in_specs=[q_t, q_r, k_r, k_r, q_t, q_r, row, row] + [ANY_SPEC] * ns,
        out_specs=[ANY_SPEC, k_r, k_r] + [ANY_SPEC] * ns,
        out_shape=[jax.ShapeDtypeStruct((H, S, W), F32)] * 3 + [jax.ShapeDtypeStruct(g.shape, g.dtype) for g in scatter],
        scratch_shapes=[pltpu.VMEM((S, W), F32), pltpu.VMEM((t, W), F32), pltpu.VMEM((t, W), F32),
                        pltpu.SemaphoreType.DMA] + (_comm_scratch(ns) if ns else []),
        compiler_params=_params("arbitrary", "arbitrary", "arbitrary"),
    )(qt, q, k, v, dot, do, lse, delta, *scatter)
    return outs[0], outs[1], outs[2], list(outs[3:])


def merge_fwd(proj, branches, D, *, name):
    S = proj.shape[0]
    ts = _pick(S, (256, 128))

    def body(g0, g1, g2, b0, b1, b2, o_ref):
        acc = (_sig(g0[...]) * b0[...].astype(F32) + _sig(g1[...]) * b1[...].astype(F32)
               + _sig(g2[...]) * b2[...].astype(F32))
        o_ref[...] = acc.astype(BF16)

    gate = lambda n: pl.BlockSpec((ts, D), lambda i: (i, n))
    row = pl.BlockSpec((ts, D), lambda i: (i, 0))
    return pl.pallas_call(
        body, name=name, grid=(S // ts,), in_specs=[gate(0), gate(1), gate(2), row, row, row], out_specs=row,
        out_shape=jax.ShapeDtypeStruct((S, D), BF16), compiler_params=_params("parallel"),
    )(proj, proj, proj, *branches)


def merge_bwd(proj, branches, dm, D, *, name):
    S = proj.shape[0]
    ts = _pick(S, (256, 128))

    def body(g0, g1, g2, b0, b1, b2, dm_ref, db0, db1, db2, dg0, dg1, dg2):
        dmv = dm_ref[...]
        for g_ref, b_ref, db_ref, dg_ref in ((g0, b0, db0, dg0), (g1, b1, db1, dg1), (g2, b2, db2, dg2)):
            s = _sig(g_ref[...])
            db_ref[...] = (dmv * s).astype(BF16)
            dg_ref[...] = (dmv * b_ref[...].astype(F32) * (s * (1.0 - s))).astype(BF16)

    gate = lambda n: pl.BlockSpec((ts, D), lambda i: (i, n))
    row = pl.BlockSpec((ts, D), lambda i: (i, 0))
    return pl.pallas_call(
        body, name=name, grid=(S // ts,), in_specs=[gate(0), gate(1), gate(2), row, row, row, row],
        out_specs=[row] * 6, out_shape=[jax.ShapeDtypeStruct((S, D), BF16)] * 6,
        compiler_params=_params("parallel"),
    )(proj, proj, proj, *branches, dm)


def xa_fwd(q, k, v, *, name):
    S, D = q.shape
    M = k.shape[0]
    dh = D // XA_HEADS
    scale = dh ** -0.5
    t = _pick(S, (512, 256, 128))

    def body(q_ref, k_ref, v_ref, o_ref):
        for h in range(XA_HEADS):
            cols = slice(h * dh, (h + 1) * dh)
            s = lax.dot_general(q_ref[:, cols], k_ref[:, cols], (((1,), (1,)), ((), ())),
                                preferred_element_type=F32) * scale
            p = jnp.exp(s - jnp.max(s, axis=-1, keepdims=True))
            p = p / jnp.sum(p, axis=-1, keepdims=True)
            o_ref[:, cols] = jnp.dot(p.astype(BF16), v_ref[:, cols], preferred_element_type=F32).astype(BF16)

    row = pl.BlockSpec((t, D), lambda i: (i, 0))
    full = pl.BlockSpec((M, D), lambda i: (0, 0))
    return pl.pallas_call(
        body, name=name, grid=(S // t,), in_specs=[row, full, full], out_specs=row,
        out_shape=jax.ShapeDtypeStruct((S, D), BF16), compiler_params=_params("parallel"),
    )(q, k, v)


def xa_bwd(q, k, v, do, *, name):
    S, D = q.shape
    M = k.shape[0]
    dh = D // XA_HEADS
    scale = dh ** -0.5
    t = _pick(S, (512, 256, 128))

    def body(q_ref, k_ref, v_ref, do_ref, dq_ref, dk_ref, dv_ref):
        i = pl.program_id(0)

        @pl.when(i == 0)
        def _():
            dk_ref[...] = jnp.zeros_like(dk_ref)
            dv_ref[...] = jnp.zeros_like(dv_ref)

        for h in range(XA_HEADS):
            cols = slice(h * dh, (h + 1) * dh)
            qh, kh, vh = q_ref[:, cols], k_ref[:, cols], v_ref[:, cols]
            dob = do_ref[:, cols].astype(BF16)
            s = lax.dot_general(qh, kh, (((1,), (1,)), ((), ())), preferred_element_type=F32) * scale
            p = jnp.exp(s - jnp.max(s, axis=-1, keepdims=True))
            p = p / jnp.sum(p, axis=-1, keepdims=True)
            dp = lax.dot_general(dob, vh, (((1,), (1,)), ((), ())), preferred_element_type=F32)
            ds = (p * (dp - jnp.sum(p * dp, axis=-1, keepdims=True)) * scale).astype(BF16)
            dq_ref[:, cols] = jnp.dot(ds, kh, preferred_element_type=F32).astype(BF16)
            dk_ref[:, cols] += lax.dot_general(ds, qh, (((0,), (0,)), ((), ())), preferred_element_type=F32)
            dv_ref[:, cols] += lax.dot_general(p.astype(BF16), dob, (((0,), (0,)), ((), ())),
                                               preferred_element_type=F32)

    row = pl.BlockSpec((t, D), lambda i: (i, 0))
    full = pl.BlockSpec((M, D), lambda i: (0, 0))
    return pl.pallas_call(
        body, name=name, grid=(S // t,), in_specs=[row, full, full, row], out_specs=[row, full, full],
        out_shape=[jax.ShapeDtypeStruct((S, D), BF16), jax.ShapeDtypeStruct((M, D), F32),
                   jax.ShapeDtypeStruct((M, D), F32)],
        compiler_params=_params("arbitrary"),
    )(q, k, v, do)


def mixer_fwd(x, w, tag, gather=()):
    S, D = x.shape
    h = rms_fwd(x, w["mix_norm"], name=f"{tag}_rms")
    proj = mm(h, w["w_in"], name=f"{tag}_proj")
    y_a = conv_fwd(proj, w["conv_w"], D, name=f"{tag}_conv")
    y_b = sg_fwd(proj, w["sg_norm"], w["sg_w"], w["sg_bt"], D, name=f"{tag}_sg")
    c = fox_cumlog(proj, w["fox_b_f"], D, name=f"{tag}_cumlog")
    qh, kh, vh, qt, vt = fox_pack(proj, c, D, FOX_HEAD_DIM ** -0.5, name=f"{tag}_foxpack")
    ot, lse, gathered = fox_fwd(qt, kh, vt, name=f"{tag}_fox", gather=gather)
    y_c, o = heads_unpack_t(ot, name=f"{tag}_foxout")
    ys = (y_a, y_b, y_c)
    branches = [mm(ys[n], w["w_branch"][n], name=f"{tag}_branch{n}", out_dtype=BF16) for n in range(N_BRANCH)]
    merged = merge_fwd(proj, branches, D, name=f"{tag}_merge")
    y = mm(merged, w["w_out"], name=f"{tag}_out", res=x)
    return y, (x, h, proj, ys, qh, kh, vh, qt, o, lse, branches, merged), gathered


def mixer_bwd(dx, saved, w, tag, scatter=()):
    x, h, proj, ys, qh, kh, vh, qt, o, lse, branches, merged = saved
    S, D = x.shape
    grads = {"w_out": mm_tn(merged, dx, name=f"{tag}_dwout")}
    dmerged = mm(dx, w["w_out_t"], name=f"{tag}_dmerged")
    outs = merge_bwd(proj, branches, dmerged, D, name=f"{tag}_dmerge")
    dbr, dgl = outs[:3], outs[3:]
    grads["w_branch"] = jnp.stack([mm_tn(ys[n], dbr[n], name=f"{tag}_dwbr{n}") for n in range(N_BRANCH)])
    dys = [mm(dbr[n], w["w_branch_t"][n], name=f"{tag}_dy{n}") for n in range(N_BRANCH)]
    d_ab, d_ac, d_ah, grads["conv_w"] = conv_bwd(proj, w["conv_w"], dys[0], D, name=f"{tag}_dconv")
    d_u, d_v, grads["sg_w"], d_sgb, grads["sg_norm"] = sg_bwd(
        proj, w["sg_norm"], w["sg_w"], w["sg_bt"], dys[1], D, name=f"{tag}_dsg")
    grads["sg_b"] = d_sgb[:, :, 0]
    do, dot = heads_pack(dys[2], name=f"{tag}_dopack")
    delta = fox_delta(do, o, name=f"{tag}_delta")
    dq, dk, dv, scattered = fox_bwd(qt, qh, kh, vh, dot, do, lse, delta.reshape(FOX_HEADS, 1, S),
                                    name=f"{tag}_foxbwd", scatter=scatter)
    dc_rows = fox_dc(dq, dk, name=f"{tag}_dc")
    d_fl, d_bf = fox_dlogit(proj, w["fox_b_f"], dc_rows, D, name=f"{tag}_dflogit")
    grads["fox_b_f"] = d_bf[0, :FOX_HEADS]
    dproj = jnp.concatenate(
        list(dgl) + [d_ab, d_ac, d_ah, d_u, d_v, heads_unpack(dq, FOX_HEAD_DIM ** -0.5, name=f"{tag}_dqout"),
                     heads_unpack(dk, 1.0, name=f"{tag}_dkout"), heads_unpack(dv, 1.0, name=f"{tag}_dvout"), d_fl],
        axis=1)
    grads["w_in"] = mm_tn(h, dproj, name=f"{tag}_dwin")
    dh = mm(dproj, w["w_in_t"], name=f"{tag}_dh")
    dx, grads["mix_norm"] = rms_bwd(x, w["mix_norm"], dh, dx, name=f"{tag}_drms")
    return dx, grads, scattered


def xattn_fwd(x, mem, w, tag):
    h = rms_fwd(x, w["xa_norm"], name=f"{tag}_rms")
    m = rms_fwd(mem, w["mem_norm"], name=f"{tag}_mrms")
    q = mm(h, w["xa_wq"], name=f"{tag}_q", out_dtype=BF16)
    k = mm(m, w["xa_wk"], name=f"{tag}_k", out_dtype=BF16)
    v = mm(m, w["xa_wv"], name=f"{tag}_v", out_dtype=BF16)
    o = xa_fwd(q, k, v, name=f"{tag}_attn")
    y = mm(o, w["xa_wo"], name=f"{tag}_o", res=x)
    return y, (x, h, m, q, k, v, o)


def xattn_bwd(dx, mem, saved, w, tag):
    x, h, m, q, k, v, o = saved
    grads = {"xa_wo": mm_tn(o, dx, name=f"{tag}_dwo")}
    do = mm(dx, w["xa_wo_t"], name=f"{tag}_do")
    dq, dk, dv = xa_bwd(q, k, v, do, name=f"{tag}_dattn")
    grads["xa_wq"] = mm_tn(h, dq, name=f"{tag}_dwq")
    grads["xa_wk"] = mm_tn(m, dk, name=f"{tag}_dwk")
    grads["xa_wv"] = mm_tn(m, dv, name=f"{tag}_dwv")
    dh = mm(dq, w["xa_wq_t"], name=f"{tag}_dh")
    dm = mm(dk, w["xa_wk_t"], name=f"{tag}_dm1")
    dm = mm(dv, w["xa_wv_t"], name=f"{tag}_dm2", res=dm)
    _, grads["mem_norm"] = rms_bwd(mem, w["mem_norm"], dm, None, name=f"{tag}_dmrms")
    dx, grads["xa_norm"] = rms_bwd(x, w["xa_norm"], dh, dx, name=f"{tag}_drms")
    return dx, grads


ANY_SPEC = pl.BlockSpec(memory_space=pl.ANY)


def all_gather(xs, *, name):
    n = len(xs)

    def body(*refs):
        comm = (refs[:n], refs[n:2 * n]) + tuple(refs[2 * n:])
        gather_start(comm)
        gather_finish(comm)

    return pl.pallas_call(
        body, name=name, out_shape=[jax.ShapeDtypeStruct((N_DEV,) + x.shape, x.dtype) for x in xs],
        in_specs=[ANY_SPEC] * n, out_specs=[ANY_SPEC] * n, scratch_shapes=_comm_scratch(n),
    )(*xs)


def all_to_all(gs, *, name):
    n = len(gs)

    def body(*refs):
        comm = (refs[:n], refs[n:2 * n]) + tuple(refs[2 * n:])
        scatter_start(comm)
        scatter_finish(comm)

    return pl.pallas_call(
        body, name=name, out_shape=[jax.ShapeDtypeStruct(g.shape, g.dtype) for g in gs],
        in_specs=[ANY_SPEC] * n, out_specs=[ANY_SPEC] * n, scratch_shapes=_comm_scratch(n),
    )(*gs)


ADAM_BLOCK_ELEMS = 256 * 1024


def reduce_adamw(parts, row0, w, m, v, *, name, wrow0=0, rows=None, prev=None):
    Rw, C = w.shape
    R = Rw if rows is None else rows
    n_parts = parts.shape[0]
    tr = R
    for cand in (512, 256, 128, 64, 32, 16):
        if R % cand == 0 and row0 % cand == 0 and wrow0 % cand == 0 and cand * C <= ADAM_BLOCK_ELEMS:
            tr = cand
            break
    assert row0 % tr == 0 and wrow0 % tr == 0
    assert tr % 16 == 0 or (row0 == 0 and wrow0 == 0 and parts.shape[1] == R == Rw)
    bc1 = 1.0 - ADAM_B1 ** ADAM_STEP
    bc2 = 1.0 - ADAM_B2 ** ADAM_STEP

    def body(p_ref, w_ref, m_ref, v_ref, *rest):
        g_ref, d_ref, nm_ref, nv_ref = rest[-4:]
        g = p_ref[0].astype(F32)
        for d in range(1, n_parts):
            g = g + p_ref[d].astype(F32)
        nm = ADAM_B1 * m_ref[...] + (1.0 - ADAM_B1) * g
        nv = ADAM_B2 * v_ref[...] + (1.0 - ADAM_B2) * (g * g)
        m_hat = nm / bc1
        v_hat = nv / bc2
        g_ref[...] = g
        d_ref[...] = -ADAM_LR * (m_hat / (jnp.sqrt(v_hat) + ADAM_EPS) + ADAM_WD * w_ref[...])
        nm_ref[...] = nm
        nv_ref[...] = nv

    blk0, wblk0 = row0 // tr, wrow0 // tr
    row = pl.BlockSpec((tr, C), lambda i: (wblk0 + i, 0))
    carried = () if prev is None else tuple(prev)
    return pl.pallas_call(
        body, name=name, grid=(R // tr,),
        in_specs=[pl.BlockSpec((n_parts, tr, C), lambda i: (0, blk0 + i, 0)), row, row, row]
        + [ANY_SPEC] * len(carried),
        out_specs=[row] * 4, out_shape=[jax.ShapeDtypeStruct((Rw, C), F32)] * 4,
        input_output_aliases={4 + k: k for k in range(len(carried))},
        compiler_params=_params("parallel"),
    )(parts, w, m, v, *carried)


SHARDED = {
    "ffn1_w_gate": 2, "ffn1_w_up": 2, "ffn1_w_down": 1, "w_in": 2, "conv_w": 2, "w_branch": 3, "w_out": 1,
    "xa_wq": 1, "xa_wk": 1, "xa_wv": 1, "xa_wo": 1, "ffn2_w_gate": 2, "ffn2_w_up": 2, "ffn2_w_down": 1,
}
GROUPS = (("ffn1_w_gate", "ffn1_w_up", "ffn2_w_gate", "ffn2_w_up"),
          ("ffn1_w_down", "ffn2_w_down", "w_out", "xa_wq", "xa_wk", "xa_wv", "xa_wo"),
          ("w_in",), ("w_branch",))
REPLICATED = ("ffn1_norm", "mix_norm", "sg_norm", "sg_w", "sg_b", "fox_b_f", "xa_norm", "mem_norm", "ffn2_norm",
              "final_norm")
WEIGHTS = ("ffn1_norm", "ffn1_w_gate", "ffn1_w_up", "ffn1_w_down", "mix_norm", "w_in", "conv_w", "sg_norm", "sg_w",
           "sg_b", "fox_b_f", "w_branch", "w_out", "xa_norm", "mem_norm", "xa_wq", "xa_wk", "xa_wv", "xa_wo",
           "ffn2_norm", "ffn2_w_gate", "ffn2_w_up", "ffn2_w_down", "final_norm")
PACK_ROWS = 1024


def _rows(a):
    return a.reshape(-1, a.shape[-1])


def _pack(arrays, dtype):
    flat = jnp.concatenate([a.reshape(-1).astype(dtype) for a in arrays])
    n = flat.shape[0]
    unit = PACK_ROWS * LANES
    total = -(-n // unit) * unit
    return jnp.pad(flat, (0, total - n)).reshape(total // LANES, LANES)


def _unpack(buf, shapes):
    flat = buf.reshape(-1)
    out, off = [], 0
    for shp in shapes:
        n = 1
        for s in shp:
            n *= s
        out.append(flat[off:off + n].reshape(tuple(shp)))
        off += n
    return out


def _to_dev_major(full, axis):
    shp = full.shape
    a = full.reshape(shp[:axis] + (N_DEV, shp[axis] // N_DEV) + shp[axis + 1:])
    return jnp.moveaxis(a, axis, 0)


def _from_dev_major(a, axis):
    a = jnp.moveaxis(a, 0, axis)
    shp = a.shape
    return a.reshape(shp[:axis] + (shp[axis] * shp[axis + 1],) + shp[axis + 2:])


def _relayout_w_in(w_in, D):
    main = 8 * SEG
    pad = jnp.zeros((w_in.shape[0], SEG - FOX_HEADS), w_in.dtype)
    return jnp.concatenate([w_in[:, main + FOX_HEADS:], w_in[:, :main], w_in[:, main:main + FOX_HEADS], pad], axis=1)


def _unlayout_w_in(g, D):
    return jnp.concatenate([g[:, 3 * D:3 * D + 8 * SEG], g[:, 3 * D + 8 * SEG:3 * D + 8 * SEG + FOX_HEADS],
                            g[:, :3 * D]], axis=1)


def _layer_weights(full, rep, l, D):
    t = lambda a: a.T
    w_in = _relayout_w_in(full["w_in"], D)
    ffn = {}
    for tag in ("ffn1", "ffn2"):
        ffn[tag] = {"norm": rep[f"{tag}_norm"][l][None, :]}
        for n in ("w_gate", "w_up", "w_down"):
            ffn[tag][n] = full[f"{tag}_{n}"]
            ffn[tag][n + "_t"] = t(full[f"{tag}_{n}"])
    mix = {
        "mix_norm": rep["mix_norm"][l][None, :], "w_in": w_in, "w_in_t": t(w_in),
        "conv_w": full["conv_w"], "sg_norm": rep["sg_norm"][l][None, :], "sg_w": rep["sg_w"][l],
        "sg_bt": rep["sg_b"][l][:, :, None],
        "fox_b_f": jnp.pad(rep["fox_b_f"][l], (0, LANES - FOX_HEADS))[None, :],
        "w_branch": full["w_branch"], "w_branch_t": jnp.swapaxes(full["w_branch"], 1, 2),
        "w_out": full["w_out"], "w_out_t": t(full["w_out"]),
    }
    xa = {"xa_norm": rep["xa_norm"][l][None, :], "mem_norm": rep["mem_norm"][l][None, :]}
    for n in ("xa_wq", "xa_wk", "xa_wv", "xa_wo"):
        xa[n] = full[n]
        xa[n + "_t"] = t(full[n])
    return ffn, mix, xa


def kernel(x, mem, ffn1_norm, ffn1_w_gate, ffn1_w_up, ffn1_w_down, mix_norm, w_in, conv_w, sg_norm, sg_w, sg_b, fox_b_f, w_branch, w_out, xa_norm, mem_norm, xa_wq, xa_wk, xa_wv, xa_wo, ffn2_norm, ffn2_w_gate, ffn2_w_up, ffn2_w_down, final_norm, loss_target, m_ffn1_norm, m_ffn1_w_gate, m_ffn1_w_up, m_ffn1_w_down, m_mix_norm, m_w_in, m_conv_w, m_sg_norm, m_sg_w, m_sg_b, m_fox_b_f, m_w_branch, m_w_out, m_xa_norm, m_mem_norm, m_xa_wq, m_xa_wk, m_xa_wv, m_xa_wo, m_ffn2_norm, m_ffn2_w_gate, m_ffn2_w_up, m_ffn2_w_down, m_final_norm, v_ffn1_norm, v_ffn1_w_gate, v_ffn1_w_up, v_ffn1_w_down, v_mix_norm, v_w_in, v_conv_w, v_sg_norm, v_sg_w, v_sg_b, v_fox_b_f, v_w_branch, v_w_out, v_xa_norm, v_mem_norm, v_xa_wq, v_xa_wk, v_xa_wv, v_xa_wo, v_ffn2_norm, v_ffn2_w_gate, v_ffn2_w_up, v_ffn2_w_down, v_final_norm):
    args = locals()
    wts = {n: args[n] for n in WEIGHTS}
    mom = {n: args["m_" + n] for n in WEIGHTS}
    var = {n: args["v_" + n] for n in WEIGHTS}
    depth = ffn1_norm.shape[0]
    S, D = x.shape[1], x.shape[2]
    xs, ms, tgt = x[0], mem[0], loss_target[0]

    layer_rows = {n: _rows(wts[n][0]).shape[0] for n in SHARDED}
    row0 = {}
    for grp in GROUPS:
        off = 0
        for n in grp:
            row0[n] = off
            off += layer_rows[n]

    def weight_groups(l):
        return [jnp.concatenate([_rows(wts[n][l]).astype(BF16) for n in grp]) for grp in GROUPS]

    def gathered_layer(got, l):
        full = {"conv_w": conv_full[l]}
        for grp, arr in zip(GROUPS, got):
            for n in grp:
                block = arr[:, row0[n]:row0[n] + layer_rows[n]].reshape((N_DEV,) + wts[n].shape[1:])
                full[n] = _from_dev_major(block, SHARDED[n] - 1)
        return full

    def grad_groups(g):
        def dev_major_rows(n):
            a = _to_dev_major(g[n], SHARDED[n] - 1)
            return a.reshape(N_DEV, -1, a.shape[-1]).astype(BF16)
        return [jnp.concatenate([dev_major_rows(n) for n in grp], axis=1) for grp in GROUPS]

    got = all_gather(weight_groups(0) + [_rows(conv_w)], name="gather_weights")
    conv_full = _from_dev_major(got[-1].reshape((N_DEV,) + conv_w.shape), SHARDED["conv_w"])
    full = gathered_layer(got[:-1], 0)
    rep = {n: wts[n] for n in REPLICATED}

    saved, layers = [], []
    h = xs
    for l in range(depth):
        ffn, mix, xa = _layer_weights(full, rep, l, D)
        layers.append((ffn, mix, xa))
        h, s1 = ffn_fwd(h, ffn["ffn1"], f"l{l}_ffn1")
        h, s2, got = mixer_fwd(h, mix, f"l{l}_mix", gather=weight_groups(l + 1) if l + 1 < depth else ())
        if l + 1 < depth:
            full = gathered_layer(got, l + 1)
        h, s3 = xattn_fwd(h, ms, xa, f"l{l}_xa")
        h, s4 = ffn_fwd(h, ffn["ffn2"], f"l{l}_ffn2")
        saved.append((s1, s2, s3, s4))
    dx, d_final, loss_cols = final_loss_bwd(h, final_norm[None, :], tgt, name="final_loss")
    loss = lax.psum(0.5 * jnp.sum(loss_cols) / D, ("x", "y", "c"))

    per_layer, parts, pending = [], {}, ()
    for l in reversed(range(depth)):
        ffn, mix, xa = layers[l]
        s1, s2, s3, s4 = saved[l]
        g = {}
        dx, g4 = ffn_bwd(dx, s4, ffn["ffn2"], f"l{l}_ffn2")
        dx, g3 = xattn_bwd(dx, ms, s3, xa, f"l{l}_xa")
        dx, g2, got = mixer_bwd(dx, s2, mix, f"l{l}_mix", scatter=pending)
        if pending:
            parts[l + 1] = got
        dx, g1 = ffn_bwd(dx, s1, ffn["ffn1"], f"l{l}_ffn1")
        for tag, gg in (("ffn1", g1), ("ffn2", g4)):
            for n in ("w_gate", "w_up", "w_down"):
                g[f"{tag}_{n}"] = gg[n]
            g[f"{tag}_norm"] = gg["norm"][0]
        g.update(g3)
        g["xa_norm"], g["mem_norm"] = g3["xa_norm"][0], g3["mem_norm"][0]
        g.update({k: v for k, v in g2.items() if k != "w_in"})
        g["w_in"] = _unlayout_w_in(g2["w_in"], D)
        g["mix_norm"], g["sg_norm"] = g2["mix_norm"][0], g2["sg_norm"][0]
        per_layer.append(g)
        pending = grad_groups(g)
    parts[0] = all_to_all(pending, name="scatter_grads")
    per_layer.reverse()
    grads = {n: jnp.stack([per_layer[l][n] for l in range(depth)]) for n in REPLICATED + ("conv_w",)
             if n != "final_norm"}
    grads["final_norm"] = d_final[0]

    res = {k: {} for k in ("g", "d", "m", "v")}
    for gi, grp in enumerate(GROUPS):
        for n in grp:
            outs = None
            for l in range(depth):
                outs = reduce_adamw(parts[l][gi], row0[n], _rows(wts[n]), _rows(mom[n]), _rows(var[n]),
                                    name=f"adamw_{n}_l{l}", wrow0=l * layer_rows[n], rows=layer_rows[n], prev=outs)
            for k, o in zip(("g", "d", "m", "v"), outs):
                res[k][n] = o.reshape(wts[n].shape)

    small = list(REPLICATED)
    shapes = [wts[n].shape for n in small]
    conv_zero = jnp.zeros(grads["conv_w"].shape, F32)
    parts = all_gather([_pack([grads[n] for n in small] + [grads["conv_w"]], F32)], name="gather_small_grads")[0]
    outs = reduce_adamw(parts, 0, _pack([wts[n] for n in small] + [conv_zero], F32),
                        _pack([mom[n] for n in small] + [conv_zero], F32),
                        _pack([var[n] for n in small] + [conv_zero], F32), name="adamw_replicated")
    for k, o in zip(("g", "d", "m", "v"), outs):
        res[k].update(dict(zip(small, _unpack(o, shapes + [conv_zero.shape])[:-1])))
    conv_g = _unpack(outs[0], shapes + [conv_zero.shape])[-1]
    me = 4 * lax.axis_index("x") + 2 * lax.axis_index("y") + lax.axis_index("c")
    width = conv_w.shape[-1]
    conv_g = lax.dynamic_slice_in_dim(conv_g, me * width, width, axis=2)
    outs = reduce_adamw(_rows(conv_g)[None], 0, _rows(conv_w), _rows(mom["conv_w"]), _rows(var["conv_w"]),
                        name="adamw_conv_w")
    for k, o in zip(("g", "d", "m", "v"), outs):
        res[k]["conv_w"] = o.reshape(conv_w.shape)

    return (loss, dx[None], *[res["g"][n] for n in WEIGHTS], *[res["d"][n] for n in WEIGHTS],
            *[res["m"][n] for n in WEIGHTS], *[res["v"][n] for n in WEIGHTS])
```

```python
import functools

import jax
import jax.numpy as jnp
from jax import lax
from jax.experimental import pallas as pl
from jax.experimental.pallas import tpu as pltpu

F32 = jnp.float32
BF16 = jnp.bfloat16

N_DEV = 8
RMS_EPS = 1e-6
SEG = 512
FOX_HEADS = 8
FOX_HEAD_DIM = 64
SG_GROUPS = 4
CHUNK = 128
XA_HEADS = 4
N_BRANCH = 3
LANES = 128
VMEM_LIMIT_BYTES = 48 * 1024 * 1024
NEG_BIG = -1e30

ADAM_LR = 0.001
ADAM_B1 = 0.9
ADAM_B2 = 0.999
ADAM_EPS = 1e-08
ADAM_WD = 0.01
ADAM_STEP = 10

_GELU_K = 0.7978845608028654
_GELU_C = 0.044715

MESH_IDS = pl.DeviceIdType.MESH


def _pick(n, candidates):
    for c in candidates:
        if c <= n and n % c == 0:
            return c
    return n


def _params(*sem):
    return pltpu.CompilerParams(dimension_semantics=sem, vmem_limit_bytes=VMEM_LIMIT_BYTES)


def _sig(x):
    return 1.0 / (1.0 + jnp.exp(-x))


def _gelu(x):
    t = jnp.tanh(_GELU_K * (x + _GELU_C * x * x * x))
    return 0.5 * x * (1.0 + t), t


def _gelu_grad(x, t):
    return 0.5 * (1.0 + t) + 0.5 * x * (1.0 - t * t) * _GELU_K * (1.0 + 3.0 * _GELU_C * x * x)


_WIDE_TILES = (1536, 1408, 1280, 1024, 768, 512, 384, 256, 128)
MM_VMEM_BUDGET = 36 * 1024 * 1024
MM_ACC_BYTES = 6 * 1024 * 1024 + 512 * 1024


SUM_ROWS = 8


def mm(a, b, *, name, out_dtype=F32, res=None, scale=1.0, extras=(), row_extras=(), epilogue=None, out_dtypes=None,
       n_sums=0, tm=1024):
    M, K = a.shape
    K2, N = b.shape
    assert K == K2
    custom = epilogue is not None
    if not custom:
        extras = () if res is None else (res,)
        out_dtypes = (out_dtype,)

        def epilogue(acc, *ex):
            if scale != 1.0:
                acc = acc * scale
            return ((ex[0] + acc) if ex else acc,)

    n_ex, n_rx, n_out = len(extras), len(row_extras), len(out_dtypes)
    tn = _pick(N, _WIDE_TILES)
    tk = K if K <= 3072 else _pick(K, (2560, 2048, 1536, 1024, 512, 256, 128))
    nk = K // tk
    tile_bytes = sum(e.dtype.itemsize for e in extras) + sum(jnp.dtype(d).itemsize for d in out_dtypes)
    for tm in (tm, 512, 256, 128):
        blocks = 2 * (tm * tk * a.dtype.itemsize + tk * tn * 2 + tm * tn * tile_bytes)
        if M % tm == 0 and blocks + (tm * tn * 4 if nk > 1 else 0) <= MM_VMEM_BUDGET:
            break
    else:
        tm = M

    def body(*refs):
        a_ref, b_ref = refs[:2]
        ex_refs = refs[2:2 + n_ex + n_rx]
        o_refs = refs[2 + n_ex + n_rx:2 + n_ex + n_rx + n_out]
        s_refs = refs[2 + n_ex + n_rx + n_out:2 + n_ex + n_rx + n_out + n_sums]

        def finish(acc):
            vals = epilogue(acc, *[r[...] for r in ex_refs])
            assert len(vals) == n_out + n_sums
            for o_ref, val, dt in zip(o_refs, vals[:n_out], out_dtypes, strict=True):
                o_ref[...] = val.astype(dt)
            for s_ref, val in zip(s_refs, vals[n_out:], strict=True):
                s_ref[...] = jnp.broadcast_to(val, (SUM_ROWS, tn))

        part = jnp.dot(a_ref[...].astype(BF16), b_ref[...].astype(BF16), preferred_element_type=F32)
        if nk == 1:
            finish(part)
        else:
            acc_ref = refs[-1]
            k = pl.program_id(2)

            @pl.when(k == 0)
            def _():
                acc_ref[...] = part

            @pl.when(k > 0)
            def _():
                acc_ref[...] += part

            @pl.when(k == nk - 1)
            def _():
                finish(acc_ref[...])

    tile = pl.BlockSpec((tm, tn), lambda i, j, k: (i, j))
    outs = pl.pallas_call(
        body, name=name, grid=(M // tm, N // tn, nk),
        in_specs=[pl.BlockSpec((tm, tk), lambda i, j, k: (i, k)), pl.BlockSpec((tk, tn), lambda i, j, k: (k, j))]
        + [tile] * n_ex + [pl.BlockSpec((1, tn), lambda i, j, k: (0, j))] * n_rx,
        out_specs=[tile] * n_out + [pl.BlockSpec((SUM_ROWS, tn), lambda i, j, k: (i, j))] * n_sums,
        out_shape=[jax.ShapeDtypeStruct((M, N), d) for d in out_dtypes]
        + [jax.ShapeDtypeStruct((M // tm * SUM_ROWS, N), F32)] * n_sums,
        scratch_shapes=[pltpu.VMEM((tm, tn), F32)] if nk > 1 else [],
        compiler_params=_params("parallel", "parallel", "arbitrary"),
    )(a, b, *extras, *row_extras)
    return tuple(outs) if custom else outs[0]


def _sum_tiles(s):
    return jnp.sum(s.reshape(-1, SUM_ROWS, s.shape[-1])[:, 0], axis=0, keepdims=True)


def _rms_grad(dh, *ex):
    *add, x, dx_in, g = ex
    for extra in add:
        dh = dh + extra
    r = lax.rsqrt(jnp.mean(x * x, axis=-1, keepdims=True) + RMS_EPS)
    xh = x * r
    gd = dh * g
    dx = dx_in + r * (gd - xh * jnp.mean(gd * xh, axis=-1, keepdims=True))
    return dx, jnp.sum(dh * xh, axis=0, keepdims=True)


def mm_tn(a, b, *, name, scale=1.0):
    M, K = a.shape
    M2, N = b.shape
    assert M == M2
    tm = _pick(M, (1024, 512, 256, 128))
    tk = _pick(K, (1408, 1024, 512, 256, 128))
    tn = next((c for c in _WIDE_TILES if N % c == 0 and tk * c * 4 <= MM_ACC_BYTES), N)
    nm = M // tm

    def body(a_ref, b_ref, o_ref):
        m = pl.program_id(2)
        part = lax.dot_general(a_ref[...].astype(BF16), b_ref[...].astype(BF16), (((0,), (0,)), ((), ())),
                               preferred_element_type=F32)

        @pl.when(m == 0)
        def _():
            o_ref[...] = part

        @pl.when(m > 0)
        def _():
            o_ref[...] += part

        if scale != 1.0:
            @pl.when(m == nm - 1)
            def _():
                o_ref[...] = o_ref[...] * scale

    return pl.pallas_call(
        body, name=name, grid=(K // tk, N // tn, nm),
        in_specs=[pl.BlockSpec((tm, tk), lambda i, j, m: (m, i)), pl.BlockSpec((tm, tn), lambda i, j, m: (m, j))],
        out_specs=pl.BlockSpec((tk, tn), lambda i, j, m: (i, j)),
        out_shape=jax.ShapeDtypeStruct((K, N), F32),
        compiler_params=_params("parallel", "parallel", "arbitrary"),
    )(a, b)


def rms_fwd(x, g, *, name):
    S, D = x.shape
    ts = _pick(S, (512, 256, 128))

    def body(x_ref, g_ref, h_ref):
        xv = x_ref[...]
        r = lax.rsqrt(jnp.mean(xv * xv, axis=-1, keepdims=True) + RMS_EPS)
        h_ref[...] = ((xv * r) * g_ref[...]).astype(BF16)

    return pl.pallas_call(
        body, name=name, grid=(S // ts,),
        in_specs=[pl.BlockSpec((ts, D), lambda i: (i, 0)), pl.BlockSpec((1, D), lambda i: (0, 0))],
        out_specs=pl.BlockSpec((ts, D), lambda i: (i, 0)),
        out_shape=jax.ShapeDtypeStruct((S, D), BF16),
        compiler_params=_params("parallel"),
    )(x, g)


def rms_bwd(x, g, dh, dx_in, *, name):
    S, D = x.shape
    ts = _pick(S, (512, 256, 128))
    has_in = dx_in is not None

    def body(*refs):
        if has_in:
            x_ref, g_ref, dh_ref, di_ref, dx_ref, dg_ref = refs
        else:
            x_ref, g_ref, dh_ref, dx_ref, dg_ref = refs
        xv = x_ref[...]
        dh_v = dh_ref[...]
        r = lax.rsqrt(jnp.mean(xv * xv, axis=-1, keepdims=True) + RMS_EPS)
        xh = xv * r
        gd = dh_v * g_ref[...]
        dx = r * (gd - xh * jnp.mean(gd * xh, axis=-1, keepdims=True))
        if has_in:
            dx = di_ref[...] + dx
        dx_ref[...] = dx
        part = jnp.sum(dh_v * xh, axis=0, keepdims=True)

        @pl.when(pl.program_id(0) == 0)
        def _():
            dg_ref[...] = part

        @pl.when(pl.program_id(0) > 0)
        def _():
            dg_ref[...] += part

    row = pl.BlockSpec((ts, D), lambda i: (i, 0))
    vec = pl.BlockSpec((1, D), lambda i: (0, 0))
    return pl.pallas_call(
        body, name=name, grid=(S // ts,),
        in_specs=[row, vec, row] + ([row] if has_in else []),
        out_specs=[row, vec],
        out_shape=[jax.ShapeDtypeStruct((S, D), F32), jax.ShapeDtypeStruct((1, D), F32)],
        compiler_params=_params("arbitrary"),
    )(*([x, g, dh] + ([dx_in] if has_in else [])))


def final_loss_bwd(x, g, target, *, name):
    S, D = x.shape
    ts = _pick(S, (512, 256, 128))

    def body(x_ref, g_ref, t_ref, dx_ref, dg_ref, ls_ref):
        xv = x_ref[...]
        gv = g_ref[...]
        r = lax.rsqrt(jnp.mean(xv * xv, axis=-1, keepdims=True) + RMS_EPS)
        xh = xv * r
        e = xh * gv - t_ref[...]
        dy = e * (1.0 / D)
        gd = dy * gv
        dx_ref[...] = r * (gd - xh * jnp.mean(gd * xh, axis=-1, keepdims=True))
        dg_part = jnp.sum(dy * xh, axis=0, keepdims=True)
        ls_part = jnp.sum(e * e, axis=0, keepdims=True)

        @pl.when(pl.program_id(0) == 0)
        def _():
            dg_ref[...] = dg_part
            ls_ref[...] = ls_part

        @pl.when(pl.program_id(0) > 0)
        def _():
            dg_ref[...] += dg_part
            ls_ref[...] += ls_part

    row = pl.BlockSpec((ts, D), lambda i: (i, 0))
    vec = pl.BlockSpec((1, D), lambda i: (0, 0))
    return pl.pallas_call(
        body, name=name, grid=(S // ts,), in_specs=[row, vec, row], out_specs=[row, vec, vec],
        out_shape=[jax.ShapeDtypeStruct((S, D), F32), jax.ShapeDtypeStruct((1, D), F32),
                   jax.ShapeDtypeStruct((1, D), F32)],
        compiler_params=_params("arbitrary"),
    )(x, g, target)


def _swiglu(up, gp):
    gp = gp.astype(F32)
    return up, gp * _sig(gp) * up


def _swiglu_grad(da, gp, up):
    da = da * 0.5
    gp, up = gp.astype(F32), up.astype(F32)
    s = _sig(gp)
    return da * up * (s * (1.0 + gp * (1.0 - s))), da * (gp * s)


def ffn_fwd(x, w, tag):
    h = rms_fwd(x, w["norm"], name=f"{tag}_rms")
    gp = mm(h, w["w_gate"], name=f"{tag}_gate", out_dtype=BF16)
    up, a = mm(h, w["w_up"], name=f"{tag}_up", extras=(gp,), epilogue=_swiglu, out_dtypes=(BF16, BF16))
    y = mm(a, w["w_down"], name=f"{tag}_down", res=x, scale=0.5)
    return y, (x, h, gp, up, a)


def ffn_bwd(dx, saved, w, tag):
    x, h, gp, up, a = saved
    grads = {"w_down": mm_tn(a, dx, name=f"{tag}_dwd", scale=0.5)}
    dgp, dup = mm(dx, w["w_down_t"], name=f"{tag}_da", extras=(gp, up), epilogue=_swiglu_grad,
                  out_dtypes=(BF16, BF16))
    grads["w_gate"] = mm_tn(h, dgp, name=f"{tag}_dwg")
    grads["w_up"] = mm_tn(h, dup, name=f"{tag}_dwu")
    dh = mm(dgp, w["w_gate_t"], name=f"{tag}_dh1")
    dx, dg = mm(dup, w["w_up_t"], name=f"{tag}_dh2", extras=(dh, x, dx), row_extras=(w["norm"],),
                epilogue=_rms_grad, out_dtypes=(F32,), n_sums=1)
    grads["norm"] = _sum_tiles(dg)
    return dx, grads


SEG_AB, SEG_AC, SEG_AH, SEG_U, SEG_V, SEG_FQ, SEG_FK, SEG_FV, SEG_FL = range(9)
N_SEG = 9


def _seg_block(D, seg):
    return 3 * D // SEG + seg


def _shift_down(z, prev8, n, rows):
    out = pltpu.roll(z, n, 0)
    for r in range(n):
        out = jnp.where(rows == r, prev8[8 - n + r:8 - n + r + 1, :], out)
    return out


def _shift_up(z, next8, n, rows, ts):
    out = pltpu.roll(z, ts - n, 0)
    for r in range(n):
        out = jnp.where(rows == ts - n + r, next8[r:r + 1, :], out)
    return out


def conv_fwd(proj, conv_w, D, *, name):
    S = proj.shape[0]
    ts = _pick(S, (512, 256, 128))
    b0 = _seg_block(D, 0)

    def body(ab_ref, ac_ref, ah_ref, pc_ref, ph_ref, w_ref, y_ref):
        i = pl.program_id(0)
        rows = lax.broadcasted_iota(jnp.int32, (ts, 1), 0)
        z = ac_ref[...] * ah_ref[...]
        zp = pc_ref[...] * ph_ref[...] * (i > 0).astype(F32)
        w = w_ref[...]
        y = w[0:1, :] * _shift_down(z, zp, 2, rows) + w[1:2, :] * _shift_down(z, zp, 1, rows) + w[2:3, :] * z
        y_ref[...] = (ab_ref[...] * y).astype(BF16)

    def seg(s):
        return pl.BlockSpec((ts, SEG), lambda i: (i, b0 + s))

    def prev(s):
        return pl.BlockSpec((8, SEG), lambda i: (jnp.maximum(i * (ts // 8) - 1, 0), b0 + s))

    return pl.pallas_call(
        body, name=name, grid=(S // ts,),
        in_specs=[seg(SEG_AB), seg(SEG_AC), seg(SEG_AH), prev(SEG_AC), prev(SEG_AH),
                  pl.BlockSpec((3, SEG), lambda i: (0, 0))],
        out_specs=pl.BlockSpec((ts, SEG), lambda i: (i, 0)),
        out_shape=jax.ShapeDtypeStruct((S, SEG), BF16), compiler_params=_params("parallel"),
    )(proj, proj, proj, proj, proj, conv_w)


def conv_bwd(proj, conv_w, dy, D, *, name):
    S = proj.shape[0]
    ts = _pick(S, (512, 256, 128))
    nt = S // ts
    b0 = _seg_block(D, 0)

    def body(ab_ref, ac_ref, ah_ref, pc_ref, ph_ref, nb_ref, dy_ref, ndy_ref, w_ref,
             dab_ref, dac_ref, dah_ref, dw_ref):
        i = pl.program_id(0)
        rows = lax.broadcasted_iota(jnp.int32, (ts, 1), 0)
        ab, ac, ah = ab_ref[...], ac_ref[...], ah_ref[...]
        z = ac * ah
        zp = pc_ref[...] * ph_ref[...] * (i > 0).astype(F32)
        w = w_ref[...]
        z1 = _shift_down(z, zp, 1, rows)
        z2 = _shift_down(z, zp, 2, rows)
        y = w[0:1, :] * z2 + w[1:2, :] * z1 + w[2:3, :] * z
        dyv = dy_ref[...]
        dab_ref[...] = (dyv * y).astype(BF16)
        dyy = dyv * ab
        nyy = ndy_ref[...] * nb_ref[...] * (i < nt - 1).astype(F32)
        dz = (w[2:3, :] * dyy + w[1:2, :] * _shift_up(dyy, nyy, 1, rows, ts)
              + w[0:1, :] * _shift_up(dyy, nyy, 2, rows, ts))
        dac_ref[...] = (dz * ah).astype(BF16)
        dah_ref[...] = (dz * ac).astype(BF16)
        parts = [jnp.sum(dyy * zz, axis=0, keepdims=True) for zz in (z2, z1, z)]

        @pl.when(i == 0)
        def _():
            for k in range(3):
                dw_ref[k:k + 1, :] = parts[k]

        @pl.when(i > 0)
        def _():
            for k in range(3):
                dw_ref[k:k + 1, :] += parts[k]

    def seg(s):
        return pl.BlockSpec((ts, SEG), lambda i: (i, b0 + s))

    def prev(s):
        return pl.BlockSpec((8, SEG), lambda i: (jnp.maximum(i * (ts // 8) - 1, 0), b0 + s))

    nxt_row = lambda i: jnp.minimum((i + 1) * (ts // 8), S // 8 - 1)
    out_row = pl.BlockSpec((ts, SEG), lambda i: (i, 0))
    return pl.pallas_call(
        body, name=name, grid=(nt,),
        in_specs=[seg(SEG_AB), seg(SEG_AC), seg(SEG_AH), prev(SEG_AC), prev(SEG_AH),
                  pl.BlockSpec((8, SEG), lambda i: (nxt_row(i), b0 + SEG_AB)),
                  out_row, pl.BlockSpec((8, SEG), lambda i: (nxt_row(i), 0)),
                  pl.BlockSpec((3, SEG), lambda i: (0, 0))],
        out_specs=[out_row, out_row, out_row, pl.BlockSpec((3, SEG), lambda i: (0, 0))],
        out_shape=[jax.ShapeDtypeStruct((S, SEG), BF16)] * 3 + [jax.ShapeDtypeStruct((3, SEG), F32)],
        compiler_params=_params("arbitrary"),
    )(proj, proj, proj, proj, proj, proj, dy, dy, conv_w)


def _tril_mask():
    r = lax.broadcasted_iota(jnp.int32, (CHUNK, CHUNK), 0)
    c = lax.broadcasted_iota(jnp.int32, (CHUNK, CHUNK), 1)
    return c <= r


def sg_fwd(proj, sg_norm, sg_w, sg_bt, D, *, name):
    S = proj.shape[0]
    ts = _pick(S, (512, 256, 128))
    b0 = _seg_block(D, 0)

    def body(u_ref, v_ref, gs_ref, w_ref, b_ref, y_ref):
        ug, _ = _gelu(u_ref[...])
        vg, _ = _gelu(v_ref[...])
        vn = ((vg * lax.rsqrt(jnp.mean(vg * vg, axis=-1, keepdims=True) + RMS_EPS)) * gs_ref[...]).astype(BF16)
        mask = _tril_mask()
        for g in range(SG_GROUPS):
            wg = jnp.where(mask, w_ref[g], 0.0).astype(BF16)
            cols = slice(g * CHUNK, (g + 1) * CHUNK)
            for n in range(ts // CHUNK):
                rws = slice(n * CHUNK, (n + 1) * CHUNK)
                sv = jnp.dot(wg, vn[rws, cols], preferred_element_type=F32) + b_ref[g]
                y_ref[rws, cols] = (ug[rws, cols] * sv).astype(BF16)

    seg = lambda s: pl.BlockSpec((ts, SEG), lambda i: (i, b0 + s))
    return pl.pallas_call(
        body, name=name, grid=(S // ts,),
        in_specs=[seg(SEG_U), seg(SEG_V), pl.BlockSpec((1, SEG), lambda i: (0, 0)),
                  pl.BlockSpec((SG_GROUPS, CHUNK, CHUNK), lambda i: (0, 0, 0)),
                  pl.BlockSpec((SG_GROUPS, CHUNK, 1), lambda i: (0, 0, 0))],
        out_specs=pl.BlockSpec((ts, SEG), lambda i: (i, 0)),
        out_shape=jax.ShapeDtypeStruct((S, SEG), BF16), compiler_params=_params("parallel"),
    )(proj, proj, sg_norm, sg_w, sg_bt)


def sg_bwd(proj, sg_norm, sg_w, sg_bt, dy, D, *, name):
    S = proj.shape[0]
    ts = _pick(S, (512, 256, 128))
    nt = S // ts
    b0 = _seg_block(D, 0)

    def body(u_ref, v_ref, dy_ref, gs_ref, w_ref, b_ref, du_ref, dv_ref, dw_ref, db_ref, dgs_ref, dvn_sc):
        i = pl.program_id(0)
        uv, vv, dyv = u_ref[...], v_ref[...], dy_ref[...]
        ug, ut = _gelu(uv)
        vg, vt = _gelu(vv)
        r = lax.rsqrt(jnp.mean(vg * vg, axis=-1, keepdims=True) + RMS_EPS)
        vh = vg * r
        gs = gs_ref[...]
        vn = (vh * gs).astype(BF16)
        dsv = dyv * ug
        dsv_b = dsv.astype(BF16)
        mask = _tril_mask()

        @pl.when(i == 0)
        def _():
            dw_ref[...] = jnp.zeros_like(dw_ref)
            db_ref[...] = jnp.zeros_like(db_ref)

        for g in range(SG_GROUPS):
            wg = jnp.where(mask, w_ref[g], 0.0).astype(BF16)
            cols = slice(g * CHUNK, (g + 1) * CHUNK)
            dw_acc = jnp.zeros((CHUNK, CHUNK), F32)
            db_acc = jnp.zeros((CHUNK, 1), F32)
            for n in range(ts // CHUNK):
                rws = slice(n * CHUNK, (n + 1) * CHUNK)
                vblk = vn[rws, cols]
                sv = jnp.dot(wg, vblk, preferred_element_type=F32) + b_ref[g]
                du_ref[rws, cols] = (dyv[rws, cols] * sv * _gelu_grad(uv[rws, cols], ut[rws, cols])).astype(BF16)
                dblk = dsv_b[rws, cols]
                dvn_sc[rws, cols] = lax.dot_general(wg, dblk, (((0,), (0,)), ((), ())), preferred_element_type=F32)
                dw_acc = dw_acc + lax.dot_general(dblk, vblk, (((1,), (1,)), ((), ())), preferred_element_type=F32)
                db_acc = db_acc + jnp.sum(dsv[rws, cols], axis=1, keepdims=True)
            dw_ref[g] += jnp.where(mask, dw_acc, 0.0)
            db_ref[g] += db_acc

        dvn = dvn_sc[...]
        gd = dvn * gs
        dvg = r * (gd - vh * jnp.mean(gd * vh, axis=-1, keepdims=True))
        dv_ref[...] = (dvg * _gelu_grad(vv, vt)).astype(BF16)
        dgs_part = jnp.sum(dvn * vh, axis=0, keepdims=True)

        @pl.when(i == 0)
        def _():
            dgs_ref[...] = dgs_part

        @pl.when(i > 0)
        def _():
            dgs_ref[...] += dgs_part

    seg = lambda s: pl.BlockSpec((ts, SEG), lambda i: (i, b0 + s))
    row = pl.BlockSpec((ts, SEG), lambda i: (i, 0))
    wspec = pl.BlockSpec((SG_GROUPS, CHUNK, CHUNK), lambda i: (0, 0, 0))
    bspec = pl.BlockSpec((SG_GROUPS, CHUNK, 1), lambda i: (0, 0, 0))
    vec = pl.BlockSpec((1, SEG), lambda i: (0, 0))
    return pl.pallas_call(
        body, name=name, grid=(nt,),
        in_specs=[seg(SEG_U), seg(SEG_V), row, vec, wspec, bspec],
        out_specs=[row, row, wspec, bspec, vec],
        out_shape=[jax.ShapeDtypeStruct((S, SEG), BF16)] * 2
        + [jax.ShapeDtypeStruct((SG_GROUPS, CHUNK, CHUNK), F32), jax.ShapeDtypeStruct((SG_GROUPS, CHUNK, 1), F32),
           jax.ShapeDtypeStruct((1, SEG), F32)],
        scratch_shapes=[pltpu.VMEM((ts, SEG), F32)],
        compiler_params=_params("arbitrary"),
    )(proj, proj, dy, sg_norm, sg_w, sg_bt)


def _mesh_pos():
    return lax.axis_index("x"), lax.axis_index("y"), lax.axis_index("c")


def _flip(v, bit):
    return 1 - v if bit else v


def _comm_scratch(n):
    return [pltpu.SemaphoreType.DMA((n, 7)), pltpu.SemaphoreType.DMA((n, 7)), pltpu.SemaphoreType.DMA((n,))]


def _gather_plan(refs):
    x_refs, out_refs, send_sems, recv_sems, local_sems = refs
    n = len(x_refs)
    mx, my, mc = _mesh_pos()
    me, sibling = (mx, my, mc), (mx, my, 1 - mc)
    chips = [(1 - mx, my), (mx, 1 - my), (1 - mx, 1 - my)]

    def copy(a, k, block, to, from_input=False):
        slot = out_refs[a].at[4 * block[0] + 2 * block[1] + block[2]]
        return pltpu.make_async_remote_copy(
            src_ref=x_refs[a] if from_input else slot, dst_ref=slot,
            send_sem=send_sems.at[a, k], recv_sem=recv_sems.at[a, k], device_id=to, device_id_type=MESH_IDS)

    local = [pltpu.make_async_copy(x_refs[a], out_refs[a].at[4 * mx + 2 * my + mc], local_sems.at[a])
             for a in range(n)]
    first = []
    for a in range(n):
        first.append(copy(a, 0, me, sibling, from_input=True))
        first += [copy(a, 1 + j, me, (*chip, mc), from_input=True) for j, chip in enumerate(chips)]
    return n, me, sibling, chips, mc, copy, local, first


def gather_start(refs):
    n, me, sibling, chips, mc, copy, local, first = _gather_plan(refs)
    for cp in local + first:
        cp.start()


def gather_finish(refs):
    n, me, sibling, chips, mc, copy, local, first = _gather_plan(refs)
    onward = []
    for j, chip in enumerate(chips):
        for a in range(n):
            copy(a, 1 + j, (*chip, mc), me).wait_recv()
            cp = copy(a, 4 + j, (*chip, mc), sibling)
            cp.start()
            onward.append(cp)
    for a in range(n):
        copy(a, 0, sibling, me).wait_recv()
        for j, chip in enumerate(chips):
            copy(a, 4 + j, (*chip, 1 - mc), me).wait_recv()
    for cp in first + onward:
        cp.wait_send()
    for cp in local:
        cp.wait()


def _scatter_plan(refs):
    g_refs, out_refs, send_sems, recv_sems, local_sems = refs
    n = len(g_refs)
    mx, my, mc = _mesh_pos()
    me = 4 * mx + 2 * my + mc
    local = [pltpu.make_async_copy(g_refs[a].at[me], out_refs[a].at[me], local_sems.at[a]) for a in range(n)]
    sends, arrivals = [], []
    for k in range(1, N_DEV):
        peer = (_flip(mx, k & 4), _flip(my, k & 2), _flip(mc, k & 1))
        peer_slot = 4 * peer[0] + 2 * peer[1] + peer[2]
        for a in range(n):
            sems = dict(send_sem=send_sems.at[a, k - 1], recv_sem=recv_sems.at[a, k - 1], device_id=peer,
                        device_id_type=MESH_IDS)
            sends.append(pltpu.make_async_remote_copy(src_ref=g_refs[a].at[peer_slot], dst_ref=out_refs[a].at[me],
                                                      **sems))
            arrivals.append(pltpu.make_async_remote_copy(src_ref=g_refs[a].at[peer_slot],
                                                         dst_ref=out_refs[a].at[peer_slot], **sems))
    return local, sends, arrivals


def scatter_start(refs):
    local, sends, arrivals = _scatter_plan(refs)
    for cp in local + sends:
        cp.start()


def scatter_finish(refs):
    local, sends, arrivals = _scatter_plan(refs)
    for cp in arrivals:
        cp.wait_recv()
    for cp in sends:
        cp.wait_send()
    for cp in local:
        cp.wait()


def _split_comm_refs(refs, n_in, n_out, n_side):
    ins, side_in = refs[:n_in], refs[n_in:n_in + n_side]
    outs = refs[n_in + n_side:n_in + n_side + n_out]
    side_out = refs[n_in + n_side + n_out:n_in + 2 * n_side + n_out]
    rest = refs[n_in + 2 * n_side + n_out:]
    if n_side == 0:
        return ins + outs + rest, None
    return ins + outs + rest[:-3], (side_in, side_out) + tuple(rest[-3:])


def _log_sigmoid(x):
    return jnp.minimum(x, 0.0) - jnp.log(1.0 + jnp.exp(-jnp.abs(x)))


def fox_cumlog(proj, b_f, D, *, name):
    S = proj.shape[0]
    ts = _pick(S, (512, 256, 128))
    blk = (3 * D + SEG_FL * SEG) // LANES

    def body(f_ref, b_ref, c_ref, carry):
        i = pl.program_id(0)

        @pl.when(i == 0)
        def _():
            carry[...] = jnp.zeros_like(carry)

        rows = lax.broadcasted_iota(jnp.int32, (ts, 1), 0)
        acc = _log_sigmoid(f_ref[...] + b_ref[...])
        d = 1
        while d < ts:
            acc = acc + jnp.where(rows >= d, pltpu.roll(acc, d, 0), 0.0)
            d *= 2
        acc = acc + carry[...]
        c_ref[...] = acc
        carry[...] = acc[ts - 1:ts, :]

    return pl.pallas_call(
        body, name=name, grid=(S // ts,),
        in_specs=[pl.BlockSpec((ts, LANES), lambda i: (i, blk)), pl.BlockSpec((1, LANES), lambda i: (0, 0))],
        out_specs=pl.BlockSpec((ts, LANES), lambda i: (i, 0)),
        out_shape=jax.ShapeDtypeStruct((S, LANES), F32),
        scratch_shapes=[pltpu.VMEM((1, LANES), F32)],
        compiler_params=_params("arbitrary"),
    )(proj, b_f)


def fox_dlogit(proj, b_f, dc, D, *, name):
    S = proj.shape[0]
    ts = _pick(S, (512, 256, 128))
    nt = S // ts
    blk = (3 * D + SEG_FL * SEG) // LANES

    def body(f_ref, b_ref, dc_ref, df_ref, db_ref, carry):
        i = pl.program_id(0)

        @pl.when(i == 0)
        def _():
            carry[...] = jnp.zeros_like(carry)

        rows = lax.broadcasted_iota(jnp.int32, (ts, 1), 0)
        acc = dc_ref[...]
        d = 1
        while d < ts:
            acc = acc + jnp.where(rows < ts - d, pltpu.roll(acc, ts - d, 0), 0.0)
            d *= 2
        acc = acc + carry[...]
        carry[...] = acc[0:1, :]
        df = acc * _sig(-(f_ref[...] + b_ref[...]))
        df_ref[...] = jnp.zeros_like(df_ref)
        df_ref[:, 0:LANES] = df.astype(BF16)
        part = jnp.sum(df, axis=0, keepdims=True)

        @pl.when(i == 0)
        def _():
            db_ref[...] = part

        @pl.when(i > 0)
        def _():
            db_ref[...] += part

    rev = lambda i: nt - 1 - i
    return pl.pallas_call(
        body, name=name, grid=(nt,),
        in_specs=[pl.BlockSpec((ts, LANES), lambda i: (rev(i), blk)), pl.BlockSpec((1, LANES), lambda i: (0, 0)),
                  pl.BlockSpec((ts, LANES), lambda i: (rev(i), 0))],
        out_specs=[pl.BlockSpec((ts, SEG), lambda i: (rev(i), 0)), pl.BlockSpec((1, LANES), lambda i: (0, 0))],
        out_shape=[jax.ShapeDtypeStruct((S, SEG), BF16), jax.ShapeDtypeStruct((1, LANES), F32)],
        scratch_shapes=[pltpu.VMEM((1, LANES), F32)],
        compiler_params=_params("arbitrary"),
    )(proj, b_f, dc)


def _causal_mask(t, keys_on_rows=True):
    r = lax.broadcasted_iota(jnp.int32, (t, t), 0)
    c = lax.broadcasted_iota(jnp.int32, (t, t), 1)
    return (r <= c) if keys_on_rows else (c <= r)


FOX_PAD = LANES
COL_C = FOX_HEAD_DIM
COL_ROWSUM = FOX_HEAD_DIM + 3
COL_L = FOX_HEAD_DIM


def _head_pair(ref, a):
    x = ref[...]
    return x if a == 0 else pltpu.roll(x, FOX_HEAD_DIM, 1)


def fox_pack(proj, c, D, scale, *, name):
    S = proj.shape[0]
    ts = _pick(S, (512, 256, 128))
    base = 3 * D // LANES
    per_seg = SEG // LANES

    def body(q_ref, k_ref, v_ref, c_ref, qo, ko, vo, qto, vto):
        hp = pl.program_id(1)
        lane = lax.broadcasted_iota(jnp.int32, (ts, LANES), 1)
        is_val = lane < FOX_HEAD_DIM
        cv = c_ref[...]
        for a in range(2):
            ch = jnp.sum(jnp.where(lane == 2 * hp + a, cv, 0.0), axis=1, keepdims=True)
            c_hi = ch.astype(BF16).astype(F32)
            r1 = ch - c_hi
            c_mid = r1.astype(BF16).astype(F32)
            c_lo = r1 - c_mid
            qa = jnp.where(is_val, _head_pair(q_ref, a) * scale, jnp.where(lane < COL_C + 3, 1.0, 0.0))
            extra = jnp.where(lane == COL_C, -c_hi, jnp.where(lane == COL_C + 1, -c_mid, jnp.where(
                lane == COL_C + 2, -c_lo, jnp.where(lane == COL_ROWSUM, 1.0, 0.0))))
            va = jnp.where(is_val, _head_pair(v_ref, a), jnp.where(lane == COL_L, 1.0, 0.0))
            qo[a] = qa.astype(BF16)
            ko[a] = jnp.where(is_val, _head_pair(k_ref, a), extra).astype(BF16)
            vo[a] = va.astype(BF16)
            qto[a] = qa.T.astype(BF16)
            vto[a] = va.T.astype(BF16)

    seg = lambda s: pl.BlockSpec((ts, LANES), lambda i, hp: (i, base + s * per_seg + hp))
    out = pl.BlockSpec((2, ts, FOX_PAD), lambda i, hp: (hp, i, 0))
    out_t = pl.BlockSpec((2, FOX_PAD, ts), lambda i, hp: (hp, 0, i))
    return pl.pallas_call(
        body, name=name, grid=(S // ts, FOX_HEADS // 2),
        in_specs=[seg(SEG_FQ), seg(SEG_FK), seg(SEG_FV), pl.BlockSpec((ts, LANES), lambda i, hp: (i, 0))],
        out_specs=[out] * 3 + [out_t] * 2,
        out_shape=[jax.ShapeDtypeStruct((FOX_HEADS, S, FOX_PAD), BF16)] * 3
        + [jax.ShapeDtypeStruct((FOX_HEADS, FOX_PAD, S), BF16)] * 2,
        compiler_params=_params("parallel", "parallel"),
    )(proj, proj, proj, c)


def heads_pack(x, *, name):
    S = x.shape[0]
    ts = _pick(S, (512, 256, 128))

    def body(x_ref, o_ref, ot_ref):
        lane = lax.broadcasted_iota(jnp.int32, (ts, LANES), 1)
        for a in range(2):
            xa = jnp.where(lane < FOX_HEAD_DIM, _head_pair(x_ref, a), 0.0)
            o_ref[a] = xa.astype(BF16)
            ot_ref[a] = xa.T.astype(BF16)

    return pl.pallas_call(
        body, name=name, grid=(S // ts, FOX_HEADS // 2),
        in_specs=[pl.BlockSpec((ts, LANES), lambda i, hp: (i, hp))],
        out_specs=[pl.BlockSpec((2, ts, FOX_PAD), lambda i, hp: (hp, i, 0)),
                   pl.BlockSpec((2, FOX_PAD, ts), lambda i, hp: (hp, 0, i))],
        out_shape=[jax.ShapeDtypeStruct((FOX_HEADS, S, FOX_PAD), BF16),
                   jax.ShapeDtypeStruct((FOX_HEADS, FOX_PAD, S), BF16)],
        compiler_params=_params("parallel", "parallel"),
    )(x)


def heads_unpack(x, scale, *, name):
    S = x.shape[1]
    ts = _pick(S, (512, 256, 128))

    def body(x_ref, o_ref):
        lane = lax.broadcasted_iota(jnp.int32, (ts, LANES), 1)
        both = jnp.where(lane < FOX_HEAD_DIM, x_ref[0], pltpu.roll(x_ref[1], FOX_HEAD_DIM, 1))
        o_ref[...] = (both * scale).astype(BF16)

    return pl.pallas_call(
        body, name=name, grid=(S // ts, FOX_HEADS // 2),
        in_specs=[pl.BlockSpec((2, ts, FOX_PAD), lambda i, hp: (hp, i, 0))],
        out_specs=pl.BlockSpec((ts, LANES), lambda i, hp: (i, hp)),
        out_shape=jax.ShapeDtypeStruct((S, FOX_HEADS * FOX_HEAD_DIM), BF16),
        compiler_params=_params("parallel", "parallel"),
    )(x)


FOX_FWD_GROUP = 1


def _fox_fwd_tile(S):
    return min(2048, max(128, S // 4))


def _fox_bwd_tile(S):
    return min(1024, max(128, S // 4))


def fox_fwd(qt, k, vt, *, name, gather=()):
    H, W, S = qt.shape
    t = _fox_fwd_tile(S)
    n = S // t
    G = FOX_FWD_GROUP
    ns = len(gather)

    def body(*refs):
        (qt_ref, k_ref, vt_ref, ot_ref, lse_ref, m_sc, acc_sc), comm = _split_comm_refs(refs, 3, 2, ns)
        h, i, j = pl.program_id(0), pl.program_id(1), pl.program_id(2)

        if ns:
            @pl.when((h == 0) & (i == 0) & (j == 0))
            def _():
                gather_start(comm)

        @pl.when(j == 0)
        def _():
            m_sc[...] = jnp.full_like(m_sc, NEG_BIG)
            acc_sc[...] = jnp.zeros_like(acc_sc)

        def step(masked):
            for g in range(G):
                st = jnp.dot(k_ref[g], qt_ref[g], preferred_element_type=F32)
                if masked:
                    st = jnp.where(_causal_mask(t, keys_on_rows=True), st, NEG_BIG)
                m_prev = m_sc[g]
                m_new = jnp.maximum(m_prev, jnp.max(st, axis=0, keepdims=True))
                pt = jnp.exp(st - m_new)
                acc_sc[g] = jnp.exp(m_prev - m_new) * acc_sc[g] + jnp.dot(vt_ref[g], pt.astype(BF16),
                                                                          preferred_element_type=F32)
                m_sc[g] = m_new

        @pl.when(j < i)
        def _():
            step(False)

        @pl.when(j == i)
        def _():
            step(True)
            for g in range(G):
                acc = acc_sc[g]
                l = acc[COL_L:COL_L + 1, :]
                ot_ref[g] = acc / l
                lse_ref[g] = m_sc[g] + jnp.log(l)

        if ns:
            @pl.when((h == H // G - 1) & (i == n - 1) & (j == n - 1))
            def _():
                gather_finish(comm)

    qs = pl.BlockSpec((G, W, t), lambda h, i, j: (h, 0, i))
    ks = pl.BlockSpec((G, t, W), lambda h, i, j: (h, jnp.minimum(j, i), 0))
    vs = pl.BlockSpec((G, W, t), lambda h, i, j: (h, 0, jnp.minimum(j, i)))
    row = pl.BlockSpec((G, 1, t), lambda h, i, j: (h, 0, i))
    outs = pl.pallas_call(
        body, name=name, grid=(H // G, n, n), in_specs=[qs, ks, vs] + [ANY_SPEC] * ns,
        out_specs=[qs, row] + [ANY_SPEC] * ns,
        out_shape=[jax.ShapeDtypeStruct((H, W, S), F32), jax.ShapeDtypeStruct((H, 1, S), F32)]
        + [jax.ShapeDtypeStruct((N_DEV,) + x.shape, x.dtype) for x in gather],
        scratch_shapes=[pltpu.VMEM((G, 1, t), F32), pltpu.VMEM((G, W, t), F32)] + (_comm_scratch(ns) if ns else []),
        compiler_params=_params("arbitrary", "arbitrary", "arbitrary") if ns
        else _params("parallel", "parallel", "arbitrary"),
    )(qt, k, vt, *gather)
    return outs[0], outs[1], list(outs[2:])


def fox_dc(dq, dk, *, name):
    H, S, W = dq.shape
    ts = _pick(S, (512, 256, 128))

    def body(dq_ref, dk_ref, o_ref):
        lane = lax.broadcasted_iota(jnp.int32, (ts, LANES), 1)
        acc = jnp.zeros((ts, LANES), F32)
        for h in range(H):
            d = dq_ref[h][:, COL_ROWSUM:COL_ROWSUM + 1] - dk_ref[h][:, COL_C:COL_C + 1]
            acc = jnp.where(lane == h, d, acc)
        o_ref[...] = acc

    blk = pl.BlockSpec((H, ts, W), lambda i: (0, i, 0))
    return pl.pallas_call(
        body, name=name, grid=(S // ts,), in_specs=[blk, blk],
        out_specs=pl.BlockSpec((ts, LANES), lambda i: (i, 0)),
        out_shape=jax.ShapeDtypeStruct((S, LANES), F32), compiler_params=_params("parallel"),
    )(dq, dk)


def heads_unpack_t(xt, *, name):
    S = xt.shape[2]
    ts = _pick(S, (512, 256, 128))

    def body(x_ref, y_ref, o_ref):
        lane = lax.broadcasted_iota(jnp.int32, (ts, LANES), 1)
        x0, x1 = x_ref[0].T, x_ref[1].T
        o_ref[0] = x0
        o_ref[1] = x1
        y_ref[...] = jnp.where(lane < FOX_HEAD_DIM, x0, pltpu.roll(x1, FOX_HEAD_DIM, 1)).astype(BF16)

    return pl.pallas_call(
        body, name=name, grid=(S // ts, FOX_HEADS // 2),
        in_specs=[pl.BlockSpec((2, FOX_PAD, ts), lambda i, hp: (hp, 0, i))],
        out_specs=[pl.BlockSpec((ts, LANES), lambda i, hp: (i, hp)),
                   pl.BlockSpec((2, ts, FOX_PAD), lambda i, hp: (hp, i, 0))],
        out_shape=[jax.ShapeDtypeStruct((S, FOX_HEADS * FOX_HEAD_DIM), BF16),
                   jax.ShapeDtypeStruct((FOX_HEADS, S, FOX_PAD), F32)],
        compiler_params=_params("parallel", "parallel"),
    )(xt)


def fox_delta(do, o, *, name):
    H, S, Dh = o.shape
    t = _pick(S, (2048, 1024, 512, 256, 128))

    def body(do_ref, o_ref, d_ref):
        d_ref[0] = jnp.sum(do_ref[0] * o_ref[0], axis=-1, keepdims=True)

    blk = pl.BlockSpec((1, t, Dh), lambda h, i: (h, i, 0))
    return pl.pallas_call(
        body, name=name, grid=(H, S // t), in_specs=[blk, blk],
        out_specs=pl.BlockSpec((1, t, 1), lambda h, i: (h, i, 0)),
        out_shape=jax.ShapeDtypeStruct((H, S, 1), F32), compiler_params=_params("parallel", "parallel"),
    )(do, o)


def fox_bwd(qt, q, k, v, dot, do, lse, delta, *, name, scatter=()):
    H, W, S = qt.shape
    t = _fox_bwd_tile(S)
    n = S // t
    ns = len(scatter)

    def body(*refs):
        own, comm = _split_comm_refs(refs, 8, 3, ns)
        (qt_ref, q_ref, k_ref, v_ref, dot_ref, do_ref, lse_ref, dl_ref, dq_hbm, dk_ref, dv_ref,
         dq_sc, dk_sc, dv_sc, sem) = own
        h, j, i = pl.program_id(0), pl.program_id(1), pl.program_id(2)

        if ns:
            @pl.when((h == 0) & (j == 0) & (i == 0))
            def _():
                scatter_start(comm)

        @pl.when((j == 0) & (i == 0))
        def _():
            dq_sc[...] = jnp.zeros_like(dq_sc)

        @pl.when(i == 0)
        def _():
            dk_sc[...] = jnp.zeros_like(dk_sc)
            dv_sc[...] = jnp.zeros_like(dv_sc)

        def step(masked):
            st = jnp.dot(k_ref[0], qt_ref[0], preferred_element_type=F32)
            pt = jnp.exp(st - lse_ref[0])
            if masked:
                pt = jnp.where(_causal_mask(t, keys_on_rows=True), pt, 0.0)
            dpt = jnp.dot(v_ref[0], dot_ref[0], preferred_element_type=F32)
            dst = (pt * (dpt - dl_ref[0])).astype(BF16)
            dv_sc[...] += jnp.dot(pt.astype(BF16), do_ref[0], preferred_element_type=F32)
            dk_sc[...] += jnp.dot(dst, q_ref[0], preferred_element_type=F32)
            rows = pl.ds(pl.multiple_of(i * t, t), t)
            dq_sc[rows, :] += lax.dot_general(dst, k_ref[0], (((0,), (0,)), ((), ())), preferred_element_type=F32)

        @pl.when(i > j)
        def _():
            step(False)

        @pl.when(i == j)
        def _():
            step(True)

        @pl.when(i == n - 1)
        def _():
            dk_ref[0] = dk_sc[...]
            dv_ref[0] = dv_sc[...]

        @pl.when((j == n - 1) & (i == n - 1))
        def _():
            out = pltpu.make_async_copy(dq_sc, dq_hbm.at[h], sem)
            out.start()
            out.wait()

        if ns:
            @pl.when((h == H - 1) & (j == n - 1) & (i == n - 1))
            def _():
                scatter_finish(comm)

    q_t = pl.BlockSpec((1, W, t), lambda h, j, i: (h, 0, jnp.maximum(i, j)))
    q_r = pl.BlockSpec((1, t, W), lambda h, j, i: (h, jnp.maximum(i, j), 0))
    k_r = pl.BlockSpec((1, t, W), lambda h, j, i: (h, j, 0))
    row = pl.BlockSpec((1, 1, t), lambda h, j, i: (h, 0, jnp.maximum(i, j)))
    outs = pl.pallas_call(
        body, name=name, grid=(H, n, n), in_specs=[q_t, q_r, k_r, k_r, q_t, q_r, row, row] + [ANY_SPEC] * ns,
        out_specs=[ANY_SPEC, k_r, k_r] + [ANY_SPEC] * ns,
        out_shape=[jax.ShapeDtypeStruct((H, S, W), F32)] * 3 + [jax.ShapeDtypeStruct(g.shape, g.dtype) for g in scatter],
        scratch_shapes=[pltpu.VMEM((S, W), F32), pltpu.VMEM((t, W), F32), pltpu.VMEM((t, W), F32),
                        pltpu.SemaphoreType.DMA] + (_comm_scratch(ns) if ns else []),
        compiler_params=_params("arbitrary", "arbitrary", "arbitrary"),
    )(qt, q, k, v, dot, do, lse, delta, *scatter)
    return outs[0], outs[1], outs[2], list(outs[3:])


def merge_fwd(proj, branches, D, *, name):
    S = proj.shape[0]
    ts = _pick(S, (256, 128))

    def body(g0, g1, g2, b0, b1, b2, o_ref):
        acc = (_sig(g0[...]) * b0[...].astype(F32) + _sig(g1[...]) * b1[...].astype(F32)
               + _sig(g2[...]) * b2[...].astype(F32))
        o_ref[...] = acc.astype(BF16)

    gate = lambda n: pl.BlockSpec((ts, D), lambda i: (i, n))
    row = pl.BlockSpec((ts, D), lambda i: (i, 0))
    return pl.pallas_call(
        body, name=name, grid=(S // ts,), in_specs=[gate(0), gate(1), gate(2), row, row, row], out_specs=row,
        out_shape=jax.ShapeDtypeStruct((S, D), BF16), compiler_params=_params("parallel"),
    )(proj, proj, proj, *branches)


def merge_bwd(proj, branches, dm, D, *, name):
    S = proj.shape[0]
    ts = _pick(S, (256, 128))

    def body(g0, g1, g2, b0, b1, b2, dm_ref, db0, db1, db2, dg0, dg1, dg2):
        dmv = dm_ref[...]
        for g_ref, b_ref, db_ref, dg_ref in ((g0, b0, db0, dg0), (g1, b1, db1, dg1), (g2, b2, db2, dg2)):
            s = _sig(g_ref[...])
            db_ref[...] = (dmv * s).astype(BF16)
            dg_ref[...] = (dmv * b_ref[...].astype(F32) * (s * (1.0 - s))).astype(BF16)

    gate = lambda n: pl.BlockSpec((ts, D), lambda i: (i, n))
    row = pl.BlockSpec((ts, D), lambda i: (i, 0))
    return pl.pallas_call(
        body, name=name, grid=(S // ts,), in_specs=[gate(0), gate(1), gate(2), row, row, row, row],
        out_specs=[row] * 6, out_shape=[jax.ShapeDtypeStruct((S, D), BF16)] * 6,
        compiler_params=_params("parallel"),
    )(proj, proj, proj, *branches, dm)


def xa_fwd(q, k, v, *, name):
    S, D = q.shape
    M = k.shape[0]
    dh = D // XA_HEADS
    scale = dh ** -0.5
    t = _pick(S, (512, 256, 128))

    def body(q_ref, k_ref, v_ref, o_ref):
        for h in range(XA_HEADS):
            cols = slice(h * dh, (h + 1) * dh)
            s = lax.dot_general(q_ref[:, cols], k_ref[:, cols], (((1,), (1,)), ((), ())),
                                preferred_element_type=F32) * scale
            p = jnp.exp(s - jnp.max(s, axis=-1, keepdims=True))
            p = p / jnp.sum(p, axis=-1, keepdims=True)
            o_ref[:, cols] = jnp.dot(p.astype(BF16), v_ref[:, cols], preferred_element_type=F32).astype(BF16)

    row = pl.BlockSpec((t, D), lambda i: (i, 0))
    full = pl.BlockSpec((M, D), lambda i: (0, 0))
    return pl.pallas_call(
        body, name=name, grid=(S // t,), in_specs=[row, full, full], out_specs=row,
        out_shape=jax.ShapeDtypeStruct((S, D), BF16), compiler_params=_params("parallel"),
    )(q, k, v)


def xa_bwd(q, k, v, do, *, name):
    S, D = q.shape
    M = k.shape[0]
    dh = D // XA_HEADS
    scale = dh ** -0.5
    t = _pick(S, (512, 256, 128))

    def body(q_ref, k_ref, v_ref, do_ref, dq_ref, dk_ref, dv_ref):
        i = pl.program_id(0)

        @pl.when(i == 0)
        def _():
            dk_ref[...] = jnp.zeros_like(dk_ref)
            dv_ref[...] = jnp.zeros_like(dv_ref)

        for h in range(XA_HEADS):
            cols = slice(h * dh, (h + 1) * dh)
            qh, kh, vh = q_ref[:, cols], k_ref[:, cols], v_ref[:, cols]
            dob = do_ref[:, cols].astype(BF16)
            s = lax.dot_general(qh, kh, (((1,), (1,)), ((), ())), preferred_element_type=F32) * scale
            p = jnp.exp(s - jnp.max(s, axis=-1, keepdims=True))
            p = p / jnp.sum(p, axis=-1, keepdims=True)
            dp = lax.dot_general(dob, vh, (((1,), (1,)), ((), ())), preferred_element_type=F32)
            ds = (p * (dp - jnp.sum(p * dp, axis=-1, keepdims=True)) * scale).astype(BF16)
            dq_ref[:, cols] = jnp.dot(ds, kh, preferred_element_type=F32).astype(BF16)
            dk_ref[:, cols] += lax.dot_general(ds, qh, (((0,), (0,)), ((), ())), preferred_element_type=F32)
            dv_ref[:, cols] += lax.dot_general(p.astype(BF16), dob, (((0,), (0,)), ((), ())),
                                               preferred_element_type=F32)

    row = pl.BlockSpec((t, D), lambda i: (i, 0))
    full = pl.BlockSpec((M, D), lambda i: (0, 0))
    return pl.pallas_call(
        body, name=name, grid=(S // t,), in_specs=[row, full, full, row], out_specs=[row, full, full],
        out_shape=[jax.ShapeDtypeStruct((S, D), BF16), jax.ShapeDtypeStruct((M, D), F32),
                   jax.ShapeDtypeStruct((M, D), F32)],
        compiler_params=_params("arbitrary"),
    )(q, k, v, do)


def mixer_fwd(x, w, tag, gather=()):
    S, D = x.shape
    h = rms_fwd(x, w["mix_norm"], name=f"{tag}_rms")
    proj = mm(h, w["w_in"], name=f"{tag}_proj")
    y_a = conv_fwd(proj, w["conv_w"], D, name=f"{tag}_conv")
    y_b = sg_fwd(proj, w["sg_norm"], w["sg_w"], w["sg_bt"], D, name=f"{tag}_sg")
    c = fox_cumlog(proj, w["fox_b_f"], D, name=f"{tag}_cumlog")
    qh, kh, vh, qt, vt = fox_pack(proj, c, D, FOX_HEAD_DIM ** -0.5, name=f"{tag}_foxpack")
    ot, lse, gathered = fox_fwd(qt, kh, vt, name=f"{tag}_fox", gather=gather)
    y_c, o = heads_unpack_t(ot, name=f"{tag}_foxout")
    ys = (y_a, y_b, y_c)
    branches = [mm(ys[n], w["w_branch"][n], name=f"{tag}_branch{n}", out_dtype=BF16) for n in range(N_BRANCH)]
    merged = merge_fwd(proj, branches, D, name=f"{tag}_merge")
    y = mm(merged, w["w_out"], name=f"{tag}_out", res=x)
    return y, (x, h, proj, ys, qh, kh, vh, qt, o, lse, branches, merged), gathered


def mixer_bwd(dx, saved, w, tag, scatter=()):
    x, h, proj, ys, qh, kh, vh, qt, o, lse, branches, merged = saved
    S, D = x.shape
    grads = {"w_out": mm_tn(merged, dx, name=f"{tag}_dwout")}
    dmerged = mm(dx, w["w_out_t"], name=f"{tag}_dmerged")
    outs = merge_bwd(proj, branches, dmerged, D, name=f"{tag}_dmerge")
    dbr, dgl = outs[:3], outs[3:]
    grads["w_branch"] = jnp.stack([mm_tn(ys[n], dbr[n], name=f"{tag}_dwbr{n}") for n in range(N_BRANCH)])
    dys = [mm(dbr[n], w["w_branch_t"][n], name=f"{tag}_dy{n}") for n in range(N_BRANCH)]
    d_ab, d_ac, d_ah, grads["conv_w"] = conv_bwd(proj, w["conv_w"], dys[0], D, name=f"{tag}_dconv")
    d_u, d_v, grads["sg_w"], d_sgb, grads["sg_norm"] = sg_bwd(
        proj, w["sg_norm"], w["sg_w"], w["sg_bt"], dys[1], D, name=f"{tag}_dsg")
    grads["sg_b"] = d_sgb[:, :, 0]
    do, dot = heads_pack(dys[2], name=f"{tag}_dopack")
    delta = fox_delta(do, o, name=f"{tag}_delta")
    dq, dk, dv, scattered = fox_bwd(qt, qh, kh, vh, dot, do, lse, delta.reshape(FOX_HEADS, 1, S),
                                    name=f"{tag}_foxbwd", scatter=scatter)
    dc_rows = fox_dc(dq, dk, name=f"{tag}_dc")
    d_fl, d_bf = fox_dlogit(proj, w["fox_b_f"], dc_rows, D, name=f"{tag}_dflogit")
    grads["fox_b_f"] = d_bf[0, :FOX_HEADS]
    dproj = jnp.concatenate(
        list(dgl) + [d_ab, d_ac, d_ah, d_u, d_v, heads_unpack(dq, FOX_HEAD_DIM ** -0.5, name=f"{tag}_dqout"),
                     heads_unpack(dk, 1.0, name=f"{tag}_dkout"), heads_unpack(dv, 1.0, name=f"{tag}_dvout"), d_fl],
        axis=1)
    grads["w_in"] = mm_tn(h, dproj, name=f"{tag}_dwin")
    dx, dg = mm(dproj, w["w_in_t"], name=f"{tag}_dh", extras=(x, dx), row_extras=(w["mix_norm"],),
                epilogue=_rms_grad, out_dtypes=(F32,), n_sums=1)
    grads["mix_norm"] = _sum_tiles(dg)
    return dx, grads, scattered


def xattn_fwd(x, mem, w, tag):
    h = rms_fwd(x, w["xa_norm"], name=f"{tag}_rms")
    m = rms_fwd(mem, w["mem_norm"], name=f"{tag}_mrms")
    q = mm(h, w["xa_wq"], name=f"{tag}_q", out_dtype=BF16)
    k = mm(m, w["xa_wk"], name=f"{tag}_k", out_dtype=BF16)
    v = mm(m, w["xa_wv"], name=f"{tag}_v", out_dtype=BF16)
    o = xa_fwd(q, k, v, name=f"{tag}_attn")
    y = mm(o, w["xa_wo"], name=f"{tag}_o", res=x)
    return y, (x, h, m, q, k, v, o)


def xattn_bwd(dx, mem, saved, w, tag):
    x, h, m, q, k, v, o = saved
    grads = {"xa_wo": mm_tn(o, dx, name=f"{tag}_dwo")}
    do = mm(dx, w["xa_wo_t"], name=f"{tag}_do")
    dq, dk, dv = xa_bwd(q, k, v, do, name=f"{tag}_dattn")
    grads["xa_wq"] = mm_tn(h, dq, name=f"{tag}_dwq")
    grads["xa_wk"] = mm_tn(m, dk, name=f"{tag}_dwk")
    grads["xa_wv"] = mm_tn(m, dv, name=f"{tag}_dwv")
    dm = mm(dk, w["xa_wk_t"], name=f"{tag}_dm1")
    dm = mm(dv, w["xa_wv_t"], name=f"{tag}_dm2", res=dm)
    _, grads["mem_norm"] = rms_bwd(mem, w["mem_norm"], dm, None, name=f"{tag}_dmrms")
    dx, dg = mm(dq, w["xa_wq_t"], name=f"{tag}_dh", extras=(x, dx), row_extras=(w["xa_norm"],),
                epilogue=_rms_grad, out_dtypes=(F32,), n_sums=1)
    grads["xa_norm"] = _sum_tiles(dg)
    return dx, grads


ANY_SPEC = pl.BlockSpec(memory_space=pl.ANY)


def all_gather(xs, *, name):
    n = len(xs)

    def body(*refs):
        comm = (refs[:n], refs[n:2 * n]) + tuple(refs[2 * n:])
        gather_start(comm)
        gather_finish(comm)

    return pl.pallas_call(
        body, name=name, out_shape=[jax.ShapeDtypeStruct((N_DEV,) + x.shape, x.dtype) for x in xs],
        in_specs=[ANY_SPEC] * n, out_specs=[ANY_SPEC] * n, scratch_shapes=_comm_scratch(n),
    )(*xs)


def all_to_all(gs, *, name):
    n = len(gs)

    def body(*refs):
        comm = (refs[:n], refs[n:2 * n]) + tuple(refs[2 * n:])
        scatter_start(comm)
        scatter_finish(comm)

    return pl.pallas_call(
        body, name=name, out_shape=[jax.ShapeDtypeStruct(g.shape, g.dtype) for g in gs],
        in_specs=[ANY_SPEC] * n, out_specs=[ANY_SPEC] * n, scratch_shapes=_comm_scratch(n),
    )(*gs)


ADAM_BLOCK_ELEMS = 256 * 1024


def reduce_adamw(parts, row0, w, m, v, *, name, wrow0=0, rows=None, prev=None):
    Rw, C = w.shape
    R = Rw if rows is None else rows
    n_parts = parts.shape[0]
    tr = R
    for cand in (512, 256, 128, 64, 32, 16):
        if R % cand == 0 and row0 % cand == 0 and wrow0 % cand == 0 and cand * C <= ADAM_BLOCK_ELEMS:
            tr = cand
            break
    assert row0 % tr == 0 and wrow0 % tr == 0
    assert tr % 16 == 0 or (row0 == 0 and wrow0 == 0 and parts.shape[1] == R == Rw)
    bc1 = 1.0 - ADAM_B1 ** ADAM_STEP
    bc2 = 1.0 - ADAM_B2 ** ADAM_STEP

    def body(p_ref, w_ref, m_ref, v_ref, *rest):
        g_ref, d_ref, nm_ref, nv_ref = rest[-4:]
        g = p_ref[0].astype(F32)
        for d in range(1, n_parts):
            g = g + p_ref[d].astype(F32)
        nm = ADAM_B1 * m_ref[...] + (1.0 - ADAM_B1) * g
        nv = ADAM_B2 * v_ref[...] + (1.0 - ADAM_B2) * (g * g)
        m_hat = nm / bc1
        v_hat = nv / bc2
        g_ref[...] = g
        d_ref[...] = -ADAM_LR * (m_hat / (jnp.sqrt(v_hat) + ADAM_EPS) + ADAM_WD * w_ref[...])
        nm_ref[...] = nm
        nv_ref[...] = nv

    blk0, wblk0 = row0 // tr, wrow0 // tr
    row = pl.BlockSpec((tr, C), lambda i: (wblk0 + i, 0))
    carried = () if prev is None else tuple(prev)
    return pl.pallas_call(
        body, name=name, grid=(R // tr,),
        in_specs=[pl.BlockSpec((n_parts, tr, C), lambda i: (0, blk0 + i, 0)), row, row, row]
        + [ANY_SPEC] * len(carried),
        out_specs=[row] * 4, out_shape=[jax.ShapeDtypeStruct((Rw, C), F32)] * 4,
        input_output_aliases={4 + k: k for k in range(len(carried))},
        compiler_params=_params("parallel"),
    )(parts, w, m, v, *carried)


SHARDED = {
    "ffn1_w_gate": 2, "ffn1_w_up": 2, "ffn1_w_down": 1, "w_in": 2, "conv_w": 2, "w_branch": 3, "w_out": 1,
    "xa_wq": 1, "xa_wk": 1, "xa_wv": 1, "xa_wo": 1, "ffn2_w_gate": 2, "ffn2_w_up": 2, "ffn2_w_down": 1,
}
GROUPS = (("ffn1_w_gate", "ffn1_w_up", "ffn2_w_gate", "ffn2_w_up"),
          ("ffn1_w_down", "ffn2_w_down", "w_out", "xa_wq", "xa_wk", "xa_wv", "xa_wo"),
          ("w_in",), ("w_branch",))
REPLICATED = ("ffn1_norm", "mix_norm", "sg_norm", "sg_w", "sg_b", "fox_b_f", "xa_norm", "mem_norm", "ffn2_norm",
              "final_norm")
WEIGHTS = ("ffn1_norm", "ffn1_w_gate", "ffn1_w_up", "ffn1_w_down", "mix_norm", "w_in", "conv_w", "sg_norm", "sg_w",
           "sg_b", "fox_b_f", "w_branch", "w_out", "xa_norm", "mem_norm", "xa_wq", "xa_wk", "xa_wv", "xa_wo",
           "ffn2_norm", "ffn2_w_gate", "ffn2_w_up", "ffn2_w_down", "final_norm")
PACK_ROWS = 1024


def _rows(a):
    return a.reshape(-1, a.shape[-1])


def _pack(arrays, dtype):
    flat = jnp.concatenate([a.reshape(-1).astype(dtype) for a in arrays])
    n = flat.shape[0]
    unit = PACK_ROWS * LANES
    total = -(-n // unit) * unit
    return jnp.pad(flat, (0, total - n)).reshape(total // LANES, LANES)


def _unpack(buf, shapes):
    flat = buf.reshape(-1)
    out, off = [], 0
    for shp in shapes:
        n = 1
        for s in shp:
            n *= s
        out.append(flat[off:off + n].reshape(tuple(shp)))
        off += n
    return out


def _to_dev_major(full, axis):
    shp = full.shape
    a = full.reshape(shp[:axis] + (N_DEV, shp[axis] // N_DEV) + shp[axis + 1:])
    return jnp.moveaxis(a, axis, 0)


def _from_dev_major(a, axis):
    a = jnp.moveaxis(a, 0, axis)
    shp = a.shape
    return a.reshape(shp[:axis] + (shp[axis] * shp[axis + 1],) + shp[axis + 2:])


def _relayout_w_in(w_in, D):
    main = 8 * SEG
    pad = jnp.zeros((w_in.shape[0], SEG - FOX_HEADS), w_in.dtype)
    return jnp.concatenate([w_in[:, main + FOX_HEADS:], w_in[:, :main], w_in[:, main:main + FOX_HEADS], pad], axis=1)


def _unlayout_w_in(g, D):
    return jnp.concatenate([g[:, 3 * D:3 * D + 8 * SEG], g[:, 3 * D + 8 * SEG:3 * D + 8 * SEG + FOX_HEADS],
                            g[:, :3 * D]], axis=1)


def _layer_weights(full, rep, l, D):
    t = lambda a: a.T
    w_in = _relayout_w_in(full["w_in"], D)
    ffn = {}
    for tag in ("ffn1", "ffn2"):
        ffn[tag] = {"norm": rep[f"{tag}_norm"][l][None, :]}
        for n in ("w_gate", "w_up", "w_down"):
            ffn[tag][n] = full[f"{tag}_{n}"]
            ffn[tag][n + "_t"] = t(full[f"{tag}_{n}"])
    mix = {
        "mix_norm": rep["mix_norm"][l][None, :], "w_in": w_in, "w_in_t": t(w_in),
        "conv_w": full["conv_w"], "sg_norm": rep["sg_norm"][l][None, :], "sg_w": rep["sg_w"][l],
        "sg_bt": rep["sg_b"][l][:, :, None],
        "fox_b_f": jnp.pad(rep["fox_b_f"][l], (0, LANES - FOX_HEADS))[None, :],
        "w_branch": full["w_branch"], "w_branch_t": jnp.swapaxes(full["w_branch"], 1, 2),
        "w_out": full["w_out"], "w_out_t": t(full["w_out"]),
    }
    xa = {"xa_norm": rep["xa_norm"][l][None, :], "mem_norm": rep["mem_norm"][l][None, :]}
    for n in ("xa_wq", "xa_wk", "xa_wv", "xa_wo"):
        xa[n] = full[n]
        xa[n + "_t"] = t(full[n])
    return ffn, mix, xa


def kernel(x, mem, ffn1_norm, ffn1_w_gate, ffn1_w_up, ffn1_w_down, mix_norm, w_in, conv_w, sg_norm, sg_w, sg_b, fox_b_f, w_branch, w_out, xa_norm, mem_norm, xa_wq, xa_wk, xa_wv, xa_wo, ffn2_norm, ffn2_w_gate, ffn2_w_up, ffn2_w_down, final_norm, loss_target, m_ffn1_norm, m_ffn1_w_gate, m_ffn1_w_up, m_ffn1_w_down, m_mix_norm, m_w_in, m_conv_w, m_sg_norm, m_sg_w, m_sg_b, m_fox_b_f, m_w_branch, m_w_out, m_xa_norm, m_mem_norm, m_xa_wq, m_xa_wk, m_xa_wv, m_xa_wo, m_ffn2_norm, m_ffn2_w_gate, m_ffn2_w_up, m_ffn2_w_down, m_final_norm, v_ffn1_norm, v_ffn1_w_gate, v_ffn1_w_up, v_ffn1_w_down, v_mix_norm, v_w_in, v_conv_w, v_sg_norm, v_sg_w, v_sg_b, v_fox_b_f, v_w_branch, v_w_out, v_xa_norm, v_mem_norm, v_xa_wq, v_xa_wk, v_xa_wv, v_xa_wo, v_ffn2_norm, v_ffn2_w_gate, v_ffn2_w_up, v_ffn2_w_down, v_final_norm):
    args = locals()
    wts = {n: args[n] for n in WEIGHTS}
    mom = {n: args["m_" + n] for n in WEIGHTS}
    var = {n: args["v_" + n] for n in WEIGHTS}
    depth = ffn1_norm.shape[0]
    S, D = x.shape[1], x.shape[2]
    xs, ms, tgt = x[0], mem[0], loss_target[0]

    layer_rows = {n: _rows(wts[n][0]).shape[0] for n in SHARDED}
    row0 = {}
    for grp in GROUPS:
        off = 0
        for n in grp:
            row0[n] = off
            off += layer_rows[n]

    def weight_groups(l):
        return [jnp.concatenate([_rows(wts[n][l]).astype(BF16) for n in grp]) for grp in GROUPS]

    def gathered_layer(got, l):
        full = {"conv_w": conv_full[l]}
        for grp, arr in zip(GROUPS, got):
            for n in grp:
                block = arr[:, row0[n]:row0[n] + layer_rows[n]].reshape((N_DEV,) + wts[n].shape[1:])
                full[n] = _from_dev_major(block, SHARDED[n] - 1)
        return full

    def grad_groups(g):
        def dev_major_rows(n):
            a = _to_dev_major(g[n], SHARDED[n] - 1)
            return a.reshape(N_DEV, -1, a.shape[-1]).astype(BF16)
        return [jnp.concatenate([dev_major_rows(n) for n in grp], axis=1) for grp in GROUPS]

    got = all_gather(weight_groups(0) + [_rows(conv_w)], name="gather_weights")
    conv_full = _from_dev_major(got[-1].reshape((N_DEV,) + conv_w.shape), SHARDED["conv_w"])
    full = gathered_layer(got[:-1], 0)
    rep = {n: wts[n] for n in REPLICATED}

    saved, layers = [], []
    h = xs
    for l in range(depth):
        ffn, mix, xa = _layer_weights(full, rep, l, D)
        layers.append((ffn, mix, xa))
        h, s1 = ffn_fwd(h, ffn["ffn1"], f"l{l}_ffn1")
        h, s2, got = mixer_fwd(h, mix, f"l{l}_mix", gather=weight_groups(l + 1) if l + 1 < depth else ())
        if l + 1 < depth:
            full = gathered_layer(got, l + 1)
        h, s3 = xattn_fwd(h, ms, xa, f"l{l}_xa")
        h, s4 = ffn_fwd(h, ffn["ffn2"], f"l{l}_ffn2")
        saved.append((s1, s2, s3, s4))
    dx, d_final, loss_cols = final_loss_bwd(h, final_norm[None, :], tgt, name="final_loss")
    loss = lax.psum(0.5 * jnp.sum(loss_cols) / D, ("x", "y", "c"))

    per_layer, parts, pending = [], {}, ()
    for l in reversed(range(depth)):
        ffn, mix, xa = layers[l]
        s1, s2, s3, s4 = saved[l]
        g = {}
        dx, g4 = ffn_bwd(dx, s4, ffn["ffn2"], f"l{l}_ffn2")
        dx, g3 = xattn_bwd(dx, ms, s3, xa, f"l{l}_xa")
        dx, g2, got = mixer_bwd(dx, s2, mix, f"l{l}_mix", scatter=pending)
        if pending:
            parts[l + 1] = got
        dx, g1 = ffn_bwd(dx, s1, ffn["ffn1"], f"l{l}_ffn1")
        for tag, gg in (("ffn1", g1), ("ffn2", g4)):
            for n in ("w_gate", "w_up", "w_down"):
                g[f"{tag}_{n}"] = gg[n]
            g[f"{tag}_norm"] = gg["norm"][0]
        g.update(g3)
        g["xa_norm"], g["mem_norm"] = g3["xa_norm"][0], g3["mem_norm"][0]
        g.update({k: v for k, v in g2.items() if k != "w_in"})
        g["w_in"] = _unlayout_w_in(g2["w_in"], D)
        g["mix_norm"], g["sg_norm"] = g2["mix_norm"][0], g2["sg_norm"][0]
        per_layer.append(g)
        pending = grad_groups(g)
    parts[0] = all_to_all(pending, name="scatter_grads")
    per_layer.reverse()
    grads = {n: jnp.stack([per_layer[l][n] for l in range(depth)]) for n in REPLICATED + ("conv_w",)
             if n != "final_norm"}
    grads["final_norm"] = d_final[0]

    res = {k: {} for k in ("g", "d", "m", "v")}
    for gi, grp in enumerate(GROUPS):
        for n in grp:
            outs = None
            for l in range(depth):
                outs = reduce_adamw(parts[l][gi], row0[n], _rows(wts[n]), _rows(mom[n]), _rows(var[n]),
                                    name=f"adamw_{n}_l{l}", wrow0=l * layer_rows[n], rows=layer_rows[n], prev=outs)
            for k, o in zip(("g", "d", "m", "v"), outs):
                res[k][n] = o.reshape(wts[n].shape)

    small = list(REPLICATED)
    shapes = [wts[n].shape for n in small]
    conv_zero = jnp.zeros(grads["conv_w"].shape, F32)
    parts = all_gather([_pack([grads[n] for n in small] + [grads["conv_w"]], F32)], name="gather_small_grads")[0]
    outs = reduce_adamw(parts, 0, _pack([wts[n] for n in small] + [conv_zero], F32),
                        _pack([mom[n] for n in small] + [conv_zero], F32),
                        _pack([var[n] for n in small] + [conv_zero], F32), name="adamw_replicated")
    for k, o in zip(("g", "d", "m", "v"), outs):
        res[k].update(dict(zip(small, _unpack(o, shapes + [conv_zero.shape])[:-1])))
    conv_g = _unpack(outs[0], shapes + [conv_zero.shape])[-1]
    me = 4 * lax.axis_index("x") + 2 * lax.axis_index("y") + lax.axis_index("c")
    width = conv_w.shape[-1]
    conv_g = lax.dynamic_slice_in_dim(conv_g, me * width, width, axis=2)
    outs = reduce_adamw(_rows(conv_g)[None], 0, _rows(conv_w), _rows(mom["conv_w"]), _rows(var["conv_w"]),
                        name="adamw_conv_w")
    for k, o in zip(("g", "d", "m", "v"), outs):
        res[k]["conv_w"] = o.reshape(conv_w.shape)

    return (loss, dx[None], *[res["g"][n] for n in WEIGHTS], *[res["d"][n] for n in WEIGHTS],
            *[res["m"][n] for n in WEIGHTS], *[res["v"][n] for n in WEIGHTS])
```

```python
import functools

import jax
import jax.numpy as jnp
from jax import lax
from jax.experimental import pallas as pl
from jax.experimental.pallas import tpu as pltpu

F32 = jnp.float32
BF16 = jnp.bfloat16

N_DEV = 8
RMS_EPS = 1e-6
SEG = 512
FOX_HEADS = 8
FOX_HEAD_DIM = 64
SG_GROUPS = 4
CHUNK = 128
XA_HEADS = 4
N_BRANCH = 3
LANES = 128
VMEM_LIMIT_BYTES = 48 * 1024 * 1024
NEG_BIG = -1e30

ADAM_LR = 0.001
ADAM_B1 = 0.9
ADAM_B2 = 0.999
ADAM_EPS = 1e-08
ADAM_WD = 0.01
ADAM_STEP = 10

_GELU_K = 0.7978845608028654
_GELU_C = 0.044715

MESH_IDS = pl.DeviceIdType.MESH


def _pick(n, candidates):
    for c in candidates:
        if c <= n and n % c == 0:
            return c
    return n


def _params(*sem):
    return pltpu.CompilerParams(dimension_semantics=sem, vmem_limit_bytes=VMEM_LIMIT_BYTES)


def _sig(x):
    return 1.0 / (1.0 + jnp.exp(-x))


def _gelu(x):
    t = jnp.tanh(_GELU_K * (x + _GELU_C * x * x * x))
    return 0.5 * x * (1.0 + t), t


def _gelu_grad(x, t):
    return 0.5 * (1.0 + t) + 0.5 * x * (1.0 - t * t) * _GELU_K * (1.0 + 3.0 * _GELU_C * x * x)


_WIDE_TILES = (1536, 1408, 1280, 1024, 768, 512, 384, 256, 128)
MM_VMEM_BUDGET = 36 * 1024 * 1024
MM_ACC_BYTES = 6 * 1024 * 1024 + 512 * 1024


SUM_ROWS = 8


def mm(a, b, *, name, out_dtype=F32, res=None, scale=1.0, extras=(), row_extras=(), epilogue=None, out_dtypes=None,
       n_sums=0, tm=1024, bt=False):
    M, K = a.shape
    N, K2 = (b.shape[0], b.shape[1]) if bt else (b.shape[1], b.shape[0])
    assert K == K2
    custom = epilogue is not None
    if not custom:
        extras = () if res is None else (res,)
        out_dtypes = (out_dtype,)

        def epilogue(acc, *ex):
            if scale != 1.0:
                acc = acc * scale
            return ((ex[0] + acc) if ex else acc,)

    n_ex, n_rx, n_out = len(extras), len(row_extras), len(out_dtypes)
    tn = _pick(N, _WIDE_TILES)
    tk = K if K <= 3072 else _pick(K, (2560, 2048, 1536, 1024, 512, 256, 128))
    nk = K // tk
    tile_bytes = sum(e.dtype.itemsize for e in extras) + sum(jnp.dtype(d).itemsize for d in out_dtypes)
    for tm in (tm, 512, 256, 128):
        blocks = 2 * (tm * tk * a.dtype.itemsize + tk * tn * 2 + tm * tn * tile_bytes)
        if M % tm == 0 and blocks + (tm * tn * 4 if nk > 1 else 0) <= MM_VMEM_BUDGET:
            break
    else:
        tm = M

    def body(*refs):
        a_ref, b_ref = refs[:2]
        ex_refs = refs[2:2 + n_ex + n_rx]
        o_refs = refs[2 + n_ex + n_rx:2 + n_ex + n_rx + n_out]
        s_refs = refs[2 + n_ex + n_rx + n_out:2 + n_ex + n_rx + n_out + n_sums]

        def finish(acc):
            vals = epilogue(acc, *[r[...] for r in ex_refs])
            assert len(vals) == n_out + n_sums
            for o_ref, val, dt in zip(o_refs, vals[:n_out], out_dtypes, strict=True):
                o_ref[...] = val.astype(dt)
            for s_ref, val in zip(s_refs, vals[n_out:], strict=True):
                s_ref[...] = jnp.broadcast_to(val, (SUM_ROWS, tn))

        part = lax.dot_general(a_ref[...].astype(BF16), b_ref[...].astype(BF16),
                               (((1,), (1 if bt else 0,)), ((), ())), preferred_element_type=F32)
        if nk == 1:
            finish(part)
        else:
            acc_ref = refs[-1]
            k = pl.program_id(2)

            @pl.when(k == 0)
            def _():
                acc_ref[...] = part

            @pl.when(k > 0)
            def _():
                acc_ref[...] += part

            @pl.when(k == nk - 1)
            def _():
                finish(acc_ref[...])

    tile = pl.BlockSpec((tm, tn), lambda i, j, k: (i, j))
    outs = pl.pallas_call(
        body, name=name, grid=(M // tm, N // tn, nk),
        in_specs=[pl.BlockSpec((tm, tk), lambda i, j, k: (i, k)),
                  pl.BlockSpec((tn, tk), lambda i, j, k: (j, k)) if bt else pl.BlockSpec((tk, tn), lambda i, j, k: (k, j))]
        + [tile] * n_ex + [pl.BlockSpec((1, tn), lambda i, j, k: (0, j))] * n_rx,
        out_specs=[tile] * n_out + [pl.BlockSpec((SUM_ROWS, tn), lambda i, j, k: (i, j))] * n_sums,
        out_shape=[jax.ShapeDtypeStruct((M, N), d) for d in out_dtypes]
        + [jax.ShapeDtypeStruct((M // tm * SUM_ROWS, N), F32)] * n_sums,
        scratch_shapes=[pltpu.VMEM((tm, tn), F32)] if nk > 1 else [],
        compiler_params=_params("parallel", "parallel", "arbitrary"),
    )(a, b, *extras, *row_extras)
    return tuple(outs) if custom else outs[0]


def _sum_tiles(s):
    return jnp.sum(s.reshape(-1, SUM_ROWS, s.shape[-1])[:, 0], axis=0, keepdims=True)


def _rms_grad(dh, *ex):
    *add, x, dx_in, g = ex
    for extra in add:
        dh = dh + extra
    r = lax.rsqrt(jnp.mean(x * x, axis=-1, keepdims=True) + RMS_EPS)
    xh = x * r
    gd = dh * g
    dx = dx_in + r * (gd - xh * jnp.mean(gd * xh, axis=-1, keepdims=True))
    return dx, jnp.sum(dh * xh, axis=0, keepdims=True)


def mm_tn(a, b, *, name, scale=1.0):
    M, K = a.shape
    M2, N = b.shape
    assert M == M2
    tm = _pick(M, (1024, 512, 256, 128))
    tk = _pick(K, (1408, 1024, 512, 256, 128))
    tn = next((c for c in _WIDE_TILES if N % c == 0 and tk * c * 4 <= MM_ACC_BYTES), N)
    nm = M // tm

    def body(a_ref, b_ref, o_ref):
        m = pl.program_id(2)
        part = lax.dot_general(a_ref[...].astype(BF16), b_ref[...].astype(BF16), (((0,), (0,)), ((), ())),
                               preferred_element_type=F32)

        @pl.when(m == 0)
        def _():
            o_ref[...] = part

        @pl.when(m > 0)
        def _():
            o_ref[...] += part

        if scale != 1.0:
            @pl.when(m == nm - 1)
            def _():
                o_ref[...] = o_ref[...] * scale

    return pl.pallas_call(
        body, name=name, grid=(K // tk, N // tn, nm),
        in_specs=[pl.BlockSpec((tm, tk), lambda i, j, m: (m, i)), pl.BlockSpec((tm, tn), lambda i, j, m: (m, j))],
        out_specs=pl.BlockSpec((tk, tn), lambda i, j, m: (i, j)),
        out_shape=jax.ShapeDtypeStruct((K, N), F32),
        compiler_params=_params("parallel", "parallel", "arbitrary"),
    )(a, b)


def rms_fwd(x, g, *, name):
    S, D = x.shape
    ts = _pick(S, (512, 256, 128))

    def body(x_ref, g_ref, h_ref):
        xv = x_ref[...]
        r = lax.rsqrt(jnp.mean(xv * xv, axis=-1, keepdims=True) + RMS_EPS)
        h_ref[...] = ((xv * r) * g_ref[...]).astype(BF16)

    return pl.pallas_call(
        body, name=name, grid=(S // ts,),
        in_specs=[pl.BlockSpec((ts, D), lambda i: (i, 0)), pl.BlockSpec((1, D), lambda i: (0, 0))],
        out_specs=pl.BlockSpec((ts, D), lambda i: (i, 0)),
        out_shape=jax.ShapeDtypeStruct((S, D), BF16),
        compiler_params=_params("parallel"),
    )(x, g)


def rms_bwd(x, g, dh, dx_in, *, name):
    S, D = x.shape
    ts = _pick(S, (512, 256, 128))
    has_in = dx_in is not None

    def body(*refs):
        if has_in:
            x_ref, g_ref, dh_ref, di_ref, dx_ref, dg_ref = refs
        else:
            x_ref, g_ref, dh_ref, dx_ref, dg_ref = refs
        xv = x_ref[...]
        dh_v = dh_ref[...]
        r = lax.rsqrt(jnp.mean(xv * xv, axis=-1, keepdims=True) + RMS_EPS)
        xh = xv * r
        gd = dh_v * g_ref[...]
        dx = r * (gd - xh * jnp.mean(gd * xh, axis=-1, keepdims=True))
        if has_in:
            dx = di_ref[...] + dx
        dx_ref[...] = dx
        part = jnp.sum(dh_v * xh, axis=0, keepdims=True)

        @pl.when(pl.program_id(0) == 0)
        def _():
            dg_ref[...] = part

        @pl.when(pl.program_id(0) > 0)
        def _():
            dg_ref[...] += part

    row = pl.BlockSpec((ts, D), lambda i: (i, 0))
    vec = pl.BlockSpec((1, D), lambda i: (0, 0))
    return pl.pallas_call(
        body, name=name, grid=(S // ts,),
        in_specs=[row, vec, row] + ([row] if has_in else []),
        out_specs=[row, vec],
        out_shape=[jax.ShapeDtypeStruct((S, D), F32), jax.ShapeDtypeStruct((1, D), F32)],
        compiler_params=_params("arbitrary"),
    )(*([x, g, dh] + ([dx_in] if has_in else [])))


def final_loss_bwd(x, g, target, *, name):
    S, D = x.shape
    ts = _pick(S, (512, 256, 128))

    def body(x_ref, g_ref, t_ref, dx_ref, dg_ref, ls_ref):
        xv = x_ref[...]
        gv = g_ref[...]
        r = lax.rsqrt(jnp.mean(xv * xv, axis=-1, keepdims=True) + RMS_EPS)
        xh = xv * r
        e = xh * gv - t_ref[...]
        dy = e * (1.0 / D)
        gd = dy * gv
        dx_ref[...] = r * (gd - xh * jnp.mean(gd * xh, axis=-1, keepdims=True))
        dg_part = jnp.sum(dy * xh, axis=0, keepdims=True)
        ls_part = jnp.sum(e * e, axis=0, keepdims=True)

        @pl.when(pl.program_id(0) == 0)
        def _():
            dg_ref[...] = dg_part
            ls_ref[...] = ls_part

        @pl.when(pl.program_id(0) > 0)
        def _():
            dg_ref[...] += dg_part
            ls_ref[...] += ls_part

    row = pl.BlockSpec((ts, D), lambda i: (i, 0))
    vec = pl.BlockSpec((1, D), lambda i: (0, 0))
    return pl.pallas_call(
        body, name=name, grid=(S // ts,), in_specs=[row, vec, row], out_specs=[row, vec, vec],
        out_shape=[jax.ShapeDtypeStruct((S, D), F32), jax.ShapeDtypeStruct((1, D), F32),
                   jax.ShapeDtypeStruct((1, D), F32)],
        compiler_params=_params("arbitrary"),
    )(x, g, target)


def _swiglu(up, gp):
    gp = gp.astype(F32)
    return up, gp * _sig(gp) * up


def _swiglu_grad(da, gp, up):
    da = da * 0.5
    gp, up = gp.astype(F32), up.astype(F32)
    s = _sig(gp)
    return da * up * (s * (1.0 + gp * (1.0 - s))), da * (gp * s)


def ffn_fwd(x, w, tag):
    h = rms_fwd(x, w["norm"], name=f"{tag}_rms")
    gp = mm(h, w["w_gate"], name=f"{tag}_gate", out_dtype=BF16)
    up, a = mm(h, w["w_up"], name=f"{tag}_up", extras=(gp,), epilogue=_swiglu, out_dtypes=(BF16, BF16))
    y = mm(a, w["w_down"], name=f"{tag}_down", res=x, scale=0.5)
    return y, (x, h, gp, up, a)


def ffn_bwd(dx, saved, w, tag):
    x, h, gp, up, a = saved
    grads = {"w_down": mm_tn(a, dx, name=f"{tag}_dwd", scale=0.5)}
    dgp, dup = mm(dx, w["w_down"], name=f"{tag}_da", extras=(gp, up), epilogue=_swiglu_grad,
                  out_dtypes=(BF16, BF16), bt=True)
    grads["w_gate"] = mm_tn(h, dgp, name=f"{tag}_dwg")
    grads["w_up"] = mm_tn(h, dup, name=f"{tag}_dwu")
    dh = mm(dgp, w["w_gate"], name=f"{tag}_dh1", bt=True)
    dx, dg = mm(dup, w["w_up"], name=f"{tag}_dh2", extras=(dh, x, dx), row_extras=(w["norm"],),
                epilogue=_rms_grad, out_dtypes=(F32,), n_sums=1, bt=True)
    grads["norm"] = _sum_tiles(dg)
    return dx, grads


SEG_AB, SEG_AC, SEG_AH, SEG_U, SEG_V, SEG_FQ, SEG_FK, SEG_FV, SEG_FL = range(9)
N_SEG = 9


def _seg_block(D, seg):
    return 3 * D // SEG + seg


def _shift_down(z, prev8, n, rows):
    out = pltpu.roll(z, n, 0)
    for r in range(n):
        out = jnp.where(rows == r, prev8[8 - n + r:8 - n + r + 1, :], out)
    return out


def _shift_up(z, next8, n, rows, ts):
    out = pltpu.roll(z, ts - n, 0)
    for r in range(n):
        out = jnp.where(rows == ts - n + r, next8[r:r + 1, :], out)
    return out


def conv_fwd(proj, conv_w, D, *, name):
    S = proj.shape[0]
    ts = _pick(S, (512, 256, 128))
    b0 = _seg_block(D, 0)

    def body(ab_ref, ac_ref, ah_ref, pc_ref, ph_ref, w_ref, y_ref):
        i = pl.program_id(0)
        rows = lax.broadcasted_iota(jnp.int32, (ts, 1), 0)
        z = ac_ref[...] * ah_ref[...]
        zp = pc_ref[...] * ph_ref[...] * (i > 0).astype(F32)
        w = w_ref[...]
        y = w[0:1, :] * _shift_down(z, zp, 2, rows) + w[1:2, :] * _shift_down(z, zp, 1, rows) + w[2:3, :] * z
        y_ref[...] = (ab_ref[...] * y).astype(BF16)

    def seg(s):
        return pl.BlockSpec((ts, SEG), lambda i: (i, b0 + s))

    def prev(s):
        return pl.BlockSpec((8, SEG), lambda i: (jnp.maximum(i * (ts // 8) - 1, 0), b0 + s))

    return pl.pallas_call(
        body, name=name, grid=(S // ts,),
        in_specs=[seg(SEG_AB), seg(SEG_AC), seg(SEG_AH), prev(SEG_AC), prev(SEG_AH),
                  pl.BlockSpec((3, SEG), lambda i: (0, 0))],
        out_specs=pl.BlockSpec((ts, SEG), lambda i: (i, 0)),
        out_shape=jax.ShapeDtypeStruct((S, SEG), BF16), compiler_params=_params("parallel"),
    )(proj, proj, proj, proj, proj, conv_w)


def conv_bwd(proj, conv_w, dy, D, *, name):
    S = proj.shape[0]
    ts = _pick(S, (512, 256, 128))
    nt = S // ts
    b0 = _seg_block(D, 0)

    def body(ab_ref, ac_ref, ah_ref, pc_ref, ph_ref, nb_ref, dy_ref, ndy_ref, w_ref,
             dab_ref, dac_ref, dah_ref, dw_ref):
        i = pl.program_id(0)
        rows = lax.broadcasted_iota(jnp.int32, (ts, 1), 0)
        ab, ac, ah = ab_ref[...], ac_ref[...], ah_ref[...]
        z = ac * ah
        zp = pc_ref[...] * ph_ref[...] * (i > 0).astype(F32)
        w = w_ref[...]
        z1 = _shift_down(z, zp, 1, rows)
        z2 = _shift_down(z, zp, 2, rows)
        y = w[0:1, :] * z2 + w[1:2, :] * z1 + w[2:3, :] * z
        dyv = dy_ref[...]
        dab_ref[...] = (dyv * y).astype(BF16)
        dyy = dyv * ab
        nyy = ndy_ref[...] * nb_ref[...] * (i < nt - 1).astype(F32)
        dz = (w[2:3, :] * dyy + w[1:2, :] * _shift_up(dyy, nyy, 1, rows, ts)
              + w[0:1, :] * _shift_up(dyy, nyy, 2, rows, ts))
        dac_ref[...] = (dz * ah).astype(BF16)
        dah_ref[...] = (dz * ac).astype(BF16)
        parts = [jnp.sum(dyy * zz, axis=0, keepdims=True) for zz in (z2, z1, z)]

        @pl.when(i == 0)
        def _():
            for k in range(3):
                dw_ref[k:k + 1, :] = parts[k]

        @pl.when(i > 0)
        def _():
            for k in range(3):
                dw_ref[k:k + 1, :] += parts[k]

    def seg(s):
        return pl.BlockSpec((ts, SEG), lambda i: (i, b0 + s))

    def prev(s):
        return pl.BlockSpec((8, SEG), lambda i: (jnp.maximum(i * (ts // 8) - 1, 0), b0 + s))

    nxt_row = lambda i: jnp.minimum((i + 1) * (ts // 8), S // 8 - 1)
    out_row = pl.BlockSpec((ts, SEG), lambda i: (i, 0))
    return pl.pallas_call(
        body, name=name, grid=(nt,),
        in_specs=[seg(SEG_AB), seg(SEG_AC), seg(SEG_AH), prev(SEG_AC), prev(SEG_AH),
                  pl.BlockSpec((8, SEG), lambda i: (nxt_row(i), b0 + SEG_AB)),
                  out_row, pl.BlockSpec((8, SEG), lambda i: (nxt_row(i), 0)),
                  pl.BlockSpec((3, SEG), lambda i: (0, 0))],
        out_specs=[out_row, out_row, out_row, pl.BlockSpec((3, SEG), lambda i: (0, 0))],
        out_shape=[jax.ShapeDtypeStruct((S, SEG), BF16)] * 3 + [jax.ShapeDtypeStruct((3, SEG), F32)],
        compiler_params=_params("arbitrary"),
    )(proj, proj, proj, proj, proj, proj, dy, dy, conv_w)


def _tril_mask():
    r = lax.broadcasted_iota(jnp.int32, (CHUNK, CHUNK), 0)
    c = lax.broadcasted_iota(jnp.int32, (CHUNK, CHUNK), 1)
    return c <= r


def sg_fwd(proj, sg_norm, sg_w, sg_bt, D, *, name):
    S = proj.shape[0]
    ts = _pick(S, (512, 256, 128))
    b0 = _seg_block(D, 0)

    def body(u_ref, v_ref, gs_ref, w_ref, b_ref, y_ref):
        ug, _ = _gelu(u_ref[...])
        vg, _ = _gelu(v_ref[...])
        vn = ((vg * lax.rsqrt(jnp.mean(vg * vg, axis=-1, keepdims=True) + RMS_EPS)) * gs_ref[...]).astype(BF16)
        mask = _tril_mask()
        for g in range(SG_GROUPS):
            wg = jnp.where(mask, w_ref[g], 0.0).astype(BF16)
            cols = slice(g * CHUNK, (g + 1) * CHUNK)
            for n in range(ts // CHUNK):
                rws = slice(n * CHUNK, (n + 1) * CHUNK)
                sv = jnp.dot(wg, vn[rws, cols], preferred_element_type=F32) + b_ref[g]
                y_ref[rws, cols] = (ug[rws, cols] * sv).astype(BF16)

    seg = lambda s: pl.BlockSpec((ts, SEG), lambda i: (i, b0 + s))
    return pl.pallas_call(
        body, name=name, grid=(S // ts,),
        in_specs=[seg(SEG_U), seg(SEG_V), pl.BlockSpec((1, SEG), lambda i: (0, 0)),
                  pl.BlockSpec((SG_GROUPS, CHUNK, CHUNK), lambda i: (0, 0, 0)),
                  pl.BlockSpec((SG_GROUPS, CHUNK, 1), lambda i: (0, 0, 0))],
        out_specs=pl.BlockSpec((ts, SEG), lambda i: (i, 0)),
        out_shape=jax.ShapeDtypeStruct((S, SEG), BF16), compiler_params=_params("parallel"),
    )(proj, proj, sg_norm, sg_w, sg_bt)


def sg_bwd(proj, sg_norm, sg_w, sg_bt, dy, D, *, name):
    S = proj.shape[0]
    ts = _pick(S, (512, 256, 128))
    nt = S // ts
    b0 = _seg_block(D, 0)

    def body(u_ref, v_ref, dy_ref, gs_ref, w_ref, b_ref, du_ref, dv_ref, dw_ref, db_ref, dgs_ref, dvn_sc):
        i = pl.program_id(0)
        uv, vv, dyv = u_ref[...], v_ref[...], dy_ref[...]
        ug, ut = _gelu(uv)
        vg, vt = _gelu(vv)
        r = lax.rsqrt(jnp.mean(vg * vg, axis=-1, keepdims=True) + RMS_EPS)
        vh = vg * r
        gs = gs_ref[...]
        vn = (vh * gs).astype(BF16)
        dsv = dyv * ug
        dsv_b = dsv.astype(BF16)
        mask = _tril_mask()

        @pl.when(i == 0)
        def _():
            dw_ref[...] = jnp.zeros_like(dw_ref)
            db_ref[...] = jnp.zeros_like(db_ref)

        for g in range(SG_GROUPS):
            wg = jnp.where(mask, w_ref[g], 0.0).astype(BF16)
            cols = slice(g * CHUNK, (g + 1) * CHUNK)
            dw_acc = jnp.zeros((CHUNK, CHUNK), F32)
            db_acc = jnp.zeros((CHUNK, 1), F32)
            for n in range(ts // CHUNK):
                rws = slice(n * CHUNK, (n + 1) * CHUNK)
                vblk = vn[rws, cols]
                sv = jnp.dot(wg, vblk, preferred_element_type=F32) + b_ref[g]
                du_ref[rws, cols] = (dyv[rws, cols] * sv * _gelu_grad(uv[rws, cols], ut[rws, cols])).astype(BF16)
                dblk = dsv_b[rws, cols]
                dvn_sc[rws, cols] = lax.dot_general(wg, dblk, (((0,), (0,)), ((), ())), preferred_element_type=F32)
                dw_acc = dw_acc + lax.dot_general(dblk, vblk, (((1,), (1,)), ((), ())), preferred_element_type=F32)
                db_acc = db_acc + jnp.sum(dsv[rws, cols], axis=1, keepdims=True)
            dw_ref[g] += jnp.where(mask, dw_acc, 0.0)
            db_ref[g] += db_acc

        dvn = dvn_sc[...]
        gd = dvn * gs
        dvg = r * (gd - vh * jnp.mean(gd * vh, axis=-1, keepdims=True))
        dv_ref[...] = (dvg * _gelu_grad(vv, vt)).astype(BF16)
        dgs_part = jnp.sum(dvn * vh, axis=0, keepdims=True)

        @pl.when(i == 0)
        def _():
            dgs_ref[...] = dgs_part

        @pl.when(i > 0)
        def _():
            dgs_ref[...] += dgs_part

    seg = lambda s: pl.BlockSpec((ts, SEG), lambda i: (i, b0 + s))
    row = pl.BlockSpec((ts, SEG), lambda i: (i, 0))
    wspec = pl.BlockSpec((SG_GROUPS, CHUNK, CHUNK), lambda i: (0, 0, 0))
    bspec = pl.BlockSpec((SG_GROUPS, CHUNK, 1), lambda i: (0, 0, 0))
    vec = pl.BlockSpec((1, SEG), lambda i: (0, 0))
    return pl.pallas_call(
        body, name=name, grid=(nt,),
        in_specs=[seg(SEG_U), seg(SEG_V), row, vec, wspec, bspec],
        out_specs=[row, row, wspec, bspec, vec],
        out_shape=[jax.ShapeDtypeStruct((S, SEG), BF16)] * 2
        + [jax.ShapeDtypeStruct((SG_GROUPS, CHUNK, CHUNK), F32), jax.ShapeDtypeStruct((SG_GROUPS, CHUNK, 1), F32),
           jax.ShapeDtypeStruct((1, SEG), F32)],
        scratch_shapes=[pltpu.VMEM((ts, SEG), F32)],
        compiler_params=_params("arbitrary"),
    )(proj, proj, dy, sg_norm, sg_w, sg_bt)


def _mesh_pos():
    return lax.axis_index("x"), lax.axis_index("y"), lax.axis_index("c")


def _flip(v, bit):
    return 1 - v if bit else v


def _comm_scratch(n):
    return [pltpu.SemaphoreType.DMA((n, 7)), pltpu.SemaphoreType.DMA((n, 7)), pltpu.SemaphoreType.DMA((n,))]


def _gather_plan(refs):
    x_refs, out_refs, send_sems, recv_sems, local_sems = refs
    n = len(x_refs)
    mx, my, mc = _mesh_pos()
    me, sibling = (mx, my, mc), (mx, my, 1 - mc)
    chips = [(1 - mx, my), (mx, 1 - my), (1 - mx, 1 - my)]

    def copy(a, k, block, to, from_input=False):
        slot = out_refs[a].at[4 * block[0] + 2 * block[1] + block[2]]
        return pltpu.make_async_remote_copy(
            src_ref=x_refs[a] if from_input else slot, dst_ref=slot,
            send_sem=send_sems.at[a, k], recv_sem=recv_sems.at[a, k], device_id=to, device_id_type=MESH_IDS)

    local = [pltpu.make_async_copy(x_refs[a], out_refs[a].at[4 * mx + 2 * my + mc], local_sems.at[a])
             for a in range(n)]
    first = []
    for a in range(n):
        first.append(copy(a, 0, me, sibling, from_input=True))
        first += [copy(a, 1 + j, me, (*chip, mc), from_input=True) for j, chip in enumerate(chips)]
    return n, me, sibling, chips, mc, copy, local, first


def gather_start(refs):
    n, me, sibling, chips, mc, copy, local, first = _gather_plan(refs)
    for cp in local + first:
        cp.start()


def gather_finish(refs):
    n, me, sibling, chips, mc, copy, local, first = _gather_plan(refs)
    onward = []
    for j, chip in enumerate(chips):
        for a in range(n):
            copy(a, 1 + j, (*chip, mc), me).wait_recv()
            cp = copy(a, 4 + j, (*chip, mc), sibling)
            cp.start()
            onward.append(cp)
    for a in range(n):
        copy(a, 0, sibling, me).wait_recv()
        for j, chip in enumerate(chips):
            copy(a, 4 + j, (*chip, 1 - mc), me).wait_recv()
    for cp in first + onward:
        cp.wait_send()
    for cp in local:
        cp.wait()


def _scatter_plan(refs):
    g_refs, out_refs, send_sems, recv_sems, local_sems = refs
    n = len(g_refs)
    mx, my, mc = _mesh_pos()
    me = 4 * mx + 2 * my + mc
    local = [pltpu.make_async_copy(g_refs[a].at[me], out_refs[a].at[me], local_sems.at[a]) for a in range(n)]
    sends, arrivals = [], []
    for k in range(1, N_DEV):
        peer = (_flip(mx, k & 4), _flip(my, k & 2), _flip(mc, k & 1))
        peer_slot = 4 * peer[0] + 2 * peer[1] + peer[2]
        for a in range(n):
            sems = dict(send_sem=send_sems.at[a, k - 1], recv_sem=recv_sems.at[a, k - 1], device_id=peer,
                        device_id_type=MESH_IDS)
            sends.append(pltpu.make_async_remote_copy(src_ref=g_refs[a].at[peer_slot], dst_ref=out_refs[a].at[me],
                                                      **sems))
            arrivals.append(pltpu.make_async_remote_copy(src_ref=g_refs[a].at[peer_slot],
                                                         dst_ref=out_refs[a].at[peer_slot], **sems))
    return local, sends, arrivals


def scatter_start(refs):
    local, sends, arrivals = _scatter_plan(refs)
    for cp in local + sends:
        cp.start()


def scatter_finish(refs):
    local, sends, arrivals = _scatter_plan(refs)
    for cp in arrivals:
        cp.wait_recv()
    for cp in sends:
        cp.wait_send()
    for cp in local:
        cp.wait()


def _split_comm_refs(refs, n_in, n_out, n_side):
    ins, side_in = refs[:n_in], refs[n_in:n_in + n_side]
    outs = refs[n_in + n_side:n_in + n_side + n_out]
    side_out = refs[n_in + n_side + n_out:n_in + 2 * n_side + n_out]
    rest = refs[n_in + 2 * n_side + n_out:]
    if n_side == 0:
        return ins + outs + rest, None
    return ins + outs + rest[:-3], (side_in, side_out) + tuple(rest[-3:])


def _log_sigmoid(x):
    return jnp.minimum(x, 0.0) - jnp.log(1.0 + jnp.exp(-jnp.abs(x)))


def fox_cumlog(proj, b_f, D, *, name):
    S = proj.shape[0]
    ts = _pick(S, (512, 256, 128))
    blk = (3 * D + SEG_FL * SEG) // LANES

    def body(f_ref, b_ref, c_ref, carry):
        i = pl.program_id(0)

        @pl.when(i == 0)
        def _():
            carry[...] = jnp.zeros_like(carry)

        rows = lax.broadcasted_iota(jnp.int32, (ts, 1), 0)
        acc = _log_sigmoid(f_ref[...] + b_ref[...])
        d = 1
        while d < ts:
            acc = acc + jnp.where(rows >= d, pltpu.roll(acc, d, 0), 0.0)
            d *= 2
        acc = acc + carry[...]
        c_ref[...] = acc
        carry[...] = acc[ts - 1:ts, :]

    return pl.pallas_call(
        body, name=name, grid=(S // ts,),
        in_specs=[pl.BlockSpec((ts, LANES), lambda i: (i, blk)), pl.BlockSpec((1, LANES), lambda i: (0, 0))],
        out_specs=pl.BlockSpec((ts, LANES), lambda i: (i, 0)),
        out_shape=jax.ShapeDtypeStruct((S, LANES), F32),
        scratch_shapes=[pltpu.VMEM((1, LANES), F32)],
        compiler_params=_params("arbitrary"),
    )(proj, b_f)


def fox_dlogit(proj, b_f, dc, D, *, name):
    S = proj.shape[0]
    ts = _pick(S, (512, 256, 128))
    nt = S // ts
    blk = (3 * D + SEG_FL * SEG) // LANES

    def body(f_ref, b_ref, dc_ref, df_ref, db_ref, carry):
        i = pl.program_id(0)

        @pl.when(i == 0)
        def _():
            carry[...] = jnp.zeros_like(carry)

        rows = lax.broadcasted_iota(jnp.int32, (ts, 1), 0)
        acc = dc_ref[...]
        d = 1
        while d < ts:
            acc = acc + jnp.where(rows < ts - d, pltpu.roll(acc, ts - d, 0), 0.0)
            d *= 2
        acc = acc + carry[...]
        carry[...] = acc[0:1, :]
        df = acc * _sig(-(f_ref[...] + b_ref[...]))
        df_ref[...] = jnp.zeros_like(df_ref)
        df_ref[:, 0:LANES] = df.astype(BF16)
        part = jnp.sum(df, axis=0, keepdims=True)

        @pl.when(i == 0)
        def _():
            db_ref[...] = part

        @pl.when(i > 0)
        def _():
            db_ref[...] += part

    rev = lambda i: nt - 1 - i
    return pl.pallas_call(
        body, name=name, grid=(nt,),
        in_specs=[pl.BlockSpec((ts, LANES), lambda i: (rev(i), blk)), pl.BlockSpec((1, LANES), lambda i: (0, 0)),
                  pl.BlockSpec((ts, LANES), lambda i: (rev(i), 0))],
        out_specs=[pl.BlockSpec((ts, SEG), lambda i: (rev(i), 0)), pl.BlockSpec((1, LANES), lambda i: (0, 0))],
        out_shape=[jax.ShapeDtypeStruct((S, SEG), BF16), jax.ShapeDtypeStruct((1, LANES), F32)],
        scratch_shapes=[pltpu.VMEM((1, LANES), F32)],
        compiler_params=_params("arbitrary"),
    )(proj, b_f, dc)


def _causal_mask(t, keys_on_rows=True):
    r = lax.broadcasted_iota(jnp.int32, (t, t), 0)
    c = lax.broadcasted_iota(jnp.int32, (t, t), 1)
    return (r <= c) if keys_on_rows else (c <= r)


FOX_PAD = LANES
COL_C = FOX_HEAD_DIM
COL_ROWSUM = FOX_HEAD_DIM + 3
COL_L = FOX_HEAD_DIM


def _head_pair(ref, a):
    x = ref[...]
    return x if a == 0 else pltpu.roll(x, FOX_HEAD_DIM, 1)


def fox_pack(proj, c, D, scale, *, name):
    S = proj.shape[0]
    ts = _pick(S, (512, 256, 128))
    base = 3 * D // LANES
    per_seg = SEG // LANES

    def body(q_ref, k_ref, v_ref, c_ref, qo, ko, vo, qto, vto):
        hp = pl.program_id(1)
        lane = lax.broadcasted_iota(jnp.int32, (ts, LANES), 1)
        is_val = lane < FOX_HEAD_DIM
        cv = c_ref[...]
        for a in range(2):
            ch = jnp.sum(jnp.where(lane == 2 * hp + a, cv, 0.0), axis=1, keepdims=True)
            c_hi = ch.astype(BF16).astype(F32)
            r1 = ch - c_hi
            c_mid = r1.astype(BF16).astype(F32)
            c_lo = r1 - c_mid
            qa = jnp.where(is_val, _head_pair(q_ref, a) * scale, jnp.where(lane < COL_C + 3, 1.0, 0.0))
            extra = jnp.where(lane == COL_C, -c_hi, jnp.where(lane == COL_C + 1, -c_mid, jnp.where(
                lane == COL_C + 2, -c_lo, jnp.where(lane == COL_ROWSUM, 1.0, 0.0))))
            va = jnp.where(is_val, _head_pair(v_ref, a), jnp.where(lane == COL_L, 1.0, 0.0))
            qo[a] = qa.astype(BF16)
            ko[a] = jnp.where(is_val, _head_pair(k_ref, a), extra).astype(BF16)
            vo[a] = va.astype(BF16)
            qto[a] = qa.T.astype(BF16)
            vto[a] = va.T.astype(BF16)

    seg = lambda s: pl.BlockSpec((ts, LANES), lambda i, hp: (i, base + s * per_seg + hp))
    out = pl.BlockSpec((2, ts, FOX_PAD), lambda i, hp: (hp, i, 0))
    out_t = pl.BlockSpec((2, FOX_PAD, ts), lambda i, hp: (hp, 0, i))
    return pl.pallas_call(
        body, name=name, grid=(S // ts, FOX_HEADS // 2),
        in_specs=[seg(SEG_FQ), seg(SEG_FK), seg(SEG_FV), pl.BlockSpec((ts, LANES), lambda i, hp: (i, 0))],
        out_specs=[out] * 3 + [out_t] * 2,
        out_shape=[jax.ShapeDtypeStruct((FOX_HEADS, S, FOX_PAD), BF16)] * 3
        + [jax.ShapeDtypeStruct((FOX_HEADS, FOX_PAD, S), BF16)] * 2,
        compiler_params=_params("parallel", "parallel"),
    )(proj, proj, proj, c)


def heads_pack(x, *, name):
    S = x.shape[0]
    ts = _pick(S, (512, 256, 128))

    def body(x_ref, o_ref, ot_ref):
        lane = lax.broadcasted_iota(jnp.int32, (ts, LANES), 1)
        for a in range(2):
            xa = jnp.where(lane < FOX_HEAD_DIM, _head_pair(x_ref, a), 0.0)
            o_ref[a] = xa.astype(BF16)
            ot_ref[a] = xa.T.astype(BF16)

    return pl.pallas_call(
        body, name=name, grid=(S // ts, FOX_HEADS // 2),
        in_specs=[pl.BlockSpec((ts, LANES), lambda i, hp: (i, hp))],
        out_specs=[pl.BlockSpec((2, ts, FOX_PAD), lambda i, hp: (hp, i, 0)),
                   pl.BlockSpec((2, FOX_PAD, ts), lambda i, hp: (hp, 0, i))],
        out_shape=[jax.ShapeDtypeStruct((FOX_HEADS, S, FOX_PAD), BF16),
                   jax.ShapeDtypeStruct((FOX_HEADS, FOX_PAD, S), BF16)],
        compiler_params=_params("parallel", "parallel"),
    )(x)


def heads_unpack(x, scale, *, name):
    S = x.shape[1]
    ts = _pick(S, (512, 256, 128))

    def body(x_ref, o_ref):
        lane = lax.broadcasted_iota(jnp.int32, (ts, LANES), 1)
        both = jnp.where(lane < FOX_HEAD_DIM, x_ref[0], pltpu.roll(x_ref[1], FOX_HEAD_DIM, 1))
        o_ref[...] = (both * scale).astype(BF16)

    return pl.pallas_call(
        body, name=name, grid=(S // ts, FOX_HEADS // 2),
        in_specs=[pl.BlockSpec((2, ts, FOX_PAD), lambda i, hp: (hp, i, 0))],
        out_specs=pl.BlockSpec((ts, LANES), lambda i, hp: (i, hp)),
        out_shape=jax.ShapeDtypeStruct((S, FOX_HEADS * FOX_HEAD_DIM), BF16),
        compiler_params=_params("parallel", "parallel"),
    )(x)


FOX_FWD_GROUP = 1


def _fox_fwd_tile(S):
    return min(2048, max(128, S // 4))


def _fox_bwd_tile(S):
    return min(1024, max(128, S // 4))


def fox_fwd(qt, k, vt, *, name, gather=()):
    H, W, S = qt.shape
    t = _fox_fwd_tile(S)
    n = S // t
    G = FOX_FWD_GROUP
    ns = len(gather)

    def body(*refs):
        (qt_ref, k_ref, vt_ref, ot_ref, lse_ref, m_sc, acc_sc), comm = _split_comm_refs(refs, 3, 2, ns)
        h, i, j = pl.program_id(0), pl.program_id(1), pl.program_id(2)

        if ns:
            @pl.when((h == 0) & (i == 0) & (j == 0))
            def _():
                gather_start(comm)

        @pl.when(j == 0)
        def _():
            m_sc[...] = jnp.full_like(m_sc, NEG_BIG)
            acc_sc[...] = jnp.zeros_like(acc_sc)

        def step(masked):
            for g in range(G):
                st = jnp.dot(k_ref[g], qt_ref[g], preferred_element_type=F32)
                if masked:
                    st = jnp.where(_causal_mask(t, keys_on_rows=True), st, NEG_BIG)
                m_prev = m_sc[g]
                m_new = jnp.maximum(m_prev, jnp.max(st, axis=0, keepdims=True))
                pt = jnp.exp(st - m_new)
                acc_sc[g] = jnp.exp(m_prev - m_new) * acc_sc[g] + jnp.dot(vt_ref[g], pt.astype(BF16),
                                                                          preferred_element_type=F32)
                m_sc[g] = m_new

        @pl.when(j < i)
        def _():
            step(False)

        @pl.when(j == i)
        def _():
            step(True)
            for g in range(G):
                acc = acc_sc[g]
                l = acc[COL_L:COL_L + 1, :]
                ot_ref[g] = acc / l
                lse_ref[g] = m_sc[g] + jnp.log(l)

        if ns:
            @pl.when((h == H // G - 1) & (i == n - 1) & (j == n - 1))
            def _():
                gather_finish(comm)

    qs = pl.BlockSpec((G, W, t), lambda h, i, j: (h, 0, i))
    ks = pl.BlockSpec((G, t, W), lambda h, i, j: (h, jnp.minimum(j, i), 0))
    vs = pl.BlockSpec((G, W, t), lambda h, i, j: (h, 0, jnp.minimum(j, i)))
    row = pl.BlockSpec((G, 1, t), lambda h, i, j: (h, 0, i))
    outs = pl.pallas_call(
        body, name=name, grid=(H // G, n, n), in_specs=[qs, ks, vs] + [ANY_SPEC] * ns,
        out_specs=[qs, row] + [ANY_SPEC] * ns,
        out_shape=[jax.ShapeDtypeStruct((H, W, S), F32), jax.ShapeDtypeStruct((H, 1, S), F32)]
        + [jax.ShapeDtypeStruct((N_DEV,) + x.shape, x.dtype) for x in gather],
        scratch_shapes=[pltpu.VMEM((G, 1, t), F32), pltpu.VMEM((G, W, t), F32)] + (_comm_scratch(ns) if ns else []),
        compiler_params=_params("arbitrary", "arbitrary", "arbitrary") if ns
        else _params("parallel", "parallel", "arbitrary"),
    )(qt, k, vt, *gather)
    return outs[0], outs[1], list(outs[2:])


def fox_dc(dq, dk, *, name):
    H, S, W = dq.shape
    ts = _pick(S, (512, 256, 128))

    def body(dq_ref, dk_ref, o_ref):
        lane = lax.broadcasted_iota(jnp.int32, (ts, LANES), 1)
        acc = jnp.zeros((ts, LANES), F32)
        for h in range(H):
            d = dq_ref[h][:, COL_ROWSUM:COL_ROWSUM + 1] - dk_ref[h][:, COL_C:COL_C + 1]
            acc = jnp.where(lane == h, d, acc)
        o_ref[...] = acc

    blk = pl.BlockSpec((H, ts, W), lambda i: (0, i, 0))
    return pl.pallas_call(
        body, name=name, grid=(S // ts,), in_specs=[blk, blk],
        out_specs=pl.BlockSpec((ts, LANES), lambda i: (i, 0)),
        out_shape=jax.ShapeDtypeStruct((S, LANES), F32), compiler_params=_params("parallel"),
    )(dq, dk)


def heads_unpack_t(xt, *, name):
    S = xt.shape[2]
    ts = _pick(S, (512, 256, 128))

    def body(x_ref, y_ref, o_ref):
        lane = lax.broadcasted_iota(jnp.int32, (ts, LANES), 1)
        x0, x1 = x_ref[0].T, x_ref[1].T
        o_ref[0] = x0
        o_ref[1] = x1
        y_ref[...] = jnp.where(lane < FOX_HEAD_DIM, x0, pltpu.roll(x1, FOX_HEAD_DIM, 1)).astype(BF16)

    return pl.pallas_call(
        body, name=name, grid=(S // ts, FOX_HEADS // 2),
        in_specs=[pl.BlockSpec((2, FOX_PAD, ts), lambda i, hp: (hp, 0, i))],
        out_specs=[pl.BlockSpec((ts, LANES), lambda i, hp: (i, hp)),
                   pl.BlockSpec((2, ts, FOX_PAD), lambda i, hp: (hp, i, 0))],
        out_shape=[jax.ShapeDtypeStruct((S, FOX_HEADS * FOX_HEAD_DIM), BF16),
                   jax.ShapeDtypeStruct((FOX_HEADS, S, FOX_PAD), F32)],
        compiler_params=_params("parallel", "parallel"),
    )(xt)


def fox_delta(do, o, *, name):
    H, S, Dh = o.shape
    t = _pick(S, (2048, 1024, 512, 256, 128))

    def body(do_ref, o_ref, d_ref):
        d_ref[0] = jnp.sum(do_ref[0] * o_ref[0], axis=-1, keepdims=True)

    blk = pl.BlockSpec((1, t, Dh), lambda h, i: (h, i, 0))
    return pl.pallas_call(
        body, name=name, grid=(H, S // t), in_specs=[blk, blk],
        out_specs=pl.BlockSpec((1, t, 1), lambda h, i: (h, i, 0)),
        out_shape=jax.ShapeDtypeStruct((H, S, 1), F32), compiler_params=_params("parallel", "parallel"),
    )(do, o)


def fox_bwd(qt, q, k, v, dot, do, lse, delta, *, name, scatter=()):
    H, W, S = qt.shape
    t = _fox_bwd_tile(S)
    n = S // t
    ns = len(scatter)

    def body(*refs):
        own, comm = _split_comm_refs(refs, 8, 3, ns)
        (qt_ref, q_ref, k_ref, v_ref, dot_ref, do_ref, lse_ref, dl_ref, dq_hbm, dk_ref, dv_ref,
         dq_sc, dk_sc, dv_sc, sem) = own
        h, j, i = pl.program_id(0), pl.program_id(1), pl.program_id(2)

        if ns:
            @pl.when((h == 0) & (j == 0) & (i == 0))
            def _():
                scatter_start(comm)

        @pl.when((j == 0) & (i == 0))
        def _():
            dq_sc[...] = jnp.zeros_like(dq_sc)

        @pl.when(i == 0)
        def _():
            dk_sc[...] = jnp.zeros_like(dk_sc)
            dv_sc[...] = jnp.zeros_like(dv_sc)

        def step(masked):
            st = jnp.dot(k_ref[0], qt_ref[0], preferred_element_type=F32)
            pt = jnp.exp(st - lse_ref[0])
            if masked:
                pt = jnp.where(_causal_mask(t, keys_on_rows=True), pt, 0.0)
            dpt = jnp.dot(v_ref[0], dot_ref[0], preferred_element_type=F32)
            dst = (pt * (dpt - dl_ref[0])).astype(BF16)
            dv_sc[...] += jnp.dot(pt.astype(BF16), do_ref[0], preferred_element_type=F32)
            dk_sc[...] += jnp.dot(dst, q_ref[0], preferred_element_type=F32)
            rows = pl.ds(pl.multiple_of(i * t, t), t)
            dq_sc[rows, :] += lax.dot_general(dst, k_ref[0], (((0,), (0,)), ((), ())), preferred_element_type=F32)

        @pl.when(i > j)
        def _():
            step(False)

        @pl.when(i == j)
        def _():
            step(True)

        @pl.when(i == n - 1)
        def _():
            dk_ref[0] = dk_sc[...]
            dv_ref[0] = dv_sc[...]

        @pl.when((j == n - 1) & (i == n - 1))
        def _():
            out = pltpu.make_async_copy(dq_sc, dq_hbm.at[h], sem)
            out.start()
            out.wait()

        if ns:
            @pl.when((h == H - 1) & (j == n - 1) & (i == n - 1))
            def _():
                scatter_finish(comm)

    q_t = pl.BlockSpec((1, W, t), lambda h, j, i: (h, 0, jnp.maximum(i, j)))
    q_r = pl.BlockSpec((1, t, W), lambda h, j, i: (h, jnp.maximum(i, j), 0))
    k_r = pl.BlockSpec((1, t, W), lambda h, j, i: (h, j, 0))
    row = pl.BlockSpec((1, 1, t), lambda h, j, i: (h, 0, jnp.maximum(i, j)))
    outs = pl.pallas_call(
        body, name=name, grid=(H, n, n), in_specs=[q_t, q_r, k_r, k_r, q_t, q_r, row, row] + [ANY_SPEC] * ns,
        out_specs=[ANY_SPEC, k_r, k_r] + [ANY_SPEC] * ns,
        out_shape=[jax.ShapeDtypeStruct((H, S, W), F32)] * 3 + [jax.ShapeDtypeStruct(g.shape, g.dtype) for g in scatter],
        scratch_shapes=[pltpu.VMEM((S, W), F32), pltpu.VMEM((t, W), F32), pltpu.VMEM((t, W), F32),
                        pltpu.SemaphoreType.DMA] + (_comm_scratch(ns) if ns else []),
        compiler_params=_params("arbitrary", "arbitrary", "arbitrary"),
    )(qt, q, k, v, dot, do, lse, delta, *scatter)
    return outs[0], outs[1], outs[2], list(outs[3:])


def merge_fwd(proj, branches, D, *, name):
    S = proj.shape[0]
    ts = _pick(S, (256, 128))

    def body(g0, g1, g2, b0, b1, b2, o_ref):
        acc = (_sig(g0[...]) * b0[...].astype(F32) + _sig(g1[...]) * b1[...].astype(F32)
               + _sig(g2[...]) * b2[...].astype(F32))
        o_ref[...] = acc.astype(BF16)

    gate = lambda n: pl.BlockSpec((ts, D), lambda i: (i, n))
    row = pl.BlockSpec((ts, D), lambda i: (i, 0))
    return pl.pallas_call(
        body, name=name, grid=(S // ts,), in_specs=[gate(0), gate(1), gate(2), row, row, row], out_specs=row,
        out_shape=jax.ShapeDtypeStruct((S, D), BF16), compiler_params=_params("parallel"),
    )(proj, proj, proj, *branches)


def merge_bwd(proj, branches, dm, D, *, name):
    S = proj.shape[0]
    ts = _pick(S, (256, 128))

    def body(g0, g1, g2, b0, b1, b2, dm_ref, db0, db1, db2, dg0, dg1, dg2):
        dmv = dm_ref[...]
        for g_ref, b_ref, db_ref, dg_ref in ((g0, b0, db0, dg0), (g1, b1, db1, dg1), (g2, b2, db2, dg2)):
            s = _sig(g_ref[...])
            db_ref[...] = (dmv * s).astype(BF16)
            dg_ref[...] = (dmv * b_ref[...].astype(F32) * (s * (1.0 - s))).astype(BF16)

    gate = lambda n: pl.BlockSpec((ts, D), lambda i: (i, n))
    row = pl.BlockSpec((ts, D), lambda i: (i, 0))
    return pl.pallas_call(
        body, name=name, grid=(S // ts,), in_specs=[gate(0), gate(1), gate(2), row, row, row, row],
        out_specs=[row] * 6, out_shape=[jax.ShapeDtypeStruct((S, D), BF16)] * 6,
        compiler_params=_params("parallel"),
    )(proj, proj, proj, *branches, dm)


def xa_fwd(q, k, v, *, name):
    S, D = q.shape
    M = k.shape[0]
    dh = D // XA_HEADS
    scale = dh ** -0.5
    t = _pick(S, (512, 256, 128))

    def body(q_ref, k_ref, v_ref, o_ref):
        for h in range(XA_HEADS):
            cols = slice(h * dh, (h + 1) * dh)
            s = lax.dot_general(q_ref[:, cols], k_ref[:, cols], (((1,), (1,)), ((), ())),
                                preferred_element_type=F32) * scale
            p = jnp.exp(s - jnp.max(s, axis=-1, keepdims=True))
            p = p / jnp.sum(p, axis=-1, keepdims=True)
            o_ref[:, cols] = jnp.dot(p.astype(BF16), v_ref[:, cols], preferred_element_type=F32).astype(BF16)

    row = pl.BlockSpec((t, D), lambda i: (i, 0))
    full = pl.BlockSpec((M, D), lambda i: (0, 0))
    return pl.pallas_call(
        body, name=name, grid=(S // t,), in_specs=[row, full, full], out_specs=row,
        out_shape=jax.ShapeDtypeStruct((S, D), BF16), compiler_params=_params("parallel"),
    )(q, k, v)


def xa_bwd(q, k, v, do, *, name):
    S, D = q.shape
    M = k.shape[0]
    dh = D // XA_HEADS
    scale = dh ** -0.5
    t = _pick(S, (512, 256, 128))

    def body(q_ref, k_ref, v_ref, do_ref, dq_ref, dk_ref, dv_ref):
        i = pl.program_id(0)

        @pl.when(i == 0)
        def _():
            dk_ref[...] = jnp.zeros_like(dk_ref)
            dv_ref[...] = jnp.zeros_like(dv_ref)

        for h in range(XA_HEADS):
            cols = slice(h * dh, (h + 1) * dh)
            qh, kh, vh = q_ref[:, cols], k_ref[:, cols], v_ref[:, cols]
            dob = do_ref[:, cols].astype(BF16)
            s = lax.dot_general(qh, kh, (((1,), (1,)), ((), ())), preferred_element_type=F32) * scale
            p = jnp.exp(s - jnp.max(s, axis=-1, keepdims=True))
            p = p / jnp.sum(p, axis=-1, keepdims=True)
            dp = lax.dot_general(dob, vh, (((1,), (1,)), ((), ())), preferred_element_type=F32)
            ds = (p * (dp - jnp.sum(p * dp, axis=-1, keepdims=True)) * scale).astype(BF16)
            dq_ref[:, cols] = jnp.dot(ds, kh, preferred_element_type=F32).astype(BF16)
            dk_ref[:, cols] += lax.dot_general(ds, qh, (((0,), (0,)), ((), ())), preferred_element_type=F32)
            dv_ref[:, cols] += lax.dot_general(p.astype(BF16), dob, (((0,), (0,)), ((), ())),
                                               preferred_element_type=F32)

    row = pl.BlockSpec((t, D), lambda i: (i, 0))
    full = pl.BlockSpec((M, D), lambda i: (0, 0))
    return pl.pallas_call(
        body, name=name, grid=(S // t,), in_specs=[row, full, full, row], out_specs=[row, full, full],
        out_shape=[jax.ShapeDtypeStruct((S, D), BF16), jax.ShapeDtypeStruct((M, D), F32),
                   jax.ShapeDtypeStruct((M, D), F32)],
        compiler_params=_params("arbitrary"),
    )(q, k, v, do)


def mixer_fwd(x, w, tag, gather=()):
    S, D = x.shape
    h = rms_fwd(x, w["mix_norm"], name=f"{tag}_rms")
    proj = mm(h, w["w_in"], name=f"{tag}_proj")
    y_a = conv_fwd(proj, w["conv_w"], D, name=f"{tag}_conv")
    y_b = sg_fwd(proj, w["sg_norm"], w["sg_w"], w["sg_bt"], D, name=f"{tag}_sg")
    c = fox_cumlog(proj, w["fox_b_f"], D, name=f"{tag}_cumlog")
    qh, kh, vh, qt, vt = fox_pack(proj, c, D, FOX_HEAD_DIM ** -0.5, name=f"{tag}_foxpack")
    ot, lse, gathered = fox_fwd(qt, kh, vt, name=f"{tag}_fox", gather=gather)
    y_c, o = heads_unpack_t(ot, name=f"{tag}_foxout")
    ys = (y_a, y_b, y_c)
    branches = [mm(ys[n], w["w_branch"][n], name=f"{tag}_branch{n}", out_dtype=BF16) for n in range(N_BRANCH)]
    merged = merge_fwd(proj, branches, D, name=f"{tag}_merge")
    y = mm(merged, w["w_out"], name=f"{tag}_out", res=x)
    return y, (x, h, proj, ys, qh, kh, vh, qt, o, lse, branches, merged), gathered


def mixer_bwd(dx, saved, w, tag, scatter=()):
    x, h, proj, ys, qh, kh, vh, qt, o, lse, branches, merged = saved
    S, D = x.shape
    grads = {"w_out": mm_tn(merged, dx, name=f"{tag}_dwout")}
    dmerged = mm(dx, w["w_out"], name=f"{tag}_dmerged", bt=True)
    outs = merge_bwd(proj, branches, dmerged, D, name=f"{tag}_dmerge")
    dbr, dgl = outs[:3], outs[3:]
    grads["w_branch"] = jnp.stack([mm_tn(ys[n], dbr[n], name=f"{tag}_dwbr{n}") for n in range(N_BRANCH)])
    dys = [mm(dbr[n], w["w_branch"][n], name=f"{tag}_dy{n}", bt=True) for n in range(N_BRANCH)]
    d_ab, d_ac, d_ah, grads["conv_w"] = conv_bwd(proj, w["conv_w"], dys[0], D, name=f"{tag}_dconv")
    d_u, d_v, grads["sg_w"], d_sgb, grads["sg_norm"] = sg_bwd(
        proj, w["sg_norm"], w["sg_w"], w["sg_bt"], dys[1], D, name=f"{tag}_dsg")
    grads["sg_b"] = d_sgb[:, :, 0]
    do, dot = heads_pack(dys[2], name=f"{tag}_dopack")
    delta = fox_delta(do, o, name=f"{tag}_delta")
    dq, dk, dv, scattered = fox_bwd(qt, qh, kh, vh, dot, do, lse, delta.reshape(FOX_HEADS, 1, S),
                                    name=f"{tag}_foxbwd", scatter=scatter)
    dc_rows = fox_dc(dq, dk, name=f"{tag}_dc")
    d_fl, d_bf = fox_dlogit(proj, w["fox_b_f"], dc_rows, D, name=f"{tag}_dflogit")
    grads["fox_b_f"] = d_bf[0, :FOX_HEADS]
    dproj = jnp.concatenate(
        list(dgl) + [d_ab, d_ac, d_ah, d_u, d_v, heads_unpack(dq, FOX_HEAD_DIM ** -0.5, name=f"{tag}_dqout"),
                     heads_unpack(dk, 1.0, name=f"{tag}_dkout"), heads_unpack(dv, 1.0, name=f"{tag}_dvout"), d_fl],
        axis=1)
    grads["w_in"] = mm_tn(h, dproj, name=f"{tag}_dwin")
    dx, dg = mm(dproj, w["w_in"], name=f"{tag}_dh", extras=(x, dx), row_extras=(w["mix_norm"],),
                epilogue=_rms_grad, out_dtypes=(F32,), n_sums=1, bt=True)
    grads["mix_norm"] = _sum_tiles(dg)
    return dx, grads, scattered


def xattn_fwd(x, mem, w, tag):
    h = rms_fwd(x, w["xa_norm"], name=f"{tag}_rms")
    m = rms_fwd(mem, w["mem_norm"], name=f"{tag}_mrms")
    q = mm(h, w["xa_wq"], name=f"{tag}_q", out_dtype=BF16)
    k = mm(m, w["xa_wk"], name=f"{tag}_k", out_dtype=BF16)
    v = mm(m, w["xa_wv"], name=f"{tag}_v", out_dtype=BF16)
    o = xa_fwd(q, k, v, name=f"{tag}_attn")
    y = mm(o, w["xa_wo"], name=f"{tag}_o", res=x)
    return y, (x, h, m, q, k, v, o)


def xattn_bwd(dx, mem, saved, w, tag):
    x, h, m, q, k, v, o = saved
    grads = {"xa_wo": mm_tn(o, dx, name=f"{tag}_dwo")}
    do = mm(dx, w["xa_wo"], name=f"{tag}_do", bt=True)
    dq, dk, dv = xa_bwd(q, k, v, do, name=f"{tag}_dattn")
    grads["xa_wq"] = mm_tn(h, dq, name=f"{tag}_dwq")
    grads["xa_wk"] = mm_tn(m, dk, name=f"{tag}_dwk")
    grads["xa_wv"] = mm_tn(m, dv, name=f"{tag}_dwv")
    dm = mm(dk, w["xa_wk"], name=f"{tag}_dm1", bt=True)
    dm = mm(dv, w["xa_wv"], name=f"{tag}_dm2", res=dm, bt=True)
    _, grads["mem_norm"] = rms_bwd(mem, w["mem_norm"], dm, None, name=f"{tag}_dmrms")
    dx, dg = mm(dq, w["xa_wq"], name=f"{tag}_dh", extras=(x, dx), row_extras=(w["xa_norm"],),
                epilogue=_rms_grad, out_dtypes=(F32,), n_sums=1, bt=True)
    grads["xa_norm"] = _sum_tiles(dg)
    return dx, grads


ANY_SPEC = pl.BlockSpec(memory_space=pl.ANY)


def all_gather(xs, *, name):
    n = len(xs)

    def body(*refs):
        comm = (refs[:n], refs[n:2 * n]) + tuple(refs[2 * n:])
        gather_start(comm)
        gather_finish(comm)

    return pl.pallas_call(
        body, name=name, out_shape=[jax.ShapeDtypeStruct((N_DEV,) + x.shape, x.dtype) for x in xs],
        in_specs=[ANY_SPEC] * n, out_specs=[ANY_SPEC] * n, scratch_shapes=_comm_scratch(n),
    )(*xs)


def all_to_all(gs, *, name):
    n = len(gs)

    def body(*refs):
        comm = (refs[:n], refs[n:2 * n]) + tuple(refs[2 * n:])
        scatter_start(comm)
        scatter_finish(comm)

    return pl.pallas_call(
        body, name=name, out_shape=[jax.ShapeDtypeStruct(g.shape, g.dtype) for g in gs],
        in_specs=[ANY_SPEC] * n, out_specs=[ANY_SPEC] * n, scratch_shapes=_comm_scratch(n),
    )(*gs)


ADAM_BLOCK_ELEMS = 256 * 1024


def reduce_adamw(parts, row0, w, m, v, *, name, wrow0=0, rows=None, prev=None):
    Rw, C = w.shape
    R = Rw if rows is None else rows
    n_parts = parts.shape[0]
    tr = R
    for cand in (512, 256, 128, 64, 32, 16):
        if R % cand == 0 and row0 % cand == 0 and wrow0 % cand == 0 and cand * C <= ADAM_BLOCK_ELEMS:
            tr = cand
            break
    assert row0 % tr == 0 and wrow0 % tr == 0
    assert tr % 16 == 0 or (row0 == 0 and wrow0 == 0 and parts.shape[1] == R == Rw)
    bc1 = 1.0 - ADAM_B1 ** ADAM_STEP
    bc2 = 1.0 - ADAM_B2 ** ADAM_STEP

    def body(p_ref, w_ref, m_ref, v_ref, *rest):
        g_ref, d_ref, nm_ref, nv_ref = rest[-4:]
        g = p_ref[0].astype(F32)
        for d in range(1, n_parts):
            g = g + p_ref[d].astype(F32)
        nm = ADAM_B1 * m_ref[...] + (1.0 - ADAM_B1) * g
        nv = ADAM_B2 * v_ref[...] + (1.0 - ADAM_B2) * (g * g)
        m_hat = nm / bc1
        v_hat = nv / bc2
        g_ref[...] = g
        d_ref[...] = -ADAM_LR * (m_hat / (jnp.sqrt(v_hat) + ADAM_EPS) + ADAM_WD * w_ref[...])
        nm_ref[...] = nm
        nv_ref[...] = nv

    blk0, wblk0 = row0 // tr, wrow0 // tr
    row = pl.BlockSpec((tr, C), lambda i: (wblk0 + i, 0))
    carried = () if prev is None else tuple(prev)
    return pl.pallas_call(
        body, name=name, grid=(R // tr,),
        in_specs=[pl.BlockSpec((n_parts, tr, C), lambda i: (0, blk0 + i, 0)), row, row, row]
        + [ANY_SPEC] * len(carried),
        out_specs=[row] * 4, out_shape=[jax.ShapeDtypeStruct((Rw, C), F32)] * 4,
        input_output_aliases={4 + k: k for k in range(len(carried))},
        compiler_params=_params("parallel"),
    )(parts, w, m, v, *carried)


SHARDED = {
    "ffn1_w_gate": 2, "ffn1_w_up": 2, "ffn1_w_down": 1, "w_in": 2, "conv_w": 2, "w_branch": 3, "w_out": 1,
    "xa_wq": 1, "xa_wk": 1, "xa_wv": 1, "xa_wo": 1, "ffn2_w_gate": 2, "ffn2_w_up": 2, "ffn2_w_down": 1,
}
GROUPS = (("ffn1_w_gate", "ffn1_w_up", "ffn2_w_gate", "ffn2_w_up"),
          ("ffn1_w_down", "ffn2_w_down", "w_out", "xa_wq", "xa_wk", "xa_wv", "xa_wo"),
          ("w_in",), ("w_branch",))
REPLICATED = ("ffn1_norm", "mix_norm", "sg_norm", "sg_w", "sg_b", "fox_b_f", "xa_norm", "mem_norm", "ffn2_norm",
              "final_norm")
WEIGHTS = ("ffn1_norm", "ffn1_w_gate", "ffn1_w_up", "ffn1_w_down", "mix_norm", "w_in", "conv_w", "sg_norm", "sg_w",
           "sg_b", "fox_b_f", "w_branch", "w_out", "xa_norm", "mem_norm", "xa_wq", "xa_wk", "xa_wv", "xa_wo",
           "ffn2_norm", "ffn2_w_gate", "ffn2_w_up", "ffn2_w_down", "final_norm")
PACK_ROWS = 1024


def _rows(a):
    return a.reshape(-1, a.shape[-1])


def _pack(arrays, dtype):
    flat = jnp.concatenate([a.reshape(-1).astype(dtype) for a in arrays])
    n = flat.shape[0]
    unit = PACK_ROWS * LANES
    total = -(-n // unit) * unit
    return jnp.pad(flat, (0, total - n)).reshape(total // LANES, LANES)


def _unpack(buf, shapes):
    flat = buf.reshape(-1)
    out, off = [], 0
    for shp in shapes:
        n = 1
        for s in shp:
            n *= s
        out.append(flat[off:off + n].reshape(tuple(shp)))
        off += n
    return out


def _to_dev_major(full, axis):
    shp = full.shape
    a = full.reshape(shp[:axis] + (N_DEV, shp[axis] // N_DEV) + shp[axis + 1:])
    return jnp.moveaxis(a, axis, 0)


def _from_dev_major(a, axis):
    a = jnp.moveaxis(a, 0, axis)
    shp = a.shape
    return a.reshape(shp[:axis] + (shp[axis] * shp[axis + 1],) + shp[axis + 2:])


def _relayout_w_in(w_in, D):
    main = 8 * SEG
    pad = jnp.zeros((w_in.shape[0], SEG - FOX_HEADS), w_in.dtype)
    return jnp.concatenate([w_in[:, main + FOX_HEADS:], w_in[:, :main], w_in[:, main:main + FOX_HEADS], pad], axis=1)


def _unlayout_w_in(g, D):
    return jnp.concatenate([g[:, 3 * D:3 * D + 8 * SEG], g[:, 3 * D + 8 * SEG:3 * D + 8 * SEG + FOX_HEADS],
                            g[:, :3 * D]], axis=1)


def _layer_weights(full, rep, l, D):
    ffn = {}
    for tag in ("ffn1", "ffn2"):
        ffn[tag] = {"norm": rep[f"{tag}_norm"][l][None, :]}
        for n in ("w_gate", "w_up", "w_down"):
            ffn[tag][n] = full[f"{tag}_{n}"]
    mix = {
        "mix_norm": rep["mix_norm"][l][None, :], "w_in": _relayout_w_in(full["w_in"], D),
        "conv_w": full["conv_w"], "sg_norm": rep["sg_norm"][l][None, :], "sg_w": rep["sg_w"][l],
        "sg_bt": rep["sg_b"][l][:, :, None],
        "fox_b_f": jnp.pad(rep["fox_b_f"][l], (0, LANES - FOX_HEADS))[None, :],
        "w_branch": full["w_branch"], "w_out": full["w_out"],
    }
    xa = {"xa_norm": rep["xa_norm"][l][None, :], "mem_norm": rep["mem_norm"][l][None, :]}
    for n in ("xa_wq", "xa_wk", "xa_wv", "xa_wo"):
        xa[n] = full[n]
    return ffn, mix, xa


def kernel(x, mem, ffn1_norm, ffn1_w_gate, ffn1_w_up, ffn1_w_down, mix_norm, w_in, conv_w, sg_norm, sg_w, sg_b, fox_b_f, w_branch, w_out, xa_norm, mem_norm, xa_wq, xa_wk, xa_wv, xa_wo, ffn2_norm, ffn2_w_gate, ffn2_w_up, ffn2_w_down, final_norm, loss_target, m_ffn1_norm, m_ffn1_w_gate, m_ffn1_w_up, m_ffn1_w_down, m_mix_norm, m_w_in, m_conv_w, m_sg_norm, m_sg_w, m_sg_b, m_fox_b_f, m_w_branch, m_w_out, m_xa_norm, m_mem_norm, m_xa_wq, m_xa_wk, m_xa_wv, m_xa_wo, m_ffn2_norm, m_ffn2_w_gate, m_ffn2_w_up, m_ffn2_w_down, m_final_norm, v_ffn1_norm, v_ffn1_w_gate, v_ffn1_w_up, v_ffn1_w_down, v_mix_norm, v_w_in, v_conv_w, v_sg_norm, v_sg_w, v_sg_b, v_fox_b_f, v_w_branch, v_w_out, v_xa_norm, v_mem_norm, v_xa_wq, v_xa_wk, v_xa_wv, v_xa_wo, v_ffn2_norm, v_ffn2_w_gate, v_ffn2_w_up, v_ffn2_w_down, v_final_norm):
    args = locals()
    wts = {n: args[n] for n in WEIGHTS}
    mom = {n: args["m_" + n] for n in WEIGHTS}
    var = {n: args["v_" + n] for n in WEIGHTS}
    depth = ffn1_norm.shape[0]
    S, D = x.shape[1], x.shape[2]
    xs, ms, tgt = x[0], mem[0], loss_target[0]

    layer_rows = {n: _rows(wts[n][0]).shape[0] for n in SHARDED}
    row0 = {}
    for grp in GROUPS:
        off = 0
        for n in grp:
            row0[n] = off
            off += layer_rows[n]

    def weight_groups(l):
        return [jnp.concatenate([_rows(wts[n][l]).astype(BF16) for n in grp]) for grp in GROUPS]

    def gathered_layer(got, l):
        full = {"conv_w": conv_full[l]}
        for grp, arr in zip(GROUPS, got):
            for n in grp:
                block = arr[:, row0[n]:row0[n] + layer_rows[n]].reshape((N_DEV,) + wts[n].shape[1:])
                full[n] = _from_dev_major(block, SHARDED[n] - 1)
        return full

    def grad_groups(g):
        def dev_major_rows(n):
            a = _to_dev_major(g[n], SHARDED[n] - 1)
            return a.reshape(N_DEV, -1, a.shape[-1]).astype(BF16)
        return [jnp.concatenate([dev_major_rows(n) for n in grp], axis=1) for grp in GROUPS]

    got = all_gather(weight_groups(0) + [_rows(conv_w)], name="gather_weights")
    conv_full = _from_dev_major(got[-1].reshape((N_DEV,) + conv_w.shape), SHARDED["conv_w"])
    full = gathered_layer(got[:-1], 0)
    rep = {n: wts[n] for n in REPLICATED}

    saved, layers = [], []
    h = xs
    for l in range(depth):
        ffn, mix, xa = _layer_weights(full, rep, l, D)
        layers.append((ffn, mix, xa))
        h, s1 = ffn_fwd(h, ffn["ffn1"], f"l{l}_ffn1")
        h, s2, got = mixer_fwd(h, mix, f"l{l}_mix", gather=weight_groups(l + 1) if l + 1 < depth else ())
        if l + 1 < depth:
            full = gathered_layer(got, l + 1)
        h, s3 = xattn_fwd(h, ms, xa, f"l{l}_xa")
        h, s4 = ffn_fwd(h, ffn["ffn2"], f"l{l}_ffn2")
        saved.append((s1, s2, s3, s4))
    dx, d_final, loss_cols = final_loss_bwd(h, final_norm[None, :], tgt, name="final_loss")
    loss = lax.psum(0.5 * jnp.sum(loss_cols) / D, ("x", "y", "c"))

    per_layer, parts, pending = [], {}, ()
    for l in reversed(range(depth)):
        ffn, mix, xa = layers[l]
        s1, s2, s3, s4 = saved[l]
        g = {}
        dx, g4 = ffn_bwd(dx, s4, ffn["ffn2"], f"l{l}_ffn2")
        dx, g3 = xattn_bwd(dx, ms, s3, xa, f"l{l}_xa")
        dx, g2, got = mixer_bwd(dx, s2, mix, f"l{l}_mix", scatter=pending)
        if pending:
            parts[l + 1] = got
        dx, g1 = ffn_bwd(dx, s1, ffn["ffn1"], f"l{l}_ffn1")
        for tag, gg in (("ffn1", g1), ("ffn2", g4)):
            for n in ("w_gate", "w_up", "w_down"):
                g[f"{tag}_{n}"] = gg[n]
            g[f"{tag}_norm"] = gg["norm"][0]
        g.update(g3)
        g["xa_norm"], g["mem_norm"] = g3["xa_norm"][0], g3["mem_norm"][0]
        g.update({k: v for k, v in g2.items() if k != "w_in"})
        g["w_in"] = _unlayout_w_in(g2["w_in"], D)
        g["mix_norm"], g["sg_norm"] = g2["mix_norm"][0], g2["sg_norm"][0]
        per_layer.append(g)
        pending = grad_groups(g)
    parts[0] = all_to_all(pending, name="scatter_grads")
    per_layer.reverse()
    grads = {n: jnp.stack([per_layer[l][n] for l in range(depth)]) for n in REPLICATED + ("conv_w",)
             if n != "final_norm"}
    grads["final_norm"] = d_final[0]

    res = {k: {} for k in ("g", "d", "m", "v")}
    for gi, grp in enumerate(GROUPS):
        for n in grp:
            outs = None
            for l in range(depth):
                outs = reduce_adamw(parts[l][gi], row0[n], _rows(wts[n]), _rows(mom[n]), _rows(var[n]),
                                    name=f"adamw_{n}_l{l}", wrow0=l * layer_rows[n], rows=layer_rows[n], prev=outs)
            for k, o in zip(("g", "d", "m", "v"), outs):
                res[k][n] = o.reshape(wts[n].shape)

    small = list(REPLICATED)
    shapes = [wts[n].shape for n in small]
    conv_zero = jnp.zeros(grads["conv_w"].shape, F32)
    parts = all_gather([_pack([grads[n] for n in small] + [grads["conv_w"]], F32)], name="gather_small_grads")[0]
    outs = reduce_adamw(parts, 0, _pack([wts[n] for n in small] + [conv_zero], F32),
                        _pack([mom[n] for n in small] + [conv_zero], F32),
                        _pack([var[n] for n in small] + [conv_zero], F32), name="adamw_replicated")
    for k, o in zip(("g", "d", "m", "v"), outs):
        res[k].update(dict(zip(small, _unpack(o, shapes + [conv_zero.shape])[:-1])))
    conv_g = _unpack(outs[0], shapes + [conv_zero.shape])[-1]
    me = 4 * lax.axis_index("x") + 2 * lax.axis_index("y") + lax.axis_index("c")
    width = conv_w.shape[-1]
    conv_g = lax.dynamic_slice_in_dim(conv_g, me * width, width, axis=2)
    outs = reduce_adamw(_rows(conv_g)[None], 0, _rows(conv_w), _rows(mom["conv_w"]), _rows(var["conv_w"]),
                        name="adamw_conv_w")
    for k, o in zip(("g", "d", "m", "v"), outs):
        res[k]["conv_w"] = o.reshape(conv_w.shape)

    return (loss, dx[None], *[res["g"][n] for n in WEIGHTS], *[res["d"][n] for n in WEIGHTS],
            *[res["m"][n] for n in WEIGHTS], *[res["v"][n] for n in WEIGHTS])
```

```python
import functools

import jax
import jax.numpy as jnp
from jax import lax
from jax.experimental import pallas as pl
from jax.experimental.pallas import tpu as pltpu

F32 = jnp.float32
BF16 = jnp.bfloat16

N_DEV = 8
RMS_EPS = 1e-6
SEG = 512
FOX_HEADS = 8
FOX_HEAD_DIM = 64
SG_GROUPS = 4
CHUNK = 128
XA_HEADS = 4
N_BRANCH = 3
LANES = 128
VMEM_LIMIT_BYTES = 48 * 1024 * 1024
NEG_BIG = -1e30

ADAM_LR = 0.001
ADAM_B1 = 0.9
ADAM_B2 = 0.999
ADAM_EPS = 1e-08
ADAM_WD = 0.01
ADAM_STEP = 10

_GELU_K = 0.7978845608028654
_GELU_C = 0.044715

MESH_IDS = pl.DeviceIdType.MESH


def _pick(n, candidates):
    for c in candidates:
        if c <= n and n % c == 0:
            return c
    return n


def _params(*sem):
    return pltpu.CompilerParams(dimension_semantics=sem, vmem_limit_bytes=VMEM_LIMIT_BYTES)


def _sig(x):
    return 1.0 / (1.0 + jnp.exp(-x))


def _gelu(x):
    t = jnp.tanh(_GELU_K * (x + _GELU_C * x * x * x))
    return 0.5 * x * (1.0 + t), t


def _gelu_grad(x, t):
    return 0.5 * (1.0 + t) + 0.5 * x * (1.0 - t * t) * _GELU_K * (1.0 + 3.0 * _GELU_C * x * x)


_WIDE_TILES = (1536, 1408, 1280, 1024, 768, 512, 384, 256, 128)
MM_VMEM_BUDGET = 36 * 1024 * 1024
MM_ACC_BYTES = 6 * 1024 * 1024 + 512 * 1024


SUM_ROWS = 8


def mm(a, b, *, name, out_dtype=F32, res=None, scale=1.0, extras=(), row_extras=(), epilogue=None, out_dtypes=None,
       n_sums=0, tm=1024):
    M, K = a.shape
    K2, N = b.shape
    assert K == K2
    custom = epilogue is not None
    if not custom:
        extras = () if res is None else (res,)
        out_dtypes = (out_dtype,)

        def epilogue(acc, *ex):
            if scale != 1.0:
                acc = acc * scale
            return ((ex[0] + acc) if ex else acc,)

    n_ex, n_rx, n_out = len(extras), len(row_extras), len(out_dtypes)
    tn = _pick(N, _WIDE_TILES)
    tk = K if K <= 3072 else _pick(K, (2560, 2048, 1536, 1024, 512, 256, 128))
    nk = K // tk
    tile_bytes = sum(e.dtype.itemsize for e in extras) + sum(jnp.dtype(d).itemsize for d in out_dtypes)
    for tm in (tm, 512, 256, 128):
        blocks = 2 * (tm * tk * a.dtype.itemsize + tk * tn * 2 + tm * tn * tile_bytes)
        if M % tm == 0 and blocks + (tm * tn * 4 if nk > 1 else 0) <= MM_VMEM_BUDGET:
            break
    else:
        tm = M

    def body(*refs):
        a_ref, b_ref = refs[:2]
        ex_refs = refs[2:2 + n_ex + n_rx]
        o_refs = refs[2 + n_ex + n_rx:2 + n_ex + n_rx + n_out]
        s_refs = refs[2 + n_ex + n_rx + n_out:2 + n_ex + n_rx + n_out + n_sums]

        def finish(acc):
            vals = epilogue(acc, *[r[...] for r in ex_refs])
            assert len(vals) == n_out + n_sums
            for o_ref, val, dt in zip(o_refs, vals[:n_out], out_dtypes, strict=True):
                o_ref[...] = val.astype(dt)
            for s_ref, val in zip(s_refs, vals[n_out:], strict=True):
                s_ref[...] = jnp.broadcast_to(val, (SUM_ROWS, tn))

        part = jnp.dot(a_ref[...].astype(BF16), b_ref[...].astype(BF16), preferred_element_type=F32)
        if nk == 1:
            finish(part)
        else:
            acc_ref = refs[-1]
            k = pl.program_id(2)

            @pl.when(k == 0)
            def _():
                acc_ref[...] = part

            @pl.when(k > 0)
            def _():
                acc_ref[...] += part

            @pl.when(k == nk - 1)
            def _():
                finish(acc_ref[...])

    tile = pl.BlockSpec((tm, tn), lambda i, j, k: (i, j))
    outs = pl.pallas_call(
        body, name=name, grid=(M // tm, N // tn, nk),
        in_specs=[pl.BlockSpec((tm, tk), lambda i, j, k: (i, k)), pl.BlockSpec((tk, tn), lambda i, j, k: (k, j))]
        + [tile] * n_ex + [pl.BlockSpec((1, tn), lambda i, j, k: (0, j))] * n_rx,
        out_specs=[tile] * n_out + [pl.BlockSpec((SUM_ROWS, tn), lambda i, j, k: (i, j))] * n_sums,
        out_shape=[jax.ShapeDtypeStruct((M, N), d) for d in out_dtypes]
        + [jax.ShapeDtypeStruct((M // tm * SUM_ROWS, N), F32)] * n_sums,
        scratch_shapes=[pltpu.VMEM((tm, tn), F32)] if nk > 1 else [],
        compiler_params=_params("parallel", "parallel", "arbitrary"),
    )(a, b, *extras, *row_extras)
    return tuple(outs) if custom else outs[0]


def _sum_tiles(s):
    return jnp.sum(s.reshape(-1, SUM_ROWS, s.shape[-1])[:, 0], axis=0, keepdims=True)


def _rms_grad(dh, *ex):
    *add, x, dx_in, g = ex
    for extra in add:
        dh = dh + extra
    r = lax.rsqrt(jnp.mean(x * x, axis=-1, keepdims=True) + RMS_EPS)
    xh = x * r
    gd = dh * g
    dx = dx_in + r * (gd - xh * jnp.mean(gd * xh, axis=-1, keepdims=True))
    return dx, jnp.sum(dh * xh, axis=0, keepdims=True)


def mm_tn(a, b, *, name, scale=1.0):
    M, K = a.shape
    M2, N = b.shape
    assert M == M2
    tm = _pick(M, (1024, 512, 256, 128))
    tk = _pick(K, (1408, 1024, 512, 256, 128))
    tn = next((c for c in _WIDE_TILES if N % c == 0 and tk * c * 4 <= MM_ACC_BYTES), N)
    nm = M // tm

    def body(a_ref, b_ref, o_ref):
        m = pl.program_id(2)
        part = lax.dot_general(a_ref[...].astype(BF16), b_ref[...].astype(BF16), (((0,), (0,)), ((), ())),
                               preferred_element_type=F32)

        @pl.when(m == 0)
        def _():
            o_ref[...] = part

        @pl.when(m > 0)
        def _():
            o_ref[...] += part

        if scale != 1.0:
            @pl.when(m == nm - 1)
            def _():
                o_ref[...] = o_ref[...] * scale

    return pl.pallas_call(
        body, name=name, grid=(K // tk, N // tn, nm),
        in_specs=[pl.BlockSpec((tm, tk), lambda i, j, m: (m, i)), pl.BlockSpec((tm, tn), lambda i, j, m: (m, j))],
        out_specs=pl.BlockSpec((tk, tn), lambda i, j, m: (i, j)),
        out_shape=jax.ShapeDtypeStruct((K, N), F32),
        compiler_params=_params("parallel", "parallel", "arbitrary"),
    )(a, b)


def rms_fwd(x, g, *, name):
    S, D = x.shape
    ts = _pick(S, (512, 256, 128))

    def body(x_ref, g_ref, h_ref):
        xv = x_ref[...]
        r = lax.rsqrt(jnp.mean(xv * xv, axis=-1, keepdims=True) + RMS_EPS)
        h_ref[...] = ((xv * r) * g_ref[...]).astype(BF16)

    return pl.pallas_call(
        body, name=name, grid=(S // ts,),
        in_specs=[pl.BlockSpec((ts, D), lambda i: (i, 0)), pl.BlockSpec((1, D), lambda i: (0, 0))],
        out_specs=pl.BlockSpec((ts, D), lambda i: (i, 0)),
        out_shape=jax.ShapeDtypeStruct((S, D), BF16),
        compiler_params=_params("parallel"),
    )(x, g)


def rms_bwd(x, g, dh, dx_in, *, name):
    S, D = x.shape
    ts = _pick(S, (512, 256, 128))
    has_in = dx_in is not None

    def body(*refs):
        if has_in:
            x_ref, g_ref, dh_ref, di_ref, dx_ref, dg_ref = refs
        else:
            x_ref, g_ref, dh_ref, dx_ref, dg_ref = refs
        xv = x_ref[...]
        dh_v = dh_ref[...]
        r = lax.rsqrt(jnp.mean(xv * xv, axis=-1, keepdims=True) + RMS_EPS)
        xh = xv * r
        gd = dh_v * g_ref[...]
        dx = r * (gd - xh * jnp.mean(gd * xh, axis=-1, keepdims=True))
        if has_in:
            dx = di_ref[...] + dx
        dx_ref[...] = dx
        part = jnp.sum(dh_v * xh, axis=0, keepdims=True)

        @pl.when(pl.program_id(0) == 0)
        def _():
            dg_ref[...] = part

        @pl.when(pl.program_id(0) > 0)
        def _():
            dg_ref[...] += part

    row = pl.BlockSpec((ts, D), lambda i: (i, 0))
    vec = pl.BlockSpec((1, D), lambda i: (0, 0))
    return pl.pallas_call(
        body, name=name, grid=(S // ts,),
        in_specs=[row, vec, row] + ([row] if has_in else []),
        out_specs=[row, vec],
        out_shape=[jax.ShapeDtypeStruct((S, D), F32), jax.ShapeDtypeStruct((1, D), F32)],
        compiler_params=_params("arbitrary"),
    )(*([x, g, dh] + ([dx_in] if has_in else [])))


def final_loss_bwd(x, g, target, *, name):
    S, D = x.shape
    ts = _pick(S, (512, 256, 128))

    def body(x_ref, g_ref, t_ref, dx_ref, dg_ref, ls_ref):
        xv = x_ref[...]
        gv = g_ref[...]
        r = lax.rsqrt(jnp.mean(xv * xv, axis=-1, keepdims=True) + RMS_EPS)
        xh = xv * r
        e = xh * gv - t_ref[...]
        dy = e * (1.0 / D)
        gd = dy * gv
        dx_ref[...] = r * (gd - xh * jnp.mean(gd * xh, axis=-1, keepdims=True))
        dg_part = jnp.sum(dy * xh, axis=0, keepdims=True)
        ls_part = jnp.sum(e * e, axis=0, keepdims=True)

        @pl.when(pl.program_id(0) == 0)
        def _():
            dg_ref[...] = dg_part
            ls_ref[...] = ls_part

        @pl.when(pl.program_id(0) > 0)
        def _():
            dg_ref[...] += dg_part
            ls_ref[...] += ls_part

    row = pl.BlockSpec((ts, D), lambda i: (i, 0))
    vec = pl.BlockSpec((1, D), lambda i: (0, 0))
    return pl.pallas_call(
        body, name=name, grid=(S // ts,), in_specs=[row, vec, row], out_specs=[row, vec, vec],
        out_shape=[jax.ShapeDtypeStruct((S, D), F32), jax.ShapeDtypeStruct((1, D), F32),
                   jax.ShapeDtypeStruct((1, D), F32)],
        compiler_params=_params("arbitrary"),
    )(x, g, target)


def _swiglu(up, gp):
    gp = gp.astype(F32)
    return up, gp * _sig(gp) * up


def _swiglu_grad(da, gp, up):
    da = da * 0.5
    gp, up = gp.astype(F32), up.astype(F32)
    s = _sig(gp)
    return da * up * (s * (1.0 + gp * (1.0 - s))), da * (gp * s)


def ffn_fwd(x, w, tag):
    h = rms_fwd(x, w["norm"], name=f"{tag}_rms")
    gp = mm(h, w["w_gate"], name=f"{tag}_gate", out_dtype=BF16)
    up, a = mm(h, w["w_up"], name=f"{tag}_up", extras=(gp,), epilogue=_swiglu, out_dtypes=(BF16, BF16))
    y = mm(a, w["w_down"], name=f"{tag}_down", res=x, scale=0.5)
    return y, (x, h, gp, up, a)


def ffn_bwd(dx, saved, w, tag):
    x, h, gp, up, a = saved
    grads = {"w_down": mm_tn(a, dx, name=f"{tag}_dwd", scale=0.5)}
    dgp, dup = mm(dx, w["w_down_t"], name=f"{tag}_da", extras=(gp, up), epilogue=_swiglu_grad,
                  out_dtypes=(BF16, BF16))
    grads["w_gate"] = mm_tn(h, dgp, name=f"{tag}_dwg")
    grads["w_up"] = mm_tn(h, dup, name=f"{tag}_dwu")
    dh = mm(dgp, w["w_gate_t"], name=f"{tag}_dh1")
    dx, dg = mm(dup, w["w_up_t"], name=f"{tag}_dh2", extras=(dh, x, dx), row_extras=(w["norm"],),
                epilogue=_rms_grad, out_dtypes=(F32,), n_sums=1)
    grads["norm"] = _sum_tiles(dg)
    return dx, grads


SEG_AB, SEG_AC, SEG_AH, SEG_U, SEG_V, SEG_FQ, SEG_FK, SEG_FV, SEG_FL = range(9)
N_SEG = 9


def _seg_block(D, seg):
    return 3 * D // SEG + seg


def _shift_down(z, prev8, n, rows):
    out = pltpu.roll(z, n, 0)
    for r in range(n):
        out = jnp.where(rows == r, prev8[8 - n + r:8 - n + r + 1, :], out)
    return out


def _shift_up(z, next8, n, rows, ts):
    out = pltpu.roll(z, ts - n, 0)
    for r in range(n):
        out = jnp.where(rows == ts - n + r, next8[r:r + 1, :], out)
    return out


def conv_fwd(proj, conv_w, D, *, name):
    S = proj.shape[0]
    ts = _pick(S, (512, 256, 128))
    b0 = _seg_block(D, 0)

    def body(ab_ref, ac_ref, ah_ref, pc_ref, ph_ref, w_ref, y_ref):
        i = pl.program_id(0)
        rows = lax.broadcasted_iota(jnp.int32, (ts, 1), 0)
        z = ac_ref[...] * ah_ref[...]
        zp = pc_ref[...] * ph_ref[...] * (i > 0).astype(F32)
        w = w_ref[...]
        y = w[0:1, :] * _shift_down(z, zp, 2, rows) + w[1:2, :] * _shift_down(z, zp, 1, rows) + w[2:3, :] * z
        y_ref[...] = (ab_ref[...] * y).astype(BF16)

    def seg(s):
        return pl.BlockSpec((ts, SEG), lambda i: (i, b0 + s))

    def prev(s):
        return pl.BlockSpec((8, SEG), lambda i: (jnp.maximum(i * (ts // 8) - 1, 0), b0 + s))

    return pl.pallas_call(
        body, name=name, grid=(S // ts,),
        in_specs=[seg(SEG_AB), seg(SEG_AC), seg(SEG_AH), prev(SEG_AC), prev(SEG_AH),
                  pl.BlockSpec((3, SEG), lambda i: (0, 0))],
        out_specs=pl.BlockSpec((ts, SEG), lambda i: (i, 0)),
        out_shape=jax.ShapeDtypeStruct((S, SEG), BF16), compiler_params=_params("parallel"),
    )(proj, proj, proj, proj, proj, conv_w)


def conv_bwd(proj, conv_w, dy, D, *, name):
    S = proj.shape[0]
    ts = _pick(S, (512, 256, 128))
    nt = S // ts
    b0 = _seg_block(D, 0)

    def body(ab_ref, ac_ref, ah_ref, pc_ref, ph_ref, nb_ref, dy_ref, ndy_ref, w_ref,
             dab_ref, dac_ref, dah_ref, dw_ref):
        i = pl.program_id(0)
        rows = lax.broadcasted_iota(jnp.int32, (ts, 1), 0)
        ab, ac, ah = ab_ref[...], ac_ref[...], ah_ref[...]
        z = ac * ah
        zp = pc_ref[...] * ph_ref[...] * (i > 0).astype(F32)
        w = w_ref[...]
        z1 = _shift_down(z, zp, 1, rows)
        z2 = _shift_down(z, zp, 2, rows)
        y = w[0:1, :] * z2 + w[1:2, :] * z1 + w[2:3, :] * z
        dyv = dy_ref[...]
        dab_ref[...] = (dyv * y).astype(BF16)
        dyy = dyv * ab
        nyy = ndy_ref[...] * nb_ref[...] * (i < nt - 1).astype(F32)
        dz = (w[2:3, :] * dyy + w[1:2, :] * _shift_up(dyy, nyy, 1, rows, ts)
              + w[0:1, :] * _shift_up(dyy, nyy, 2, rows, ts))
        dac_ref[...] = (dz * ah).astype(BF16)
        dah_ref[...] = (dz * ac).astype(BF16)
        parts = [jnp.sum(dyy * zz, axis=0, keepdims=True) for zz in (z2, z1, z)]

        @pl.when(i == 0)
        def _():
            for k in range(3):
                dw_ref[k:k + 1, :] = parts[k]

        @pl.when(i > 0)
        def _():
            for k in range(3):
                dw_ref[k:k + 1, :] += parts[k]

    def seg(s):
        return pl.BlockSpec((ts, SEG), lambda i: (i, b0 + s))

    def prev(s):
        return pl.BlockSpec((8, SEG), lambda i: (jnp.maximum(i * (ts // 8) - 1, 0), b0 + s))

    nxt_row = lambda i: jnp.minimum((i + 1) * (ts // 8), S // 8 - 1)
    out_row = pl.BlockSpec((ts, SEG), lambda i: (i, 0))
    return pl.pallas_call(
        body, name=name, grid=(nt,),
        in_specs=[seg(SEG_AB), seg(SEG_AC), seg(SEG_AH), prev(SEG_AC), prev(SEG_AH),
                  pl.BlockSpec((8, SEG), lambda i: (nxt_row(i), b0 + SEG_AB)),
                  out_row, pl.BlockSpec((8, SEG), lambda i: (nxt_row(i), 0)),
                  pl.BlockSpec((3, SEG), lambda i: (0, 0))],
        out_specs=[out_row, out_row, out_row, pl.BlockSpec((3, SEG), lambda i: (0, 0))],
        out_shape=[jax.ShapeDtypeStruct((S, SEG), BF16)] * 3 + [jax.ShapeDtypeStruct((3, SEG), F32)],
        compiler_params=_params("arbitrary"),
    )(proj, proj, proj, proj, proj, proj, dy, dy, conv_w)


def _tril_mask():
    r = lax.broadcasted_iota(jnp.int32, (CHUNK, CHUNK), 0)
    c = lax.broadcasted_iota(jnp.int32, (CHUNK, CHUNK), 1)
    return c <= r


def sg_fwd(proj, sg_norm, sg_w, sg_bt, D, *, name):
    S = proj.shape[0]
    ts = _pick(S, (512, 256, 128))
    b0 = _seg_block(D, 0)

    def body(u_ref, v_ref, gs_ref, w_ref, b_ref, y_ref):
        ug, _ = _gelu(u_ref[...])
        vg, _ = _gelu(v_ref[...])
        vn = ((vg * lax.rsqrt(jnp.mean(vg * vg, axis=-1, keepdims=True) + RMS_EPS)) * gs_ref[...]).astype(BF16)
        mask = _tril_mask()
        for g in range(SG_GROUPS):
            wg = jnp.where(mask, w_ref[g], 0.0).astype(BF16)
            cols = slice(g * CHUNK, (g + 1) * CHUNK)
            for n in range(ts // CHUNK):
                rws = slice(n * CHUNK, (n + 1) * CHUNK)
                sv = jnp.dot(wg, vn[rws, cols], preferred_element_type=F32) + b_ref[g]
                y_ref[rws, cols] = (ug[rws, cols] * sv).astype(BF16)

    seg = lambda s: pl.BlockSpec((ts, SEG), lambda i: (i, b0 + s))
    return pl.pallas_call(
        body, name=name, grid=(S // ts,),
        in_specs=[seg(SEG_U), seg(SEG_V), pl.BlockSpec((1, SEG), lambda i: (0, 0)),
                  pl.BlockSpec((SG_GROUPS, CHUNK, CHUNK), lambda i: (0, 0, 0)),
                  pl.BlockSpec((SG_GROUPS, CHUNK, 1), lambda i: (0, 0, 0))],
        out_specs=pl.BlockSpec((ts, SEG), lambda i: (i, 0)),
        out_shape=jax.ShapeDtypeStruct((S, SEG), BF16), compiler_params=_params("parallel"),
    )(proj, proj, sg_norm, sg_w, sg_bt)


def sg_bwd(proj, sg_norm, sg_w, sg_bt, dy, D, *, name):
    S = proj.shape[0]
    ts = _pick(S, (512, 256, 128))
    nt = S // ts
    b0 = _seg_block(D, 0)

    def body(u_ref, v_ref, dy_ref, gs_ref, w_ref, b_ref, du_ref, dv_ref, dw_ref, db_ref, dgs_ref, dvn_sc):
        i = pl.program_id(0)
        uv, vv, dyv = u_ref[...], v_ref[...], dy_ref[...]
        ug, ut = _gelu(uv)
        vg, vt = _gelu(vv)
        r = lax.rsqrt(jnp.mean(vg * vg, axis=-1, keepdims=True) + RMS_EPS)
        vh = vg * r
        gs = gs_ref[...]
        vn = (vh * gs).astype(BF16)
        dsv = dyv * ug
        dsv_b = dsv.astype(BF16)
        mask = _tril_mask()

        @pl.when(i == 0)
        def _():
            dw_ref[...] = jnp.zeros_like(dw_ref)
            db_ref[...] = jnp.zeros_like(db_ref)

        for g in range(SG_GROUPS):
            wg = jnp.where(mask, w_ref[g], 0.0).astype(BF16)
            cols = slice(g * CHUNK, (g + 1) * CHUNK)
            dw_acc = jnp.zeros((CHUNK, CHUNK), F32)
            db_acc = jnp.zeros((CHUNK, 1), F32)
            for n in range(ts // CHUNK):
                rws = slice(n * CHUNK, (n + 1) * CHUNK)
                vblk = vn[rws, cols]
                sv = jnp.dot(wg, vblk, preferred_element_type=F32) + b_ref[g]
                du_ref[rws, cols] = (dyv[rws, cols] * sv * _gelu_grad(uv[rws, cols], ut[rws, cols])).astype(BF16)
                dblk = dsv_b[rws, cols]
                dvn_sc[rws, cols] = lax.dot_general(wg, dblk, (((0,), (0,)), ((), ())), preferred_element_type=F32)
                dw_acc = dw_acc + lax.dot_general(dblk, vblk, (((1,), (1,)), ((), ())), preferred_element_type=F32)
                db_acc = db_acc + jnp.sum(dsv[rws, cols], axis=1, keepdims=True)
            dw_ref[g] += jnp.where(mask, dw_acc, 0.0)
            db_ref[g] += db_acc

        dvn = dvn_sc[...]
        gd = dvn * gs
        dvg = r * (gd - vh * jnp.mean(gd * vh, axis=-1, keepdims=True))
        dv_ref[...] = (dvg * _gelu_grad(vv, vt)).astype(BF16)
        dgs_part = jnp.sum(dvn * vh, axis=0, keepdims=True)

        @pl.when(i == 0)
        def _():
            dgs_ref[...] = dgs_part

        @pl.when(i > 0)
        def _():
            dgs_ref[...] += dgs_part

    seg = lambda s: pl.BlockSpec((ts, SEG), lambda i: (i, b0 + s))
    row = pl.BlockSpec((ts, SEG), lambda i: (i, 0))
    wspec = pl.BlockSpec((SG_GROUPS, CHUNK, CHUNK), lambda i: (0, 0, 0))
    bspec = pl.BlockSpec((SG_GROUPS, CHUNK, 1), lambda i: (0, 0, 0))
    vec = pl.BlockSpec((1, SEG), lambda i: (0, 0))
    return pl.pallas_call(
        body, name=name, grid=(nt,),
        in_specs=[seg(SEG_U), seg(SEG_V), row, vec, wspec, bspec],
        out_specs=[row, row, wspec, bspec, vec],
        out_shape=[jax.ShapeDtypeStruct((S, SEG), BF16)] * 2
        + [jax.ShapeDtypeStruct((SG_GROUPS, CHUNK, CHUNK), F32), jax.ShapeDtypeStruct((SG_GROUPS, CHUNK, 1), F32),
           jax.ShapeDtypeStruct((1, SEG), F32)],
        scratch_shapes=[pltpu.VMEM((ts, SEG), F32)],
        compiler_params=_params("arbitrary"),
    )(proj, proj, dy, sg_norm, sg_w, sg_bt)


def _mesh_pos():
    return lax.axis_index("x"), lax.axis_index("y"), lax.axis_index("c")


def _flip(v, bit):
    return 1 - v if bit else v


def _comm_scratch(n):
    return [pltpu.SemaphoreType.DMA((n, 7)), pltpu.SemaphoreType.DMA((n, 7)), pltpu.SemaphoreType.DMA((n,))]


def _gather_plan(refs):
    x_refs, out_refs, send_sems, recv_sems, local_sems = refs
    n = len(x_refs)
    mx, my, mc = _mesh_pos()
    me, sibling = (mx, my, mc), (mx, my, 1 - mc)
    chips = [(1 - mx, my), (mx, 1 - my), (1 - mx, 1 - my)]

    def copy(a, k, block, to, from_input=False):
        slot = out_refs[a].at[4 * block[0] + 2 * block[1] + block[2]]
        return pltpu.make_async_remote_copy(
            src_ref=x_refs[a] if from_input else slot, dst_ref=slot,
            send_sem=send_sems.at[a, k], recv_sem=recv_sems.at[a, k], device_id=to, device_id_type=MESH_IDS)

    local = [pltpu.make_async_copy(x_refs[a], out_refs[a].at[4 * mx + 2 * my + mc], local_sems.at[a])
             for a in range(n)]
    first = []
    for a in range(n):
        first.append(copy(a, 0, me, sibling, from_input=True))
        first += [copy(a, 1 + j, me, (*chip, mc), from_input=True) for j, chip in enumerate(chips)]
    return n, me, sibling, chips, mc, copy, local, first


def gather_start(refs):
    n, me, sibling, chips, mc, copy, local, first = _gather_plan(refs)
    for cp in local + first:
        cp.start()


def gather_finish(refs):
    n, me, sibling, chips, mc, copy, local, first = _gather_plan(refs)
    onward = []
    for j, chip in enumerate(chips):
        for a in range(n):
            copy(a, 1 + j, (*chip, mc), me).wait_recv()
            cp = copy(a, 4 + j, (*chip, mc), sibling)
            cp.start()
            onward.append(cp)
    for a in range(n):
        copy(a, 0, sibling, me).wait_recv()
        for j, chip in enumerate(chips):
            copy(a, 4 + j, (*chip, 1 - mc), me).wait_recv()
    for cp in first + onward:
        cp.wait_send()
    for cp in local:
        cp.wait()


def _scatter_plan(refs):
    g_refs, out_refs, send_sems, recv_sems, local_sems = refs
    n = len(g_refs)
    mx, my, mc = _mesh_pos()
    me = 4 * mx + 2 * my + mc
    local = [pltpu.make_async_copy(g_refs[a].at[me], out_refs[a].at[me], local_sems.at[a]) for a in range(n)]
    sends, arrivals = [], []
    for k in range(1, N_DEV):
        peer = (_flip(mx, k & 4), _flip(my, k & 2), _flip(mc, k & 1))
        peer_slot = 4 * peer[0] + 2 * peer[1] + peer[2]
        for a in range(n):
            sems = dict(send_sem=send_sems.at[a, k - 1], recv_sem=recv_sems.at[a, k - 1], device_id=peer,
                        device_id_type=MESH_IDS)
            sends.append(pltpu.make_async_remote_copy(src_ref=g_refs[a].at[peer_slot], dst_ref=out_refs[a].at[me],
                                                      **sems))
            arrivals.append(pltpu.make_async_remote_copy(src_ref=g_refs[a].at[peer_slot],
                                                         dst_ref=out_refs[a].at[peer_slot], **sems))
    return local, sends, arrivals


def scatter_start(refs):
    local, sends, arrivals = _scatter_plan(refs)
    for cp in local + sends:
        cp.start()


def scatter_finish(refs):
    local, sends, arrivals = _scatter_plan(refs)
    for cp in arrivals:
        cp.wait_recv()
    for cp in sends:
        cp.wait_send()
    for cp in local:
        cp.wait()


def _split_comm_refs(refs, n_in, n_out, n_side):
    ins, side_in = refs[:n_in], refs[n_in:n_in + n_side]
    outs = refs[n_in + n_side:n_in + n_side + n_out]
    side_out = refs[n_in + n_side + n_out:n_in + 2 * n_side + n_out]
    rest = refs[n_in + 2 * n_side + n_out:]
    if n_side == 0:
        return ins + outs + rest, None
    return ins + outs + rest[:-3], (side_in, side_out) + tuple(rest[-3:])


def _log_sigmoid(x):
    return jnp.minimum(x, 0.0) - jnp.log(1.0 + jnp.exp(-jnp.abs(x)))


def fox_cumlog(proj, b_f, D, *, name):
    S = proj.shape[0]
    ts = _pick(S, (512, 256, 128))
    blk = (3 * D + SEG_FL * SEG) // LANES

    def body(f_ref, b_ref, c_ref, carry):
        i = pl.program_id(0)

        @pl.when(i == 0)
        def _():
            carry[...] = jnp.zeros_like(carry)

        rows = lax.broadcasted_iota(jnp.int32, (ts, 1), 0)
        acc = _log_sigmoid(f_ref[...] + b_ref[...])
        d = 1
        while d < ts:
            acc = acc + jnp.where(rows >= d, pltpu.roll(acc, d, 0), 0.0)
            d *= 2
        acc = acc + carry[...]
        c_ref[...] = acc
        carry[...] = acc[ts - 1:ts, :]

    return pl.pallas_call(
        body, name=name, grid=(S // ts,),
        in_specs=[pl.BlockSpec((ts, LANES), lambda i: (i, blk)), pl.BlockSpec((1, LANES), lambda i: (0, 0))],
        out_specs=pl.BlockSpec((ts, LANES), lambda i: (i, 0)),
        out_shape=jax.ShapeDtypeStruct((S, LANES), F32),
        scratch_shapes=[pltpu.VMEM((1, LANES), F32)],
        compiler_params=_params("arbitrary"),
    )(proj, b_f)


def fox_dlogit(proj, b_f, dc, D, *, name):
    S = proj.shape[0]
    ts = _pick(S, (512, 256, 128))
    nt = S // ts
    blk = (3 * D + SEG_FL * SEG) // LANES

    def body(f_ref, b_ref, dc_ref, df_ref, db_ref, carry):
        i = pl.program_id(0)

        @pl.when(i == 0)
        def _():
            carry[...] = jnp.zeros_like(carry)

        rows = lax.broadcasted_iota(jnp.int32, (ts, 1), 0)
        acc = dc_ref[...]
        d = 1
        while d < ts:
            acc = acc + jnp.where(rows < ts - d, pltpu.roll(acc, ts - d, 0), 0.0)
            d *= 2
        acc = acc + carry[...]
        carry[...] = acc[0:1, :]
        df = acc * _sig(-(f_ref[...] + b_ref[...]))
        df_ref[...] = jnp.zeros_like(df_ref)
        df_ref[:, 0:LANES] = df.astype(BF16)
        part = jnp.sum(df, axis=0, keepdims=True)

        @pl.when(i == 0)
        def _():
            db_ref[...] = part

        @pl.when(i > 0)
        def _():
            db_ref[...] += part

    rev = lambda i: nt - 1 - i
    return pl.pallas_call(
        body, name=name, grid=(nt,),
        in_specs=[pl.BlockSpec((ts, LANES), lambda i: (rev(i), blk)), pl.BlockSpec((1, LANES), lambda i: (0, 0)),
                  pl.BlockSpec((ts, LANES), lambda i: (rev(i), 0))],
        out_specs=[pl.BlockSpec((ts, SEG), lambda i: (rev(i), 0)), pl.BlockSpec((1, LANES), lambda i: (0, 0))],
        out_shape=[jax.ShapeDtypeStruct((S, SEG), BF16), jax.ShapeDtypeStruct((1, LANES), F32)],
        scratch_shapes=[pltpu.VMEM((1, LANES), F32)],
        compiler_params=_params("arbitrary"),
    )(proj, b_f, dc)


def _causal_mask(t, keys_on_rows=True):
    r = lax.broadcasted_iota(jnp.int32, (t, t), 0)
    c = lax.broadcasted_iota(jnp.int32, (t, t), 1)
    return (r <= c) if keys_on_rows else (c <= r)


FOX_PAD = LANES
COL_C = FOX_HEAD_DIM
COL_ROWSUM = FOX_HEAD_DIM + 3
COL_L = FOX_HEAD_DIM


PACK_TILES = (2048, 1024, 512, 256, 128)


def _head_pair(ref, a):
    x = ref[...]
    return x if a == 0 else pltpu.roll(x, FOX_HEAD_DIM, 1)


def fox_pack(proj, c, D, scale, *, name):
    S = proj.shape[0]
    ts = _pick(S, PACK_TILES)
    base = 3 * D // LANES
    per_seg = SEG // LANES

    def body(q_ref, k_ref, v_ref, c_ref, qo, ko, vo, qto, vto):
        hp = pl.program_id(1)
        lane = lax.broadcasted_iota(jnp.int32, (ts, LANES), 1)
        is_val = lane < FOX_HEAD_DIM
        cv = c_ref[...]
        for a in range(2):
            ch = jnp.sum(jnp.where(lane == 2 * hp + a, cv, 0.0), axis=1, keepdims=True)
            c_hi = ch.astype(BF16).astype(F32)
            r1 = ch - c_hi
            c_mid = r1.astype(BF16).astype(F32)
            c_lo = r1 - c_mid
            qa = jnp.where(is_val, _head_pair(q_ref, a) * scale, jnp.where(lane < COL_C + 3, 1.0, 0.0))
            extra = jnp.where(lane == COL_C, -c_hi, jnp.where(lane == COL_C + 1, -c_mid, jnp.where(
                lane == COL_C + 2, -c_lo, jnp.where(lane == COL_ROWSUM, 1.0, 0.0))))
            va = jnp.where(is_val, _head_pair(v_ref, a), jnp.where(lane == COL_L, 1.0, 0.0))
            qo[a] = qa.astype(BF16)
            ko[a] = jnp.where(is_val, _head_pair(k_ref, a), extra).astype(BF16)
            vo[a] = va.astype(BF16)
            qto[a] = qa.T.astype(BF16)
            vto[a] = va.T.astype(BF16)

    seg = lambda s: pl.BlockSpec((ts, LANES), lambda i, hp: (i, base + s * per_seg + hp))
    out = pl.BlockSpec((2, ts, FOX_PAD), lambda i, hp: (hp, i, 0))
    out_t = pl.BlockSpec((2, FOX_PAD, ts), lambda i, hp: (hp, 0, i))
    return pl.pallas_call(
        body, name=name, grid=(S // ts, FOX_HEADS // 2),
        in_specs=[seg(SEG_FQ), seg(SEG_FK), seg(SEG_FV), pl.BlockSpec((ts, LANES), lambda i, hp: (i, 0))],
        out_specs=[out] * 3 + [out_t] * 2,
        out_shape=[jax.ShapeDtypeStruct((FOX_HEADS, S, FOX_PAD), BF16)] * 3
        + [jax.ShapeDtypeStruct((FOX_HEADS, FOX_PAD, S), BF16)] * 2,
        compiler_params=_params("parallel", "parallel"),
    )(proj, proj, proj, c)


def heads_pack(x, *, name):
    S = x.shape[0]
    ts = _pick(S, PACK_TILES)

    def body(x_ref, o_ref, ot_ref):
        lane = lax.broadcasted_iota(jnp.int32, (ts, LANES), 1)
        for a in range(2):
            xa = jnp.where(lane < FOX_HEAD_DIM, _head_pair(x_ref, a), 0.0)
            o_ref[a] = xa.astype(BF16)
            ot_ref[a] = xa.T.astype(BF16)

    return pl.pallas_call(
        body, name=name, grid=(S // ts, FOX_HEADS // 2),
        in_specs=[pl.BlockSpec((ts, LANES), lambda i, hp: (i, hp))],
        out_specs=[pl.BlockSpec((2, ts, FOX_PAD), lambda i, hp: (hp, i, 0)),
                   pl.BlockSpec((2, FOX_PAD, ts), lambda i, hp: (hp, 0, i))],
        out_shape=[jax.ShapeDtypeStruct((FOX_HEADS, S, FOX_PAD), BF16),
                   jax.ShapeDtypeStruct((FOX_HEADS, FOX_PAD, S), BF16)],
        compiler_params=_params("parallel", "parallel"),
    )(x)


def heads_unpack(x, scale, *, name):
    S = x.shape[1]
    ts = _pick(S, PACK_TILES)

    def body(x_ref, o_ref):
        lane = lax.broadcasted_iota(jnp.int32, (ts, LANES), 1)
        both = jnp.where(lane < FOX_HEAD_DIM, x_ref[0], pltpu.roll(x_ref[1], FOX_HEAD_DIM, 1))
        o_ref[...] = (both * scale).astype(BF16)

    return pl.pallas_call(
        body, name=name, grid=(S // ts, FOX_HEADS // 2),
        in_specs=[pl.BlockSpec((2, ts, FOX_PAD), lambda i, hp: (hp, i, 0))],
        out_specs=pl.BlockSpec((ts, LANES), lambda i, hp: (i, hp)),
        out_shape=jax.ShapeDtypeStruct((S, FOX_HEADS * FOX_HEAD_DIM), BF16),
        compiler_params=_params("parallel", "parallel"),
    )(x)


FOX_FWD_GROUP = 1


def _fox_fwd_tile(S):
    return min(2048, max(128, S // 4))


def _fox_bwd_tile(S):
    return min(1024, max(128, S // 4))


def fox_fwd(qt, k, vt, *, name, gather=()):
    H, W, S = qt.shape
    t = _fox_fwd_tile(S)
    n = S // t
    G = FOX_FWD_GROUP
    ns = len(gather)

    def body(*refs):
        (qt_ref, k_ref, vt_ref, ot_ref, lse_ref, m_sc, acc_sc), comm = _split_comm_refs(refs, 3, 2, ns)
        h, i, j = pl.program_id(0), pl.program_id(1), pl.program_id(2)

        if ns:
            @pl.when((h == 0) & (i == 0) & (j == 0))
            def _():
                gather_start(comm)

        @pl.when(j == 0)
        def _():
            m_sc[...] = jnp.full_like(m_sc, NEG_BIG)
            acc_sc[...] = jnp.zeros_like(acc_sc)

        def step(masked):
            for g in range(G):
                st = jnp.dot(k_ref[g], qt_ref[g], preferred_element_type=F32)
                if masked:
                    st = jnp.where(_causal_mask(t, keys_on_rows=True), st, NEG_BIG)
                m_prev = m_sc[g]
                m_new = jnp.maximum(m_prev, jnp.max(st, axis=0, keepdims=True))
                pt = jnp.exp(st - m_new)
                acc_sc[g] = jnp.exp(m_prev - m_new) * acc_sc[g] + jnp.dot(vt_ref[g], pt.astype(BF16),
                                                                          preferred_element_type=F32)
                m_sc[g] = m_new

        @pl.when(j < i)
        def _():
            step(False)

        @pl.when(j == i)
        def _():
            step(True)
            for g in range(G):
                acc = acc_sc[g]
                l = acc[COL_L:COL_L + 1, :]
                ot_ref[g] = acc / l
                lse_ref[g] = m_sc[g] + jnp.log(l)

        if ns:
            @pl.when((h == H // G - 1) & (i == n - 1) & (j == n - 1))
            def _():
                gather_finish(comm)

    qs = pl.BlockSpec((G, W, t), lambda h, i, j: (h, 0, i))
    ks = pl.BlockSpec((G, t, W), lambda h, i, j: (h, jnp.minimum(j, i), 0))
    vs = pl.BlockSpec((G, W, t), lambda h, i, j: (h, 0, jnp.minimum(j, i)))
    row = pl.BlockSpec((G, 1, t), lambda h, i, j: (h, 0, i))
    outs = pl.pallas_call(
        body, name=name, grid=(H // G, n, n), in_specs=[qs, ks, vs] + [ANY_SPEC] * ns,
        out_specs=[qs, row] + [ANY_SPEC] * ns,
        out_shape=[jax.ShapeDtypeStruct((H, W, S), F32), jax.ShapeDtypeStruct((H, 1, S), F32)]
        + [jax.ShapeDtypeStruct((N_DEV,) + x.shape, x.dtype) for x in gather],
        scratch_shapes=[pltpu.VMEM((G, 1, t), F32), pltpu.VMEM((G, W, t), F32)] + (_comm_scratch(ns) if ns else []),
        compiler_params=_params("arbitrary", "arbitrary", "arbitrary") if ns
        else _params("parallel", "parallel", "arbitrary"),
    )(qt, k, vt, *gather)
    return outs[0], outs[1], list(outs[2:])


def fox_dc(dq, dk, *, name):
    H, S, W = dq.shape
    ts = _pick(S, (512, 256, 128))

    def body(dq_ref, dk_ref, o_ref):
        lane = lax.broadcasted_iota(jnp.int32, (ts, LANES), 1)
        acc = jnp.zeros((ts, LANES), F32)
        for h in range(H):
            d = dq_ref[h][:, COL_ROWSUM:COL_ROWSUM + 1] - dk_ref[h][:, COL_C:COL_C + 1]
            acc = jnp.where(lane == h, d, acc)
        o_ref[...] = acc

    blk = pl.BlockSpec((H, ts, W), lambda i: (0, i, 0))
    return pl.pallas_call(
        body, name=name, grid=(S // ts,), in_specs=[blk, blk],
        out_specs=pl.BlockSpec((ts, LANES), lambda i: (i, 0)),
        out_shape=jax.ShapeDtypeStruct((S, LANES), F32), compiler_params=_params("parallel"),
    )(dq, dk)


def heads_unpack_t(xt, *, name):
    S = xt.shape[2]
    ts = _pick(S, PACK_TILES)

    def body(x_ref, y_ref, o_ref):
        lane = lax.broadcasted_iota(jnp.int32, (ts, LANES), 1)
        x0, x1 = x_ref[0].T, x_ref[1].T
        o_ref[0] = x0
        o_ref[1] = x1
        y_ref[...] = jnp.where(lane < FOX_HEAD_DIM, x0, pltpu.roll(x1, FOX_HEAD_DIM, 1)).astype(BF16)

    return pl.pallas_call(
        body, name=name, grid=(S // ts, FOX_HEADS // 2),
        in_specs=[pl.BlockSpec((2, FOX_PAD, ts), lambda i, hp: (hp, 0, i))],
        out_specs=[pl.BlockSpec((ts, LANES), lambda i, hp: (i, hp)),
                   pl.BlockSpec((2, ts, FOX_PAD), lambda i, hp: (hp, i, 0))],
        out_shape=[jax.ShapeDtypeStruct((S, FOX_HEADS * FOX_HEAD_DIM), BF16),
                   jax.ShapeDtypeStruct((FOX_HEADS, S, FOX_PAD), F32)],
        compiler_params=_params("parallel", "parallel"),
    )(xt)


def fox_delta(do, o, *, name):
    H, S, Dh = o.shape
    t = _pick(S, (2048, 1024, 512, 256, 128))

    def body(do_ref, o_ref, d_ref):
        d_ref[0] = jnp.sum(do_ref[0] * o_ref[0], axis=-1, keepdims=True)

    blk = pl.BlockSpec((1, t, Dh), lambda h, i: (h, i, 0))
    return pl.pallas_call(
        body, name=name, grid=(H, S // t), in_specs=[blk, blk],
        out_specs=pl.BlockSpec((1, t, 1), lambda h, i: (h, i, 0)),
        out_shape=jax.ShapeDtypeStruct((H, S, 1), F32), compiler_params=_params("parallel", "parallel"),
    )(do, o)


def fox_bwd(qt, q, k, v, dot, do, lse, delta, *, name, scatter=()):
    H, W, S = qt.shape
    t = _fox_bwd_tile(S)
    n = S // t
    ns = len(scatter)

    def body(*refs):
        own, comm = _split_comm_refs(refs, 8, 3, ns)
        (qt_ref, q_ref, k_ref, v_ref, dot_ref, do_ref, lse_ref, dl_ref, dq_hbm, dk_ref, dv_ref,
         dq_sc, dk_sc, dv_sc, sem) = own
        h, j, i = pl.program_id(0), pl.program_id(1), pl.program_id(2)

        if ns:
            @pl.when((h == 0) & (j == 0) & (i == 0))
            def _():
                scatter_start(comm)

        @pl.when((j == 0) & (i == 0))
        def _():
            dq_sc[...] = jnp.zeros_like(dq_sc)

        @pl.when(i == 0)
        def _():
            dk_sc[...] = jnp.zeros_like(dk_sc)
            dv_sc[...] = jnp.zeros_like(dv_sc)

        def step(masked):
            st = jnp.dot(k_ref[0], qt_ref[0], preferred_element_type=F32)
            pt = jnp.exp(st - lse_ref[0])
            if masked:
                pt = jnp.where(_causal_mask(t, keys_on_rows=True), pt, 0.0)
            dpt = jnp.dot(v_ref[0], dot_ref[0], preferred_element_type=F32)
            dst = (pt * (dpt - dl_ref[0])).astype(BF16)
            dv_sc[...] += jnp.dot(pt.astype(BF16), do_ref[0], preferred_element_type=F32)
            dk_sc[...] += jnp.dot(dst, q_ref[0], preferred_element_type=F32)
            rows = pl.ds(pl.multiple_of(i * t, t), t)
            dq_sc[rows, :] += lax.dot_general(dst, k_ref[0], (((0,), (0,)), ((), ())), preferred_element_type=F32)

        @pl.when(i > j)
        def _():
            step(False)

        @pl.when(i == j)
        def _():
            step(True)

        @pl.when(i == n - 1)
        def _():
            dk_ref[0] = dk_sc[...]
            dv_ref[0] = dv_sc[...]

        @pl.when((j == n - 1) & (i == n - 1))
        def _():
            out = pltpu.make_async_copy(dq_sc, dq_hbm.at[h], sem)
            out.start()
            out.wait()

        if ns:
            @pl.when((h == H - 1) & (j == n - 1) & (i == n - 1))
            def _():
                scatter_finish(comm)

    q_t = pl.BlockSpec((1, W, t), lambda h, j, i: (h, 0, jnp.maximum(i, j)))
    q_r = pl.BlockSpec((1, t, W), lambda h, j, i: (h, jnp.maximum(i, j), 0))
    k_r = pl.BlockSpec((1, t, W), lambda h, j, i: (h, j, 0))
    row = pl.BlockSpec((1, 1, t), lambda h, j, i: (h, 0, jnp.maximum(i, j)))
    outs = pl.pallas_call(
        body, name=name, grid=(H, n, n), in_specs=[q_t, q_r, k_r, k_r, q_t, q_r, row, row] + [ANY_SPEC] * ns,
        out_specs=[ANY_SPEC, k_r, k_r] + [ANY_SPEC] * ns,
        out_shape=[jax.ShapeDtypeStruct((H, S, W), F32)] * 3 + [jax.ShapeDtypeStruct(g.shape, g.dtype) for g in scatter],
        scratch_shapes=[pltpu.VMEM((S, W), F32), pltpu.VMEM((t, W), F32), pltpu.VMEM((t, W), F32),
                        pltpu.SemaphoreType.DMA] + (_comm_scratch(ns) if ns else []),
        compiler_params=_params("arbitrary", "arbitrary", "arbitrary"),
    )(qt, q, k, v, dot, do, lse, delta, *scatter)
    return outs[0], outs[1], outs[2], list(outs[3:])


def merge_fwd(proj, branches, D, *, name):
    S = proj.shape[0]
    ts = _pick(S, (256, 128))

    def body(g0, g1, g2, b0, b1, b2, o_ref):
        acc = (_sig(g0[...]) * b0[...].astype(F32) + _sig(g1[...]) * b1[...].astype(F32)
               + _sig(g2[...]) * b2[...].astype(F32))
        o_ref[...] = acc.astype(BF16)

    gate = lambda n: pl.BlockSpec((ts, D), lambda i: (i, n))
    row = pl.BlockSpec((ts, D), lambda i: (i, 0))
    return pl.pallas_call(
        body, name=name, grid=(S // ts,), in_specs=[gate(0), gate(1), gate(2), row, row, row], out_specs=row,
        out_shape=jax.ShapeDtypeStruct((S, D), BF16), compiler_params=_params("parallel"),
    )(proj, proj, proj, *branches)


def merge_bwd(proj, branches, dm, D, *, name):
    S = proj.shape[0]
    ts = _pick(S, (256, 128))

    def body(g0, g1, g2, b0, b1, b2, dm_ref, db0, db1, db2, dg0, dg1, dg2):
        dmv = dm_ref[...]
        for g_ref, b_ref, db_ref, dg_ref in ((g0, b0, db0, dg0), (g1, b1, db1, dg1), (g2, b2, db2, dg2)):
            s = _sig(g_ref[...])
            db_ref[...] = (dmv * s).astype(BF16)
            dg_ref[...] = (dmv * b_ref[...].astype(F32) * (s * (1.0 - s))).astype(BF16)

    gate = lambda n: pl.BlockSpec((ts, D), lambda i: (i, n))
    row = pl.BlockSpec((ts, D), lambda i: (i, 0))
    return pl.pallas_call(
        body, name=name, grid=(S // ts,), in_specs=[gate(0), gate(1), gate(2), row, row, row, row],
        out_specs=[row] * 6, out_shape=[jax.ShapeDtypeStruct((S, D), BF16)] * 6,
        compiler_params=_params("parallel"),
    )(proj, proj, proj, *branches, dm)


def xa_fwd(q, k, v, *, name):
    S, D = q.shape
    M = k.shape[0]
    dh = D // XA_HEADS
    scale = dh ** -0.5
    t = _pick(S, (512, 256, 128))

    def body(q_ref, k_ref, v_ref, o_ref):
        for h in range(XA_HEADS):
            cols = slice(h * dh, (h + 1) * dh)
            s = lax.dot_general(q_ref[:, cols], k_ref[:, cols], (((1,), (1,)), ((), ())),
                                preferred_element_type=F32) * scale
            p = jnp.exp(s - jnp.max(s, axis=-1, keepdims=True))
            p = p / jnp.sum(p, axis=-1, keepdims=True)
            o_ref[:, cols] = jnp.dot(p.astype(BF16), v_ref[:, cols], preferred_element_type=F32).astype(BF16)

    row = pl.BlockSpec((t, D), lambda i: (i, 0))
    full = pl.BlockSpec((M, D), lambda i: (0, 0))
    return pl.pallas_call(
        body, name=name, grid=(S // t,), in_specs=[row, full, full], out_specs=row,
        out_shape=jax.ShapeDtypeStruct((S, D), BF16), compiler_params=_params("parallel"),
    )(q, k, v)


def xa_bwd(q, k, v, do, *, name):
    S, D = q.shape
    M = k.shape[0]
    dh = D // XA_HEADS
    scale = dh ** -0.5
    t = _pick(S, (512, 256, 128))

    def body(q_ref, k_ref, v_ref, do_ref, dq_ref, dk_ref, dv_ref):
        i = pl.program_id(0)

        @pl.when(i == 0)
        def _():
            dk_ref[...] = jnp.zeros_like(dk_ref)
            dv_ref[...] = jnp.zeros_like(dv_ref)

        for h in range(XA_HEADS):
            cols = slice(h * dh, (h + 1) * dh)
            qh, kh, vh = q_ref[:, cols], k_ref[:, cols], v_ref[:, cols]
            dob = do_ref[:, cols].astype(BF16)
            s = lax.dot_general(qh, kh, (((1,), (1,)), ((), ())), preferred_element_type=F32) * scale
            p = jnp.exp(s - jnp.max(s, axis=-1, keepdims=True))
            p = p / jnp.sum(p, axis=-1, keepdims=True)
            dp = lax.dot_general(dob, vh, (((1,), (1,)), ((), ())), preferred_element_type=F32)
            ds = (p * (dp - jnp.sum(p * dp, axis=-1, keepdims=True)) * scale).astype(BF16)
            dq_ref[:, cols] = jnp.dot(ds, kh, preferred_element_type=F32).astype(BF16)
            dk_ref[:, cols] += lax.dot_general(ds, qh, (((0,), (0,)), ((), ())), preferred_element_type=F32)
            dv_ref[:, cols] += lax.dot_general(p.astype(BF16), dob, (((0,), (0,)), ((), ())),
                                               preferred_element_type=F32)

    row = pl.BlockSpec((t, D), lambda i: (i, 0))
    full = pl.BlockSpec((M, D), lambda i: (0, 0))
    return pl.pallas_call(
        body, name=name, grid=(S // t,), in_specs=[row, full, full, row], out_specs=[row, full, full],
        out_shape=[jax.ShapeDtypeStruct((S, D), BF16), jax.ShapeDtypeStruct((M, D), F32),
                   jax.ShapeDtypeStruct((M, D), F32)],
        compiler_params=_params("arbitrary"),
    )(q, k, v, do)


def mixer_fwd(x, w, tag, gather=()):
    S, D = x.shape
    h = rms_fwd(x, w["mix_norm"], name=f"{tag}_rms")
    proj = mm(h, w["w_in"], name=f"{tag}_proj")
    y_a = conv_fwd(proj, w["conv_w"], D, name=f"{tag}_conv")
    y_b = sg_fwd(proj, w["sg_norm"], w["sg_w"], w["sg_bt"], D, name=f"{tag}_sg")
    c = fox_cumlog(proj, w["fox_b_f"], D, name=f"{tag}_cumlog")
    qh, kh, vh, qt, vt = fox_pack(proj, c, D, FOX_HEAD_DIM ** -0.5, name=f"{tag}_foxpack")
    ot, lse, gathered = fox_fwd(qt, kh, vt, name=f"{tag}_fox", gather=gather)
    y_c, o = heads_unpack_t(ot, name=f"{tag}_foxout")
    ys = (y_a, y_b, y_c)
    branches = [mm(ys[n], w["w_branch"][n], name=f"{tag}_branch{n}", out_dtype=BF16) for n in range(N_BRANCH)]
    merged = merge_fwd(proj, branches, D, name=f"{tag}_merge")
    y = mm(merged, w["w_out"], name=f"{tag}_out", res=x)
    return y, (x, h, proj, ys, qh, kh, vh, qt, o, lse, branches, merged), gathered


def mixer_bwd(dx, saved, w, tag, scatter=()):
    x, h, proj, ys, qh, kh, vh, qt, o, lse, branches, merged = saved
    S, D = x.shape
    grads = {"w_out": mm_tn(merged, dx, name=f"{tag}_dwout")}
    dmerged = mm(dx, w["w_out_t"], name=f"{tag}_dmerged")
    outs = merge_bwd(proj, branches, dmerged, D, name=f"{tag}_dmerge")
    dbr, dgl = outs[:3], outs[3:]
    grads["w_branch"] = jnp.stack([mm_tn(ys[n], dbr[n], name=f"{tag}_dwbr{n}") for n in range(N_BRANCH)])
    dys = [mm(dbr[n], w["w_branch_t"][n], name=f"{tag}_dy{n}") for n in range(N_BRANCH)]
    d_ab, d_ac, d_ah, grads["conv_w"] = conv_bwd(proj, w["conv_w"], dys[0], D, name=f"{tag}_dconv")
    d_u, d_v, grads["sg_w"], d_sgb, grads["sg_norm"] = sg_bwd(
        proj, w["sg_norm"], w["sg_w"], w["sg_bt"], dys[1], D, name=f"{tag}_dsg")
    grads["sg_b"] = d_sgb[:, :, 0]
    do, dot = heads_pack(dys[2], name=f"{tag}_dopack")
    delta = fox_delta(do, o, name=f"{tag}_delta")
    dq, dk, dv, scattered = fox_bwd(qt, qh, kh, vh, dot, do, lse, delta.reshape(FOX_HEADS, 1, S),
                                    name=f"{tag}_foxbwd", scatter=scatter)
    dc_rows = fox_dc(dq, dk, name=f"{tag}_dc")
    d_fl, d_bf = fox_dlogit(proj, w["fox_b_f"], dc_rows, D, name=f"{tag}_dflogit")
    grads["fox_b_f"] = d_bf[0, :FOX_HEADS]
    dproj = jnp.concatenate(
        list(dgl) + [d_ab, d_ac, d_ah, d_u, d_v, heads_unpack(dq, FOX_HEAD_DIM ** -0.5, name=f"{tag}_dqout"),
                     heads_unpack(dk, 1.0, name=f"{tag}_dkout"), heads_unpack(dv, 1.0, name=f"{tag}_dvout"), d_fl],
        axis=1)
    grads["w_in"] = mm_tn(h, dproj, name=f"{tag}_dwin")
    dx, dg = mm(dproj, w["w_in_t"], name=f"{tag}_dh", extras=(x, dx), row_extras=(w["mix_norm"],),
                epilogue=_rms_grad, out_dtypes=(F32,), n_sums=1)
    grads["mix_norm"] = _sum_tiles(dg)
    return dx, grads, scattered


def xattn_fwd(x, mem, w, tag):
    h = rms_fwd(x, w["xa_norm"], name=f"{tag}_rms")
    m = rms_fwd(mem, w["mem_norm"], name=f"{tag}_mrms")
    q = mm(h, w["xa_wq"], name=f"{tag}_q", out_dtype=BF16)
    k = mm(m, w["xa_wk"], name=f"{tag}_k", out_dtype=BF16)
    v = mm(m, w["xa_wv"], name=f"{tag}_v", out_dtype=BF16)
    o = xa_fwd(q, k, v, name=f"{tag}_attn")
    y = mm(o, w["xa_wo"], name=f"{tag}_o", res=x)
    return y, (x, h, m, q, k, v, o)


def xattn_bwd(dx, mem, saved, w, tag):
    x, h, m, q, k, v, o = saved
    grads = {"xa_wo": mm_tn(o, dx, name=f"{tag}_dwo")}
    do = mm(dx, w["xa_wo_t"], name=f"{tag}_do")
    dq, dk, dv = xa_bwd(q, k, v, do, name=f"{tag}_dattn")
    grads["xa_wq"] = mm_tn(h, dq, name=f"{tag}_dwq")
    grads["xa_wk"] = mm_tn(m, dk, name=f"{tag}_dwk")
    grads["xa_wv"] = mm_tn(m, dv, name=f"{tag}_dwv")
    dm = mm(dk, w["xa_wk_t"], name=f"{tag}_dm1")
    dm = mm(dv, w["xa_wv_t"], name=f"{tag}_dm2", res=dm)
    _, grads["mem_norm"] = rms_bwd(mem, w["mem_norm"], dm, None, name=f"{tag}_dmrms")
    dx, dg = mm(dq, w["xa_wq_t"], name=f"{tag}_dh", extras=(x, dx), row_extras=(w["xa_norm"],),
                epilogue=_rms_grad, out_dtypes=(F32,), n_sums=1)
    grads["xa_norm"] = _sum_tiles(dg)
    return dx, grads


ANY_SPEC = pl.BlockSpec(memory_space=pl.ANY)


def all_gather(xs, *, name):
    n = len(xs)

    def body(*refs):
        comm = (refs[:n], refs[n:2 * n]) + tuple(refs[2 * n:])
        gather_start(comm)
        gather_finish(comm)

    return pl.pallas_call(
        body, name=name, out_shape=[jax.ShapeDtypeStruct((N_DEV,) + x.shape, x.dtype) for x in xs],
        in_specs=[ANY_SPEC] * n, out_specs=[ANY_SPEC] * n, scratch_shapes=_comm_scratch(n),
    )(*xs)


def all_to_all(gs, *, name):
    n = len(gs)

    def body(*refs):
        comm = (refs[:n], refs[n:2 * n]) + tuple(refs[2 * n:])
        scatter_start(comm)
        scatter_finish(comm)

    return pl.pallas_call(
        body, name=name, out_shape=[jax.ShapeDtypeStruct(g.shape, g.dtype) for g in gs],
        in_specs=[ANY_SPEC] * n, out_specs=[ANY_SPEC] * n, scratch_shapes=_comm_scratch(n),
    )(*gs)


ADAM_BLOCK_ELEMS = 256 * 1024


def reduce_adamw(parts, row0, w, m, v, *, name, wrow0=0, rows=None, prev=None):
    Rw, C = w.shape
    R = Rw if rows is None else rows
    n_parts = parts.shape[0]
    tr = R
    for cand in (512, 256, 128, 64, 32, 16):
        if R % cand == 0 and row0 % cand == 0 and wrow0 % cand == 0 and cand * C <= ADAM_BLOCK_ELEMS:
            tr = cand
            break
    assert row0 % tr == 0 and wrow0 % tr == 0
    assert tr % 16 == 0 or (row0 == 0 and wrow0 == 0 and parts.shape[1] == R == Rw)
    bc1 = 1.0 - ADAM_B1 ** ADAM_STEP
    bc2 = 1.0 - ADAM_B2 ** ADAM_STEP

    def body(p_ref, w_ref, m_ref, v_ref, *rest):
        g_ref, d_ref, nm_ref, nv_ref = rest[-4:]
        g = p_ref[0].astype(F32)
        for d in range(1, n_parts):
            g = g + p_ref[d].astype(F32)
        nm = ADAM_B1 * m_ref[...] + (1.0 - ADAM_B1) * g
        nv = ADAM_B2 * v_ref[...] + (1.0 - ADAM_B2) * (g * g)
        m_hat = nm / bc1
        v_hat = nv / bc2
        g_ref[...] = g
        d_ref[...] = -ADAM_LR * (m_hat / (jnp.sqrt(v_hat) + ADAM_EPS) + ADAM_WD * w_ref[...])
        nm_ref[...] = nm
        nv_ref[...] = nv

    blk0, wblk0 = row0 // tr, wrow0 // tr
    row = pl.BlockSpec((tr, C), lambda i: (wblk0 + i, 0))
    carried = () if prev is None else tuple(prev)
    return pl.pallas_call(
        body, name=name, grid=(R // tr,),
        in_specs=[pl.BlockSpec((n_parts, tr, C), lambda i: (0, blk0 + i, 0)), row, row, row]
        + [ANY_SPEC] * len(carried),
        out_specs=[row] * 4, out_shape=[jax.ShapeDtypeStruct((Rw, C), F32)] * 4,
        input_output_aliases={4 + k: k for k in range(len(carried))},
        compiler_params=_params("parallel"),
    )(parts, w, m, v, *carried)


SHARDED = {
    "ffn1_w_gate": 2, "ffn1_w_up": 2, "ffn1_w_down": 1, "w_in": 2, "conv_w": 2, "w_branch": 3, "w_out": 1,
    "xa_wq": 1, "xa_wk": 1, "xa_wv": 1, "xa_wo": 1, "ffn2_w_gate": 2, "ffn2_w_up": 2, "ffn2_w_down": 1,
}
GROUPS = (("ffn1_w_gate", "ffn1_w_up", "ffn2_w_gate", "ffn2_w_up"),
          ("ffn1_w_down", "ffn2_w_down", "w_out", "xa_wq", "xa_wk", "xa_wv", "xa_wo"),
          ("w_in",), ("w_branch",))
REPLICATED = ("ffn1_norm", "mix_norm", "sg_norm", "sg_w", "sg_b", "fox_b_f", "xa_norm", "mem_norm", "ffn2_norm",
              "final_norm")
WEIGHTS = ("ffn1_norm", "ffn1_w_gate", "ffn1_w_up", "ffn1_w_down", "mix_norm", "w_in", "conv_w", "sg_norm", "sg_w",
           "sg_b", "fox_b_f", "w_branch", "w_out", "xa_norm", "mem_norm", "xa_wq", "xa_wk", "xa_wv", "xa_wo",
           "ffn2_norm", "ffn2_w_gate", "ffn2_w_up", "ffn2_w_down", "final_norm")
PACK_ROWS = 1024


def _rows(a):
    return a.reshape(-1, a.shape[-1])


def _pack(arrays, dtype):
    flat = jnp.concatenate([a.reshape(-1).astype(dtype) for a in arrays])
    n = flat.shape[0]
    unit = PACK_ROWS * LANES
    total = -(-n // unit) * unit
    return jnp.pad(flat, (0, total - n)).reshape(total // LANES, LANES)


def _unpack(buf, shapes):
    flat = buf.reshape(-1)
    out, off = [], 0
    for shp in shapes:
        n = 1
        for s in shp:
            n *= s
        out.append(flat[off:off + n].reshape(tuple(shp)))
        off += n
    return out


def _to_dev_major(full, axis):
    shp = full.shape
    a = full.reshape(shp[:axis] + (N_DEV, shp[axis] // N_DEV) + shp[axis + 1:])
    return jnp.moveaxis(a, axis, 0)


def _from_dev_major(a, axis):
    a = jnp.moveaxis(a, 0, axis)
    shp = a.shape
    return a.reshape(shp[:axis] + (shp[axis] * shp[axis + 1],) + shp[axis + 2:])


def _relayout_w_in(w_in, D):
    main = 8 * SEG
    pad = jnp.zeros((w_in.shape[0], SEG - FOX_HEADS), w_in.dtype)
    return jnp.concatenate([w_in[:, main + FOX_HEADS:], w_in[:, :main], w_in[:, main:main + FOX_HEADS], pad], axis=1)


def _unlayout_w_in(g, D):
    return jnp.concatenate([g[:, 3 * D:3 * D + 8 * SEG], g[:, 3 * D + 8 * SEG:3 * D + 8 * SEG + FOX_HEADS],
                            g[:, :3 * D]], axis=1)


def _layer_weights(full, rep, l, D):
    t = lambda a: a.T
    w_in = _relayout_w_in(full["w_in"], D)
    ffn = {}
    for tag in ("ffn1", "ffn2"):
        ffn[tag] = {"norm": rep[f"{tag}_norm"][l][None, :]}
        for n in ("w_gate", "w_up", "w_down"):
            ffn[tag][n] = full[f"{tag}_{n}"]
            ffn[tag][n + "_t"] = t(full[f"{tag}_{n}"])
    mix = {
        "mix_norm": rep["mix_norm"][l][None, :], "w_in": w_in, "w_in_t": t(w_in),
        "conv_w": full["conv_w"], "sg_norm": rep["sg_norm"][l][None, :], "sg_w": rep["sg_w"][l],
        "sg_bt": rep["sg_b"][l][:, :, None],
        "fox_b_f": jnp.pad(rep["fox_b_f"][l], (0, LANES - FOX_HEADS))[None, :],
        "w_branch": full["w_branch"], "w_branch_t": jnp.swapaxes(full["w_branch"], 1, 2),
        "w_out": full["w_out"], "w_out_t": t(full["w_out"]),
    }
    xa = {"xa_norm": rep["xa_norm"][l][None, :], "mem_norm": rep["mem_norm"][l][None, :]}
    for n in ("xa_wq", "xa_wk", "xa_wv", "xa_wo"):
        xa[n] = full[n]
        xa[n + "_t"] = t(full[n])
    return ffn, mix, xa


def kernel(x, mem, ffn1_norm, ffn1_w_gate, ffn1_w_up, ffn1_w_down, mix_norm, w_in, conv_w, sg_norm, sg_w, sg_b, fox_b_f, w_branch, w_out, xa_norm, mem_norm, xa_wq, xa_wk, xa_wv, xa_wo, ffn2_norm, ffn2_w_gate, ffn2_w_up, ffn2_w_down, final_norm, loss_target, m_ffn1_norm, m_ffn1_w_gate, m_ffn1_w_up, m_ffn1_w_down, m_mix_norm, m_w_in, m_conv_w, m_sg_norm, m_sg_w, m_sg_b, m_fox_b_f, m_w_branch, m_w_out, m_xa_norm, m_mem_norm, m_xa_wq, m_xa_wk, m_xa_wv, m_xa_wo, m_ffn2_norm, m_ffn2_w_gate, m_ffn2_w_up, m_ffn2_w_down, m_final_norm, v_ffn1_norm, v_ffn1_w_gate, v_ffn1_w_up, v_ffn1_w_down, v_mix_norm, v_w_in, v_conv_w, v_sg_norm, v_sg_w, v_sg_b, v_fox_b_f, v_w_branch, v_w_out, v_xa_norm, v_mem_norm, v_xa_wq, v_xa_wk, v_xa_wv, v_xa_wo, v_ffn2_norm, v_ffn2_w_gate, v_ffn2_w_up, v_ffn2_w_down, v_final_norm):
    args = locals()
    wts = {n: args[n] for n in WEIGHTS}
    mom = {n: args["m_" + n] for n in WEIGHTS}
    var = {n: args["v_" + n] for n in WEIGHTS}
    depth = ffn1_norm.shape[0]
    S, D = x.shape[1], x.shape[2]
    xs, ms, tgt = x[0], mem[0], loss_target[0]

    layer_rows = {n: _rows(wts[n][0]).shape[0] for n in SHARDED}
    row0 = {}
    for grp in GROUPS:
        off = 0
        for n in grp:
            row0[n] = off
            off += layer_rows[n]

    def weight_groups(l):
        return [jnp.concatenate([_rows(wts[n][l]).astype(BF16) for n in grp]) for grp in GROUPS]

    def gathered_layer(got, l):
        full = {"conv_w": conv_full[l]}
        for grp, arr in zip(GROUPS, got):
            for n in grp:
                block = arr[:, row0[n]:row0[n] + layer_rows[n]].reshape((N_DEV,) + wts[n].shape[1:])
                full[n] = _from_dev_major(block, SHARDED[n] - 1)
        return full

    def grad_groups(g):
        def dev_major_rows(n):
            a = _to_dev_major(g[n], SHARDED[n] - 1)
            return a.reshape(N_DEV, -1, a.shape[-1]).astype(BF16)
        return [jnp.concatenate([dev_major_rows(n) for n in grp], axis=1) for grp in GROUPS]

    got = all_gather(weight_groups(0) + [_rows(conv_w)], name="gather_weights")
    conv_full = _from_dev_major(got[-1].reshape((N_DEV,) + conv_w.shape), SHARDED["conv_w"])
    full = gathered_layer(got[:-1], 0)
    rep = {n: wts[n] for n in REPLICATED}

    saved, layers = [], []
    h = xs
    for l in range(depth):
        ffn, mix, xa = _layer_weights(full, rep, l, D)
        layers.append((ffn, mix, xa))
        h, s1 = ffn_fwd(h, ffn["ffn1"], f"l{l}_ffn1")
        h, s2, got = mixer_fwd(h, mix, f"l{l}_mix", gather=weight_groups(l + 1) if l + 1 < depth else ())
        if l + 1 < depth:
            full = gathered_layer(got, l + 1)
        h, s3 = xattn_fwd(h, ms, xa, f"l{l}_xa")
        h, s4 = ffn_fwd(h, ffn["ffn2"], f"l{l}_ffn2")
        saved.append((s1, s2, s3, s4))
    dx, d_final, loss_cols = final_loss_bwd(h, final_norm[None, :], tgt, name="final_loss")
    loss = lax.psum(0.5 * jnp.sum(loss_cols) / D, ("x", "y", "c"))

    per_layer, parts, pending = [], {}, ()
    for l in reversed(range(depth)):
        ffn, mix, xa = layers[l]
        s1, s2, s3, s4 = saved[l]
        g = {}
        dx, g4 = ffn_bwd(dx, s4, ffn["ffn2"], f"l{l}_ffn2")
        dx, g3 = xattn_bwd(dx, ms, s3, xa, f"l{l}_xa")
        dx, g2, got = mixer_bwd(dx, s2, mix, f"l{l}_mix", scatter=pending)
        if pending:
            parts[l + 1] = got
        dx, g1 = ffn_bwd(dx, s1, ffn["ffn1"], f"l{l}_ffn1")
        for tag, gg in (("ffn1", g1), ("ffn2", g4)):
            for n in ("w_gate", "w_up", "w_down"):
                g[f"{tag}_{n}"] = gg[n]
            g[f"{tag}_norm"] = gg["norm"][0]
        g.update(g3)
        g["xa_norm"], g["mem_norm"] = g3["xa_norm"][0], g3["mem_norm"][0]
        g.update({k: v for k, v in g2.items() if k != "w_in"})
        g["w_in"] = _unlayout_w_in(g2["w_in"], D)
        g["mix_norm"], g["sg_norm"] = g2["mix_norm"][0], g2["sg_norm"][0]
        per_layer.append(g)
        pending = grad_groups(g)
    parts[0] = all_to_all(pending, name="scatter_grads")
    per_layer.reverse()
    grads = {n: jnp.stack([per_layer[l][n] for l in range(depth)]) for n in REPLICATED + ("conv_w",)
             if n != "final_norm"}
    grads["final_norm"] = d_final[0]

    res = {k: {} for k in ("g", "d", "m", "v")}
    for gi, grp in enumerate(GROUPS):
        for n in grp:
            outs = None
            for l in range(depth):
                outs = reduce_adamw(parts[l][gi], row0[n], _rows(wts[n]), _rows(mom[n]), _rows(var[n]),
                                    name=f"adamw_{n}_l{l}", wrow0=l * layer_rows[n], rows=layer_rows[n], prev=outs)
            for k, o in zip(("g", "d", "m", "v"), outs):
                res[k][n] = o.reshape(wts[n].shape)

    small = list(REPLICATED)
    shapes = [wts[n].shape for n in small]
    conv_zero = jnp.zeros(grads["conv_w"].shape, F32)
    parts = all_gather([_pack([grads[n] for n in small] + [grads["conv_w"]], F32)], name="gather_small_grads")[0]
    outs = reduce_adamw(parts, 0, _pack([wts[n] for n in small] + [conv_zero], F32),
                        _pack([mom[n] for n in small] + [conv_zero], F32),
                        _pack([var[n] for n in small] + [conv_zero], F32), name="adamw_replicated")
    for k, o in zip(("g", "d", "m", "v"), outs):
        res[k].update(dict(zip(small, _unpack(o, shapes + [conv_zero.shape])[:-1])))
    conv_g = _unpack(outs[0], shapes + [conv_zero.shape])[-1]
    me = 4 * lax.axis_index("x") + 2 * lax.axis_index("y") + lax.axis_index("c")
    width = conv_w.shape[-1]
    conv_g = lax.dynamic_slice_in_dim(conv_g, me * width, width, axis=2)
    outs = reduce_adamw(_rows(conv_g)[None], 0, _rows(conv_w), _rows(mom["conv_w"]), _rows(var["conv_w"]),
                        name="adamw_conv_w")
    for k, o in zip(("g", "d", "m", "v"), outs):
        res[k]["conv_w"] = o.reshape(conv_w.shape)

    return (loss, dx[None], *[res["g"][n] for n in WEIGHTS], *[res["d"][n] for n in WEIGHTS],
            *[res["m"][n] for n in WEIGHTS], *[res["v"][n] for n in WEIGHTS])
```
